```python
import numpy as np
import jax
import jax.numpy as jnp
from jax import lax

D_MODEL = 1024
BATCH = 16
SEQ = 2048
DEPTH = 2

CTX_LEN = 256
GRID_W = 64
N_DIR = 2
N_BRANCH = 3
BRANCH_WIDTH = 512
RMS_EPS = 1e-6
NEG_INF = -1e30

RWKV_HEAD = 64
RWKV_HEADS = BRANCH_WIDTH // RWKV_HEAD
RWKV_DECAY_LORA = 64
RWKV_A_LORA = 64
RWKV_V_LORA = 32
RWKV_G_LORA = 128
RWKV_LNX_EPS = 64e-5

NA_HEAD = 64
NA_HEADS = BRANCH_WIDTH // NA_HEAD
NA_WIN_R = 8
NA_WIN_C = 16
NA_QC = 16
NA_BAND = NA_QC + NA_WIN_C
NA_NCB = GRID_W // NA_QC

MLSTM_HEADS = 4
MLSTM_DQK = 64
MLSTM_DV = BRANCH_WIDTH // MLSTM_HEADS
MLSTM_CHUNK = 64
GATE_CAP = 15.0
ROPE_ROT = MLSTM_DQK // 4
ROPE_BASE = 10000.0

N_EXPERTS = 64
TOP_K = 8
N_GROUPS = 8
TOPK_GROUPS = 4
D_EXPERT = 128
D_SHARED = 256
ROUTED_SCALE = 2.5

RWKV_COLS = 3 * BRANCH_WIDTH + N_DIR * RWKV_DECAY_LORA + N_DIR * RWKV_A_LORA + RWKV_G_LORA
NA_COLS = 3 * BRANCH_WIDTH
MLSTM_QK = MLSTM_HEADS * MLSTM_DQK
MLSTM_COLS = 2 * MLSTM_QK + 2 * BRANCH_WIDTH + 2 * N_DIR * MLSTM_HEADS
GATE_COLS = N_BRANCH * D_MODEL
IN_COLS = RWKV_COLS + NA_COLS + MLSTM_COLS + GATE_COLS
IN_SPLITS = [RWKV_COLS, RWKV_COLS + NA_COLS, RWKV_COLS + NA_COLS + MLSTM_COLS]
RWKV_SPLITS = [BRANCH_WIDTH, 2 * BRANCH_WIDTH, 3 * BRANCH_WIDTH, 3 * BRANCH_WIDTH + N_DIR * RWKV_DECAY_LORA, 3 * BRANCH_WIDTH + N_DIR * (RWKV_DECAY_LORA + RWKV_A_LORA)]
MLSTM_SPLITS = [MLSTM_QK, 2 * MLSTM_QK, 2 * MLSTM_QK + BRANCH_WIDTH, 2 * MLSTM_QK + 2 * BRANCH_WIDTH, 2 * MLSTM_QK + 2 * BRANCH_WIDTH + N_DIR * MLSTM_HEADS]

kernel_name = 'hybrid_rwkv7_natten_mlstm_moe_dit'


def rms_norm(z, g):
    zf = z.astype(jnp.float32)
    zf = zf * lax.rsqrt(jnp.mean(zf * zf, axis=-1, keepdims=True) + RMS_EPS)
    return zf.astype(z.dtype) * g


def soft_cap(z):
    return GATE_CAP * jnp.tanh(z / GATE_CAP)


def token_shift(z, mu):
    zp = jnp.pad(z, ((0, 0), (1, 1), (0, 0)))
    return z + mu * (0.5 * (zp[:, :-2] + zp[:, 2:]) - z)


def rwkv_features(z, v_first, mu, w0, w2, a0, a2, k_k, k_a, g2, vres):
    B, T, _ = z.shape
    z = token_shift(z, mu)
    r, k, v, wd, ad, gd = jnp.split(z, RWKV_SPLITS, axis=-1)
    wd = wd.reshape(B, T, N_DIR, RWKV_DECAY_LORA)
    ad = ad.reshape(B, T, N_DIR, RWKV_A_LORA)
    w = -jax.nn.softplus(-(w0 + jnp.einsum('btdl,dlc->btdc', jnp.tanh(wd), w2))) - 0.5
    decay = jnp.exp(-jnp.exp(w.astype(jnp.float32)))
    a = jax.nn.sigmoid(a0 + jnp.einsum('btdl,dlc->btdc', ad, a2))
    kk = (k * k_k).astype(jnp.float32).reshape(B, T, RWKV_HEADS, RWKV_HEAD)
    kk = kk / jnp.maximum(jnp.sqrt(jnp.sum(kk * kk, axis=-1, keepdims=True)), 1e-12)
    k_dir = k[:, :, None, :] * (1 + (a - 1) * k_a)
    if vres is None:
        v_first = v
    else:
        v0, v1, v2 = vres
        v = v + (v_first - v) * jax.nn.sigmoid(v0 + (v @ v1) @ v2)
    g = jax.nn.sigmoid(gd) @ g2
    return (r, k_dir, v, decay, a, kk, g), v_first


def rwkv_scan_inputs(feats, d):
    r, k_dir, v, decay, a, kk, g = feats
    B, T, _ = r.shape
    heads = lambda t: t.astype(jnp.float32).reshape(B, T, RWKV_HEADS, RWKV_HEAD)
    return heads(r), heads(decay[:, :, d]), -kk, kk * heads(a[:, :, d]), heads(k_dir[:, :, d]), heads(v)


def wkv_scan(state, r, w, a, b, k, v, reverse):
    def step(S, inp):
        r_t, w_t, a_t, b_t, k_t, v_t = inp
        sa = jnp.einsum('bhvk,bhk->bhv', S, a_t)
        S = S * w_t[:, :, None, :] + sa[..., None] * b_t[:, :, None, :] + v_t[..., None] * k_t[:, :, None, :]
        return S, jnp.einsum('bhvk,bhk->bhv', S, r_t)
    xs = tuple(jnp.moveaxis(t, 1, 0) for t in (r, w, a, b, k, v))
    state, y = lax.scan(step, state, xs, reverse=reverse)
    return state, jnp.moveaxis(y, 0, 1)


def rwkv_readout(y, feats, r_k, lnx_g, lnx_b):
    r, k_dir, v, decay, a, kk, g = feats
    B, T, C = r.shape
    mean = jnp.mean(y, axis=-1, keepdims=True)
    var = jnp.mean(jnp.square(y - mean), axis=-1, keepdims=True)
    yn = ((y - mean) * lax.rsqrt(var + RWKV_LNX_EPS)).reshape(B, T, C).astype(r.dtype) * lnx_g + lnx_b
    rh = r.reshape(B, T, RWKV_HEADS, RWKV_HEAD)
    kh = k_dir.reshape(B, T, N_DIR, RWKV_HEADS, RWKV_HEAD)
    bonus = jnp.einsum('bthn,btdhn,hn->bth', rh, kh, r_k)[..., None] * v.reshape(B, T, RWKV_HEADS, RWKV_HEAD)
    return (yn + bonus.reshape(B, T, C)) * g


def rwkv_mix(zx, zc, vf_x, vf_c, mu, w0, w2, a0, a2, k_k, k_a, r_k, g2, lnx_g, lnx_b, vres, need_ctx):
    fx, vf_x = rwkv_features(zx, vf_x, mu, w0, w2, a0, a2, k_k, k_a, g2, vres)
    fc, vf_c = rwkv_features(zc, vf_c, mu, w0, w2, a0, a2, k_k, k_a, g2, vres)
    B = zx.shape[0]
    y_x = 0.0
    y_c = 0.0
    for d in range(N_DIR):
        rev = d == 1
        s0 = jnp.zeros((B, RWKV_HEADS, RWKV_HEAD, RWKV_HEAD), jnp.float32)
        s_ctx, yc = wkv_scan(s0, *rwkv_scan_inputs(fc, d), reverse=rev)
        _, yx = wkv_scan(s_ctx, *rwkv_scan_inputs(fx, d), reverse=rev)
        y_x = y_x + yx
        y_c = y_c + yc
    out_x = rwkv_readout(y_x, fx, r_k, lnx_g, lnx_b)
    out_c = rwkv_readout(y_c, fc, r_k, lnx_g, lnx_b) if need_ctx else None
    return out_x, out_c, vf_x, vf_c


def neighbourhood_attention(q, k, v, kc, vc, rpb, rows):
    B, S, H, DH = q.shape
    win_r = min(NA_WIN_R, rows)
    qcol = np.arange(GRID_W).reshape(NA_NCB, NA_QC)
    band0 = np.clip(np.arange(NA_NCB) * NA_QC - NA_WIN_C // 2, 0, GRID_W - NA_BAND)
    keycol = band0[:, None] + np.arange(NA_BAND)[None, :]
    cs = np.clip(qcol - NA_WIN_C // 2, 0, GRID_W - NA_WIN_C)
    kcol = keycol[:, None, :]
    col_valid = (kcol >= cs[..., None]) & (kcol < cs[..., None] + NA_WIN_C)
    col_idx = np.clip(kcol - qcol[..., None] + NA_WIN_C - 1, 0, 2 * NA_WIN_C - 2)
    qg = (q * DH ** -0.5).reshape(B, rows, GRID_W, H, DH)
    kg = k.reshape(B, rows, GRID_W, H, DH)
    vg = v.reshape(B, rows, GRID_W, H, DH)
    n_loc = win_r * NA_BAND

    def row_block(r):
        rs = jnp.clip(r - win_r // 2, 0, rows - win_r)
        q_b = lax.dynamic_index_in_dim(qg, r, axis=1, keepdims=False).reshape(B, NA_NCB, NA_QC, H, DH)
        k_b = lax.dynamic_slice_in_dim(kg, rs, win_r, axis=1)[:, :, keycol]
        v_b = lax.dynamic_slice_in_dim(vg, rs, win_r, axis=1)[:, :, keycol]
        s_loc = jnp.einsum('bnqhd,brnkhd->bhnqrk', q_b, k_b).astype(jnp.float32)
        row_off = rs + jnp.arange(win_r) - r + NA_WIN_R - 1
        bias = rpb[:, row_off][:, :, col_idx].transpose(0, 2, 3, 1, 4).astype(jnp.float32)
        s_loc = jnp.where(col_valid[:, :, None, :], s_loc + bias, NEG_INF)
        s_ctx = jnp.einsum('bnqhd,bchd->bhnqc', q_b, kc).astype(jnp.float32)
        s = jnp.concatenate([s_loc.reshape(B, H, NA_NCB, NA_QC, n_loc), s_ctx], axis=-1)
        p = jax.nn.softmax(s, axis=-1).astype(v.dtype)
        p_loc = p[..., :n_loc].reshape(B, H, NA_NCB, NA_QC, win_r, NA_BAND)
        o = jnp.einsum('bhnqrk,brnkhd->bnqhd', p_loc, v_b) + jnp.einsum('bhnqc,bchd->bnqhd', p[..., n_loc:], vc)
        return o.reshape(B, GRID_W, H * DH)

    out = lax.map(row_block, jnp.arange(rows))
    return jnp.moveaxis(out, 0, 1).reshape(B, S, H * DH)


def ctx_attention(q, k, v):
    s = jnp.einsum('bqhd,bkhd->bhqk', q * q.shape[-1] ** -0.5, k).astype(jnp.float32)
    p = jax.nn.softmax(s, axis=-1).astype(v.dtype)
    return jnp.einsum('bhqk,bkhd->bqhd', p, v)


def na_mix(zx, zc, qn_g, kn_g, rpb, rows, need_ctx):
    heads = lambda t: t.reshape(t.shape[0], t.shape[1], NA_HEADS, NA_HEAD)
    qx, kx, vx = jnp.split(zx, 3, axis=-1)
    qc, kc, vc = jnp.split(zc, 3, axis=-1)
    kc = rms_norm(heads(kc), kn_g)
    vc = heads(vc)
    out_x = neighbourhood_attention(rms_norm(heads(qx), qn_g), rms_norm(heads(kx), kn_g), heads(vx), kc, vc, rpb, rows)
    out_c = None
    if need_ctx:
        out_c = ctx_attention(rms_norm(heads(qc), qn_g), kc, vc)
        out_c = out_c.reshape(out_c.shape[0], out_c.shape[1], BRANCH_WIDTH)
    return out_x, out_c


def axial_rope_tables(n_tokens, dtype):
    t = jnp.arange(n_tokens)
    pos = jnp.stack([t // GRID_W, t % GRID_W], axis=-1).astype(jnp.float32)
    inv = ROPE_BASE ** (-jnp.arange(ROPE_ROT, dtype=jnp.float32) / ROPE_ROT)
    ang = pos[:, :, None] * inv
    return jnp.cos(ang).astype(dtype), jnp.sin(ang).astype(dtype)


def apply_axial_rope(z, cos, sin):
    B, T, H, DK = z.shape
    zr = z.reshape(B, T, H, 2, 2, ROPE_ROT)
    z1, z2 = zr[..., 0, :], zr[..., 1, :]
    c, s = cos[None, :, None], sin[None, :, None]
    return jnp.stack([z1 * c - z2 * s, z2 * c + z1 * s], axis=-2).reshape(B, T, H, DK)


def mlstm_features(z, i_bias, f_bias, cos, sin):
    B, T, _ = z.shape
    q, k, v, o, ig, fg = jnp.split(z, MLSTM_SPLITS, axis=-1)
    q = q.reshape(B, T, MLSTM_HEADS, MLSTM_DQK)
    k = k.reshape(B, T, MLSTM_HEADS, MLSTM_DQK)
    if cos is not None:
        q = apply_axial_rope(q, cos, sin)
        k = apply_axial_rope(k, cos, sin)
    v = v.reshape(B, T, MLSTM_HEADS, MLSTM_DV)
    hf = lambda t: t.astype(jnp.float32).transpose(0, 2, 1, 3)
    gate = lambda g, bias: soft_cap(g.reshape(B, T, N_DIR, MLSTM_HEADS).astype(jnp.float32) + bias.astype(jnp.float32)).transpose(2, 0, 3, 1)
    return hf(q), hf(k), hf(v), o, gate(ig, i_bias), gate(fg, f_bias)


def mlstm_chunkwise(q, k, v, ig, fg, state):
    B, H, T, DK = q.shape
    DV = v.shape[-1]
    L = MLSTM_CHUNK
    NC = T // L
    q = q * DK ** -0.5
    split = lambda t: jnp.moveaxis(t.reshape(B, H, NC, L, *t.shape[3:]), 2, 0)
    causal = jnp.tril(jnp.ones((L, L), dtype=bool))

    def step(carry, inp):
        C, n, m = carry
        qc, kc, vc, ic, fc = inp
        b = jnp.cumsum(jax.nn.log_sigmoid(fc), axis=-1)
        d = jnp.where(causal, b[..., :, None] - b[..., None, :] + ic[..., None, :], NEG_INF)
        inter = b + m[..., None]
        m_t = jnp.maximum(inter, jnp.max(d, axis=-1))
        s = jnp.einsum('bhtk,bhsk->bhts', qc, kc) * jnp.exp(d - m_t[..., None])
        sc_inter = jnp.exp(inter - m_t)
        num = sc_inter[..., None] * jnp.einsum('bhtk,bhkv->bhtv', qc, C) + jnp.einsum('bhts,bhsv->bhtv', s, vc)
        den = sc_inter * jnp.einsum('bhtk,bhk->bht', qc, n) + jnp.sum(s, axis=-1)
        h = num / jnp.maximum(jnp.abs(den), jnp.exp(-m_t))[..., None]
        b_last = b[..., -1]
        wlog = b_last[..., None] - b + ic
        m_new = jnp.maximum(b_last + m, jnp.max(wlog, axis=-1))
        dec = jnp.exp(b_last + m - m_new)
        wexp = jnp.exp(wlog - m_new[..., None])
        C = dec[..., None, None] * C + jnp.einsum('bhs,bhsk,bhsv->bhkv', wexp, kc, vc)
        n = dec[..., None] * n + jnp.einsum('bhs,bhsk->bhk', wexp, kc)
        return (C, n, m_new), h

    state, h = lax.scan(step, state, (split(q), split(k), split(v), split(ig), split(fg)))
    return jnp.moveaxis(h, 0, 2).reshape(B, H, T, DV), state


def mlstm_readout(h, o, norm_g):
    B, H, T, DV = h.shape
    h = h.transpose(0, 2, 1, 3)
    h = h * lax.rsqrt(jnp.mean(h * h, axis=-1, keepdims=True) + RMS_EPS)
    h = h.reshape(B, T, H * DV).astype(o.dtype) * norm_g
    return h * jax.nn.sigmoid(o)


def mlstm_mix(zx, zc, i_bias, f_bias, norm_g, cos, sin, need_ctx):
    qx, kx, vx, ox, ix, fx = mlstm_features(zx, i_bias, f_bias, cos, sin)
    qc, kc, vc, oc, ic, fc = mlstm_features(zc, i_bias, f_bias, None, None)
    B = zx.shape[0]
    h_x = 0.0
    h_c = 0.0
    for d in range(N_DIR):
        fl = (lambda t: jnp.flip(t, axis=2)) if d == 1 else (lambda t: t)
        st0 = (jnp.zeros((B, MLSTM_HEADS, MLSTM_DQK, MLSTM_DV), jnp.float32),
               jnp.zeros((B, MLSTM_HEADS, MLSTM_DQK), jnp.float32),
               jnp.zeros((B, MLSTM_HEADS), jnp.float32))
        hc_d, st_ctx = mlstm_chunkwise(fl(qc), fl(kc), fl(vc), fl(ic[d]), fl(fc[d]), st0)
        hx_d, _ = mlstm_chunkwise(fl(qx), fl(kx), fl(vx), fl(ix[d]), fl(fx[d]), st_ctx)
        h_x = h_x + fl(hx_d)
        h_c = h_c + fl(hc_d)
    out_x = mlstm_readout(h_x, ox, norm_g)
    out_c = mlstm_readout(h_c, oc, norm_g) if need_ctx else None
    return out_x, out_c


def merge_branches(ya, yb, yc, gate_cols, w_branch, w_out):
    ga, gb, gc = jnp.split(jax.nn.sigmoid(gate_cols), N_BRANCH, axis=-1)
    merged = ga * (ya @ w_branch[0]) + gb * (yb @ w_branch[1]) + gc * (yc @ w_branch[2])
    return merged @ w_out


def moe_ffn(h, w_router, router_bias, w_gate, w_up, w_down, s_gate, s_up, s_down):
    def per_sample(hs):
        T = hs.shape[0]
        scores = jax.nn.sigmoid((hs @ w_router).astype(jnp.float32))
        sel = scores + router_bias.astype(jnp.float32)
        grp = sel.reshape(T, N_GROUPS, N_EXPERTS // N_GROUPS)
        gscore = jnp.sum(lax.top_k(grp, 2)[0], axis=-1)
        gidx = lax.top_k(gscore, TOPK_GROUPS)[1]
        gmask = jnp.sum(jax.nn.one_hot(gidx, N_GROUPS, dtype=jnp.float32), axis=-2) > 0
        emask = jnp.repeat(gmask, N_EXPERTS // N_GROUPS, axis=-1)
        eidx = lax.top_k(jnp.where(emask, sel, NEG_INF), TOP_K)[1]
        w = jnp.take_along_axis(scores, eidx, axis=-1)
        w = w / jnp.sum(w, axis=-1, keepdims=True) * ROUTED_SCALE
        gates = jnp.sum(jax.nn.one_hot(eidx, N_EXPERTS, dtype=jnp.float32) * w[..., None], axis=-2).astype(hs.dtype)
        hg = jnp.einsum('td,edf->tef', hs, w_gate)
        hu = jnp.einsum('td,edf->tef', hs, w_up)
        routed = jnp.einsum('tef,efd->td', jax.nn.silu(hg) * hu * gates[:, :, None], w_down)
        shared = (jax.nn.silu(hs @ s_gate) * (hs @ s_up)) @ s_down
        return routed + shared
    return lax.map(per_sample, h)


def setup_inputs(seed: int = 0) -> dict:
    key = jax.random.key(seed)
    keys = iter(jax.random.split(key, 64))

    def nrm(shape, scale, shift=0.0):
        return shift + scale * jax.random.normal(next(keys), shape, jnp.float32)

    L, D, C = DEPTH, D_MODEL, BRANCH_WIDTH
    return {
        'x': nrm((BATCH, SEQ, D), 1.0),
        'c': nrm((BATCH, D), 1.0),
        'ctx': nrm((BATCH, CTX_LEN, D), 1.0),
        'c_ctx': nrm((D,), 1.0),
        'w_ada': nrm((L, D, 6 * D), 0.5 * D ** -0.5),
        'b_ada': nrm((L, 6 * D), 0.02),
        'norm1_g': nrm((L, D), 0.1, 1.0),
        'norm2_g': nrm((L, D), 0.1, 1.0),
        'w_in': nrm((L, D, IN_COLS), D ** -0.5),
        'rw_mu': jax.random.uniform(next(keys), (L, RWKV_COLS), jnp.float32),
        'rw_w0': nrm((L, N_DIR, C), 0.5, -1.0),
        'rw_w2': nrm((L, N_DIR, RWKV_DECAY_LORA, C), 0.1),
        'rw_a0': nrm((L, N_DIR, C), 0.5),
        'rw_a2': nrm((L, N_DIR, RWKV_A_LORA, C), 0.1),
        'rw_k_k': nrm((L, C), 0.1, 1.0),
        'rw_k_a': nrm((L, C), 0.1, 1.0),
        'rw_r_k': nrm((L, RWKV_HEADS, RWKV_HEAD), 0.1),
        'rw_g2': nrm((L, RWKV_G_LORA, C), RWKV_G_LORA ** -0.5),
        'rw_lnx_g': nrm((L, C), 0.1, 1.0),
        'rw_lnx_b': nrm((L, C), 0.02),
        'rw_v0': nrm((L - 1, C), 0.5),
        'rw_v1': nrm((L - 1, C, RWKV_V_LORA), C ** -0.5),
        'rw_v2': nrm((L - 1, RWKV_V_LORA, C), 0.1),
        'na_qn_g': nrm((L, NA_HEAD), 0.1, 1.0),
        'na_kn_g': nrm((L, NA_HEAD), 0.1, 1.0),
        'na_rpb': nrm((L, NA_HEADS, 2 * NA_WIN_R - 1, 2 * NA_WIN_C - 1), 0.5),
        'ml_i_bias': nrm((L, N_DIR, MLSTM_HEADS), 0.5, -1.0),
        'ml_f_bias': nrm((L, N_DIR, MLSTM_HEADS), 0.5, 3.0),
        'ml_norm_g': nrm((L, C), 0.1, 1.0),
        'w_branch': nrm((L, N_BRANCH, C, D), C ** -0.5),
        'w_out': nrm((L, D, D), D ** -0.5),
        'moe_router': nrm((L, D, N_EXPERTS), D ** -0.5),
        'moe_bias': nrm((L, N_EXPERTS), 0.01),
        'moe_w_gate': nrm((L, N_EXPERTS, D, D_EXPERT), D ** -0.5),
        'moe_w_up': nrm((L, N_EXPERTS, D, D_EXPERT), D ** -0.5),
        'moe_w_down': nrm((L, N_EXPERTS, D_EXPERT, D), D_EXPERT ** -0.5),
        'sh_w_gate': nrm((L, D, D_SHARED), D ** -0.5),
        'sh_w_up': nrm((L, D, D_SHARED), D ** -0.5),
        'sh_w_down': nrm((L, D_SHARED, D), D_SHARED ** -0.5),
    }


def reference(x, c, ctx, c_ctx, w_ada, b_ada, norm1_g, norm2_g, w_in, rw_mu, rw_w0, rw_w2, rw_a0, rw_a2,
              rw_k_k, rw_k_a, rw_r_k, rw_g2, rw_lnx_g, rw_lnx_b, rw_v0, rw_v1, rw_v2, na_qn_g, na_kn_g, na_rpb,
              ml_i_bias, ml_f_bias, ml_norm_g, w_branch, w_out, moe_router, moe_bias, moe_w_gate, moe_w_up,
              moe_w_down, sh_w_gate, sh_w_up, sh_w_down):
    B, S, D = x.shape
    rows = S // GRID_W
    cos, sin = axial_rope_tables(S, x.dtype)
    s_c = jax.nn.silu(c)
    s_cctx = jax.nn.silu(c_ctx)
    vf_x = None
    vf_c = None
    for l in range(DEPTH):
        need_ctx = l < DEPTH - 1
        mod_x = (s_c @ w_ada[l] + b_ada[l])[:, None, :]
        mod_c = s_cctx @ w_ada[l] + b_ada[l]
        shx1, scx1, gx1, shx2, scx2, gx2 = jnp.split(mod_x, 6, axis=-1)
        shc1, scc1, gc1, shc2, scc2, gc2 = jnp.split(mod_c, 6, axis=-1)
        hx = rms_norm(x, norm1_g[l]) * (1 + scx1) + shx1
        hc = rms_norm(ctx, norm1_g[l]) * (1 + scc1) + shc1
        px = hx @ w_in[l]
        pc = hc @ w_in[l]
        ax, bx, cx, gx = jnp.split(px, IN_SPLITS, axis=-1)
        ac, bc, cc, gcol = jnp.split(pc, IN_SPLITS, axis=-1)
        vres = None if l == 0 else (rw_v0[l - 1], rw_v1[l - 1], rw_v2[l - 1])
        ya_x, ya_c, vf_x, vf_c = rwkv_mix(ax, ac, vf_x, vf_c, rw_mu[l], rw_w0[l], rw_w2[l], rw_a0[l], rw_a2[l],
                                          rw_k_k[l], rw_k_a[l], rw_r_k[l], rw_g2[l], rw_lnx_g[l], rw_lnx_b[l],
                                          vres, need_ctx)
        yb_x, yb_c = na_mix(bx, bc, na_qn_g[l], na_kn_g[l], na_rpb[l], rows, need_ctx)
        yc_x, yc_c = mlstm_mix(cx, cc, ml_i_bias[l], ml_f_bias[l], ml_norm_g[l], cos, sin, need_ctx)
        x = x + gx1 * merge_branches(ya_x, yb_x, yc_x, gx, w_branch[l], w_out[l])
        h2 = rms_norm(x, norm2_g[l]) * (1 + scx2) + shx2
        x = x + gx2 * moe_ffn(h2, moe_router[l], moe_bias[l], moe_w_gate[l], moe_w_up[l], moe_w_down[l],
                              sh_w_gate[l], sh_w_up[l], sh_w_down[l])
        if need_ctx:
            ctx = ctx + gc1 * merge_branches(ya_c, yb_c, yc_c, gcol, w_branch[l], w_out[l])
            hc2 = rms_norm(ctx, norm2_g[l]) * (1 + scc2) + shc2
            ctx = ctx + gc2 * moe_ffn(hc2, moe_router[l], moe_bias[l], moe_w_gate[l], moe_w_up[l], moe_w_down[l],
                                      sh_w_gate[l], sh_w_up[l], sh_w_down[l])
    return x
```

```python
import functools

import numpy as np
import jax
import jax.numpy as jnp
from jax import lax
from jax.experimental import pallas as pl
from jax.experimental.pallas import tpu as pltpu

D_MODEL = 1024
DEPTH = 2
CTX_LEN = 256
GRID_W = 64
N_DIR = 2
N_BRANCH = 3
BRANCH_WIDTH = 512
RMS_EPS = 1e-6
NEG_INF = -1e30

RWKV_HEAD = 64
RWKV_HEADS = BRANCH_WIDTH // RWKV_HEAD
RWKV_DECAY_LORA = 64
RWKV_A_LORA = 64
RWKV_V_LORA = 32
RWKV_G_LORA = 128
RWKV_LNX_EPS = 64e-5

NA_HEAD = 64
NA_HEADS = BRANCH_WIDTH // NA_HEAD
NA_WIN_R = 8
NA_WIN_C = 16
NA_QC = 16
NA_BAND = NA_QC + NA_WIN_C
NA_NCB = GRID_W // NA_QC

MLSTM_HEADS = 4
MLSTM_DQK = 64
MLSTM_DV = BRANCH_WIDTH // MLSTM_HEADS
MLSTM_CHUNK = 64
GATE_CAP = 15.0
ROPE_ROT = MLSTM_DQK // 4
ROPE_BASE = 10000.0

N_EXPERTS = 64
TOP_K = 8
N_GROUPS = 8
TOPK_GROUPS = 4
D_EXPERT = 128
D_SHARED = 256
ROUTED_SCALE = 2.5

RWKV_COLS = 3 * BRANCH_WIDTH + N_DIR * RWKV_DECAY_LORA + N_DIR * RWKV_A_LORA + RWKV_G_LORA
NA_COLS = 3 * BRANCH_WIDTH
MLSTM_QK = MLSTM_HEADS * MLSTM_DQK
MLSTM_COLS = 2 * MLSTM_QK + 2 * BRANCH_WIDTH + 2 * N_DIR * MLSTM_HEADS
GATE_COLS = N_BRANCH * D_MODEL
IN_COLS = RWKV_COLS + NA_COLS + MLSTM_COLS + GATE_COLS
IN_SPLITS = [RWKV_COLS, RWKV_COLS + NA_COLS, RWKV_COLS + NA_COLS + MLSTM_COLS]
RWKV_SPLITS = [BRANCH_WIDTH, 2 * BRANCH_WIDTH, 3 * BRANCH_WIDTH, 3 * BRANCH_WIDTH + N_DIR * RWKV_DECAY_LORA, 3 * BRANCH_WIDTH + N_DIR * (RWKV_DECAY_LORA + RWKV_A_LORA)]
MLSTM_SPLITS = [MLSTM_QK, 2 * MLSTM_QK, 2 * MLSTM_QK + BRANCH_WIDTH, 2 * MLSTM_QK + 2 * BRANCH_WIDTH, 2 * MLSTM_QK + 2 * BRANCH_WIDTH + N_DIR * MLSTM_HEADS]

V7X_LANES = 128
VMEM_LIMIT_BYTES = 48 * 1024 * 1024


def _mm_kernel(x_ref, w_ref, o_ref):
    part = jnp.dot(x_ref[...].astype(jnp.bfloat16), w_ref[...].astype(jnp.bfloat16),
                   preferred_element_type=jnp.float32)

    @pl.when(pl.program_id(2) == 0)
    def _():
        o_ref[...] = part

    @pl.when(pl.program_id(2) > 0)
    def _():
        o_ref[...] += part


def _pick_tile(n, cands):
    for c in cands:
        if n % c == 0:
            return c
    return n


def pmm(x, w):
    M, K = x.shape
    N = w.shape[1]
    n_pad = (-N) % V7X_LANES
    if n_pad:
        w = jnp.pad(w, ((0, 0), (0, n_pad)))
    m_pad = (-M) % 8
    if m_pad:
        x = jnp.pad(x, ((0, m_pad), (0, 0)))
    Mp, Np = M + m_pad, N + n_pad
    tm = _pick_tile(Mp, (512, 256, 128, 64, 32, 16, 8))
    tn = _pick_tile(Np, (512, 384, 256, 128))
    tk = _pick_tile(K, (1024,)) if K > 1024 else K
    out = pl.pallas_call(
        _mm_kernel,
        out_shape=jax.ShapeDtypeStruct((Mp, Np), jnp.float32),
        grid=(Mp // tm, Np // tn, K // tk),
        in_specs=[pl.BlockSpec((tm, tk), lambda i, j, k: (i, k)),
                  pl.BlockSpec((tk, tn), lambda i, j, k: (k, j))],
        out_specs=pl.BlockSpec((tm, tn), lambda i, j, k: (i, j)),
        compiler_params=pltpu.CompilerParams(dimension_semantics=("parallel", "parallel", "arbitrary"),
                                             vmem_limit_bytes=VMEM_LIMIT_BYTES),
    )(x, w)
    return out[:M, :N]


def mm(x, w):
    lead = x.shape[:-1]
    return pmm(x.reshape(-1, x.shape[-1]), w).reshape(*lead, w.shape[-1])


def rms_norm(z, g):
    zf = z * lax.rsqrt(jnp.mean(z * z, axis=-1, keepdims=True) + RMS_EPS)
    return zf * g


def soft_cap(z):
    return GATE_CAP * jnp.tanh(z / GATE_CAP)


def token_shift(z, mu):
    zp = jnp.pad(z, ((0, 0), (1, 1), (0, 0)))
    return z + mu * (0.5 * (zp[:, :-2] + zp[:, 2:]) - z)


def rwkv_features(z, v_first, mu, w0, w2, a0, a2, k_k, k_a, g2, vres):
    B, T, _ = z.shape
    z = token_shift(z, mu)
    r, k, v, wd, ad, gd = jnp.split(z, RWKV_SPLITS, axis=-1)
    wd = wd.reshape(B, T, N_DIR, RWKV_DECAY_LORA)
    ad = ad.reshape(B, T, N_DIR, RWKV_A_LORA)
    w = -jax.nn.softplus(-(w0 + jnp.einsum('btdl,dlc->btdc', jnp.tanh(wd), w2))) - 0.5
    decay = jnp.exp(-jnp.exp(w))
    a = jax.nn.sigmoid(a0 + jnp.einsum('btdl,dlc->btdc', ad, a2))
    kk = (k * k_k).reshape(B, T, RWKV_HEADS, RWKV_HEAD)
    kk = kk / jnp.maximum(jnp.sqrt(jnp.sum(kk * kk, axis=-1, keepdims=True)), 1e-12)
    k_dir = k[:, :, None, :] * (1 + (a - 1) * k_a)
    if vres is None:
        v_first = v
    else:
        v0, v1, v2 = vres
        v = v + (v_first - v) * jax.nn.sigmoid(v0 + (v @ v1) @ v2)
    g = jax.nn.sigmoid(gd) @ g2
    return (r, k_dir, v, decay, a, kk, g), v_first


def rwkv_scan_inputs(feats, d):
    r, k_dir, v, decay, a, kk, g = feats
    B, T, _ = r.shape
    heads = lambda t: t.reshape(B, T, RWKV_HEADS, RWKV_HEAD)
    return heads(r), heads(decay[:, :, d]), -kk, kk * heads(a[:, :, d]), heads(k_dir[:, :, d]), heads(v)


def wkv_scan(state, r, w, a, b, k, v, reverse):
    def step(S, inp):
        r_t, w_t, a_t, b_t, k_t, v_t = inp
        sa = jnp.einsum('bhvk,bhk->bhv', S, a_t)
        S = S * w_t[:, :, None, :] + sa[..., None] * b_t[:, :, None, :] + v_t[..., None] * k_t[:, :, None, :]
        return S, jnp.einsum('bhvk,bhk->bhv', S, r_t)
    xs = tuple(jnp.moveaxis(t, 1, 0) for t in (r, w, a, b, k, v))
    state, y = lax.scan(step, state, xs, reverse=reverse)
    return state, jnp.moveaxis(y, 0, 1)


def rwkv_readout(y, feats, r_k, lnx_g, lnx_b):
    r, k_dir, v, decay, a, kk, g = feats
    B, T, C = r.shape
    mean = jnp.mean(y, axis=-1, keepdims=True)
    var = jnp.mean(jnp.square(y - mean), axis=-1, keepdims=True)
    yn = ((y - mean) * lax.rsqrt(var + RWKV_LNX_EPS)).reshape(B, T, C) * lnx_g + lnx_b
    rh = r.reshape(B, T, RWKV_HEADS, RWKV_HEAD)
    kh = k_dir.reshape(B, T, N_DIR, RWKV_HEADS, RWKV_HEAD)
    bonus = jnp.einsum('bthn,btdhn,hn->bth', rh, kh, r_k)[..., None] * v.reshape(B, T, RWKV_HEADS, RWKV_HEAD)
    return (yn + bonus.reshape(B, T, C)) * g


def rwkv_mix(zx, zc, vf_x, vf_c, mu, w0, w2, a0, a2, k_k, k_a, r_k, g2, lnx_g, lnx_b, vres, need_ctx):
    fx, vf_x = rwkv_features(zx, vf_x, mu, w0, w2, a0, a2, k_k, k_a, g2, vres)
    fc, vf_c = rwkv_features(zc, vf_c, mu, w0, w2, a0, a2, k_k, k_a, g2, vres)
    B = zx.shape[0]
    y_x = 0.0
    y_c = 0.0
    for d in range(N_DIR):
        rev = d == 1
        s0 = jnp.zeros((B, RWKV_HEADS, RWKV_HEAD, RWKV_HEAD), jnp.float32)
        s_ctx, yc = wkv_scan(s0, *rwkv_scan_inputs(fc, d), reverse=rev)
        _, yx = wkv_scan(s_ctx, *rwkv_scan_inputs(fx, d), reverse=rev)
        y_x = y_x + yx
        y_c = y_c + yc
    out_x = rwkv_readout(y_x, fx, r_k, lnx_g, lnx_b)
    out_c = rwkv_readout(y_c, fc, r_k, lnx_g, lnx_b) if need_ctx else None
    return out_x, out_c, vf_x, vf_c


def neighbourhood_attention(q, k, v, kc, vc, rpb, rows):
    B, S, H, DH = q.shape
    win_r = min(NA_WIN_R, rows)
    qcol = np.arange(GRID_W).reshape(NA_NCB, NA_QC)
    band0 = np.clip(np.arange(NA_NCB) * NA_QC - NA_WIN_C // 2, 0, GRID_W - NA_BAND)
    keycol = band0[:, None] + np.arange(NA_BAND)[None, :]
    cs = np.clip(qcol - NA_WIN_C // 2, 0, GRID_W - NA_WIN_C)
    kcol = keycol[:, None, :]
    col_valid = (kcol >= cs[..., None]) & (kcol < cs[..., None] + NA_WIN_C)
    col_idx = np.clip(kcol - qcol[..., None] + NA_WIN_C - 1, 0, 2 * NA_WIN_C - 2)
    qg = (q * DH ** -0.5).reshape(B, rows, GRID_W, H, DH)
    kg = k.reshape(B, rows, GRID_W, H, DH)
    vg = v.reshape(B, rows, GRID_W, H, DH)
    n_loc = win_r * NA_BAND

    def row_block(r):
        rs = jnp.clip(r - win_r // 2, 0, rows - win_r)
        q_b = lax.dynamic_index_in_dim(qg, r, axis=1, keepdims=False).reshape(B, NA_NCB, NA_QC, H, DH)
        k_b = lax.dynamic_slice_in_dim(kg, rs, win_r, axis=1)[:, :, keycol]
        v_b = lax.dynamic_slice_in_dim(vg, rs, win_r, axis=1)[:, :, keycol]
        s_loc = jnp.einsum('bnqhd,brnkhd->bhnqrk', q_b, k_b)
        row_off = rs + jnp.arange(win_r) - r + NA_WIN_R - 1
        bias = rpb[:, row_off][:, :, col_idx].transpose(0, 2, 3, 1, 4)
        s_loc = jnp.where(col_valid[:, :, None, :], s_loc + bias, NEG_INF)
        s_ctx = jnp.einsum('bnqhd,bchd->bhnqc', q_b, kc)
        s = jnp.concatenate([s_loc.reshape(B, H, NA_NCB, NA_QC, n_loc), s_ctx], axis=-1)
        p = jax.nn.softmax(s, axis=-1)
        p_loc = p[..., :n_loc].reshape(B, H, NA_NCB, NA_QC, win_r, NA_BAND)
        o = jnp.einsum('bhnqrk,brnkhd->bnqhd', p_loc, v_b) + jnp.einsum('bhnqc,bchd->bnqhd', p[..., n_loc:], vc)
        return o.reshape(B, GRID_W, H * DH)

    out = lax.map(row_block, jnp.arange(rows))
    return jnp.moveaxis(out, 0, 1).reshape(B, S, H * DH)


def ctx_attention(q, k, v):
    s = jnp.einsum('bqhd,bkhd->bhqk', q * q.shape[-1] ** -0.5, k)
    p = jax.nn.softmax(s, axis=-1)
    return jnp.einsum('bhqk,bkhd->bqhd', p, v)


def na_mix(zx, zc, qn_g, kn_g, rpb, rows, need_ctx):
    heads = lambda t: t.reshape(t.shape[0], t.shape[1], NA_HEADS, NA_HEAD)
    qx, kx, vx = jnp.split(zx, 3, axis=-1)
    qc, kc, vc = jnp.split(zc, 3, axis=-1)
    kc = rms_norm(heads(kc), kn_g)
    vc = heads(vc)
    out_x = neighbourhood_attention(rms_norm(heads(qx), qn_g), rms_norm(heads(kx), kn_g), heads(vx), kc, vc, rpb, rows)
    out_c = None
    if need_ctx:
        out_c = ctx_attention(rms_norm(heads(qc), qn_g), kc, vc)
        out_c = out_c.reshape(out_c.shape[0], out_c.shape[1], BRANCH_WIDTH)
    return out_x, out_c


def axial_rope_tables(n_tokens, dtype):
    t = jnp.arange(n_tokens)
    pos = jnp.stack([t // GRID_W, t % GRID_W], axis=-1).astype(jnp.float32)
    inv = ROPE_BASE ** (-jnp.arange(ROPE_ROT, dtype=jnp.float32) / ROPE_ROT)
    ang = pos[:, :, None] * inv
    return jnp.cos(ang).astype(dtype), jnp.sin(ang).astype(dtype)


def apply_axial_rope(z, cos, sin):
    B, T, H, DK = z.shape
    zr = z.reshape(B, T, H, 2, 2, ROPE_ROT)
    z1, z2 = zr[..., 0, :], zr[..., 1, :]
    c, s = cos[None, :, None], sin[None, :, None]
    return jnp.stack([z1 * c - z2 * s, z2 * c + z1 * s], axis=-2).reshape(B, T, H, DK)


def mlstm_features(z, i_bias, f_bias, cos, sin):
    B, T, _ = z.shape
    q, k, v, o, ig, fg = jnp.split(z, MLSTM_SPLITS, axis=-1)
    q = q.reshape(B, T, MLSTM_HEADS, MLSTM_DQK)
    k = k.reshape(B, T, MLSTM_HEADS, MLSTM_DQK)
    if cos is not None:
        q = apply_axial_rope(q, cos, sin)
        k = apply_axial_rope(k, cos, sin)
    v = v.reshape(B, T, MLSTM_HEADS, MLSTM_DV)
    hf = lambda t: t.transpose(0, 2, 1, 3)
    gate = lambda g, bias: soft_cap(g.reshape(B, T, N_DIR, MLSTM_HEADS) + bias).transpose(2, 0, 3, 1)
    return hf(q), hf(k), hf(v), o, gate(ig, i_bias), gate(fg, f_bias)


def mlstm_chunkwise(q, k, v, ig, fg, state):
    B, H, T, DK = q.shape
    DV = v.shape[-1]
    L = MLSTM_CHUNK
    NC = T // L
    q = q * DK ** -0.5
    split = lambda t: jnp.moveaxis(t.reshape(B, H, NC, L, *t.shape[3:]), 2, 0)
    causal = jnp.tril(jnp.ones((L, L), dtype=bool))

    def step(carry, inp):
        C, n, m = carry
        qc, kc, vc, ic, fc = inp
        b = jnp.cumsum(jax.nn.log_sigmoid(fc), axis=-1)
        d = jnp.where(causal, b[..., :, None] - b[..., None, :] + ic[..., None, :], NEG_INF)
        inter = b + m[..., None]
        m_t = jnp.maximum(inter, jnp.max(d, axis=-1))
        s = jnp.einsum('bhtk,bhsk->bhts', qc, kc) * jnp.exp(d - m_t[..., None])
        sc_inter = jnp.exp(inter - m_t)
        num = sc_inter[..., None] * jnp.einsum('bhtk,bhkv->bhtv', qc, C) + jnp.einsum('bhts,bhsv->bhtv', s, vc)
        den = sc_inter * jnp.einsum('bhtk,bhk->bht', qc, n) + jnp.sum(s, axis=-1)
        h = num / jnp.maximum(jnp.abs(den), jnp.exp(-m_t))[..., None]
        b_last = b[..., -1]
        wlog = b_last[..., None] - b + ic
        m_new = jnp.maximum(b_last + m, jnp.max(wlog, axis=-1))
        dec = jnp.exp(b_last + m - m_new)
        wexp = jnp.exp(wlog - m_new[..., None])
        C = dec[..., None, None] * C + jnp.einsum('bhs,bhsk,bhsv->bhkv', wexp, kc, vc)
        n = dec[..., None] * n + jnp.einsum('bhs,bhsk->bhk', wexp, kc)
        return (C, n, m_new), h

    state, h = lax.scan(step, state, (split(q), split(k), split(v), split(ig), split(fg)))
    return jnp.moveaxis(h, 0, 2).reshape(B, H, T, DV), state


def mlstm_readout(h, o, norm_g):
    B, H, T, DV = h.shape
    h = h.transpose(0, 2, 1, 3)
    h = h * lax.rsqrt(jnp.mean(h * h, axis=-1, keepdims=True) + RMS_EPS)
    h = h.reshape(B, T, H * DV) * norm_g
    return h * jax.nn.sigmoid(o)


def mlstm_mix(zx, zc, i_bias, f_bias, norm_g, cos, sin, need_ctx):
    qx, kx, vx, ox, ix, fx = mlstm_features(zx, i_bias, f_bias, cos, sin)
    qc, kc, vc, oc, ic, fc = mlstm_features(zc, i_bias, f_bias, None, None)
    B = zx.shape[0]
    h_x = 0.0
    h_c = 0.0
    for d in range(N_DIR):
        fl = (lambda t: jnp.flip(t, axis=2)) if d == 1 else (lambda t: t)
        st0 = (jnp.zeros((B, MLSTM_HEADS, MLSTM_DQK, MLSTM_DV), jnp.float32),
               jnp.zeros((B, MLSTM_HEADS, MLSTM_DQK), jnp.float32),
               jnp.zeros((B, MLSTM_HEADS), jnp.float32))
        hc_d, st_ctx = mlstm_chunkwise(fl(qc), fl(kc), fl(vc), fl(ic[d]), fl(fc[d]), st0)
        hx_d, _ = mlstm_chunkwise(fl(qx), fl(kx), fl(vx), fl(ix[d]), fl(fx[d]), st_ctx)
        h_x = h_x + fl(hx_d)
        h_c = h_c + fl(hc_d)
    out_x = mlstm_readout(h_x, ox, norm_g)
    out_c = mlstm_readout(h_c, oc, norm_g) if need_ctx else None
    return out_x, out_c


def merge_branches(ya, yb, yc, gate_cols, w_branch, w_out):
    ga, gb, gc = jnp.split(jax.nn.sigmoid(gate_cols), N_BRANCH, axis=-1)
    merged = ga * mm(ya, w_branch[0]) + gb * mm(yb, w_branch[1]) + gc * mm(yc, w_branch[2])
    return mm(merged, w_out)


def moe_ffn(h, w_router, router_bias, w_gate, w_up, w_down, s_gate, s_up, s_down):
    lead = h.shape[:-1]
    hs = h.reshape(-1, D_MODEL)
    T = hs.shape[0]
    scores = jax.nn.sigmoid(mm(hs, w_router))
    sel = scores + router_bias
    grp = sel.reshape(T, N_GROUPS, N_EXPERTS // N_GROUPS)
    gscore = jnp.sum(lax.top_k(grp, 2)[0], axis=-1)
    gidx = lax.top_k(gscore, TOPK_GROUPS)[1]
    gmask = jnp.sum(jax.nn.one_hot(gidx, N_GROUPS, dtype=jnp.float32), axis=-2) > 0
    emask = jnp.repeat(gmask, N_EXPERTS // N_GROUPS, axis=-1)
    eidx = lax.top_k(jnp.where(emask, sel, NEG_INF), TOP_K)[1]
    w = jnp.take_along_axis(scores, eidx, axis=-1)
    w = w / jnp.sum(w, axis=-1, keepdims=True) * ROUTED_SCALE
    gates = jnp.sum(jax.nn.one_hot(eidx, N_EXPERTS, dtype=jnp.float32) * w[..., None], axis=-2)
    wg = w_gate.transpose(1, 0, 2).reshape(D_MODEL, N_EXPERTS * D_EXPERT)
    wu = w_up.transpose(1, 0, 2).reshape(D_MODEL, N_EXPERTS * D_EXPERT)
    hg = mm(hs, wg)
    hu = mm(hs, wu)
    act = (jax.nn.silu(hg) * hu).reshape(T, N_EXPERTS, D_EXPERT) * gates[:, :, None]
    routed = mm(act.reshape(T, N_EXPERTS * D_EXPERT), w_down.reshape(N_EXPERTS * D_EXPERT, D_MODEL))
    shared = mm(jax.nn.silu(mm(hs, s_gate)) * mm(hs, s_up), s_down)
    return (routed + shared).reshape(*lead, D_MODEL)


def kernel(x, c, ctx, c_ctx, w_ada, b_ada, norm1_g, norm2_g, w_in, rw_mu, rw_w0, rw_w2, rw_a0, rw_a2, rw_k_k, rw_k_a, rw_r_k, rw_g2, rw_lnx_g, rw_lnx_b, rw_v0, rw_v1, rw_v2, na_qn_g, na_kn_g, na_rpb, ml_i_bias, ml_f_bias, ml_norm_g, w_branch, w_out, moe_router, moe_bias, moe_w_gate, moe_w_up, moe_w_down, sh_w_gate, sh_w_up, sh_w_down):
    B, S, D = x.shape
    rows = S // GRID_W
    cos, sin = axial_rope_tables(S, x.dtype)
    s_c = jax.nn.silu(c)
    s_cctx = jax.nn.silu(c_ctx)
    vf_x = None
    vf_c = None
    for l in range(DEPTH):
        need_ctx = l < DEPTH - 1
        mod_x = (s_c @ w_ada[l] + b_ada[l])[:, None, :]
        mod_c = s_cctx @ w_ada[l] + b_ada[l]
        shx1, scx1, gx1, shx2, scx2, gx2 = jnp.split(mod_x, 6, axis=-1)
        shc1, scc1, gc1, shc2, scc2, gc2 = jnp.split(mod_c, 6, axis=-1)
        hx = rms_norm(x, norm1_g[l]) * (1 + scx1) + shx1
        hc = rms_norm(ctx, norm1_g[l]) * (1 + scc1) + shc1
        px = mm(hx, w_in[l])
        pc = mm(hc, w_in[l])
        ax, bx, cx, gx = jnp.split(px, IN_SPLITS, axis=-1)
        ac, bc, cc, gcol = jnp.split(pc, IN_SPLITS, axis=-1)
        vres = None if l == 0 else (rw_v0[l - 1], rw_v1[l - 1], rw_v2[l - 1])
        ya_x, ya_c, vf_x, vf_c = rwkv_mix(ax, ac, vf_x, vf_c, rw_mu[l], rw_w0[l], rw_w2[l], rw_a0[l], rw_a2[l],
                                          rw_k_k[l], rw_k_a[l], rw_r_k[l], rw_g2[l], rw_lnx_g[l], rw_lnx_b[l],
                                          vres, need_ctx)
        yb_x, yb_c = na_mix(bx, bc, na_qn_g[l], na_kn_g[l], na_rpb[l], rows, need_ctx)
        yc_x, yc_c = mlstm_mix(cx, cc, ml_i_bias[l], ml_f_bias[l], ml_norm_g[l], cos, sin, need_ctx)
        x = x + gx1 * merge_branches(ya_x, yb_x, yc_x, gx, w_branch[l], w_out[l])
        h2 = rms_norm(x, norm2_g[l]) * (1 + scx2) + shx2
        x = x + gx2 * moe_ffn(h2, moe_router[l], moe_bias[l], moe_w_gate[l], moe_w_up[l], moe_w_down[l],
                              sh_w_gate[l], sh_w_up[l], sh_w_down[l])
        if need_ctx:
            ctx = ctx + gc1 * merge_branches(ya_c, yb_c, yc_c, gcol, w_branch[l], w_out[l])
            hc2 = rms_norm(ctx, norm2_g[l]) * (1 + scc2) + shc2
            ctx = ctx + gc2 * moe_ffn(hc2, moe_router[l], moe_bias[l], moe_w_gate[l], moe_w_up[l], moe_w_down[l],
                                      sh_w_gate[l], sh_w_up[l], sh_w_down[l])
    return x
```

```python
import functools

import numpy as np
import jax
import jax.numpy as jnp
from jax import lax
from jax.experimental import pallas as pl
from jax.experimental.pallas import tpu as pltpu

D_MODEL = 1024
DEPTH = 2
GRID_W = 64
N_DIR = 2
N_BRANCH = 3
BRANCH_WIDTH = 512
RMS_EPS = 1e-6
NEG_INF = -1e30

RWKV_HEAD = 64
RWKV_DECAY_LORA = 64
RWKV_LORA_COLS = 384
RWKV_COLS = 3 * BRANCH_WIDTH + RWKV_LORA_COLS
RWKV_LNX_EPS = 64e-5
RWKV_CHUNK = 64

NA_HEAD = 64
NA_HEADS = BRANCH_WIDTH // NA_HEAD
NA_WIN_R = 8
NA_WIN_C = 16
NA_COLS = 3 * BRANCH_WIDTH

MLSTM_HEADS = 4
MLSTM_DQK = 64
MLSTM_DV = BRANCH_WIDTH // MLSTM_HEADS
MLSTM_QK = MLSTM_HEADS * MLSTM_DQK
MLSTM_CHUNK = 64
MLSTM_MAIN_COLS = 2 * MLSTM_QK + 2 * BRANCH_WIDTH
MLSTM_GATES = 2 * N_DIR * MLSTM_HEADS
GATE_CAP = 15.0
ROPE_ROT = MLSTM_DQK // 4
ROPE_BASE = 10000.0

N_EXPERTS = 64
TOP_K = 8
N_GROUPS = 8
TOPK_GROUPS = 4
D_EXPERT = 128
ROUTED_SCALE = 2.5
MOE_STEP_EXPERTS = 8

GATE_COLS = N_BRANCH * D_MODEL
IN_COLS = RWKV_COLS + NA_COLS + MLSTM_MAIN_COLS + MLSTM_GATES + GATE_COLS

V7X_LANES = 128
V7X_SUBLANES = 8
HP = 2 * RWKV_HEAD
VMEM_LIMIT_BYTES = 48 * 1024 * 1024

PROJ_COLS = 8192
NA_BLK = 0
ML_BLK = 1
ML_OGATE_BLK = (NA_COLS + 2 * MLSTM_QK + BRANCH_WIDTH) // BRANCH_WIDTH
GATE_BLK0 = (NA_COLS + MLSTM_MAIN_COLS) // D_MODEL
RWKV_COL0 = NA_COLS + MLSTM_MAIN_COLS + GATE_COLS
RWKV_BLK0 = RWKV_COL0 // BRANCH_WIDTH
RWKV_LORA_BLK = (RWKV_COL0 + 3 * BRANCH_WIDTH) // RWKV_LORA_COLS
ML_GATE_BLK = (RWKV_COL0 + RWKV_COLS) // V7X_LANES

BF = jnp.bfloat16
HIGHEST = lax.Precision.HIGHEST


def _cparams(*sem):
    return pltpu.CompilerParams(dimension_semantics=sem, vmem_limit_bytes=VMEM_LIMIT_BYTES)


def _bdot(a, b):
    return jnp.dot(a.astype(BF), b.astype(BF), preferred_element_type=jnp.float32)


def _dot_nt(a, b):
    return lax.dot_general(a.astype(BF), b.astype(BF), (((1,), (1,)), ((), ())), preferred_element_type=jnp.float32)


def _dot_tn(a, b):
    return lax.dot_general(a.astype(BF), b.astype(BF), (((0,), (0,)), ((), ())), preferred_element_type=jnp.float32)


def _split_dot(x, w, parts):
    out = None
    rem = x
    for _ in range(parts):
        piece = rem.astype(BF)
        rem = rem - piece.astype(jnp.float32)
        t = jnp.dot(piece, w, preferred_element_type=jnp.float32)
        out = t if out is None else out + t
    return out


def _head_ones(width, head):
    i = np.arange(width) // head
    return jnp.asarray(i[:, None] == i[None, :], BF)


def _stack_pair(x, lo):
    zero = jnp.zeros_like(x)
    return jnp.concatenate([jnp.where(lo, x, zero), jnp.where(lo, zero, x)], axis=0)


def _lo_lanes(n):
    return lax.broadcasted_iota(jnp.int32, (n, HP), 1) < (HP // 2)


def _mm_kernel(x_ref, w_ref, o_ref):
    part = _bdot(x_ref[...], w_ref[...])

    @pl.when(pl.program_id(2) == 0)
    def _():
        o_ref[...] = part

    @pl.when(pl.program_id(2) > 0)
    def _():
        o_ref[...] += part


def _pick_tile(n, cands):
    for c in cands:
        if n % c == 0:
            return c
    return n


def pmm(x, w):
    M, K = x.shape
    N = w.shape[1]
    tm = _pick_tile(M, (1024, 512, 256, 128, 64, 32, 16, 8))
    tn = _pick_tile(N, (1024, 512, 384, 256, 128))
    tk = _pick_tile(K, (1024,)) if K > 1024 else K
    return pl.pallas_call(
        _mm_kernel,
        out_shape=jax.ShapeDtypeStruct((M, N), jnp.float32),
        grid=(M // tm, N // tn, K // tk),
        in_specs=[pl.BlockSpec((tm, tk), lambda i, j, k: (i, k)),
                  pl.BlockSpec((tk, tn), lambda i, j, k: (k, j))],
        out_specs=pl.BlockSpec((tm, tn), lambda i, j, k: (i, j)),
        compiler_params=_cparams("parallel", "parallel", "arbitrary"),
        name="tiled_matmul",
    )(x, w)


def _norm_mod(x, g, scale, shift):
    xn = x * lax.rsqrt(jnp.mean(x * x, axis=-1, keepdims=True) + RMS_EPS)
    return xn * g * (1.0 + scale) + shift


def _norm_mod_kernel(x_ref, g_ref, mod_ref, o_ref, *, row):
    o_ref[0] = _norm_mod(x_ref[0], g_ref[...], mod_ref[0, row + 1:row + 2, :], mod_ref[0, row:row + 1, :]).astype(BF)


def norm_mod(x, g, mod, row, tm):
    B, T, D = x.shape
    tok = pl.BlockSpec((1, tm, D), lambda b, i: (b, i, 0))
    return pl.pallas_call(
        functools.partial(_norm_mod_kernel, row=row),
        out_shape=jax.ShapeDtypeStruct((B, T, D), BF), grid=(B, T // tm),
        in_specs=[tok, pl.BlockSpec((1, D), lambda b, i: (0, 0)), pl.BlockSpec((1, 6, D), lambda b, i: (b, 0, 0))],
        out_specs=tok,
        compiler_params=_cparams("parallel", "parallel"),
        name="norm_mod",
    )(x, g.reshape(1, D), mod)


def _shifted(z, prev_row, next_row):
    n = z.shape[0]
    row = lax.broadcasted_iota(jnp.int32, z.shape, 0)
    zp = jnp.where(row == 0, prev_row, pltpu.roll(z, 1, axis=0))
    zn = jnp.where(row == n - 1, next_row, pltpu.roll(z, n - 1, axis=0))
    return zp, zn


def _rwkv_feat_kernel(zr_ref, zk_ref, zv_ref, zl_ref, prev_ref, next_ref, mu_ref, w0_ref, w2_ref, a0_ref, a2_ref,
                      kk_ref, g2_ref, ones_ref, *rest, has_vres):
    if has_vres:
        v0_ref, v1_ref, v2_ref, vf_ref = rest[:4]
        outs = rest[4:]
    else:
        outs = rest
    r_ref, k_ref, v_ref, kn_ref, g_ref, lw0_ref, lw1_ref, ag0_ref, ag1_ref = outs
    C = BRANCH_WIDTH

    def shift(z_ref, c0, c1):
        z = z_ref[0]
        zp, zn = _shifted(z, prev_ref[0, 0, :, c0:c1], next_ref[0, 0, :, c0:c1])
        return z + mu_ref[:, c0:c1] * (0.5 * (zp + zn) - z)

    r = shift(zr_ref, 0, C)
    k = shift(zk_ref, C, 2 * C)
    v = shift(zv_ref, 2 * C, 3 * C)
    zl = shift(zl_ref, 3 * C, RWKV_COLS)
    wd = jnp.tanh(zl[:, 0:2 * RWKV_DECAY_LORA])
    ad = zl[:, 2 * RWKV_DECAY_LORA:4 * RWKV_DECAY_LORA]
    gd = zl[:, 4 * RWKV_DECAY_LORA:]
    for d, (lw_ref, ag_ref) in enumerate(((lw0_ref, ag0_ref), (lw1_ref, ag1_ref))):
        wl = -jax.nn.softplus(-(w0_ref[d:d + 1, :] + _bdot(wd, w2_ref[d]))) - 0.5
        lw_ref[0] = -jnp.exp(wl)
        ag_ref[0] = jax.nn.sigmoid(a0_ref[d:d + 1, :] + _bdot(ad, a2_ref[d]))
    kq = k * kk_ref[...]
    ss = _split_dot(kq * kq, ones_ref[...], 2)
    kn_ref[0] = kq / jnp.maximum(jnp.sqrt(ss), 1e-12)
    if has_vres:
        lora = _bdot(_bdot(v, v1_ref[...]), v2_ref[...])
        v = v + (vf_ref[0] - v) * jax.nn.sigmoid(v0_ref[...] + lora)
    g_ref[0] = _bdot(jax.nn.sigmoid(gd), g2_ref[...])
    r_ref[0] = r
    k_ref[0] = k
    v_ref[0] = v


def rwkv_features(proj, mu, w0, w2, a0, a2, k_k, g2, vres, v_first, tm):
    B, T, _ = proj.shape
    C = BRANCH_WIDTH
    nt = T // tm
    zsl = proj[:, :, RWKV_COL0:RWKV_COL0 + RWKV_COLS]
    zero = jnp.zeros((B, 1, RWKV_COLS), proj.dtype)
    prev = jnp.concatenate([zero, zsl[:, tm - 1:T - 1:tm]], axis=1).reshape(B, nt, 1, RWKV_COLS)
    nxt = jnp.concatenate([zsl[:, tm::tm], zero], axis=1).reshape(B, nt, 1, RWKV_COLS)
    zpad = jnp.zeros((RWKV_DECAY_LORA, C), jnp.float32)
    pad_dirs = lambda w: jnp.stack([jnp.concatenate([w[0], zpad], 0), jnp.concatenate([zpad, w[1]], 0)]).astype(BF)
    blk = lambda w, j: pl.BlockSpec((1, tm, w), lambda b, i: (b, i, j))
    tok = pl.BlockSpec((1, tm, C), lambda b, i: (b, i, 0))
    edge = pl.BlockSpec((1, 1, 1, RWKV_COLS), lambda b, i: (b, i, 0, 0))
    full = lambda a: pl.BlockSpec(a.shape, lambda b, i: (0,) * a.ndim)
    params = [mu.reshape(1, -1), w0, pad_dirs(w2), a0, pad_dirs(a2), k_k.reshape(1, -1), g2.astype(BF),
              _head_ones(C, RWKV_HEAD)]
    args = [proj, proj, proj, proj, prev, nxt] + params
    specs = [blk(C, RWKV_BLK0), blk(C, RWKV_BLK0 + 1), blk(C, RWKV_BLK0 + 2), blk(RWKV_LORA_COLS, RWKV_LORA_BLK),
             edge, edge] + [full(a) for a in params]
    if vres is not None:
        v0, v1, v2 = vres
        extra = [v0.reshape(1, -1), jnp.pad(v1, ((0, 0), (0, V7X_LANES - v1.shape[1]))).astype(BF),
                 jnp.pad(v2, ((0, V7X_LANES - v2.shape[0]), (0, 0))).astype(BF)]
        args += extra + [v_first]
        specs += [full(a) for a in extra] + [tok]
    return pl.pallas_call(
        functools.partial(_rwkv_feat_kernel, has_vres=vres is not None),
        out_shape=(jax.ShapeDtypeStruct((B, T, C), jnp.float32),) * 9,
        grid=(B, nt), in_specs=specs, out_specs=(tok,) * 9,
        compiler_params=_cparams("parallel", "parallel"),
        name="rwkv_features",
    )(*args)


def _wkv_kernel(r_ref, lw_ref, kk_ref, a_ref, k_ref, v_ref, ka_ref, s0_ref, *rest, reverse, n_pairs, has_prev):
    if has_prev:
        yprev_ref, y_ref, sout_ref, s_scr = rest
    else:
        y_ref, sout_ref, s_scr = rest
    C = RWKV_CHUNK
    c_idx = pl.program_id(1)

    @pl.when(c_idx == 0)
    def _():
        s_scr[...] = s0_ref[0]

    ti = lax.broadcasted_iota(jnp.int32, (C, C), 0)
    si = lax.broadcasted_iota(jnp.int32, (C, C), 1)
    tri = ((si >= ti) if reverse else (si <= ti)).astype(jnp.float32)
    t2 = lax.broadcasted_iota(jnp.int32, (2 * C, 2 * C), 0)
    s2 = lax.broadcasted_iota(jnp.int32, (2 * C, 2 * C), 1)
    m_strict = (s2 % C > t2 % C) if reverse else (s2 % C < t2 % C)
    m_incl = (s2 % C >= t2 % C) if reverse else (s2 % C <= t2 % C)
    eye = (t2 == s2).astype(jnp.float32)
    same_head = (t2 // C) == (s2 // C)
    lo = _lo_lanes(C)

    for p in range(n_pairs):
        sl = slice(p * HP, (p + 1) * HP)
        r = r_ref[0, :, sl]
        lw = lw_ref[0, :, sl]
        kk = kk_ref[0, :, sl]
        ag = a_ref[0, :, sl]
        kd = k_ref[0, :, sl] * (1.0 + (ag - 1.0) * ka_ref[:, sl])
        v = v_ref[0, :, sl]
        cum = jnp.dot(tri, lw, precision=HIGHEST, preferred_element_type=jnp.float32)
        tot = jnp.sum(lw, axis=0, keepdims=True)
        e_neg = jnp.exp(-cum)
        e_end = jnp.exp(tot - cum)
        b = kk * ag
        ar = jnp.concatenate([_stack_pair(-kk * jnp.exp(cum - lw), lo), _stack_pair(r * jnp.exp(cum), lo)], axis=0)
        bk = jnp.concatenate([_stack_pair(b * e_neg, lo), _stack_pair(kd * e_neg, lo)], axis=0)
        gram = _dot_nt(ar, bk)
        l_ab = jnp.where(m_strict, gram[:2 * C, :2 * C], 0.0)
        l_ak = jnp.where(m_strict, gram[:2 * C, 2 * C:], 0.0)
        l_rbk = jnp.concatenate([jnp.where(m_incl, gram[2 * C:, :2 * C], 0.0),
                                 jnp.where(m_incl, gram[2 * C:, 2 * C:], 0.0)], axis=1)
        pw = l_ab
        tinv = eye + l_ab
        for _ in range(5):
            pw = _bdot(pw, pw)
            tinv = tinv + _bdot(tinv, pw)
        vs = _stack_pair(v, lo)
        s0 = s_scr[p]
        proj = _dot_nt(ar, s0)
        u = _bdot(tinv, proj[:2 * C] + _bdot(l_ak, vs))
        uv = jnp.concatenate([u, vs], axis=0)
        ys = proj[2 * C:] + _bdot(l_rbk, uv)
        y = ys[:C] + ys[C:]
        if has_prev:
            y = y + yprev_ref[0, :, sl]
        y_ref[0, :, sl] = y
        bkh = jnp.concatenate([_stack_pair(b * e_end, lo), _stack_pair(kd * e_end, lo)], axis=0)
        s_scr[p] = s0 * jnp.exp(tot) + jnp.where(same_head, _dot_tn(uv, bkh), 0.0)

    @pl.when(c_idx == pl.num_programs(1) - 1)
    def _():
        sout_ref[0] = s_scr[...]


def wkv_chunked(r, lw, kk, ag, k, v, k_a, s0, y_prev, reverse):
    B, T, W = r.shape
    C = RWKV_CHUNK
    nc = T // C
    n_pairs = W // HP
    cmap = (lambda b, c: (b, nc - 1 - c, 0)) if reverse else (lambda b, c: (b, c, 0))
    tok = pl.BlockSpec((1, C, W), cmap)
    st = pl.BlockSpec((1, n_pairs, HP, HP), lambda b, c: (b, 0, 0, 0))
    has_prev = y_prev is not None
    args = [r, lw, kk, ag, k, v, k_a, s0] + ([y_prev] if has_prev else [])
    return pl.pallas_call(
        functools.partial(_wkv_kernel, reverse=reverse, n_pairs=n_pairs, has_prev=has_prev),
        out_shape=(jax.ShapeDtypeStruct((B, T, W), jnp.float32), jax.ShapeDtypeStruct(s0.shape, jnp.float32)),
        grid=(B, nc),
        in_specs=[tok] * 6 + [pl.BlockSpec((1, W), lambda b, c: (0, 0)), st] + ([tok] if has_prev else []),
        out_specs=(tok, st),
        scratch_shapes=[pltpu.VMEM((n_pairs, HP, HP), jnp.float32)],
        compiler_params=_cparams("parallel", "arbitrary"),
        name="wkv_chunked",
    )(*args)


def _rwkv_readout_kernel(y_ref, r_ref, k_ref, v_ref, g_ref, ag0_ref, ag1_ref, ka_ref, rk_ref, lg_ref, lb_ref,
                         ones_ref, o_ref):
    ones_bd = ones_ref[...]
    y = y_ref[0]
    mean = _split_dot(y, ones_bd, 2) * (1.0 / RWKV_HEAD)
    yc = y - mean
    var = _split_dot(yc * yc, ones_bd, 2) * (1.0 / RWKV_HEAD)
    yn = yc * lax.rsqrt(var + RWKV_LNX_EPS) * lg_ref[...] + lb_ref[...]
    ksum = k_ref[0] * (2.0 + (ag0_ref[0] + ag1_ref[0] - 2.0) * ka_ref[...])
    bonus = _split_dot(r_ref[0] * ksum * rk_ref[...], ones_bd, 2) * v_ref[0]
    o_ref[0] = (yn + bonus) * g_ref[0]


def rwkv_readout(y, r, k, v, g, ag0, ag1, k_a, r_k, lnx_g, lnx_b, tm):
    B, T, C = y.shape
    tok = pl.BlockSpec((1, tm, C), lambda b, i: (b, i, 0))
    par = pl.BlockSpec((1, C), lambda b, i: (0, 0))
    return pl.pallas_call(
        _rwkv_readout_kernel,
        out_shape=jax.ShapeDtypeStruct((B, T, C), jnp.float32), grid=(B, T // tm),
        in_specs=[tok] * 7 + [par] * 4 + [pl.BlockSpec((C, C), lambda b, i: (0, 0))],
        out_specs=tok,
        compiler_params=_cparams("parallel", "parallel"),
        name="rwkv_readout",
    )(y, r, k, v, g, ag0, ag1, k_a, r_k.reshape(1, -1), lnx_g.reshape(1, -1), lnx_b.reshape(1, -1),
      _head_ones(C, RWKV_HEAD))


def rwkv_mix(proj_x, proj_c, vf_x, vf_c, mu, w0, w2, a0, a2, k_k, k_a, r_k, g2, lnx_g, lnx_b, vres, need_ctx, tm):
    B = proj_x.shape[0]
    fx = rwkv_features(proj_x, mu, w0, w2, a0, a2, k_k, g2, vres, vf_x, tm)
    fc = rwkv_features(proj_c, mu, w0, w2, a0, a2, k_k, g2, vres, vf_c, tm)
    ka = k_a.reshape(1, -1)
    y_x = y_c = None
    for d in range(N_DIR):
        s0 = jnp.zeros((B, BRANCH_WIDTH // HP, HP, HP), jnp.float32)
        y_c, s_ctx = wkv_chunked(fc[0], fc[5 + d], fc[3], fc[7 + d], fc[1], fc[2], ka, s0, y_c, d == 1)
        y_x, _ = wkv_chunked(fx[0], fx[5 + d], fx[3], fx[7 + d], fx[1], fx[2], ka, s_ctx, y_x, d == 1)
    out_x = rwkv_readout(y_x, fx[0], fx[1], fx[2], fx[4], fx[7], fx[8], ka, r_k, lnx_g, lnx_b, tm)
    out_c = rwkv_readout(y_c, fc[0], fc[1], fc[2], fc[4], fc[7], fc[8], ka, r_k, lnx_g, lnx_b, tm) if need_ctx else None
    vf_x = fx[2] if vres is None else vf_x
    vf_c = fc[2] if vres is None else vf_c
    return out_x, out_c, vf_x, vf_c


def _qknorm_kernel(z_ref, qg_ref, kg_ref, ones_ref, q_ref, k_ref, v_ref):
    ones_bd = ones_ref[...]
    C = BRANCH_WIDTH
    z = z_ref[0]
    q = z[:, 0:C]
    k = z[:, C:2 * C]
    qn = q * lax.rsqrt(_split_dot(q * q, ones_bd, 2) * (1.0 / NA_HEAD) + RMS_EPS) * qg_ref[...]
    kn = k * lax.rsqrt(_split_dot(k * k, ones_bd, 2) * (1.0 / NA_HEAD) + RMS_EPS) * kg_ref[...]
    q_ref[0] = (qn * NA_HEAD ** -0.5).astype(BF)
    k_ref[0] = kn.astype(BF)
    v_ref[0] = z[:, 2 * C:3 * C].astype(BF)


def na_qknorm(proj, qn_g, kn_g, tm):
    B, T, _ = proj.shape
    C = BRANCH_WIDTH
    tok = pl.BlockSpec((1, tm, C), lambda b, i: (b, i, 0))
    par = pl.BlockSpec((1, C), lambda b, i: (0, 0))
    sd = jax.ShapeDtypeStruct((B, T, C), BF)
    return pl.pallas_call(
        _qknorm_kernel, out_shape=(sd, sd, sd), grid=(B, T // tm),
        in_specs=[pl.BlockSpec((1, tm, NA_COLS), lambda b, i: (b, i, NA_BLK)), par, par,
                  pl.BlockSpec((C, C), lambda b, i: (0, 0))],
        out_specs=(tok, tok, tok),
        compiler_params=_cparams("parallel", "parallel"),
        name="na_qknorm",
    )(proj, jnp.tile(qn_g, NA_HEADS).reshape(1, C), jnp.tile(kn_g, NA_HEADS).reshape(1, C), _head_ones(C, NA_HEAD))


def na_bias_table(rpb):
    qc = np.arange(GRID_W)[:, None]
    kc = np.arange(GRID_W)[None, :]
    cs = np.clip(qc - NA_WIN_C // 2, 0, GRID_W - NA_WIN_C)
    valid = (kc >= cs) & (kc < cs + NA_WIN_C)
    cidx = np.clip(kc - qc + NA_WIN_C - 1, 0, 2 * NA_WIN_C - 2)
    t = jnp.where(valid[None, None], rpb[:, :, cidx], NEG_INF)
    t2 = jnp.concatenate([t[:, :-1], t[:, 1:]], axis=-1)
    H = rpb.shape[0]
    t2 = t2.reshape(H // 2, 2, 2 * NA_WIN_R - 2, GRID_W, 2 * GRID_W).transpose(0, 2, 1, 3, 4)
    return t2.reshape(H // 2, 2 * NA_WIN_R - 2, 2 * GRID_W, 2 * GRID_W)


def _na_kernel(q_ref, k_ref, v_ref, kc_ref, vc_ref, bias_ref, o_ref, *, rows):
    r = pl.program_id(1)
    rs = jnp.clip(r - NA_WIN_R // 2, 0, rows - NA_WIN_R)
    base = rs - r + NA_WIN_R - 1
    k0 = pl.multiple_of(rs * GRID_W, GRID_W)
    nwin = NA_WIN_R * GRID_W
    lo = _lo_lanes(GRID_W)
    for p in range(BRANCH_WIDTH // HP):
        sl = slice(p * HP, (p + 1) * HP)
        qs = _stack_pair(q_ref[0, :, sl], lo)
        kw = k_ref[0, pl.ds(k0, nwin), sl]
        vw = v_ref[0, pl.ds(k0, nwin), sl]
        bias = jnp.concatenate([bias_ref[p, base + 2 * j] for j in range(NA_WIN_R // 2)], axis=1)
        s_loc = _dot_nt(qs, kw) + bias
        s_ctx = _dot_nt(qs, kc_ref[0, :, sl])
        m = jnp.maximum(jnp.max(s_loc, axis=1, keepdims=True), jnp.max(s_ctx, axis=1, keepdims=True))
        p_loc = jnp.exp(s_loc - m)
        p_ctx = jnp.exp(s_ctx - m)
        den = jnp.sum(p_loc, axis=1, keepdims=True) + jnp.sum(p_ctx, axis=1, keepdims=True)
        o = (_bdot(p_loc, vw) + _bdot(p_ctx, vc_ref[0, :, sl])) / den
        o_ref[0, :, sl] = jnp.where(lo, o[:GRID_W], o[GRID_W:])


def na_attention(q, k, v, kc, vc, bias_tab):
    B, S, C = q.shape
    rows = S // GRID_W
    n_ctx = kc.shape[1]
    seq = pl.BlockSpec((1, S, C), lambda b, r: (b, 0, 0))
    cx = pl.BlockSpec((1, n_ctx, C), lambda b, r: (b, 0, 0))
    row = pl.BlockSpec((1, GRID_W, C), lambda b, r: (b, r, 0))
    return pl.pallas_call(
        functools.partial(_na_kernel, rows=rows),
        out_shape=jax.ShapeDtypeStruct((B, S, C), jnp.float32),
        grid=(B, rows),
        in_specs=[row, seq, seq, cx, cx, pl.BlockSpec(bias_tab.shape, lambda b, r: (0, 0, 0, 0))],
        out_specs=row,
        compiler_params=_cparams("parallel", "arbitrary"),
        name="na_attention",
    )(q, k, v, kc, vc, bias_tab)


def _ctx_attn_kernel(q_ref, k_ref, v_ref, o_ref):
    n = q_ref.shape[1]
    lo = _lo_lanes(n)
    for p in range(BRANCH_WIDTH // HP):
        sl = slice(p * HP, (p + 1) * HP)
        s = _dot_nt(_stack_pair(q_ref[0, :, sl], lo), k_ref[0, :, sl])
        e = jnp.exp(s - jnp.max(s, axis=1, keepdims=True))
        o = _bdot(e, v_ref[0, :, sl]) / jnp.sum(e, axis=1, keepdims=True)
        o_ref[0, :, sl] = jnp.where(lo, o[:n], o[n:])


def ctx_attention(q, k, v):
    B, n, C = q.shape
    blk = pl.BlockSpec((1, n, C), lambda b: (b, 0, 0))
    return pl.pallas_call(
        _ctx_attn_kernel, out_shape=jax.ShapeDtypeStruct((B, n, C), jnp.float32), grid=(B,),
        in_specs=[blk, blk, blk], out_specs=blk,
        compiler_params=_cparams("parallel"),
        name="ctx_attention",
    )(q, k, v)


def na_mix(proj_x, proj_c, qn_g, kn_g, rpb, need_ctx, tm):
    q, k, v = na_qknorm(proj_x, qn_g, kn_g, tm)
    qc, kc, vc = na_qknorm(proj_c, qn_g, kn_g, tm)
    out_x = na_attention(q, k, v, kc, vc, na_bias_table(rpb))
    out_c = ctx_attention(qc, kc, vc) if need_ctx else None
    return out_x, out_c


def rope_tables(n_tokens):
    t = jnp.arange(n_tokens)
    pos = jnp.stack([t // GRID_W, t % GRID_W], axis=-1).astype(jnp.float32)
    inv = ROPE_BASE ** (-jnp.arange(ROPE_ROT, dtype=jnp.float32) / ROPE_ROT)
    ang = pos[:, :, None] * inv
    cos_h = jnp.concatenate([jnp.cos(ang), jnp.cos(ang)], axis=-1).reshape(n_tokens, MLSTM_DQK)
    sin_h = jnp.concatenate([-jnp.sin(ang), jnp.sin(ang)], axis=-1).reshape(n_tokens, MLSTM_DQK)
    col = np.arange(MLSTM_QK)
    partner = np.where((col % (2 * ROPE_ROT)) < ROPE_ROT, col + ROPE_ROT, col - ROPE_ROT)
    perm = np.zeros((MLSTM_QK, MLSTM_QK), np.float32)
    perm[partner, col] = 1.0
    return jnp.tile(cos_h, (1, MLSTM_HEADS)), jnp.tile(sin_h, (1, MLSTM_HEADS)), jnp.asarray(perm, BF)


def _mlstm_prep_kernel(z_ref, gp_ref, ib_ref, fb_ref, *rest, rope):
    if rope:
        cos_ref, sin_ref, perm_ref, q_ref, k_ref, v_ref, g_ref = rest
    else:
        q_ref, k_ref, v_ref, g_ref = rest
    z = z_ref[0]
    q = z[:, 0:MLSTM_QK]
    k = z[:, MLSTM_QK:2 * MLSTM_QK]
    if rope:
        perm = perm_ref[...]
        q = q * cos_ref[...] + _split_dot(q, perm, 3) * sin_ref[...]
        k = k * cos_ref[...] + _split_dot(k, perm, 3) * sin_ref[...]
    q_ref[0] = (q * MLSTM_DQK ** -0.5).astype(BF)
    k_ref[0] = k.astype(BF)
    v_ref[0] = z[:, 2 * MLSTM_QK:2 * MLSTM_QK + BRANCH_WIDTH].astype(BF)
    gp = gp_ref[0]
    lane = lax.broadcasted_iota(jnp.int32, gp.shape, 1)
    ig = GATE_CAP * jnp.tanh((gp + ib_ref[...]) / GATE_CAP)
    fg = GATE_CAP * jnp.tanh((gp + fb_ref[...]) / GATE_CAP)
    g_ref[0] = jnp.where(lane < N_DIR * MLSTM_HEADS, ig, jax.nn.log_sigmoid(fg))


def mlstm_prep(proj, i_bias, f_bias, rope, tm):
    B, T, _ = proj.shape
    ng = N_DIR * MLSTM_HEADS
    ib = jnp.zeros((1, V7X_LANES), jnp.float32).at[0, 0:ng].set(i_bias.reshape(-1))
    fb = jnp.zeros((1, V7X_LANES), jnp.float32).at[0, ng:2 * ng].set(f_bias.reshape(-1))
    tok = lambda w: pl.BlockSpec((1, tm, w), lambda b, i: (b, i, 0))
    par = pl.BlockSpec((1, V7X_LANES), lambda b, i: (0, 0))
    args = [proj, proj, ib, fb]
    specs = [pl.BlockSpec((1, tm, MLSTM_MAIN_COLS), lambda b, i: (b, i, ML_BLK)),
             pl.BlockSpec((1, tm, V7X_LANES), lambda b, i: (b, i, ML_GATE_BLK)), par, par]
    if rope is not None:
        args += list(rope)
        specs += [pl.BlockSpec((tm, MLSTM_QK), lambda b, i: (i, 0)), pl.BlockSpec((tm, MLSTM_QK), lambda b, i: (i, 0)),
                  pl.BlockSpec((MLSTM_QK, MLSTM_QK), lambda b, i: (0, 0))]
    return pl.pallas_call(
        functools.partial(_mlstm_prep_kernel, rope=rope is not None),
        out_shape=(jax.ShapeDtypeStruct((B, T, MLSTM_QK), BF), jax.ShapeDtypeStruct((B, T, MLSTM_QK), BF),
                   jax.ShapeDtypeStruct((B, T, BRANCH_WIDTH), BF), jax.ShapeDtypeStruct((B, T, V7X_LANES), jnp.float32)),
        grid=(B, T // tm), in_specs=specs,
        out_specs=(tok(MLSTM_QK), tok(MLSTM_QK), tok(BRANCH_WIDTH), tok(V7X_LANES)),
        compiler_params=_cparams("parallel", "parallel"),
        name="mlstm_prep",
    )(*args)


def _mlstm_kernel(q_ref, k_ref, v_ref, gc_ref, gr_ref, c0_ref, n0_ref, m0_ref, *rest, reverse, direction, has_prev):
    if has_prev:
        hprev_ref, h_ref, cout_ref, nout_ref, mout_ref, c_scr, n_scr, m_scr = rest
    else:
        h_ref, cout_ref, nout_ref, mout_ref, c_scr, n_scr, m_scr = rest
    L = MLSTM_CHUNK
    DV = MLSTM_DV
    cidx = pl.program_id(1)

    @pl.when(cidx == 0)
    def _():
        c_scr[...] = c0_ref[0]
        n_scr[...] = n0_ref[0]
        m_scr[...] = m0_ref[0]

    ti = lax.broadcasted_iota(jnp.int32, (L, L), 0)
    si = lax.broadcasted_iota(jnp.int32, (L, L), 1)
    tri = ((si >= ti) if reverse else (si <= ti)).astype(jnp.float32)
    t2 = lax.broadcasted_iota(jnp.int32, (2 * L, L), 0)
    s2 = lax.broadcasted_iota(jnp.int32, (2 * L, L), 1)
    causal = (s2 >= t2 % L) if reverse else (s2 <= t2 % L)
    top = t2 < L
    top_col = lax.broadcasted_iota(jnp.int32, (2 * L, 1), 0) < L
    lo = _lo_lanes(L)
    lane1 = lax.broadcasted_iota(jnp.int32, (1, HP), 1)
    lo_row = lane1 < (HP // 2)
    lo_col = lax.broadcasted_iota(jnp.int32, (HP, 1), 0) < (HP // 2)
    col = lambda a, c: a[:, c:c + 1]
    rowv = lambda a, c: a[c:c + 1, :]

    gc = gc_ref[0]
    gr = gr_ref[0, 0]
    bcol_all = jnp.dot(tri, gc, precision=HIGHEST, preferred_element_type=jnp.float32)
    brow_all = lax.dot_general(gr, tri, (((1,), (1,)), ((), ())), precision=HIGHEST, preferred_element_type=jnp.float32)
    tot_all = jnp.sum(gc, axis=0, keepdims=True)
    m_all = m_scr[...]
    for p in range(MLSTM_HEADS // 2):
        h0, h1 = 2 * p, 2 * p + 1
        ci = lambda h: direction * MLSTM_HEADS + h
        cf = lambda h: N_DIR * MLSTM_HEADS + direction * MLSTM_HEADS + h
        bcol_s = jnp.concatenate([col(bcol_all, cf(h0)), col(bcol_all, cf(h1))], axis=0)
        icol_s = jnp.concatenate([col(gc, ci(h0)), col(gc, ci(h1))], axis=0)
        brow_s = jnp.where(top, rowv(brow_all, cf(h0)), rowv(brow_all, cf(h1)))
        irow_s = jnp.where(top, rowv(gr, ci(h0)), rowv(gr, ci(h1)))
        m0, m1 = col(m_all, h0), col(m_all, h1)
        mprev_s = jnp.where(top_col, m0, m1)
        tot0, tot1 = col(tot_all, cf(h0)), col(tot_all, cf(h1))
        tot_s = jnp.where(top_col, tot0, tot1)

        sl = slice(p * HP, (p + 1) * HP)
        qs = _stack_pair(q_ref[0, :, sl], lo)
        kp = k_ref[0, :, sl]
        v0 = v_ref[0, :, h0 * DV:(h0 + 1) * DV]
        v1 = v_ref[0, :, h1 * DV:(h1 + 1) * DV]
        c_pair = c_scr[p]
        n_pair = n_scr[p]

        dmat = jnp.where(causal, bcol_s - brow_s + irow_s, NEG_INF)
        inter = bcol_s + mprev_s
        m_t = jnp.maximum(inter, jnp.max(dmat, axis=1, keepdims=True))
        smat = _dot_nt(qs, kp) * jnp.exp(dmat - m_t)
        sc_inter = jnp.exp(inter - m_t)
        sb = smat.astype(BF)
        sv = jnp.concatenate([_bdot(sb[:L], v0), _bdot(sb[L:], v1)], axis=0)
        num = sc_inter * _bdot(qs, c_pair) + sv
        qn = jnp.sum(qs.astype(jnp.float32) * n_pair, axis=1, keepdims=True)
        den = sc_inter * qn + jnp.sum(smat, axis=1, keepdims=True)
        hout = num / jnp.maximum(jnp.abs(den), jnp.exp(-m_t))
        for h, part in ((h0, hout[:L]), (h1, hout[L:])):
            hs = slice(h * DV, (h + 1) * DV)
            h_ref[0, :, hs] = (part + hprev_ref[0, :, hs]) if has_prev else part

        wlog_s = tot_s - bcol_s + icol_s
        mn0 = jnp.maximum(tot0 + m0, jnp.max(wlog_s[:L], axis=0, keepdims=True))
        mn1 = jnp.maximum(tot1 + m1, jnp.max(wlog_s[L:], axis=0, keepdims=True))
        dec0 = jnp.exp(tot0 + m0 - mn0)
        dec1 = jnp.exp(tot1 + m1 - mn1)
        wexp_s = jnp.exp(wlog_s - jnp.where(top_col, mn0, mn1))
        kws = _stack_pair(kp, lo).astype(jnp.float32) * wexp_s
        vst = jnp.concatenate([v0, v1], axis=0)
        c_scr[p] = jnp.where(lo_col, dec0, dec1) * c_pair + _dot_tn(kws, vst)
        n_scr[p] = jnp.where(lo_row, dec0, dec1) * n_pair + jnp.sum(kws, axis=0, keepdims=True)
        m_all = jnp.where(lane1 == h0, mn0, jnp.where(lane1 == h1, mn1, m_all))
    m_scr[...] = m_all

    @pl.when(cidx == pl.num_programs(1) - 1)
    def _():
        cout_ref[0] = c_scr[...]
        nout_ref[0] = n_scr[...]
        mout_ref[0] = m_scr[...]


def mlstm_chunked(q, k, v, gates, state, h_prev, direction):
    B, T, _ = q.shape
    L = MLSTM_CHUNK
    nc = T // L
    reverse = direction == 1
    g_rows = gates[:, :, :MLSTM_GATES].reshape(B, nc, L, MLSTM_GATES).transpose(0, 1, 3, 2)
    cm = (lambda c: nc - 1 - c) if reverse else (lambda c: c)
    tok = lambda w: pl.BlockSpec((1, L, w), lambda b, c: (b, cm(c), 0))
    st = lambda a: pl.BlockSpec((1,) + a.shape[1:], lambda b, c: (b,) + (0,) * (a.ndim - 1))
    c0, n0, m0 = state
    has_prev = h_prev is not None
    args = [q, k, v, gates, g_rows, c0, n0, m0] + ([h_prev] if has_prev else [])
    outs = pl.pallas_call(
        functools.partial(_mlstm_kernel, reverse=reverse, direction=direction, has_prev=has_prev),
        out_shape=(jax.ShapeDtypeStruct((B, T, BRANCH_WIDTH), jnp.float32),) + tuple(
            jax.ShapeDtypeStruct(a.shape, jnp.float32) for a in state),
        grid=(B, nc),
        in_specs=[tok(MLSTM_QK), tok(MLSTM_QK), tok(BRANCH_WIDTH), tok(V7X_LANES),
                  pl.BlockSpec((1, 1, MLSTM_GATES, L), lambda b, c: (b, cm(c), 0, 0)), st(c0), st(n0), st(m0)]
                 + ([tok(BRANCH_WIDTH)] if has_prev else []),
        out_specs=(tok(BRANCH_WIDTH), st(c0), st(n0), st(m0)),
        scratch_shapes=[pltpu.VMEM(a.shape[1:], jnp.float32) for a in state],
        compiler_params=_cparams("parallel", "arbitrary"),
        name="mlstm_chunked",
    )(*args)
    return outs[0], outs[1:]


def _mlstm_readout_kernel(h_ref, o_ref, g_ref, out_ref):
    for h in range(MLSTM_HEADS):
        sl = slice(h * MLSTM_DV, (h + 1) * MLSTM_DV)
        x = h_ref[0, :, sl]
        xn = x * lax.rsqrt(jnp.mean(x * x, axis=1, keepdims=True) + RMS_EPS)
        out_ref[0, :, sl] = xn * g_ref[:, sl] * jax.nn.sigmoid(o_ref[0, :, sl])


def mlstm_readout(h, proj, norm_g, tm):
    B, T, C = h.shape
    tok = pl.BlockSpec((1, tm, C), lambda b, i: (b, i, 0))
    return pl.pallas_call(
        _mlstm_readout_kernel, out_shape=jax.ShapeDtypeStruct((B, T, C), jnp.float32), grid=(B, T // tm),
        in_specs=[tok, pl.BlockSpec((1, tm, C), lambda b, i: (b, i, ML_OGATE_BLK)), pl.BlockSpec((1, C), lambda b, i: (0, 0))],
        out_specs=tok,
        compiler_params=_cparams("parallel", "parallel"),
        name="mlstm_readout",
    )(h, proj, norm_g.reshape(1, C))


def mlstm_mix(proj_x, proj_c, i_bias, f_bias, norm_g, rope, need_ctx, tm):
    B = proj_x.shape[0]
    qx, kx, vx, gx = mlstm_prep(proj_x, i_bias, f_bias, rope, tm)
    qc, kc, vc, gc = mlstm_prep(proj_c, i_bias, f_bias, None, tm)
    h_x = h_c = None
    for d in range(N_DIR):
        st0 = (jnp.zeros((B, MLSTM_HEADS // 2, HP, HP), jnp.float32), jnp.zeros((B, MLSTM_HEADS // 2, 1, HP), jnp.float32),
               jnp.zeros((B, 1, HP), jnp.float32))
        h_c, st_ctx = mlstm_chunked(qc, kc, vc, gc, st0, h_c, d)
        h_x, _ = mlstm_chunked(qx, kx, vx, gx, st_ctx, h_x, d)
    out_x = mlstm_readout(h_x, proj_x, norm_g, tm)
    out_c = mlstm_readout(h_c, proj_c, norm_g, tm) if need_ctx else None
    return out_x, out_c


def _merge_kernel(ya_ref, yb_ref, yc_ref, ga_ref, gb_ref, gc_ref, x_ref, mod_ref, wb_ref, wo_ref, o_ref):
    merged = None
    for i, (y_ref, g_ref) in enumerate(((ya_ref, ga_ref), (yb_ref, gb_ref), (yc_ref, gc_ref))):
        t = jax.nn.sigmoid(g_ref[0]) * _bdot(y_ref[0], wb_ref[i])
        merged = t if merged is None else merged + t
    o_ref[0] = x_ref[0] + mod_ref[0, 2:3, :] * _bdot(merged, wo_ref[...])


def merge_apply(ya, yb, yc, proj, x, mod, w_branch, w_out, tm):
    B, T, D = x.shape
    tok = lambda w: pl.BlockSpec((1, tm, w), lambda b, i: (b, i, 0))
    gate = lambda k: pl.BlockSpec((1, tm, D), lambda b, i: (b, i, GATE_BLK0 + k))
    return pl.pallas_call(
        _merge_kernel, out_shape=jax.ShapeDtypeStruct((B, T, D), jnp.float32), grid=(B, T // tm),
        in_specs=[tok(BRANCH_WIDTH)] * 3 + [gate(0), gate(1), gate(2), tok(D), pl.BlockSpec((1, 6, D), lambda b, i: (b, 0, 0)),
                  pl.BlockSpec(w_branch.shape, lambda b, i: (0, 0, 0)), pl.BlockSpec(w_out.shape, lambda b, i: (0, 0))],
        out_specs=tok(D),
        compiler_params=_cparams("parallel", "parallel"),
        name="merge_branches",
    )(ya, yb, yc, proj, proj, proj, x, mod, w_branch, w_out)


def _route_kernel(x_ref, g_ref, mod_ref, wr_ref, rb_ref, h_ref, gate_ref):
    h = _norm_mod(x_ref[0], g_ref[...], mod_ref[0, 4:5, :], mod_ref[0, 3:4, :])
    h_ref[0] = h.astype(BF)
    tm = h.shape[0]
    logits = lax.dot_general(wr_ref[...], h, (((1,), (1,)), ((), ())), precision=HIGHEST, preferred_element_type=jnp.float32)
    scores = jax.nn.sigmoid(logits)
    sel = scores + rb_ref[...]
    gsz = N_EXPERTS // N_GROUPS
    grp = sel.reshape(N_GROUPS, gsz, tm)
    iota_in = lax.broadcasted_iota(jnp.int32, grp.shape, 1)
    m1 = jnp.max(grp, axis=1, keepdims=True)
    first = jnp.min(jnp.where(grp == m1, iota_in, gsz), axis=1, keepdims=True)
    m2 = jnp.max(jnp.where(iota_in == first, -jnp.inf, grp), axis=1, keepdims=True)
    gscore = (m1 + m2).reshape(N_GROUPS, tm)
    gi = lax.broadcasted_iota(jnp.int32, (N_GROUPS, tm), 0)
    rank = jnp.zeros((N_GROUPS, tm), jnp.int32)
    for g2 in range(N_GROUPS):
        other = gscore[g2:g2 + 1, :]
        rank = rank + ((other > gscore) | ((other == gscore) & (g2 < gi))).astype(jnp.int32)
    gmask = rank < TOPK_GROUPS
    emask = jnp.broadcast_to(gmask.reshape(N_GROUPS, 1, tm), (N_GROUPS, gsz, tm)).reshape(N_EXPERTS, tm)
    cand = jnp.where(emask, sel, NEG_INF)
    ei = lax.broadcasted_iota(jnp.int32, (N_EXPERTS, tm), 0)
    chosen = jnp.zeros((N_EXPERTS, tm), jnp.bool_)
    for _ in range(TOP_K):
        mx = jnp.max(cand, axis=0, keepdims=True)
        idx = jnp.min(jnp.where(cand == mx, ei, N_EXPERTS), axis=0, keepdims=True)
        hit = ei == idx
        chosen = chosen | hit
        cand = jnp.where(hit, -jnp.inf, cand)
    w = jnp.where(chosen, scores, 0.0)
    w = w / jnp.sum(w, axis=0, keepdims=True) * ROUTED_SCALE
    gate_ref[0] = jnp.concatenate([w, jnp.zeros((V7X_LANES - N_EXPERTS, tm), jnp.float32)], axis=0).T


def moe_route(x, g, mod, w_router, router_bias, tm):
    B, T, D = x.shape
    tok = pl.BlockSpec((1, tm, D), lambda b, i: (b, i, 0))
    return pl.pallas_call(
        _route_kernel,
        out_shape=(jax.ShapeDtypeStruct((B, T, D), BF), jax.ShapeDtypeStruct((B, T, V7X_LANES), jnp.float32)),
        grid=(B, T // tm),
        in_specs=[tok, pl.BlockSpec((1, D), lambda b, i: (0, 0)), pl.BlockSpec((1, 6, D), lambda b, i: (b, 0, 0)),
                  pl.BlockSpec((N_EXPERTS, D), lambda b, i: (0, 0)), pl.BlockSpec((N_EXPERTS, 1), lambda b, i: (0, 0))],
        out_specs=(tok, pl.BlockSpec((1, tm, V7X_LANES), lambda b, i: (b, i, 0))),
        compiler_params=_cparams("parallel", "parallel"),
        name="moe_route",
    )(x, g.reshape(1, D), mod, w_router.T, router_bias.reshape(N_EXPERTS, 1))


def _moe_kernel(h_ref, gate_ref, x_ref, mod_ref, ex_ref, wg_ref, wu_ref, wd_ref, sg_ref, su_ref, sd_ref, o_ref, acc_ref):
    j = pl.program_id(2)
    h = h_ref[0]

    @pl.when(j == 0)
    def _():
        sh = jax.nn.silu(_bdot(h, sg_ref[...])) * _bdot(h, su_ref[...])
        acc_ref[...] = _bdot(sh, sd_ref[...])

    gexp = _split_dot(gate_ref[0], ex_ref[0], 2)
    act = jax.nn.silu(_bdot(h, wg_ref[...])) * _bdot(h, wu_ref[...]) * gexp
    acc_ref[...] += _bdot(act, wd_ref[...])

    @pl.when(j == pl.num_programs(2) - 1)
    def _():
        o_ref[0] = x_ref[0] + mod_ref[0, 5:6, :] * acc_ref[...]


def _moe_expand_table():
    se = MOE_STEP_EXPERTS
    t = np.zeros((N_EXPERTS // se, V7X_LANES, se * D_EXPERT), np.float32)
    for j in range(N_EXPERTS // se):
        for e in range(se):
            t[j, j * se + e, e * D_EXPERT:(e + 1) * D_EXPERT] = 1.0
    return jnp.asarray(t, BF)


def moe_apply(h2, gates, x, mod, wg, wu, wd, sg, su, sd, tm):
    B, T, D = x.shape
    sw = MOE_STEP_EXPERTS * D_EXPERT
    tok = pl.BlockSpec((1, tm, D), lambda b, i, j: (b, i, 0))
    full = lambda a: pl.BlockSpec(a.shape, lambda b, i, j: (0,) * a.ndim)
    return pl.pallas_call(
        _moe_kernel, out_shape=jax.ShapeDtypeStruct((B, T, D), jnp.float32),
        grid=(B, T // tm, N_EXPERTS // MOE_STEP_EXPERTS),
        in_specs=[tok, pl.BlockSpec((1, tm, V7X_LANES), lambda b, i, j: (b, i, 0)), tok,
                  pl.BlockSpec((1, 6, D), lambda b, i, j: (b, 0, 0)),
                  pl.BlockSpec((1, V7X_LANES, sw), lambda b, i, j: (j, 0, 0)),
                  pl.BlockSpec((D, sw), lambda b, i, j: (0, j)), pl.BlockSpec((D, sw), lambda b, i, j: (0, j)),
                  pl.BlockSpec((sw, D), lambda b, i, j: (j, 0)), full(sg), full(su), full(sd)],
        out_specs=tok,
        scratch_shapes=[pltpu.VMEM((tm, D), jnp.float32)],
        compiler_params=_cparams("parallel", "parallel", "arbitrary"),
        name="moe_experts",
    )(h2, gates, x, mod, _moe_expand_table(), wg, wu, wd, sg, su, sd)


def _reorder_w_in(w):
    o_na = RWKV_COLS
    o_ml = o_na + NA_COLS
    o_mg = o_ml + MLSTM_MAIN_COLS
    o_gate = o_mg + MLSTM_GATES
    pad = jnp.zeros((w.shape[0], V7X_LANES - MLSTM_GATES), w.dtype)
    return jnp.concatenate([w[:, o_na:o_ml], w[:, o_ml:o_mg], w[:, o_gate:], w[:, :RWKV_COLS], w[:, o_mg:o_gate], pad],
                           axis=1).astype(BF)


def kernel(x, c, ctx, c_ctx, w_ada, b_ada, norm1_g, norm2_g, w_in, rw_mu, rw_w0, rw_w2, rw_a0, rw_a2, rw_k_k, rw_k_a, rw_r_k, rw_g2, rw_lnx_g, rw_lnx_b, rw_v0, rw_v1, rw_v2, na_qn_g, na_kn_g, na_rpb, ml_i_bias, ml_f_bias, ml_norm_g, w_branch, w_out, moe_router, moe_bias, moe_w_gate, moe_w_up, moe_w_down, sh_w_gate, sh_w_up, sh_w_down):
    B, S, D = x.shape
    n_ctx = ctx.shape[1]
    tm = 256
    assert S % tm == 0 and n_ctx % tm == 0 and PROJ_COLS == IN_COLS + V7X_LANES - MLSTM_GATES
    rope = rope_tables(S)
    n_cond = B + 1
    cond_pad = (-n_cond) % V7X_SUBLANES
    s_cond = jnp.pad(jnp.concatenate([jax.nn.silu(c), jax.nn.silu(c_ctx)[None]], axis=0), ((0, cond_pad), (0, 0)))
    vf_x = vf_c = None
    for l in range(DEPTH):
        need_ctx = l < DEPTH - 1
        mod = pmm(s_cond, w_ada[l]) + b_ada[l]
        mod_x = mod[:B].reshape(B, 6, D)
        mod_c = jnp.broadcast_to(mod[B].reshape(1, 6, D), (B, 6, D))
        w_proj = _reorder_w_in(w_in[l])
        hx = norm_mod(x, norm1_g[l], mod_x, 0, tm)
        hc = norm_mod(ctx, norm1_g[l], mod_c, 0, tm)
        proj_x = pmm(hx.reshape(B * S, D), w_proj).reshape(B, S, PROJ_COLS)
        proj_c = pmm(hc.reshape(B * n_ctx, D), w_proj).reshape(B, n_ctx, PROJ_COLS)
        vres = None if l == 0 else (rw_v0[l - 1], rw_v1[l - 1], rw_v2[l - 1])
        ya_x, ya_c, vf_x, vf_c = rwkv_mix(proj_x, proj_c, vf_x, vf_c, rw_mu[l], rw_w0[l], rw_w2[l], rw_a0[l], rw_a2[l],
                                          rw_k_k[l], rw_k_a[l], rw_r_k[l], rw_g2[l], rw_lnx_g[l], rw_lnx_b[l],
                                          vres, need_ctx, tm)
        yb_x, yb_c = na_mix(proj_x, proj_c, na_qn_g[l], na_kn_g[l], na_rpb[l], need_ctx, tm)
        yc_x, yc_c = mlstm_mix(proj_x, proj_c, ml_i_bias[l], ml_f_bias[l], ml_norm_g[l], rope, need_ctx, tm)
        wb = w_branch[l].astype(BF)
        wo = w_out[l].astype(BF)
        wg = moe_w_gate[l].transpose(1, 0, 2).reshape(D, N_EXPERTS * D_EXPERT).astype(BF)
        wu = moe_w_up[l].transpose(1, 0, 2).reshape(D, N_EXPERTS * D_EXPERT).astype(BF)
        wd = moe_w_down[l].reshape(N_EXPERTS * D_EXPERT, D).astype(BF)
        shared = (sh_w_gate[l].astype(BF), sh_w_up[l].astype(BF), sh_w_down[l].astype(BF))
        x = merge_apply(ya_x, yb_x, yc_x, proj_x, x, mod_x, wb, wo, tm)
        h2, gates = moe_route(x, norm2_g[l], mod_x, moe_router[l], moe_bias[l], tm)
        x = moe_apply(h2, gates, x, mod_x, wg, wu, wd, *shared, tm)
        if need_ctx:
            ctx = merge_apply(ya_c, yb_c, yc_c, proj_c, ctx, mod_c, wb, wo, tm)
            h2, gates = moe_route(ctx, norm2_g[l], mod_c, moe_router[l], moe_bias[l], tm)
            ctx = moe_apply(h2, gates, ctx, mod_c, wg, wu, wd, *shared, tm)
    return x
```

```python
import functools

import numpy as np
import jax
import jax.numpy as jnp
from jax import lax
from jax.experimental import pallas as pl
from jax.experimental.pallas import tpu as pltpu

D_MODEL = 1024
DEPTH = 2
GRID_W = 64
N_DIR = 2
N_BRANCH = 3
BRANCH_WIDTH = 512
RMS_EPS = 1e-6
NEG_INF = -1e30

RWKV_HEAD = 64
RWKV_DECAY_LORA = 64
RWKV_LORA_COLS = 384
RWKV_COLS = 3 * BRANCH_WIDTH + RWKV_LORA_COLS
RWKV_LNX_EPS = 64e-5
RWKV_CHUNK = 64
WKV_SEQS_PER_STEP = 4

NA_HEAD = 64
NA_HEADS = BRANCH_WIDTH // NA_HEAD
NA_WIN_R = 8
NA_WIN_C = 16
NA_COLS = 3 * BRANCH_WIDTH

MLSTM_HEADS = 4
MLSTM_DQK = 64
MLSTM_DV = BRANCH_WIDTH // MLSTM_HEADS
MLSTM_QK = MLSTM_HEADS * MLSTM_DQK
MLSTM_CHUNK = 64
MLSTM_MAIN_COLS = 2 * MLSTM_QK + 2 * BRANCH_WIDTH
MLSTM_GATES = 2 * N_DIR * MLSTM_HEADS
GATE_CAP = 15.0
ROPE_ROT = MLSTM_DQK // 4
ROPE_BASE = 10000.0

N_EXPERTS = 64
TOP_K = 8
N_GROUPS = 8
TOPK_GROUPS = 4
D_EXPERT = 128
ROUTED_SCALE = 2.5
MOE_STEP_EXPERTS = 8

GATE_COLS = N_BRANCH * D_MODEL
IN_COLS = RWKV_COLS + NA_COLS + MLSTM_MAIN_COLS + MLSTM_GATES + GATE_COLS

V7X_LANES = 128
V7X_SUBLANES = 8
HP = 2 * RWKV_HEAD
VMEM_LIMIT_BYTES = 48 * 1024 * 1024

PROJ_COLS = 8192
NA_BLK = 0
ML_BLK = 1
ML_OGATE_BLK = (NA_COLS + 2 * MLSTM_QK + BRANCH_WIDTH) // BRANCH_WIDTH
GATE_BLK0 = (NA_COLS + MLSTM_MAIN_COLS) // D_MODEL
RWKV_COL0 = NA_COLS + MLSTM_MAIN_COLS + GATE_COLS
RWKV_BLK0 = RWKV_COL0 // BRANCH_WIDTH
RWKV_LORA_BLK = (RWKV_COL0 + 3 * BRANCH_WIDTH) // RWKV_LORA_COLS
ML_GATE_BLK = (RWKV_COL0 + RWKV_COLS) // V7X_LANES

BF = jnp.bfloat16
HIGHEST = lax.Precision.HIGHEST


def _cparams(*sem):
    return pltpu.CompilerParams(dimension_semantics=sem, vmem_limit_bytes=VMEM_LIMIT_BYTES)


def _bdot(a, b):
    return jnp.dot(a.astype(BF), b.astype(BF), preferred_element_type=jnp.float32)


def _dot_nt(a, b):
    return lax.dot_general(a.astype(BF), b.astype(BF), (((1,), (1,)), ((), ())), preferred_element_type=jnp.float32)


def _dot_tn(a, b):
    return lax.dot_general(a.astype(BF), b.astype(BF), (((0,), (0,)), ((), ())), preferred_element_type=jnp.float32)


def _split_dot(x, w, parts):
    out = None
    rem = x
    for _ in range(parts):
        piece = rem.astype(BF)
        rem = rem - piece.astype(jnp.float32)
        t = jnp.dot(piece, w, preferred_element_type=jnp.float32)
        out = t if out is None else out + t
    return out


def _head_ones(width, head):
    i = np.arange(width) // head
    return jnp.asarray(i[:, None] == i[None, :], BF)


def _stack_pair(x, lo):
    zero = jnp.zeros_like(x)
    return jnp.concatenate([jnp.where(lo, x, zero), jnp.where(lo, zero, x)], axis=0)


def _lo_lanes(n):
    return lax.broadcasted_iota(jnp.int32, (n, HP), 1) < (HP // 2)


def _mm_kernel(x_ref, w_ref, o_ref):
    part = _bdot(x_ref[...], w_ref[...])

    @pl.when(pl.program_id(2) == 0)
    def _():
        o_ref[...] = part

    @pl.when(pl.program_id(2) > 0)
    def _():
        o_ref[...] += part


def _pick_tile(n, cands):
    for c in cands:
        if n % c == 0:
            return c
    return n


def pmm(x, w):
    M, K = x.shape
    N = w.shape[1]
    tm = _pick_tile(M, (1024, 512, 256, 128, 64, 32, 16, 8))
    tn = _pick_tile(N, (1024, 512, 384, 256, 128))
    tk = _pick_tile(K, (1024,)) if K > 1024 else K
    return pl.pallas_call(
        _mm_kernel,
        out_shape=jax.ShapeDtypeStruct((M, N), jnp.float32),
        grid=(M // tm, N // tn, K // tk),
        in_specs=[pl.BlockSpec((tm, tk), lambda i, j, k: (i, k)),
                  pl.BlockSpec((tk, tn), lambda i, j, k: (k, j))],
        out_specs=pl.BlockSpec((tm, tn), lambda i, j, k: (i, j)),
        compiler_params=_cparams("parallel", "parallel", "arbitrary"),
        name="tiled_matmul",
    )(x, w)


def _norm_mod(x, g, scale, shift):
    xn = x * lax.rsqrt(jnp.mean(x * x, axis=-1, keepdims=True) + RMS_EPS)
    return xn * g * (1.0 + scale) + shift


def _norm_mod_kernel(x_ref, g_ref, mod_ref, o_ref, *, row):
    o_ref[0] = _norm_mod(x_ref[0], g_ref[...], mod_ref[0, row + 1:row + 2, :], mod_ref[0, row:row + 1, :]).astype(BF)


def norm_mod(x, g, mod, row, tm):
    B, T, D = x.shape
    tok = pl.BlockSpec((1, tm, D), lambda b, i: (b, i, 0))
    return pl.pallas_call(
        functools.partial(_norm_mod_kernel, row=row),
        out_shape=jax.ShapeDtypeStruct((B, T, D), BF), grid=(B, T // tm),
        in_specs=[tok, pl.BlockSpec((1, D), lambda b, i: (0, 0)), pl.BlockSpec((1, 6, D), lambda b, i: (b, 0, 0))],
        out_specs=tok,
        compiler_params=_cparams("parallel", "parallel"),
        name="norm_mod",
    )(x, g.reshape(1, D), mod)


def _shifted(z, prev_row, next_row):
    n = z.shape[0]
    row = lax.broadcasted_iota(jnp.int32, z.shape, 0)
    zp = jnp.where(row == 0, prev_row, pltpu.roll(z, 1, axis=0))
    zn = jnp.where(row == n - 1, next_row, pltpu.roll(z, n - 1, axis=0))
    return zp, zn


def _rwkv_feat_kernel(zr_ref, zk_ref, zv_ref, zl_ref, prev_ref, next_ref, mu_ref, w0_ref, w2_ref, a0_ref, a2_ref,
                      kk_ref, g2_ref, ones_ref, *rest, has_vres):
    if has_vres:
        v0_ref, v1_ref, v2_ref, vf_ref = rest[:4]
        outs = rest[4:]
    else:
        outs = rest
    r_ref, k_ref, v_ref, kn_ref, g_ref, lw0_ref, lw1_ref, ag0_ref, ag1_ref = outs
    C = BRANCH_WIDTH

    def shift(z_ref, c0, c1):
        z = z_ref[0]
        zp, zn = _shifted(z, prev_ref[0, 0, :, c0:c1], next_ref[0, 0, :, c0:c1])
        return z + mu_ref[:, c0:c1] * (0.5 * (zp + zn) - z)

    r = shift(zr_ref, 0, C)
    k = shift(zk_ref, C, 2 * C)
    v = shift(zv_ref, 2 * C, 3 * C)
    zl = shift(zl_ref, 3 * C, RWKV_COLS)
    wd = jnp.tanh(zl[:, 0:2 * RWKV_DECAY_LORA])
    ad = zl[:, 2 * RWKV_DECAY_LORA:4 * RWKV_DECAY_LORA]
    gd = zl[:, 4 * RWKV_DECAY_LORA:]
    for d, (lw_ref, ag_ref) in enumerate(((lw0_ref, ag0_ref), (lw1_ref, ag1_ref))):
        wl = -jax.nn.softplus(-(w0_ref[d:d + 1, :] + _bdot(wd, w2_ref[d]))) - 0.5
        lw_ref[0] = -jnp.exp(wl)
        ag_ref[0] = jax.nn.sigmoid(a0_ref[d:d + 1, :] + _bdot(ad, a2_ref[d]))
    kq = k * kk_ref[...]
    ss = _split_dot(kq * kq, ones_ref[...], 2)
    kn_ref[0] = kq / jnp.maximum(jnp.sqrt(ss), 1e-12)
    if has_vres:
        lora = _bdot(_bdot(v, v1_ref[...]), v2_ref[...])
        v = v + (vf_ref[0] - v) * jax.nn.sigmoid(v0_ref[...] + lora)
    g_ref[0] = _bdot(jax.nn.sigmoid(gd), g2_ref[...])
    r_ref[0] = r
    k_ref[0] = k
    v_ref[0] = v


def rwkv_features(proj, mu, w0, w2, a0, a2, k_k, g2, vres, v_first, tm):
    B, T, _ = proj.shape
    C = BRANCH_WIDTH
    nt = T // tm
    cols = slice(RWKV_COL0, RWKV_COL0 + RWKV_COLS)
    zero = jnp.zeros((B, 1, RWKV_COLS), proj.dtype)
    prev = jnp.concatenate([zero, proj[:, tm - 1:T - 1:tm, cols]], axis=1).reshape(B, nt, 1, RWKV_COLS)
    nxt = jnp.concatenate([proj[:, tm::tm, cols], zero], axis=1).reshape(B, nt, 1, RWKV_COLS)
    zpad = jnp.zeros((RWKV_DECAY_LORA, C), jnp.float32)
    pad_dirs = lambda w: jnp.stack([jnp.concatenate([w[0], zpad], 0), jnp.concatenate([zpad, w[1]], 0)]).astype(BF)
    blk = lambda w, j: pl.BlockSpec((1, tm, w), lambda b, i: (b, i, j))
    tok = pl.BlockSpec((1, tm, C), lambda b, i: (b, i, 0))
    edge = pl.BlockSpec((1, 1, 1, RWKV_COLS), lambda b, i: (b, i, 0, 0))
    full = lambda a: pl.BlockSpec(a.shape, lambda b, i: (0,) * a.ndim)
    params = [mu.reshape(1, -1), w0, pad_dirs(w2), a0, pad_dirs(a2), k_k.reshape(1, -1), g2.astype(BF),
              _head_ones(C, RWKV_HEAD)]
    args = [proj, proj, proj, proj, prev, nxt] + params
    specs = [blk(C, RWKV_BLK0), blk(C, RWKV_BLK0 + 1), blk(C, RWKV_BLK0 + 2), blk(RWKV_LORA_COLS, RWKV_LORA_BLK),
             edge, edge] + [full(a) for a in params]
    if vres is not None:
        v0, v1, v2 = vres
        extra = [v0.reshape(1, -1), jnp.pad(v1, ((0, 0), (0, V7X_LANES - v1.shape[1]))).astype(BF),
                 jnp.pad(v2, ((0, V7X_LANES - v2.shape[0]), (0, 0))).astype(BF)]
        args += extra + [v_first]
        specs += [full(a) for a in extra] + [tok]
    return pl.pallas_call(
        functools.partial(_rwkv_feat_kernel, has_vres=vres is not None),
        out_shape=(jax.ShapeDtypeStruct((B, T, C), jnp.float32),) * 9,
        grid=(B, nt), in_specs=specs, out_specs=(tok,) * 9,
        compiler_params=_cparams("parallel", "parallel"),
        name="rwkv_features",
    )(*args)


def _wkv_kernel(r_ref, lw_ref, kk_ref, a_ref, k_ref, v_ref, ka_ref, s0_ref, *rest, reverse, bb, n_pairs, has_prev):
    if has_prev:
        yprev_ref, y_ref, sout_ref, s_scr = rest
    else:
        y_ref, sout_ref, s_scr = rest
    C = RWKV_CHUNK
    c_idx = pl.program_id(1)

    @pl.when(c_idx == 0)
    def _():
        s_scr[...] = s0_ref[...]

    ti = lax.broadcasted_iota(jnp.int32, (C, C), 0)
    si = lax.broadcasted_iota(jnp.int32, (C, C), 1)
    tri = ((si >= ti) if reverse else (si <= ti)).astype(jnp.float32)
    t2 = lax.broadcasted_iota(jnp.int32, (2 * C, 2 * C), 0)
    s2 = lax.broadcasted_iota(jnp.int32, (2 * C, 2 * C), 1)
    m_strict = (s2 % C > t2 % C) if reverse else (s2 % C < t2 % C)
    m_incl = (s2 % C >= t2 % C) if reverse else (s2 % C <= t2 % C)
    eye = (t2 == s2).astype(jnp.float32)
    same_head = (t2 // C) == (s2 // C)
    lo = _lo_lanes(C)
    units = [(bi, slice(p * HP, (p + 1) * HP), p) for bi in range(bb) for p in range(n_pairs)]
    n = len(units)

    ar, bk, bkh, vs, e_tot = [], [], [], [], []
    for bi, sl, _ in units:
        lw = lw_ref[bi, :, sl]
        kk = kk_ref[bi, :, sl]
        ag = a_ref[bi, :, sl]
        kd = k_ref[bi, :, sl] * (1.0 + (ag - 1.0) * ka_ref[:, sl])
        cum = jnp.dot(tri, lw, precision=HIGHEST, preferred_element_type=jnp.float32)
        tot = jnp.sum(lw, axis=0, keepdims=True)
        e_neg = jnp.exp(-cum)
        e_end = jnp.exp(tot - cum)
        b = kk * ag
        ar.append(jnp.concatenate([_stack_pair(-kk * jnp.exp(cum - lw), lo),
                                   _stack_pair(r_ref[bi, :, sl] * jnp.exp(cum), lo)], axis=0).astype(BF))
        bk.append(jnp.concatenate([_stack_pair(b * e_neg, lo), _stack_pair(kd * e_neg, lo)], axis=0).astype(BF))
        bkh.append(jnp.concatenate([_stack_pair(b * e_end, lo), _stack_pair(kd * e_end, lo)], axis=0).astype(BF))
        vs.append(_stack_pair(v_ref[bi, :, sl], lo).astype(BF))
        e_tot.append(jnp.exp(tot))
    gram = [_dot_nt(ar[i], bk[i]) for i in range(n)]
    l_ab = [jnp.where(m_strict, g[:2 * C, :2 * C], 0.0) for g in gram]
    l_ak = [jnp.where(m_strict, g[:2 * C, 2 * C:], 0.0).astype(BF) for g in gram]
    l_rbk = [jnp.concatenate([jnp.where(m_incl, g[2 * C:, :2 * C], 0.0),
                              jnp.where(m_incl, g[2 * C:, 2 * C:], 0.0)], axis=1).astype(BF) for g in gram]
    s0 = [s_scr[bi, p] for bi, _, p in units]
    proj = [_dot_nt(ar[i], s0[i]) for i in range(n)]
    lv = [_bdot(l_ak[i], vs[i]) for i in range(n)]
    pw = [m.astype(BF) for m in l_ab]
    tinv = [eye + m for m in l_ab]
    for _ in range(5):
        pw = [_bdot(m, m).astype(BF) for m in pw]
        tinv = [tinv[i] + _bdot(tinv[i], pw[i]) for i in range(n)]
    u = [_bdot(tinv[i], proj[i][:2 * C] + lv[i]) for i in range(n)]
    uv = [jnp.concatenate([u[i].astype(BF), vs[i]], axis=0) for i in range(n)]
    ys = [proj[i][2 * C:] + _bdot(l_rbk[i], uv[i]) for i in range(n)]
    upd = [_dot_tn(uv[i], bkh[i]) for i in range(n)]
    for i, (bi, sl, p) in enumerate(units):
        y = ys[i][:C] + ys[i][C:]
        if has_prev:
            y = y + yprev_ref[bi, :, sl]
        y_ref[bi, :, sl] = y
        s_scr[bi, p] = s0[i] * e_tot[i] + jnp.where(same_head, upd[i], 0.0)

    @pl.when(c_idx == pl.num_programs(1) - 1)
    def _():
        sout_ref[...] = s_scr[...]


def wkv_chunked(r, lw, kk, ag, k, v, k_a, s0, y_prev, reverse):
    B, T, W = r.shape
    C = RWKV_CHUNK
    bb = WKV_SEQS_PER_STEP
    nc = T // C
    n_pairs = W // HP
    cmap = (lambda b, c: (b, nc - 1 - c, 0)) if reverse else (lambda b, c: (b, c, 0))
    tok = pl.BlockSpec((bb, C, W), cmap)
    st = pl.BlockSpec((bb, n_pairs, HP, HP), lambda b, c: (b, 0, 0, 0))
    has_prev = y_prev is not None
    args = [r, lw, kk, ag, k, v, k_a, s0] + ([y_prev] if has_prev else [])
    return pl.pallas_call(
        functools.partial(_wkv_kernel, reverse=reverse, bb=bb, n_pairs=n_pairs, has_prev=has_prev),
        out_shape=(jax.ShapeDtypeStruct((B, T, W), jnp.float32), jax.ShapeDtypeStruct(s0.shape, jnp.float32)),
        grid=(B // bb, nc),
        in_specs=[tok] * 6 + [pl.BlockSpec((1, W), lambda b, c: (0, 0)), st] + ([tok] if has_prev else []),
        out_specs=(tok, st),
        scratch_shapes=[pltpu.VMEM((bb, n_pairs, HP, HP), jnp.float32)],
        compiler_params=_cparams("parallel", "arbitrary"),
        name="wkv_chunked",
    )(*args)


def _rwkv_readout_kernel(y_ref, r_ref, k_ref, v_ref, g_ref, ag0_ref, ag1_ref, ka_ref, rk_ref, lg_ref, lb_ref,
                         ones_ref, o_ref):
    ones_bd = ones_ref[...]
    y = y_ref[0]
    mean = _split_dot(y, ones_bd, 2) * (1.0 / RWKV_HEAD)
    yc = y - mean
    var = _split_dot(yc * yc, ones_bd, 2) * (1.0 / RWKV_HEAD)
    yn = yc * lax.rsqrt(var + RWKV_LNX_EPS) * lg_ref[...] + lb_ref[...]
    ksum = k_ref[0] * (2.0 + (ag0_ref[0] + ag1_ref[0] - 2.0) * ka_ref[...])
    bonus = _split_dot(r_ref[0] * ksum * rk_ref[...], ones_bd, 2) * v_ref[0]
    o_ref[0] = (yn + bonus) * g_ref[0]


def rwkv_readout(y, r, k, v, g, ag0, ag1, k_a, r_k, lnx_g, lnx_b, tm):
    B, T, C = y.shape
    tok = pl.BlockSpec((1, tm, C), lambda b, i: (b, i, 0))
    par = pl.BlockSpec((1, C), lambda b, i: (0, 0))
    return pl.pallas_call(
        _rwkv_readout_kernel,
        out_shape=jax.ShapeDtypeStruct((B, T, C), jnp.float32), grid=(B, T // tm),
        in_specs=[tok] * 7 + [par] * 4 + [pl.BlockSpec((C, C), lambda b, i: (0, 0))],
        out_specs=tok,
        compiler_params=_cparams("parallel", "parallel"),
        name="rwkv_readout",
    )(y, r, k, v, g, ag0, ag1, k_a, r_k.reshape(1, -1), lnx_g.reshape(1, -1), lnx_b.reshape(1, -1),
      _head_ones(C, RWKV_HEAD))


def rwkv_mix(proj_x, proj_c, vf_x, vf_c, mu, w0, w2, a0, a2, k_k, k_a, r_k, g2, lnx_g, lnx_b, vres, need_ctx, tm):
    B = proj_x.shape[0]
    fx = rwkv_features(proj_x, mu, w0, w2, a0, a2, k_k, g2, vres, vf_x, tm)
    fc = rwkv_features(proj_c, mu, w0, w2, a0, a2, k_k, g2, vres, vf_c, tm)
    ka = k_a.reshape(1, -1)
    y_x = y_c = None
    for d in range(N_DIR):
        s0 = jnp.zeros((B, BRANCH_WIDTH // HP, HP, HP), jnp.float32)
        y_c, s_ctx = wkv_chunked(fc[0], fc[5 + d], fc[3], fc[7 + d], fc[1], fc[2], ka, s0, y_c, d == 1)
        y_x, _ = wkv_chunked(fx[0], fx[5 + d], fx[3], fx[7 + d], fx[1], fx[2], ka, s_ctx, y_x, d == 1)
    out_x = rwkv_readout(y_x, fx[0], fx[1], fx[2], fx[4], fx[7], fx[8], ka, r_k, lnx_g, lnx_b, tm)
    out_c = rwkv_readout(y_c, fc[0], fc[1], fc[2], fc[4], fc[7], fc[8], ka, r_k, lnx_g, lnx_b, tm) if need_ctx else None
    vf_x = fx[2] if vres is None else vf_x
    vf_c = fc[2] if vres is None else vf_c
    return out_x, out_c, vf_x, vf_c


def _qknorm_kernel(z_ref, qg_ref, kg_ref, ones_ref, q_ref, k_ref, v_ref):
    ones_bd = ones_ref[...]
    C = BRANCH_WIDTH
    z = z_ref[0]
    q = z[:, 0:C]
    k = z[:, C:2 * C]
    qn = q * lax.rsqrt(_split_dot(q * q, ones_bd, 2) * (1.0 / NA_HEAD) + RMS_EPS) * qg_ref[...]
    kn = k * lax.rsqrt(_split_dot(k * k, ones_bd, 2) * (1.0 / NA_HEAD) + RMS_EPS) * kg_ref[...]
    q_ref[0] = (qn * NA_HEAD ** -0.5).astype(BF)
    k_ref[0] = kn.astype(BF)
    v_ref[0] = z[:, 2 * C:3 * C].astype(BF)


def na_qknorm(proj, qn_g, kn_g, tm):
    B, T, _ = proj.shape
    C = BRANCH_WIDTH
    tok = pl.BlockSpec((1, tm, C), lambda b, i: (b, i, 0))
    par = pl.BlockSpec((1, C), lambda b, i: (0, 0))
    sd = jax.ShapeDtypeStruct((B, T, C), BF)
    return pl.pallas_call(
        _qknorm_kernel, out_shape=(sd, sd, sd), grid=(B, T // tm),
        in_specs=[pl.BlockSpec((1, tm, NA_COLS), lambda b, i: (b, i, NA_BLK)), par, par,
                  pl.BlockSpec((C, C), lambda b, i: (0, 0))],
        out_specs=(tok, tok, tok),
        compiler_params=_cparams("parallel", "parallel"),
        name="na_qknorm",
    )(proj, jnp.tile(qn_g, NA_HEADS).reshape(1, C), jnp.tile(kn_g, NA_HEADS).reshape(1, C), _head_ones(C, NA_HEAD))


def na_bias_table(rpb):
    qc = np.arange(GRID_W)[:, None]
    kc = np.arange(GRID_W)[None, :]
    cs = np.clip(qc - NA_WIN_C // 2, 0, GRID_W - NA_WIN_C)
    valid = (kc >= cs) & (kc < cs + NA_WIN_C)
    cidx = np.clip(kc - qc + NA_WIN_C - 1, 0, 2 * NA_WIN_C - 2)
    t = jnp.where(valid[None, None], rpb[:, :, cidx], NEG_INF)
    t2 = jnp.concatenate([t[:, :-1], t[:, 1:]], axis=-1)
    H = rpb.shape[0]
    t2 = t2.reshape(H // 2, 2, 2 * NA_WIN_R - 2, GRID_W, 2 * GRID_W).transpose(0, 2, 1, 3, 4)
    return t2.reshape(H // 2, 2 * NA_WIN_R - 2, 2 * GRID_W, 2 * GRID_W)


def _na_kernel(q_ref, k_ref, v_ref, kc_ref, vc_ref, bias_ref, o_ref, *, rows):
    r = pl.program_id(1)
    rs = jnp.clip(r - NA_WIN_R // 2, 0, rows - NA_WIN_R)
    base = rs - r + NA_WIN_R - 1
    k0 = pl.multiple_of(rs * GRID_W, GRID_W)
    nwin = NA_WIN_R * GRID_W
    lo = _lo_lanes(GRID_W)
    sls = [slice(p * HP, (p + 1) * HP) for p in range(BRANCH_WIDTH // HP)]
    qs = [_stack_pair(q_ref[0, :, sl], lo) for sl in sls]
    s_loc = [_dot_nt(qs[p], k_ref[0, pl.ds(k0, nwin), sl]) for p, sl in enumerate(sls)]
    s_ctx = [_dot_nt(qs[p], kc_ref[0, :, sl]) for p, sl in enumerate(sls)]
    p_loc, p_ctx, den = [], [], []
    for p in range(len(sls)):
        sl_b = s_loc[p] + jnp.concatenate([bias_ref[p, base + 2 * j] for j in range(NA_WIN_R // 2)], axis=1)
        m = jnp.maximum(jnp.max(sl_b, axis=1, keepdims=True), jnp.max(s_ctx[p], axis=1, keepdims=True))
        el = jnp.exp(sl_b - m)
        ec = jnp.exp(s_ctx[p] - m)
        den.append(jnp.sum(el, axis=1, keepdims=True) + jnp.sum(ec, axis=1, keepdims=True))
        p_loc.append(el.astype(BF))
        p_ctx.append(ec.astype(BF))
    o_loc = [_bdot(p_loc[p], v_ref[0, pl.ds(k0, nwin), sl]) for p, sl in enumerate(sls)]
    o_ctx = [_bdot(p_ctx[p], vc_ref[0, :, sl]) for p, sl in enumerate(sls)]
    for p, sl in enumerate(sls):
        o = (o_loc[p] + o_ctx[p]) / den[p]
        o_ref[0, :, sl] = jnp.where(lo, o[:GRID_W], o[GRID_W:])


def na_attention(q, k, v, kc, vc, bias_tab):
    B, S, C = q.shape
    rows = S // GRID_W
    n_ctx = kc.shape[1]
    seq = pl.BlockSpec((1, S, C), lambda b, r: (b, 0, 0))
    cx = pl.BlockSpec((1, n_ctx, C), lambda b, r: (b, 0, 0))
    row = pl.BlockSpec((1, GRID_W, C), lambda b, r: (b, r, 0))
    return pl.pallas_call(
        functools.partial(_na_kernel, rows=rows),
        out_shape=jax.ShapeDtypeStruct((B, S, C), jnp.float32),
        grid=(B, rows),
        in_specs=[row, seq, seq, cx, cx, pl.BlockSpec(bias_tab.shape, lambda b, r: (0, 0, 0, 0))],
        out_specs=row,
        compiler_params=_cparams("parallel", "arbitrary"),
        name="na_attention",
    )(q, k, v, kc, vc, bias_tab)


def _ctx_attn_kernel(q_ref, k_ref, v_ref, o_ref):
    n = q_ref.shape[1]
    lo = _lo_lanes(n)
    sls = [slice(p * HP, (p + 1) * HP) for p in range(BRANCH_WIDTH // HP)]
    sc = [_dot_nt(_stack_pair(q_ref[0, :, sl], lo), k_ref[0, :, sl]) for sl in sls]
    e = [jnp.exp(x - jnp.max(x, axis=1, keepdims=True)) for x in sc]
    o = [_bdot(e[p], v_ref[0, :, sl]) / jnp.sum(e[p], axis=1, keepdims=True) for p, sl in enumerate(sls)]
    for p, sl in enumerate(sls):
        o_ref[0, :, sl] = jnp.where(lo, o[p][:n], o[p][n:])


def ctx_attention(q, k, v):
    B, n, C = q.shape
    blk = pl.BlockSpec((1, n, C), lambda b: (b, 0, 0))
    return pl.pallas_call(
        _ctx_attn_kernel, out_shape=jax.ShapeDtypeStruct((B, n, C), jnp.float32), grid=(B,),
        in_specs=[blk, blk, blk], out_specs=blk,
        compiler_params=_cparams("parallel"),
        name="ctx_attention",
    )(q, k, v)


def na_mix(proj_x, proj_c, qn_g, kn_g, rpb, need_ctx, tm):
    q, k, v = na_qknorm(proj_x, qn_g, kn_g, tm)
    qc, kc, vc = na_qknorm(proj_c, qn_g, kn_g, tm)
    out_x = na_attention(q, k, v, kc, vc, na_bias_table(rpb))
    out_c = ctx_attention(qc, kc, vc) if need_ctx else None
    return out_x, out_c


def rope_tables(n_tokens):
    t = jnp.arange(n_tokens)
    pos = jnp.stack([t // GRID_W, t % GRID_W], axis=-1).astype(jnp.float32)
    inv = ROPE_BASE ** (-jnp.arange(ROPE_ROT, dtype=jnp.float32) / ROPE_ROT)
    ang = pos[:, :, None] * inv
    cos_h = jnp.concatenate([jnp.cos(ang), jnp.cos(ang)], axis=-1).reshape(n_tokens, MLSTM_DQK)
    sin_h = jnp.concatenate([-jnp.sin(ang), jnp.sin(ang)], axis=-1).reshape(n_tokens, MLSTM_DQK)
    col = np.arange(MLSTM_QK)
    partner = np.where((col % (2 * ROPE_ROT)) < ROPE_ROT, col + ROPE_ROT, col - ROPE_ROT)
    perm = np.zeros((MLSTM_QK, MLSTM_QK), np.float32)
    perm[partner, col] = 1.0
    return jnp.tile(cos_h, (1, MLSTM_HEADS)), jnp.tile(sin_h, (1, MLSTM_HEADS)), jnp.asarray(perm, BF)


def _mlstm_prep_kernel(z_ref, gp_ref, ib_ref, fb_ref, *rest, rope):
    if rope:
        cos_ref, sin_ref, perm_ref, q_ref, k_ref, v_ref, g_ref = rest
    else:
        q_ref, k_ref, v_ref, g_ref = rest
    z = z_ref[0]
    q = z[:, 0:MLSTM_QK]
    k = z[:, MLSTM_QK:2 * MLSTM_QK]
    if rope:
        perm = perm_ref[...]
        q = q * cos_ref[...] + _split_dot(q, perm, 3) * sin_ref[...]
        k = k * cos_ref[...] + _split_dot(k, perm, 3) * sin_ref[...]
    q_ref[0] = (q * MLSTM_DQK ** -0.5).astype(BF)
    k_ref[0] = k.astype(BF)
    v_ref[0] = z[:, 2 * MLSTM_QK:2 * MLSTM_QK + BRANCH_WIDTH].astype(BF)
    gp = gp_ref[0]
    lane = lax.broadcasted_iota(jnp.int32, gp.shape, 1)
    ig = GATE_CAP * jnp.tanh((gp + ib_ref[...]) / GATE_CAP)
    fg = GATE_CAP * jnp.tanh((gp + fb_ref[...]) / GATE_CAP)
    g_ref[0] = jnp.where(lane < N_DIR * MLSTM_HEADS, ig, jax.nn.log_sigmoid(fg))


def mlstm_prep(proj, i_bias, f_bias, rope, tm):
    B, T, _ = proj.shape
    ng = N_DIR * MLSTM_HEADS
    ib = jnp.zeros((1, V7X_LANES), jnp.float32).at[0, 0:ng].set(i_bias.reshape(-1))
    fb = jnp.zeros((1, V7X_LANES), jnp.float32).at[0, ng:2 * ng].set(f_bias.reshape(-1))
    tok = lambda w: pl.BlockSpec((1, tm, w), lambda b, i: (b, i, 0))
    par = pl.BlockSpec((1, V7X_LANES), lambda b, i: (0, 0))
    args = [proj, proj, ib, fb]
    specs = [pl.BlockSpec((1, tm, MLSTM_MAIN_COLS), lambda b, i: (b, i, ML_BLK)),
             pl.BlockSpec((1, tm, V7X_LANES), lambda b, i: (b, i, ML_GATE_BLK)), par, par]
    if rope is not None:
        args += list(rope)
        specs += [pl.BlockSpec((tm, MLSTM_QK), lambda b, i: (i, 0)), pl.BlockSpec((tm, MLSTM_QK), lambda b, i: (i, 0)),
                  pl.BlockSpec((MLSTM_QK, MLSTM_QK), lambda b, i: (0, 0))]
    return pl.pallas_call(
        functools.partial(_mlstm_prep_kernel, rope=rope is not None),
        out_shape=(jax.ShapeDtypeStruct((B, T, MLSTM_QK), BF), jax.ShapeDtypeStruct((B, T, MLSTM_QK), BF),
                   jax.ShapeDtypeStruct((B, T, BRANCH_WIDTH), BF), jax.ShapeDtypeStruct((B, T, V7X_LANES), jnp.float32)),
        grid=(B, T // tm), in_specs=specs,
        out_specs=(tok(MLSTM_QK), tok(MLSTM_QK), tok(BRANCH_WIDTH), tok(V7X_LANES)),
        compiler_params=_cparams("parallel", "parallel"),
        name="mlstm_prep",
    )(*args)


def _mlstm_kernel(q_ref, k_ref, v_ref, gc_ref, gr_ref, c0_ref, n0_ref, m0_ref, *rest, reverse, direction, has_prev):
    if has_prev:
        hprev_ref, h_ref, cout_ref, nout_ref, mout_ref, c_scr, n_scr, m_scr = rest
    else:
        h_ref, cout_ref, nout_ref, mout_ref, c_scr, n_scr, m_scr = rest
    L = MLSTM_CHUNK
    DV = MLSTM_DV
    cidx = pl.program_id(1)

    @pl.when(cidx == 0)
    def _():
        c_scr[...] = c0_ref[0]
        n_scr[...] = n0_ref[0]
        m_scr[...] = m0_ref[0]

    ti = lax.broadcasted_iota(jnp.int32, (L, L), 0)
    si = lax.broadcasted_iota(jnp.int32, (L, L), 1)
    tri = ((si >= ti) if reverse else (si <= ti)).astype(jnp.float32)
    t2 = lax.broadcasted_iota(jnp.int32, (2 * L, L), 0)
    s2 = lax.broadcasted_iota(jnp.int32, (2 * L, L), 1)
    causal = (s2 >= t2 % L) if reverse else (s2 <= t2 % L)
    top = t2 < L
    top_col = lax.broadcasted_iota(jnp.int32, (2 * L, 1), 0) < L
    lo = _lo_lanes(L)
    lane1 = lax.broadcasted_iota(jnp.int32, (1, HP), 1)
    lo_row = lane1 < (HP // 2)
    lo_col = lax.broadcasted_iota(jnp.int32, (HP, 1), 0) < (HP // 2)
    col = lambda a, c: a[:, c:c + 1]
    rowv = lambda a, c: a[c:c + 1, :]

    gc = gc_ref[0]
    gr = gr_ref[0, 0]
    bcol_all = jnp.dot(tri, gc, precision=HIGHEST, preferred_element_type=jnp.float32)
    brow_all = lax.dot_general(gr, tri, (((1,), (1,)), ((), ())), precision=HIGHEST, preferred_element_type=jnp.float32)
    tot_all = jnp.sum(gc, axis=0, keepdims=True)
    m_all = m_scr[...]
    for p in range(MLSTM_HEADS // 2):
        h0, h1 = 2 * p, 2 * p + 1
        ci = lambda h: direction * MLSTM_HEADS + h
        cf = lambda h: N_DIR * MLSTM_HEADS + direction * MLSTM_HEADS + h
        bcol_s = jnp.concatenate([col(bcol_all, cf(h0)), col(bcol_all, cf(h1))], axis=0)
        icol_s = jnp.concatenate([col(gc, ci(h0)), col(gc, ci(h1))], axis=0)
        brow_s = jnp.where(top, rowv(brow_all, cf(h0)), rowv(brow_all, cf(h1)))
        irow_s = jnp.where(top, rowv(gr, ci(h0)), rowv(gr, ci(h1)))
        m0, m1 = col(m_all, h0), col(m_all, h1)
        mprev_s = jnp.where(top_col, m0, m1)
        tot0, tot1 = col(tot_all, cf(h0)), col(tot_all, cf(h1))
        tot_s = jnp.where(top_col, tot0, tot1)

        sl = slice(p * HP, (p + 1) * HP)
        qs = _stack_pair(q_ref[0, :, sl], lo)
        kp = k_ref[0, :, sl]
        v0 = v_ref[0, :, h0 * DV:(h0 + 1) * DV]
        v1 = v_ref[0, :, h1 * DV:(h1 + 1) * DV]
        c_pair = c_scr[p]
        n_pair = n_scr[p]

        dmat = jnp.where(causal, bcol_s - brow_s + irow_s, NEG_INF)
        inter = bcol_s + mprev_s
        m_t = jnp.maximum(inter, jnp.max(dmat, axis=1, keepdims=True))
        smat = _dot_nt(qs, kp) * jnp.exp(dmat - m_t)
        sc_inter = jnp.exp(inter - m_t)
        sb = smat.astype(BF)
        sv = jnp.concatenate([_bdot(sb[:L], v0), _bdot(sb[L:], v1)], axis=0)
        num = sc_inter * _bdot(qs, c_pair) + sv
        qn = jnp.sum(qs.astype(jnp.float32) * n_pair, axis=1, keepdims=True)
        den = sc_inter * qn + jnp.sum(smat, axis=1, keepdims=True)
        hout = num / jnp.maximum(jnp.abs(den), jnp.exp(-m_t))
        for h, part in ((h0, hout[:L]), (h1, hout[L:])):
            hs = slice(h * DV, (h + 1) * DV)
            h_ref[0, :, hs] = (part + hprev_ref[0, :, hs]) if has_prev else part

        wlog_s = tot_s - bcol_s + icol_s
        mn0 = jnp.maximum(tot0 + m0, jnp.max(wlog_s[:L], axis=0, keepdims=True))
        mn1 = jnp.maximum(tot1 + m1, jnp.max(wlog_s[L:], axis=0, keepdims=True))
        dec0 = jnp.exp(tot0 + m0 - mn0)
        dec1 = jnp.exp(tot1 + m1 - mn1)
        wexp_s = jnp.exp(wlog_s - jnp.where(top_col, mn0, mn1))
        kws = _stack_pair(kp, lo).astype(jnp.float32) * wexp_s
        vst = jnp.concatenate([v0, v1], axis=0)
        c_scr[p] = jnp.where(lo_col, dec0, dec1) * c_pair + _dot_tn(kws, vst)
        n_scr[p] = jnp.where(lo_row, dec0, dec1) * n_pair + jnp.sum(kws, axis=0, keepdims=True)
        m_all = jnp.where(lane1 == h0, mn0, jnp.where(lane1 == h1, mn1, m_all))
    m_scr[...] = m_all

    @pl.when(cidx == pl.num_programs(1) - 1)
    def _():
        cout_ref[0] = c_scr[...]
        nout_ref[0] = n_scr[...]
        mout_ref[0] = m_scr[...]


def mlstm_chunked(q, k, v, gates, state, h_prev, direction):
    B, T, _ = q.shape
    L = MLSTM_CHUNK
    nc = T // L
    reverse = direction == 1
    g_rows = gates[:, :, :MLSTM_GATES].reshape(B, nc, L, MLSTM_GATES).transpose(0, 1, 3, 2)
    cm = (lambda c: nc - 1 - c) if reverse else (lambda c: c)
    tok = lambda w: pl.BlockSpec((1, L, w), lambda b, c: (b, cm(c), 0))
    st = lambda a: pl.BlockSpec((1,) + a.shape[1:], lambda b, c: (b,) + (0,) * (a.ndim - 1))
    c0, n0, m0 = state
    has_prev = h_prev is not None
    args = [q, k, v, gates, g_rows, c0, n0, m0] + ([h_prev] if has_prev else [])
    outs = pl.pallas_call(
        functools.partial(_mlstm_kernel, reverse=reverse, direction=direction, has_prev=has_prev),
        out_shape=(jax.ShapeDtypeStruct((B, T, BRANCH_WIDTH), jnp.float32),) + tuple(
            jax.ShapeDtypeStruct(a.shape, jnp.float32) for a in state),
        grid=(B, nc),
        in_specs=[tok(MLSTM_QK), tok(MLSTM_QK), tok(BRANCH_WIDTH), tok(V7X_LANES),
                  pl.BlockSpec((1, 1, MLSTM_GATES, L), lambda b, c: (b, cm(c), 0, 0)), st(c0), st(n0), st(m0)]
                 + ([tok(BRANCH_WIDTH)] if has_prev else []),
        out_specs=(tok(BRANCH_WIDTH), st(c0), st(n0), st(m0)),
        scratch_shapes=[pltpu.VMEM(a.shape[1:], jnp.float32) for a in state],
        compiler_params=_cparams("parallel", "arbitrary"),
        name="mlstm_chunked",
    )(*args)
    return outs[0], outs[1:]


def _mlstm_readout_kernel(h_ref, o_ref, g_ref, out_ref):
    for h in range(MLSTM_HEADS):
        sl = slice(h * MLSTM_DV, (h + 1) * MLSTM_DV)
        x = h_ref[0, :, sl]
        xn = x * lax.rsqrt(jnp.mean(x * x, axis=1, keepdims=True) + RMS_EPS)
        out_ref[0, :, sl] = xn * g_ref[:, sl] * jax.nn.sigmoid(o_ref[0, :, sl])


def mlstm_readout(h, proj, norm_g, tm):
    B, T, C = h.shape
    tok = pl.BlockSpec((1, tm, C), lambda b, i: (b, i, 0))
    return pl.pallas_call(
        _mlstm_readout_kernel, out_shape=jax.ShapeDtypeStruct((B, T, C), jnp.float32), grid=(B, T // tm),
        in_specs=[tok, pl.BlockSpec((1, tm, C), lambda b, i: (b, i, ML_OGATE_BLK)), pl.BlockSpec((1, C), lambda b, i: (0, 0))],
        out_specs=tok,
        compiler_params=_cparams("parallel", "parallel"),
        name="mlstm_readout",
    )(h, proj, norm_g.reshape(1, C))


def mlstm_mix(proj_x, proj_c, i_bias, f_bias, norm_g, rope, need_ctx, tm):
    B = proj_x.shape[0]
    qx, kx, vx, gx = mlstm_prep(proj_x, i_bias, f_bias, rope, tm)
    qc, kc, vc, gc = mlstm_prep(proj_c, i_bias, f_bias, None, tm)
    h_x = h_c = None
    for d in range(N_DIR):
        st0 = (jnp.zeros((B, MLSTM_HEADS // 2, HP, HP), jnp.float32), jnp.zeros((B, MLSTM_HEADS // 2, 1, HP), jnp.float32),
               jnp.zeros((B, 1, HP), jnp.float32))
        h_c, st_ctx = mlstm_chunked(qc, kc, vc, gc, st0, h_c, d)
        h_x, _ = mlstm_chunked(qx, kx, vx, gx, st_ctx, h_x, d)
    out_x = mlstm_readout(h_x, proj_x, norm_g, tm)
    out_c = mlstm_readout(h_c, proj_c, norm_g, tm) if need_ctx else None
    return out_x, out_c


def _merge_kernel(ya_ref, yb_ref, yc_ref, ga_ref, gb_ref, gc_ref, x_ref, mod_ref, wb_ref, wo_ref, o_ref):
    merged = None
    for i, (y_ref, g_ref) in enumerate(((ya_ref, ga_ref), (yb_ref, gb_ref), (yc_ref, gc_ref))):
        t = jax.nn.sigmoid(g_ref[0]) * _bdot(y_ref[0], wb_ref[i])
        merged = t if merged is None else merged + t
    o_ref[0] = x_ref[0] + mod_ref[0, 2:3, :] * _bdot(merged, wo_ref[...])


def merge_apply(ya, yb, yc, proj, x, mod, w_branch, w_out, tm):
    B, T, D = x.shape
    tok = lambda w: pl.BlockSpec((1, tm, w), lambda b, i: (b, i, 0))
    gate = lambda k: pl.BlockSpec((1, tm, D), lambda b, i: (b, i, GATE_BLK0 + k))
    return pl.pallas_call(
        _merge_kernel, out_shape=jax.ShapeDtypeStruct((B, T, D), jnp.float32), grid=(B, T // tm),
        in_specs=[tok(BRANCH_WIDTH)] * 3 + [gate(0), gate(1), gate(2), tok(D), pl.BlockSpec((1, 6, D), lambda b, i: (b, 0, 0)),
                  pl.BlockSpec(w_branch.shape, lambda b, i: (0, 0, 0)), pl.BlockSpec(w_out.shape, lambda b, i: (0, 0))],
        out_specs=tok(D),
        compiler_params=_cparams("parallel", "parallel"),
        name="merge_branches",
    )(ya, yb, yc, proj, proj, proj, x, mod, w_branch, w_out)


def _route_kernel(x_ref, g_ref, mod_ref, wr_ref, rb_ref, h_ref, gate_ref):
    h = _norm_mod(x_ref[0], g_ref[...], mod_ref[0, 4:5, :], mod_ref[0, 3:4, :])
    h_ref[0] = h.astype(BF)
    tm = h.shape[0]
    logits = lax.dot_general(wr_ref[...], h, (((1,), (1,)), ((), ())), precision=HIGHEST, preferred_element_type=jnp.float32)
    scores = jax.nn.sigmoid(logits)
    sel = scores + rb_ref[...]
    gsz = N_EXPERTS // N_GROUPS
    grp = sel.reshape(N_GROUPS, gsz, tm)
    iota_in = lax.broadcasted_iota(jnp.int32, grp.shape, 1)
    m1 = jnp.max(grp, axis=1, keepdims=True)
    first = jnp.min(jnp.where(grp == m1, iota_in, gsz), axis=1, keepdims=True)
    m2 = jnp.max(jnp.where(iota_in == first, -jnp.inf, grp), axis=1, keepdims=True)
    gscore = (m1 + m2).reshape(N_GROUPS, tm)
    gi = lax.broadcasted_iota(jnp.int32, (N_GROUPS, tm), 0)
    rank = jnp.zeros((N_GROUPS, tm), jnp.int32)
    for g2 in range(N_GROUPS):
        other = gscore[g2:g2 + 1, :]
        rank = rank + ((other > gscore) | ((other == gscore) & (g2 < gi))).astype(jnp.int32)
    gmask = rank < TOPK_GROUPS
    emask = jnp.broadcast_to(gmask.reshape(N_GROUPS, 1, tm), (N_GROUPS, gsz, tm)).reshape(N_EXPERTS, tm)
    cand = jnp.where(emask, sel, NEG_INF)
    ei = lax.broadcasted_iota(jnp.int32, (N_EXPERTS, tm), 0)
    chosen = jnp.zeros((N_EXPERTS, tm), jnp.bool_)
    for _ in range(TOP_K):
        mx = jnp.max(cand, axis=0, keepdims=True)
        idx = jnp.min(jnp.where(cand == mx, ei, N_EXPERTS), axis=0, keepdims=True)
        hit = ei == idx
        chosen = chosen | hit
        cand = jnp.where(hit, -jnp.inf, cand)
    w = jnp.where(chosen, scores, 0.0)
    w = w / jnp.sum(w, axis=0, keepdims=True) * ROUTED_SCALE
    gate_ref[0] = jnp.concatenate([w, jnp.zeros((V7X_LANES - N_EXPERTS, tm), jnp.float32)], axis=0).T


def moe_route(x, g, mod, w_router, router_bias, tm):
    B, T, D = x.shape
    tok = pl.BlockSpec((1, tm, D), lambda b, i: (b, i, 0))
    return pl.pallas_call(
        _route_kernel,
        out_shape=(jax.ShapeDtypeStruct((B, T, D), BF), jax.ShapeDtypeStruct((B, T, V7X_LANES), jnp.float32)),
        grid=(B, T // tm),
        in_specs=[tok, pl.BlockSpec((1, D), lambda b, i: (0, 0)), pl.BlockSpec((1, 6, D), lambda b, i: (b, 0, 0)),
                  pl.BlockSpec((N_EXPERTS, D), lambda b, i: (0, 0)), pl.BlockSpec((N_EXPERTS, 1), lambda b, i: (0, 0))],
        out_specs=(tok, pl.BlockSpec((1, tm, V7X_LANES), lambda b, i: (b, i, 0))),
        compiler_params=_cparams("parallel", "parallel"),
        name="moe_route",
    )(x, g.reshape(1, D), mod, w_router.T, router_bias.reshape(N_EXPERTS, 1))


def _moe_kernel(h_ref, gate_ref, x_ref, mod_ref, ex_ref, wg_ref, wu_ref, wd_ref, sg_ref, su_ref, sd_ref, o_ref, acc_ref):
    j = pl.program_id(2)
    h = h_ref[0]

    @pl.when(j == 0)
    def _():
        sh = jax.nn.silu(_bdot(h, sg_ref[...])) * _bdot(h, su_ref[...])
        acc_ref[...] = _bdot(sh, sd_ref[...])

    gexp = _split_dot(gate_ref[0], ex_ref[0], 2)
    act = jax.nn.silu(_bdot(h, wg_ref[...])) * _bdot(h, wu_ref[...]) * gexp
    acc_ref[...] += _bdot(act, wd_ref[...])

    @pl.when(j == pl.num_programs(2) - 1)
    def _():
        o_ref[0] = x_ref[0] + mod_ref[0, 5:6, :] * acc_ref[...]


def _moe_expand_table():
    se = MOE_STEP_EXPERTS
    t = np.zeros((N_EXPERTS // se, V7X_LANES, se * D_EXPERT), np.float32)
    for j in range(N_EXPERTS // se):
        for e in range(se):
            t[j, j * se + e, e * D_EXPERT:(e + 1) * D_EXPERT] = 1.0
    return jnp.asarray(t, BF)


def moe_apply(h2, gates, x, mod, wg, wu, wd, sg, su, sd, tm):
    B, T, D = x.shape
    sw = MOE_STEP_EXPERTS * D_EXPERT
    tok = pl.BlockSpec((1, tm, D), lambda b, i, j: (b, i, 0))
    full = lambda a: pl.BlockSpec(a.shape, lambda b, i, j: (0,) * a.ndim)
    return pl.pallas_call(
        _moe_kernel, out_shape=jax.ShapeDtypeStruct((B, T, D), jnp.float32),
        grid=(B, T // tm, N_EXPERTS // MOE_STEP_EXPERTS),
        in_specs=[tok, pl.BlockSpec((1, tm, V7X_LANES), lambda b, i, j: (b, i, 0)), tok,
                  pl.BlockSpec((1, 6, D), lambda b, i, j: (b, 0, 0)),
                  pl.BlockSpec((1, V7X_LANES, sw), lambda b, i, j: (j, 0, 0)),
                  pl.BlockSpec((D, sw), lambda b, i, j: (0, j)), pl.BlockSpec((D, sw), lambda b, i, j: (0, j)),
                  pl.BlockSpec((sw, D), lambda b, i, j: (j, 0)), full(sg), full(su), full(sd)],
        out_specs=tok,
        scratch_shapes=[pltpu.VMEM((tm, D), jnp.float32)],
        compiler_params=_cparams("parallel", "parallel", "arbitrary"),
        name="moe_experts",
    )(h2, gates, x, mod, _moe_expand_table(), wg, wu, wd, sg, su, sd)


def _reorder_w_in(w):
    o_na = RWKV_COLS
    o_ml = o_na + NA_COLS
    o_mg = o_ml + MLSTM_MAIN_COLS
    o_gate = o_mg + MLSTM_GATES
    pad = jnp.zeros((w.shape[0], V7X_LANES - MLSTM_GATES), w.dtype)
    return jnp.concatenate([w[:, o_na:o_ml], w[:, o_ml:o_mg], w[:, o_gate:], w[:, :RWKV_COLS], w[:, o_mg:o_gate], pad],
                           axis=1).astype(BF)


def kernel(x, c, ctx, c_ctx, w_ada, b_ada, norm1_g, norm2_g, w_in, rw_mu, rw_w0, rw_w2, rw_a0, rw_a2, rw_k_k, rw_k_a, rw_r_k, rw_g2, rw_lnx_g, rw_lnx_b, rw_v0, rw_v1, rw_v2, na_qn_g, na_kn_g, na_rpb, ml_i_bias, ml_f_bias, ml_norm_g, w_branch, w_out, moe_router, moe_bias, moe_w_gate, moe_w_up, moe_w_down, sh_w_gate, sh_w_up, sh_w_down):
    B, S, D = x.shape
    n_ctx = ctx.shape[1]
    tm = 256
    assert S % tm == 0 and n_ctx % tm == 0 and PROJ_COLS == IN_COLS + V7X_LANES - MLSTM_GATES
    rope = rope_tables(S)
    n_cond = B + 1
    cond_pad = (-n_cond) % V7X_SUBLANES
    s_cond = jnp.pad(jnp.concatenate([jax.nn.silu(c), jax.nn.silu(c_ctx)[None]], axis=0), ((0, cond_pad), (0, 0)))
    vf_x = vf_c = None
    for l in range(DEPTH):
        need_ctx = l < DEPTH - 1
        mod = pmm(s_cond, w_ada[l]) + b_ada[l]
        mod_x = mod[:B].reshape(B, 6, D)
        mod_c = jnp.broadcast_to(mod[B].reshape(1, 6, D), (B, 6, D))
        w_proj = _reorder_w_in(w_in[l])
        hx = norm_mod(x, norm1_g[l], mod_x, 0, tm)
        hc = norm_mod(ctx, norm1_g[l], mod_c, 0, tm)
        proj_x = pmm(hx.reshape(B * S, D), w_proj).reshape(B, S, PROJ_COLS)
        proj_c = pmm(hc.reshape(B * n_ctx, D), w_proj).reshape(B, n_ctx, PROJ_COLS)
        vres = None if l == 0 else (rw_v0[l - 1], rw_v1[l - 1], rw_v2[l - 1])
        ya_x, ya_c, vf_x, vf_c = rwkv_mix(proj_x, proj_c, vf_x, vf_c, rw_mu[l], rw_w0[l], rw_w2[l], rw_a0[l], rw_a2[l],
                                          rw_k_k[l], rw_k_a[l], rw_r_k[l], rw_g2[l], rw_lnx_g[l], rw_lnx_b[l],
                                          vres, need_ctx, tm)
        yb_x, yb_c = na_mix(proj_x, proj_c, na_qn_g[l], na_kn_g[l], na_rpb[l], need_ctx, tm)
        yc_x, yc_c = mlstm_mix(proj_x, proj_c, ml_i_bias[l], ml_f_bias[l], ml_norm_g[l], rope, need_ctx, tm)
        wb = w_branch[l].astype(BF)
        wo = w_out[l].astype(BF)
        wg = moe_w_gate[l].transpose(1, 0, 2).reshape(D, N_EXPERTS * D_EXPERT).astype(BF)
        wu = moe_w_up[l].transpose(1, 0, 2).reshape(D, N_EXPERTS * D_EXPERT).astype(BF)
        wd = moe_w_down[l].reshape(N_EXPERTS * D_EXPERT, D).astype(BF)
        shared = (sh_w_gate[l].astype(BF), sh_w_up[l].astype(BF), sh_w_down[l].astype(BF))
        x = merge_apply(ya_x, yb_x, yc_x, proj_x, x, mod_x, wb, wo, tm)
        h2, gates = moe_route(x, norm2_g[l], mod_x, moe_router[l], moe_bias[l], tm)
        x = moe_apply(h2, gates, x, mod_x, wg, wu, wd, *shared, tm)
        if need_ctx:
            ctx = merge_apply(ya_c, yb_c, yc_c, proj_c, ctx, mod_c, wb, wo, tm)
            h2, gates = moe_route(ctx, norm2_g[l], mod_c, moe_router[l], moe_bias[l], tm)
            ctx = moe_apply(h2, gates, ctx, mod_c, wg, wu, wd, *shared, tm)
    return x
```

```python
import functools

import numpy as np
import jax
import jax.numpy as jnp
from jax import lax
from jax.experimental import pallas as pl
from jax.experimental.pallas import tpu as pltpu

D_MODEL = 1024
DEPTH = 2
GRID_W = 64
N_DIR = 2
N_BRANCH = 3
BRANCH_WIDTH = 512
RMS_EPS = 1e-6
NEG_INF = -1e30

RWKV_HEAD = 64
RWKV_DECAY_LORA = 64
RWKV_LORA_COLS = 384
RWKV_COLS = 3 * BRANCH_WIDTH + RWKV_LORA_COLS
RWKV_LNX_EPS = 64e-5
RWKV_CHUNK = 64
WKV_SEQS_PER_STEP = 4

NA_HEAD = 64
NA_HEADS = BRANCH_WIDTH // NA_HEAD
NA_WIN_R = 8
NA_WIN_C = 16
NA_COLS = 3 * BRANCH_WIDTH
NA_ROWS_PER_STEP = 2

MLSTM_HEADS = 4
MLSTM_DQK = 64
MLSTM_DV = BRANCH_WIDTH // MLSTM_HEADS
MLSTM_QK = MLSTM_HEADS * MLSTM_DQK
MLSTM_CHUNK = 64
MLSTM_SEQS_PER_STEP = 4
MLSTM_MAIN_COLS = 2 * MLSTM_QK + 2 * BRANCH_WIDTH
MLSTM_GATES = 2 * N_DIR * MLSTM_HEADS
GATE_CAP = 15.0
ROPE_ROT = MLSTM_DQK // 4
ROPE_BASE = 10000.0

N_EXPERTS = 64
TOP_K = 8
N_GROUPS = 8
TOPK_GROUPS = 4
D_EXPERT = 128
ROUTED_SCALE = 2.5
MOE_STEP_EXPERTS = 8
MOE_TOKEN_TILE = 512

GATE_COLS = N_BRANCH * D_MODEL
IN_COLS = RWKV_COLS + NA_COLS + MLSTM_MAIN_COLS + MLSTM_GATES + GATE_COLS

V7X_LANES = 128
V7X_SUBLANES = 8
HP = 2 * RWKV_HEAD
VMEM_LIMIT_BYTES = 48 * 1024 * 1024
MOE_VMEM_LIMIT_BYTES = 56 * 1024 * 1024

PROJ_COLS = 8192
NA_BLK = 0
ML_BLK = 1
ML_OGATE_BLK = (NA_COLS + 2 * MLSTM_QK + BRANCH_WIDTH) // BRANCH_WIDTH
GATE_BLK0 = (NA_COLS + MLSTM_MAIN_COLS) // D_MODEL
RWKV_COL0 = NA_COLS + MLSTM_MAIN_COLS + GATE_COLS
RWKV_BLK0 = RWKV_COL0 // BRANCH_WIDTH
RWKV_LORA_BLK = (RWKV_COL0 + 3 * BRANCH_WIDTH) // RWKV_LORA_COLS
ML_GATE_BLK = (RWKV_COL0 + RWKV_COLS) // V7X_LANES

BF = jnp.bfloat16
HIGHEST = lax.Precision.HIGHEST


def _cparams(*sem):
    return pltpu.CompilerParams(dimension_semantics=sem, vmem_limit_bytes=VMEM_LIMIT_BYTES)


def _bdot(a, b):
    return jnp.dot(a.astype(BF), b.astype(BF), preferred_element_type=jnp.float32)


def _dot_nt(a, b):
    return lax.dot_general(a.astype(BF), b.astype(BF), (((1,), (1,)), ((), ())), preferred_element_type=jnp.float32)


def _dot_tn(a, b):
    return lax.dot_general(a.astype(BF), b.astype(BF), (((0,), (0,)), ((), ())), preferred_element_type=jnp.float32)


def _split_dot(x, w, parts):
    out = None
    rem = x
    for _ in range(parts):
        piece = rem.astype(BF)
        rem = rem - piece.astype(jnp.float32)
        t = jnp.dot(piece, w, preferred_element_type=jnp.float32)
        out = t if out is None else out + t
    return out


def _head_ones(width, head):
    i = np.arange(width) // head
    return jnp.asarray(i[:, None] == i[None, :], BF)


def _stack_pair(x, lo):
    zero = jnp.zeros_like(x)
    return jnp.concatenate([jnp.where(lo, x, zero), jnp.where(lo, zero, x)], axis=0)


def _lo_lanes(n):
    return lax.broadcasted_iota(jnp.int32, (n, HP), 1) < (HP // 2)


def _mm_kernel(x_ref, w_ref, o_ref):
    part = _bdot(x_ref[...], w_ref[...])

    @pl.when(pl.program_id(2) == 0)
    def _():
        o_ref[...] = part

    @pl.when(pl.program_id(2) > 0)
    def _():
        o_ref[...] += part


def _pick_tile(n, cands):
    for c in cands:
        if n % c == 0:
            return c
    return n


def pmm(x, w):
    M, K = x.shape
    N = w.shape[1]
    tm = _pick_tile(M, (1024, 512, 256, 128, 64, 32, 16, 8))
    tn = _pick_tile(N, (1024, 512, 384, 256, 128))
    tk = _pick_tile(K, (1024,)) if K > 1024 else K
    return pl.pallas_call(
        _mm_kernel,
        out_shape=jax.ShapeDtypeStruct((M, N), jnp.float32),
        grid=(M // tm, N // tn, K // tk),
        in_specs=[pl.BlockSpec((tm, tk), lambda i, j, k: (i, k)),
                  pl.BlockSpec((tk, tn), lambda i, j, k: (k, j))],
        out_specs=pl.BlockSpec((tm, tn), lambda i, j, k: (i, j)),
        compiler_params=_cparams("parallel", "parallel", "arbitrary"),
        name="tiled_matmul",
    )(x, w)


def _norm_mod(x, g, scale, shift):
    xn = x * lax.rsqrt(jnp.mean(x * x, axis=-1, keepdims=True) + RMS_EPS)
    return xn * g * (1.0 + scale) + shift


def _norm_mod_kernel(x_ref, g_ref, mod_ref, o_ref, *, row):
    o_ref[0] = _norm_mod(x_ref[0], g_ref[...], mod_ref[0, row + 1:row + 2, :], mod_ref[0, row:row + 1, :]).astype(BF)


def norm_mod(x, g, mod, row, tm):
    B, T, D = x.shape
    tok = pl.BlockSpec((1, tm, D), lambda b, i: (b, i, 0))
    return pl.pallas_call(
        functools.partial(_norm_mod_kernel, row=row),
        out_shape=jax.ShapeDtypeStruct((B, T, D), BF), grid=(B, T // tm),
        in_specs=[tok, pl.BlockSpec((1, D), lambda b, i: (0, 0)), pl.BlockSpec((1, 6, D), lambda b, i: (b, 0, 0))],
        out_specs=tok,
        compiler_params=_cparams("parallel", "parallel"),
        name="norm_mod",
    )(x, g.reshape(1, D), mod)


def _shifted(z, prev_row, next_row):
    n = z.shape[0]
    row = lax.broadcasted_iota(jnp.int32, z.shape, 0)
    zp = jnp.where(row == 0, prev_row, pltpu.roll(z, 1, axis=0))
    zn = jnp.where(row == n - 1, next_row, pltpu.roll(z, n - 1, axis=0))
    return zp, zn


def _rwkv_feat_kernel(zr_ref, zk_ref, zv_ref, zl_ref, prev_ref, next_ref, mu_ref, w0_ref, w2_ref, a0_ref, a2_ref,
                      kk_ref, g2_ref, ones_ref, *rest, has_vres):
    if has_vres:
        v0_ref, v1_ref, v2_ref, vf_ref = rest[:4]
        outs = rest[4:]
    else:
        outs = rest
    r_ref, k_ref, v_ref, kn_ref, g_ref, lw0_ref, lw1_ref, ag0_ref, ag1_ref = outs
    C = BRANCH_WIDTH

    def shift(z_ref, c0, c1):
        z = z_ref[0]
        zp, zn = _shifted(z, prev_ref[0, 0, :, c0:c1], next_ref[0, 0, :, c0:c1])
        return z + mu_ref[:, c0:c1] * (0.5 * (zp + zn) - z)

    r = shift(zr_ref, 0, C)
    k = shift(zk_ref, C, 2 * C)
    v = shift(zv_ref, 2 * C, 3 * C)
    zl = shift(zl_ref, 3 * C, RWKV_COLS)
    wd = jnp.tanh(zl[:, 0:2 * RWKV_DECAY_LORA])
    ad = zl[:, 2 * RWKV_DECAY_LORA:4 * RWKV_DECAY_LORA]
    gd = zl[:, 4 * RWKV_DECAY_LORA:]
    for d, (lw_ref, ag_ref) in enumerate(((lw0_ref, ag0_ref), (lw1_ref, ag1_ref))):
        wl = -jax.nn.softplus(-(w0_ref[d:d + 1, :] + _bdot(wd, w2_ref[d]))) - 0.5
        lw_ref[0] = -jnp.exp(wl)
        ag_ref[0] = jax.nn.sigmoid(a0_ref[d:d + 1, :] + _bdot(ad, a2_ref[d]))
    kq = k * kk_ref[...]
    ss = _split_dot(kq * kq, ones_ref[...], 2)
    kn_ref[0] = kq / jnp.maximum(jnp.sqrt(ss), 1e-12)
    if has_vres:
        lora = _bdot(_bdot(v, v1_ref[...]), v2_ref[...])
        v = v + (vf_ref[0] - v) * jax.nn.sigmoid(v0_ref[...] + lora)
    g_ref[0] = _bdot(jax.nn.sigmoid(gd), g2_ref[...])
    r_ref[0] = r
    k_ref[0] = k
    v_ref[0] = v


def rwkv_features(proj, mu, w0, w2, a0, a2, k_k, g2, vres, v_first, tm):
    B, T, _ = proj.shape
    C = BRANCH_WIDTH
    nt = T // tm
    cols = slice(RWKV_COL0, RWKV_COL0 + RWKV_COLS)
    zero = jnp.zeros((B, 1, RWKV_COLS), proj.dtype)
    prev = jnp.concatenate([zero, proj[:, tm - 1:T - 1:tm, cols]], axis=1).reshape(B, nt, 1, RWKV_COLS)
    nxt = jnp.concatenate([proj[:, tm::tm, cols], zero], axis=1).reshape(B, nt, 1, RWKV_COLS)
    zpad = jnp.zeros((RWKV_DECAY_LORA, C), jnp.float32)
    pad_dirs = lambda w: jnp.stack([jnp.concatenate([w[0], zpad], 0), jnp.concatenate([zpad, w[1]], 0)]).astype(BF)
    blk = lambda w, j: pl.BlockSpec((1, tm, w), lambda b, i: (b, i, j))
    tok = pl.BlockSpec((1, tm, C), lambda b, i: (b, i, 0))
    edge = pl.BlockSpec((1, 1, 1, RWKV_COLS), lambda b, i: (b, i, 0, 0))
    full = lambda a: pl.BlockSpec(a.shape, lambda b, i: (0,) * a.ndim)
    params = [mu.reshape(1, -1), w0, pad_dirs(w2), a0, pad_dirs(a2), k_k.reshape(1, -1), g2.astype(BF),
              _head_ones(C, RWKV_HEAD)]
    args = [proj, proj, proj, proj, prev, nxt] + params
    specs = [blk(C, RWKV_BLK0), blk(C, RWKV_BLK0 + 1), blk(C, RWKV_BLK0 + 2), blk(RWKV_LORA_COLS, RWKV_LORA_BLK),
             edge, edge] + [full(a) for a in params]
    if vres is not None:
        v0, v1, v2 = vres
        extra = [v0.reshape(1, -1), jnp.pad(v1, ((0, 0), (0, V7X_LANES - v1.shape[1]))).astype(BF),
                 jnp.pad(v2, ((0, V7X_LANES - v2.shape[0]), (0, 0))).astype(BF)]
        args += extra + [v_first]
        specs += [full(a) for a in extra] + [tok]
    return pl.pallas_call(
        functools.partial(_rwkv_feat_kernel, has_vres=vres is not None),
        out_shape=(jax.ShapeDtypeStruct((B, T, C), jnp.float32),) * 9,
        grid=(B, nt), in_specs=specs, out_specs=(tok,) * 9,
        compiler_params=_cparams("parallel", "parallel"),
        name="rwkv_features",
    )(*args)


def _wkv_kernel(r_ref, lw_ref, kk_ref, a_ref, k_ref, v_ref, ka_ref, s0_ref, *rest, reverse, bb, n_pairs, has_prev):
    if has_prev:
        yprev_ref, y_ref, sout_ref, s_scr = rest
    else:
        y_ref, sout_ref, s_scr = rest
    C = RWKV_CHUNK
    c_idx = pl.program_id(1)

    @pl.when(c_idx == 0)
    def _():
        s_scr[...] = s0_ref[...]

    ti = lax.broadcasted_iota(jnp.int32, (C, C), 0)
    si = lax.broadcasted_iota(jnp.int32, (C, C), 1)
    tri = ((si >= ti) if reverse else (si <= ti)).astype(jnp.float32)
    t2 = lax.broadcasted_iota(jnp.int32, (2 * C, 2 * C), 0)
    s2 = lax.broadcasted_iota(jnp.int32, (2 * C, 2 * C), 1)
    m_strict = (s2 % C > t2 % C) if reverse else (s2 % C < t2 % C)
    m_incl = (s2 % C >= t2 % C) if reverse else (s2 % C <= t2 % C)
    eye = (t2 == s2).astype(jnp.float32)
    same_head = (t2 // C) == (s2 // C)
    lo = _lo_lanes(C)
    units = [(bi, slice(p * HP, (p + 1) * HP), p) for bi in range(bb) for p in range(n_pairs)]
    n = len(units)

    ar, bk, bkh, vs, e_tot = [], [], [], [], []
    for bi, sl, _ in units:
        lw = lw_ref[bi, :, sl]
        kk = kk_ref[bi, :, sl]
        ag = a_ref[bi, :, sl]
        kd = k_ref[bi, :, sl] * (1.0 + (ag - 1.0) * ka_ref[:, sl])
        cum = jnp.dot(tri, lw, precision=HIGHEST, preferred_element_type=jnp.float32)
        tot = jnp.sum(lw, axis=0, keepdims=True)
        e_neg = jnp.exp(-cum)
        e_end = jnp.exp(tot - cum)
        b = kk * ag
        ar.append(jnp.concatenate([_stack_pair(-kk * jnp.exp(cum - lw), lo),
                                   _stack_pair(r_ref[bi, :, sl] * jnp.exp(cum), lo)], axis=0).astype(BF))
        bk.append(jnp.concatenate([_stack_pair(b * e_neg, lo), _stack_pair(kd * e_neg, lo)], axis=0).astype(BF))
        bkh.append(jnp.concatenate([_stack_pair(b * e_end, lo), _stack_pair(kd * e_end, lo)], axis=0).astype(BF))
        vs.append(_stack_pair(v_ref[bi, :, sl], lo).astype(BF))
        e_tot.append(jnp.exp(tot))
    gram = [_dot_nt(ar[i], bk[i]) for i in range(n)]
    l_ab = [jnp.where(m_strict, g[:2 * C, :2 * C], 0.0) for g in gram]
    l_ak = [jnp.where(m_strict, g[:2 * C, 2 * C:], 0.0).astype(BF) for g in gram]
    l_rbk = [jnp.concatenate([jnp.where(m_incl, g[2 * C:, :2 * C], 0.0),
                              jnp.where(m_incl, g[2 * C:, 2 * C:], 0.0)], axis=1).astype(BF) for g in gram]
    s0 = [s_scr[bi, p] for bi, _, p in units]
    proj = [_dot_nt(ar[i], s0[i]) for i in range(n)]
    lv = [_bdot(l_ak[i], vs[i]) for i in range(n)]
    pw = [m.astype(BF) for m in l_ab]
    tinv = [eye + m for m in l_ab]
    for _ in range(5):
        pw = [_bdot(m, m).astype(BF) for m in pw]
        tinv = [tinv[i] + _bdot(tinv[i], pw[i]) for i in range(n)]
    u = [_bdot(tinv[i], proj[i][:2 * C] + lv[i]) for i in range(n)]
    uv = [jnp.concatenate([u[i].astype(BF), vs[i]], axis=0) for i in range(n)]
    ys = [proj[i][2 * C:] + _bdot(l_rbk[i], uv[i]) for i in range(n)]
    upd = [_dot_tn(uv[i], bkh[i]) for i in range(n)]
    for i, (bi, sl, p) in enumerate(units):
        y = ys[i][:C] + ys[i][C:]
        if has_prev:
            y = y + yprev_ref[bi, :, sl]
        y_ref[bi, :, sl] = y
        s_scr[bi, p] = s0[i] * e_tot[i] + jnp.where(same_head, upd[i], 0.0)

    @pl.when(c_idx == pl.num_programs(1) - 1)
    def _():
        sout_ref[...] = s_scr[...]


def wkv_chunked(r, lw, kk, ag, k, v, k_a, s0, y_prev, reverse):
    B, T, W = r.shape
    C = RWKV_CHUNK
    bb = WKV_SEQS_PER_STEP
    nc = T // C
    n_pairs = W // HP
    cmap = (lambda b, c: (b, nc - 1 - c, 0)) if reverse else (lambda b, c: (b, c, 0))
    tok = pl.BlockSpec((bb, C, W), cmap)
    st = pl.BlockSpec((bb, n_pairs, HP, HP), lambda b, c: (b, 0, 0, 0))
    has_prev = y_prev is not None
    args = [r, lw, kk, ag, k, v, k_a, s0] + ([y_prev] if has_prev else [])
    return pl.pallas_call(
        functools.partial(_wkv_kernel, reverse=reverse, bb=bb, n_pairs=n_pairs, has_prev=has_prev),
        out_shape=(jax.ShapeDtypeStruct((B, T, W), jnp.float32), jax.ShapeDtypeStruct(s0.shape, jnp.float32)),
        grid=(B // bb, nc),
        in_specs=[tok] * 6 + [pl.BlockSpec((1, W), lambda b, c: (0, 0)), st] + ([tok] if has_prev else []),
        out_specs=(tok, st),
        scratch_shapes=[pltpu.VMEM((bb, n_pairs, HP, HP), jnp.float32)],
        compiler_params=_cparams("parallel", "arbitrary"),
        name="wkv_chunked",
    )(*args)


def _rwkv_readout_kernel(y_ref, r_ref, k_ref, v_ref, g_ref, ag0_ref, ag1_ref, ka_ref, rk_ref, lg_ref, lb_ref,
                         ones_ref, o_ref):
    ones_bd = ones_ref[...]
    y = y_ref[0]
    mean = _split_dot(y, ones_bd, 2) * (1.0 / RWKV_HEAD)
    yc = y - mean
    var = _split_dot(yc * yc, ones_bd, 2) * (1.0 / RWKV_HEAD)
    yn = yc * lax.rsqrt(var + RWKV_LNX_EPS) * lg_ref[...] + lb_ref[...]
    ksum = k_ref[0] * (2.0 + (ag0_ref[0] + ag1_ref[0] - 2.0) * ka_ref[...])
    bonus = _split_dot(r_ref[0] * ksum * rk_ref[...], ones_bd, 2) * v_ref[0]
    o_ref[0] = (yn + bonus) * g_ref[0]


def rwkv_readout(y, r, k, v, g, ag0, ag1, k_a, r_k, lnx_g, lnx_b, tm):
    B, T, C = y.shape
    tok = pl.BlockSpec((1, tm, C), lambda b, i: (b, i, 0))
    par = pl.BlockSpec((1, C), lambda b, i: (0, 0))
    return pl.pallas_call(
        _rwkv_readout_kernel,
        out_shape=jax.ShapeDtypeStruct((B, T, C), jnp.float32), grid=(B, T // tm),
        in_specs=[tok] * 7 + [par] * 4 + [pl.BlockSpec((C, C), lambda b, i: (0, 0))],
        out_specs=tok,
        compiler_params=_cparams("parallel", "parallel"),
        name="rwkv_readout",
    )(y, r, k, v, g, ag0, ag1, k_a, r_k.reshape(1, -1), lnx_g.reshape(1, -1), lnx_b.reshape(1, -1),
      _head_ones(C, RWKV_HEAD))


def rwkv_mix(proj_x, proj_c, vf_x, vf_c, mu, w0, w2, a0, a2, k_k, k_a, r_k, g2, lnx_g, lnx_b, vres, need_ctx, tm):
    B = proj_x.shape[0]
    fx = rwkv_features(proj_x, mu, w0, w2, a0, a2, k_k, g2, vres, vf_x, tm)
    fc = rwkv_features(proj_c, mu, w0, w2, a0, a2, k_k, g2, vres, vf_c, tm)
    ka = k_a.reshape(1, -1)
    y_x = y_c = None
    for d in range(N_DIR):
        s0 = jnp.zeros((B, BRANCH_WIDTH // HP, HP, HP), jnp.float32)
        y_c, s_ctx = wkv_chunked(fc[0], fc[5 + d], fc[3], fc[7 + d], fc[1], fc[2], ka, s0, y_c, d == 1)
        y_x, _ = wkv_chunked(fx[0], fx[5 + d], fx[3], fx[7 + d], fx[1], fx[2], ka, s_ctx, y_x, d == 1)
    out_x = rwkv_readout(y_x, fx[0], fx[1], fx[2], fx[4], fx[7], fx[8], ka, r_k, lnx_g, lnx_b, tm)
    out_c = rwkv_readout(y_c, fc[0], fc[1], fc[2], fc[4], fc[7], fc[8], ka, r_k, lnx_g, lnx_b, tm) if need_ctx else None
    vf_x = fx[2] if vres is None else vf_x
    vf_c = fc[2] if vres is None else vf_c
    return out_x, out_c, vf_x, vf_c


def _qknorm_kernel(z_ref, qg_ref, kg_ref, ones_ref, q_ref, k_ref, v_ref):
    ones_bd = ones_ref[...]
    C = BRANCH_WIDTH
    z = z_ref[0]
    q = z[:, 0:C]
    k = z[:, C:2 * C]
    qn = q * lax.rsqrt(_split_dot(q * q, ones_bd, 2) * (1.0 / NA_HEAD) + RMS_EPS) * qg_ref[...]
    kn = k * lax.rsqrt(_split_dot(k * k, ones_bd, 2) * (1.0 / NA_HEAD) + RMS_EPS) * kg_ref[...]
    q_ref[0] = (qn * NA_HEAD ** -0.5).astype(BF)
    k_ref[0] = kn.astype(BF)
    v_ref[0] = z[:, 2 * C:3 * C].astype(BF)


def na_qknorm(proj, qn_g, kn_g, tm):
    B, T, _ = proj.shape
    C = BRANCH_WIDTH
    tok = pl.BlockSpec((1, tm, C), lambda b, i: (b, i, 0))
    par = pl.BlockSpec((1, C), lambda b, i: (0, 0))
    sd = jax.ShapeDtypeStruct((B, T, C), BF)
    return pl.pallas_call(
        _qknorm_kernel, out_shape=(sd, sd, sd), grid=(B, T // tm),
        in_specs=[pl.BlockSpec((1, tm, NA_COLS), lambda b, i: (b, i, NA_BLK)), par, par,
                  pl.BlockSpec((C, C), lambda b, i: (0, 0))],
        out_specs=(tok, tok, tok),
        compiler_params=_cparams("parallel", "parallel"),
        name="na_qknorm",
    )(proj, jnp.tile(qn_g, NA_HEADS).reshape(1, C), jnp.tile(kn_g, NA_HEADS).reshape(1, C), _head_ones(C, NA_HEAD))


def na_bias_table(rpb):
    qc = np.arange(GRID_W)[:, None]
    kc = np.arange(GRID_W)[None, :]
    cs = np.clip(qc - NA_WIN_C // 2, 0, GRID_W - NA_WIN_C)
    valid = (kc >= cs) & (kc < cs + NA_WIN_C)
    cidx = np.clip(kc - qc + NA_WIN_C - 1, 0, 2 * NA_WIN_C - 2)
    t = jnp.where(valid[None, None], rpb[:, :, cidx], NEG_INF)
    t2 = jnp.concatenate([t[:, :-1], t[:, 1:]], axis=-1)
    H = rpb.shape[0]
    t2 = t2.reshape(H // 2, 2, 2 * NA_WIN_R - 2, GRID_W, 2 * GRID_W).transpose(0, 2, 1, 3, 4)
    return t2.reshape(H // 2, 2 * NA_WIN_R - 2, 2 * GRID_W, 2 * GRID_W)


def _na_kernel(q_ref, k_ref, v_ref, kc_ref, vc_ref, bias_ref, o_ref, *, rows):
    nwin = NA_WIN_R * GRID_W
    lo = _lo_lanes(GRID_W)
    units = []
    for rr in range(NA_ROWS_PER_STEP):
        r = pl.program_id(1) * NA_ROWS_PER_STEP + rr
        rs = jnp.clip(r - NA_WIN_R // 2, 0, rows - NA_WIN_R)
        k0 = pl.multiple_of(rs * GRID_W, GRID_W)
        for p in range(BRANCH_WIDTH // HP):
            units.append((rr, p, slice(p * HP, (p + 1) * HP), rs - r + NA_WIN_R - 1, k0))
    qs = [_stack_pair(q_ref[0, rr * GRID_W:(rr + 1) * GRID_W, sl], lo) for rr, _, sl, _, _ in units]
    s_loc = [_dot_nt(qs[i], k_ref[0, pl.ds(u[4], nwin), u[2]]) for i, u in enumerate(units)]
    s_ctx = [_dot_nt(qs[i], kc_ref[0, :, u[2]]) for i, u in enumerate(units)]
    p_loc, p_ctx, den = [], [], []
    for i, (_, p, _, base, _) in enumerate(units):
        sl_b = s_loc[i] + jnp.concatenate([bias_ref[p, base + 2 * j] for j in range(NA_WIN_R // 2)], axis=1)
        m = jnp.maximum(jnp.max(sl_b, axis=1, keepdims=True), jnp.max(s_ctx[i], axis=1, keepdims=True))
        el = jnp.exp(sl_b - m)
        ec = jnp.exp(s_ctx[i] - m)
        den.append(jnp.sum(el, axis=1, keepdims=True) + jnp.sum(ec, axis=1, keepdims=True))
        p_loc.append(el.astype(BF))
        p_ctx.append(ec.astype(BF))
    o_loc = [_bdot(p_loc[i], v_ref[0, pl.ds(u[4], nwin), u[2]]) for i, u in enumerate(units)]
    o_ctx = [_bdot(p_ctx[i], vc_ref[0, :, u[2]]) for i, u in enumerate(units)]
    for i, (rr, _, sl, _, _) in enumerate(units):
        o = (o_loc[i] + o_ctx[i]) / den[i]
        o_ref[0, rr * GRID_W:(rr + 1) * GRID_W, sl] = jnp.where(lo, o[:GRID_W], o[GRID_W:])


def na_attention(q, k, v, kc, vc, bias_tab):
    B, S, C = q.shape
    rows = S // GRID_W
    n_ctx = kc.shape[1]
    seq = pl.BlockSpec((1, S, C), lambda b, r: (b, 0, 0))
    cx = pl.BlockSpec((1, n_ctx, C), lambda b, r: (b, 0, 0))
    row = pl.BlockSpec((1, NA_ROWS_PER_STEP * GRID_W, C), lambda b, r: (b, r, 0))
    return pl.pallas_call(
        functools.partial(_na_kernel, rows=rows),
        out_shape=jax.ShapeDtypeStruct((B, S, C), jnp.float32),
        grid=(B, rows // NA_ROWS_PER_STEP),
        in_specs=[row, seq, seq, cx, cx, pl.BlockSpec(bias_tab.shape, lambda b, r: (0, 0, 0, 0))],
        out_specs=row,
        compiler_params=_cparams("parallel", "arbitrary"),
        name="na_attention",
    )(q, k, v, kc, vc, bias_tab)


def _ctx_attn_kernel(q_ref, k_ref, v_ref, o_ref):
    n = q_ref.shape[1]
    lo = _lo_lanes(n)
    sls = [slice(p * HP, (p + 1) * HP) for p in range(BRANCH_WIDTH // HP)]
    sc = [_dot_nt(_stack_pair(q_ref[0, :, sl], lo), k_ref[0, :, sl]) for sl in sls]
    e = [jnp.exp(x - jnp.max(x, axis=1, keepdims=True)) for x in sc]
    o = [_bdot(e[p], v_ref[0, :, sl]) / jnp.sum(e[p], axis=1, keepdims=True) for p, sl in enumerate(sls)]
    for p, sl in enumerate(sls):
        o_ref[0, :, sl] = jnp.where(lo, o[p][:n], o[p][n:])


def ctx_attention(q, k, v):
    B, n, C = q.shape
    blk = pl.BlockSpec((1, n, C), lambda b: (b, 0, 0))
    return pl.pallas_call(
        _ctx_attn_kernel, out_shape=jax.ShapeDtypeStruct((B, n, C), jnp.float32), grid=(B,),
        in_specs=[blk, blk, blk], out_specs=blk,
        compiler_params=_cparams("parallel"),
        name="ctx_attention",
    )(q, k, v)


def na_mix(proj_x, proj_c, qn_g, kn_g, rpb, need_ctx, tm):
    q, k, v = na_qknorm(proj_x, qn_g, kn_g, tm)
    qc, kc, vc = na_qknorm(proj_c, qn_g, kn_g, tm)
    out_x = na_attention(q, k, v, kc, vc, na_bias_table(rpb))
    out_c = ctx_attention(qc, kc, vc) if need_ctx else None
    return out_x, out_c


def rope_tables(n_tokens):
    t = jnp.arange(n_tokens)
    pos = jnp.stack([t // GRID_W, t % GRID_W], axis=-1).astype(jnp.float32)
    inv = ROPE_BASE ** (-jnp.arange(ROPE_ROT, dtype=jnp.float32) / ROPE_ROT)
    ang = pos[:, :, None] * inv
    cos_h = jnp.concatenate([jnp.cos(ang), jnp.cos(ang)], axis=-1).reshape(n_tokens, MLSTM_DQK)
    sin_h = jnp.concatenate([-jnp.sin(ang), jnp.sin(ang)], axis=-1).reshape(n_tokens, MLSTM_DQK)
    col = np.arange(MLSTM_QK)
    partner = np.where((col % (2 * ROPE_ROT)) < ROPE_ROT, col + ROPE_ROT, col - ROPE_ROT)
    perm = np.zeros((MLSTM_QK, MLSTM_QK), np.float32)
    perm[partner, col] = 1.0
    return jnp.tile(cos_h, (1, MLSTM_HEADS)), jnp.tile(sin_h, (1, MLSTM_HEADS)), jnp.asarray(perm, BF)


def _mlstm_prep_kernel(z_ref, gp_ref, ib_ref, fb_ref, *rest, rope):
    if rope:
        cos_ref, sin_ref, perm_ref, q_ref, k_ref, v_ref, g_ref = rest
    else:
        q_ref, k_ref, v_ref, g_ref = rest
    z = z_ref[0]
    q = z[:, 0:MLSTM_QK]
    k = z[:, MLSTM_QK:2 * MLSTM_QK]
    if rope:
        perm = perm_ref[...]
        q = q * cos_ref[...] + _split_dot(q, perm, 3) * sin_ref[...]
        k = k * cos_ref[...] + _split_dot(k, perm, 3) * sin_ref[...]
    q_ref[0] = (q * MLSTM_DQK ** -0.5).astype(BF)
    k_ref[0] = k.astype(BF)
    v_ref[0] = z[:, 2 * MLSTM_QK:2 * MLSTM_QK + BRANCH_WIDTH].astype(BF)
    gp = gp_ref[0]
    lane = lax.broadcasted_iota(jnp.int32, gp.shape, 1)
    ig = GATE_CAP * jnp.tanh((gp + ib_ref[...]) / GATE_CAP)
    fg = GATE_CAP * jnp.tanh((gp + fb_ref[...]) / GATE_CAP)
    g_ref[0] = jnp.where(lane < N_DIR * MLSTM_HEADS, ig, jax.nn.log_sigmoid(fg))


def mlstm_prep(proj, i_bias, f_bias, rope, tm):
    B, T, _ = proj.shape
    ng = N_DIR * MLSTM_HEADS
    ib = jnp.zeros((1, V7X_LANES), jnp.float32).at[0, 0:ng].set(i_bias.reshape(-1))
    fb = jnp.zeros((1, V7X_LANES), jnp.float32).at[0, ng:2 * ng].set(f_bias.reshape(-1))
    tok = lambda w: pl.BlockSpec((1, tm, w), lambda b, i: (b, i, 0))
    par = pl.BlockSpec((1, V7X_LANES), lambda b, i: (0, 0))
    args = [proj, proj, ib, fb]
    specs = [pl.BlockSpec((1, tm, MLSTM_MAIN_COLS), lambda b, i: (b, i, ML_BLK)),
             pl.BlockSpec((1, tm, V7X_LANES), lambda b, i: (b, i, ML_GATE_BLK)), par, par]
    if rope is not None:
        args += list(rope)
        specs += [pl.BlockSpec((tm, MLSTM_QK), lambda b, i: (i, 0)), pl.BlockSpec((tm, MLSTM_QK), lambda b, i: (i, 0)),
                  pl.BlockSpec((MLSTM_QK, MLSTM_QK), lambda b, i: (0, 0))]
    return pl.pallas_call(
        functools.partial(_mlstm_prep_kernel, rope=rope is not None),
        out_shape=(jax.ShapeDtypeStruct((B, T, MLSTM_QK), BF), jax.ShapeDtypeStruct((B, T, MLSTM_QK), BF),
                   jax.ShapeDtypeStruct((B, T, BRANCH_WIDTH), BF), jax.ShapeDtypeStruct((B, T, V7X_LANES), jnp.float32)),
        grid=(B, T // tm), in_specs=specs,
        out_specs=(tok(MLSTM_QK), tok(MLSTM_QK), tok(BRANCH_WIDTH), tok(V7X_LANES)),
        compiler_params=_cparams("parallel", "parallel"),
        name="mlstm_prep",
    )(*args)


def _mlstm_kernel(q_ref, k_ref, v_ref, gc_ref, gr_ref, c0_ref, n0_ref, m0_ref, *rest, reverse, direction, bb, has_prev):
    if has_prev:
        hprev_ref, h_ref, cout_ref, nout_ref, mout_ref, c_scr, n_scr, m_scr = rest
    else:
        h_ref, cout_ref, nout_ref, mout_ref, c_scr, n_scr, m_scr = rest
    L = MLSTM_CHUNK
    DV = MLSTM_DV
    H = MLSTM_HEADS
    cidx = pl.program_id(1)

    @pl.when(cidx == 0)
    def _():
        c_scr[...] = c0_ref[...]
        n_scr[...] = n0_ref[...]
        m_scr[...] = m0_ref[...]

    ti = lax.broadcasted_iota(jnp.int32, (L, L), 0)
    si = lax.broadcasted_iota(jnp.int32, (L, L), 1)
    before = (si >= ti) if reverse else (si <= ti)
    tri = before.astype(jnp.float32)
    lo = _lo_lanes(L)
    lane1 = lax.broadcasted_iota(jnp.int32, (1, HP), 1)
    lo_row = lane1 < (HP // 2)
    lo_col = lax.broadcasted_iota(jnp.int32, (HP, 1), 0) < (HP // 2)
    col = lambda a, c: a[:, c:c + 1]
    rowv = lambda a, c: a[c:c + 1, :]
    ci = lambda h: direction * H + h
    cf = lambda h: N_DIR * H + direction * H + h
    units = [(bi, p) for bi in range(bb) for p in range(H // 2)]
    blocks = [(bi, h) for bi in range(bb) for h in range(H)]
    cat = lambda xs: jnp.concatenate(xs, axis=0)
    rows_of = lambda col11: jnp.broadcast_to(col11, (L, 1))

    bcol, icol, brow, irow, tot_b, m_b = [], [], [], [], [], []
    for bi in range(bb):
        gc = gc_ref[bi]
        gr = gr_ref[bi, 0]
        bcol_all = jnp.dot(tri, gc, precision=HIGHEST, preferred_element_type=jnp.float32)
        brow_all = lax.dot_general(gr, tri, (((1,), (1,)), ((), ())), precision=HIGHEST, preferred_element_type=jnp.float32)
        tot_all = jnp.sum(gc, axis=0, keepdims=True)
        m_all = m_scr[bi, 0]
        for h in range(H):
            bcol.append(col(bcol_all, cf(h)))
            icol.append(col(gc, ci(h)))
            brow.append(jnp.broadcast_to(rowv(brow_all, cf(h)), (L, L)))
            irow.append(jnp.broadcast_to(rowv(gr, ci(h)), (L, L)))
            tot_b.append(col(tot_all, cf(h)))
            m_b.append(col(m_all, h))
    nb = len(blocks)
    bcol_r, icol_r = cat(bcol), cat(icol)
    tot_r = cat([rows_of(t) for t in tot_b])
    mprev_r = cat([rows_of(m) for m in m_b])
    causal_r = cat([before] * nb)
    dmat = jnp.where(causal_r, bcol_r - cat(brow) + cat(irow), NEG_INF)
    inter = bcol_r + mprev_r
    m_t = jnp.maximum(inter, jnp.max(dmat, axis=1, keepdims=True))
    decay = jnp.exp(dmat - m_t)
    sc_inter = jnp.exp(inter - m_t)
    inv_floor = jnp.exp(-m_t)

    qs = [_stack_pair(q_ref[bi, :, p * HP:(p + 1) * HP], lo) for bi, p in units]
    kp = [k_ref[bi, :, p * HP:(p + 1) * HP] for bi, p in units]
    vb = [v_ref[bi, :, h * DV:(h + 1) * DV] for bi, h in blocks]
    c_pair = [c_scr[bi, p] for bi, p in units]
    n_pair = [n_scr[bi, p] for bi, p in units]
    qk = cat([_dot_nt(qs[u], kp[u]) for u in range(len(units))])
    qc = cat([_bdot(qs[u], c_pair[u]) for u in range(len(units))])
    qn = cat([jnp.sum(qs[u].astype(jnp.float32) * n_pair[u], axis=1, keepdims=True) for u in range(len(units))])
    smat = qk * decay
    den = sc_inter * qn + jnp.sum(smat, axis=1, keepdims=True)
    sb = smat.astype(BF)
    sv = cat([_bdot(sb[j * L:(j + 1) * L], vb[j]) for j in range(nb)])
    hout = (sc_inter * qc + sv) / jnp.maximum(jnp.abs(den), inv_floor)
    for j, (bi, h) in enumerate(blocks):
        hs = slice(h * DV, (h + 1) * DV)
        part = hout[j * L:(j + 1) * L]
        h_ref[bi, :, hs] = (part + hprev_ref[bi, :, hs]) if has_prev else part

    wlog = tot_r - bcol_r + icol_r
    m_new = [jnp.maximum(tot_b[j] + m_b[j], jnp.max(wlog[j * L:(j + 1) * L], axis=0, keepdims=True)) for j in range(nb)]
    dec = [jnp.exp(tot_b[j] + m_b[j] - m_new[j]) for j in range(nb)]
    wexp = jnp.exp(wlog - cat([rows_of(m) for m in m_new]))
    for u, (bi, p) in enumerate(units):
        j0 = bi * H + 2 * p
        kws = _stack_pair(kp[u], lo).astype(jnp.float32) * wexp[2 * u * L:2 * (u + 1) * L]
        upd = _dot_tn(kws, cat([vb[j0], vb[j0 + 1]]))
        c_scr[bi, p] = jnp.where(lo_col, dec[j0], dec[j0 + 1]) * c_pair[u] + upd
        n_scr[bi, p] = jnp.where(lo_row, dec[j0], dec[j0 + 1]) * n_pair[u] + jnp.sum(kws, axis=0, keepdims=True)
    for bi in range(bb):
        m_all = m_scr[bi, 0]
        for h in range(H):
            m_all = jnp.where(lane1 == h, m_new[bi * H + h], m_all)
        m_scr[bi, 0] = m_all

    @pl.when(cidx == pl.num_programs(1) - 1)
    def _():
        cout_ref[...] = c_scr[...]
        nout_ref[...] = n_scr[...]
        mout_ref[...] = m_scr[...]


def mlstm_chunked(q, k, v, gates, state, h_prev, direction):
    B, T, _ = q.shape
    L = MLSTM_CHUNK
    bb = MLSTM_SEQS_PER_STEP
    nc = T // L
    reverse = direction == 1
    g_rows = gates[:, :, :MLSTM_GATES].reshape(B, nc, L, MLSTM_GATES).transpose(0, 1, 3, 2)
    cm = (lambda c: nc - 1 - c) if reverse else (lambda c: c)
    tok = lambda w: pl.BlockSpec((bb, L, w), lambda b, c: (b, cm(c), 0))
    st = lambda a: pl.BlockSpec((bb,) + a.shape[1:], lambda b, c: (b,) + (0,) * (a.ndim - 1))
    c0, n0, m0 = state
    has_prev = h_prev is not None
    args = [q, k, v, gates, g_rows, c0, n0, m0] + ([h_prev] if has_prev else [])
    outs = pl.pallas_call(
        functools.partial(_mlstm_kernel, reverse=reverse, direction=direction, bb=bb, has_prev=has_prev),
        out_shape=(jax.ShapeDtypeStruct((B, T, BRANCH_WIDTH), jnp.float32),) + tuple(
            jax.ShapeDtypeStruct(a.shape, jnp.float32) for a in state),
        grid=(B // bb, nc),
        in_specs=[tok(MLSTM_QK), tok(MLSTM_QK), tok(BRANCH_WIDTH), tok(V7X_LANES),
                  pl.BlockSpec((bb, 1, MLSTM_GATES, L), lambda b, c: (b, cm(c), 0, 0)), st(c0), st(n0), st(m0)]
                 + ([tok(BRANCH_WIDTH)] if has_prev else []),
        out_specs=(tok(BRANCH_WIDTH), st(c0), st(n0), st(m0)),
        scratch_shapes=[pltpu.VMEM((bb,) + a.shape[1:], jnp.float32) for a in state],
        compiler_params=_cparams("parallel", "arbitrary"),
        name="mlstm_chunked",
    )(*args)
    return outs[0], outs[1:]


def _mlstm_readout_kernel(h_ref, o_ref, g_ref, out_ref):
    for h in range(MLSTM_HEADS):
        sl = slice(h * MLSTM_DV, (h + 1) * MLSTM_DV)
        x = h_ref[0, :, sl]
        xn = x * lax.rsqrt(jnp.mean(x * x, axis=1, keepdims=True) + RMS_EPS)
        out_ref[0, :, sl] = xn * g_ref[:, sl] * jax.nn.sigmoid(o_ref[0, :, sl])


def mlstm_readout(h, proj, norm_g, tm):
    B, T, C = h.shape
    tok = pl.BlockSpec((1, tm, C), lambda b, i: (b, i, 0))
    return pl.pallas_call(
        _mlstm_readout_kernel, out_shape=jax.ShapeDtypeStruct((B, T, C), jnp.float32), grid=(B, T // tm),
        in_specs=[tok, pl.BlockSpec((1, tm, C), lambda b, i: (b, i, ML_OGATE_BLK)), pl.BlockSpec((1, C), lambda b, i: (0, 0))],
        out_specs=tok,
        compiler_params=_cparams("parallel", "parallel"),
        name="mlstm_readout",
    )(h, proj, norm_g.reshape(1, C))


def mlstm_mix(proj_x, proj_c, i_bias, f_bias, norm_g, rope, need_ctx, tm):
    B = proj_x.shape[0]
    qx, kx, vx, gx = mlstm_prep(proj_x, i_bias, f_bias, rope, tm)
    qc, kc, vc, gc = mlstm_prep(proj_c, i_bias, f_bias, None, tm)
    h_x = h_c = None
    for d in range(N_DIR):
        st0 = (jnp.zeros((B, MLSTM_HEADS // 2, HP, HP), jnp.float32), jnp.zeros((B, MLSTM_HEADS // 2, 1, HP), jnp.float32),
               jnp.zeros((B, 1, 1, HP), jnp.float32))
        h_c, st_ctx = mlstm_chunked(qc, kc, vc, gc, st0, h_c, d)
        h_x, _ = mlstm_chunked(qx, kx, vx, gx, st_ctx, h_x, d)
    out_x = mlstm_readout(h_x, proj_x, norm_g, tm)
    out_c = mlstm_readout(h_c, proj_c, norm_g, tm) if need_ctx else None
    return out_x, out_c


def _merge_kernel(ya_ref, yb_ref, yc_ref, ga_ref, gb_ref, gc_ref, x_ref, mod_ref, wb_ref, wo_ref, o_ref):
    merged = None
    for i, (y_ref, g_ref) in enumerate(((ya_ref, ga_ref), (yb_ref, gb_ref), (yc_ref, gc_ref))):
        t = jax.nn.sigmoid(g_ref[0]) * _bdot(y_ref[0], wb_ref[i])
        merged = t if merged is None else merged + t
    o_ref[0] = x_ref[0] + mod_ref[0, 2:3, :] * _bdot(merged, wo_ref[...])


def merge_apply(ya, yb, yc, proj, x, mod, w_branch, w_out, tm):
    B, T, D = x.shape
    tok = lambda w: pl.BlockSpec((1, tm, w), lambda b, i: (b, i, 0))
    gate = lambda k: pl.BlockSpec((1, tm, D), lambda b, i: (b, i, GATE_BLK0 + k))
    return pl.pallas_call(
        _merge_kernel, out_shape=jax.ShapeDtypeStruct((B, T, D), jnp.float32), grid=(B, T // tm),
        in_specs=[tok(BRANCH_WIDTH)] * 3 + [gate(0), gate(1), gate(2), tok(D), pl.BlockSpec((1, 6, D), lambda b, i: (b, 0, 0)),
                  pl.BlockSpec(w_branch.shape, lambda b, i: (0, 0, 0)), pl.BlockSpec(w_out.shape, lambda b, i: (0, 0))],
        out_specs=tok(D),
        compiler_params=_cparams("parallel", "parallel"),
        name="merge_branches",
    )(ya, yb, yc, proj, proj, proj, x, mod, w_branch, w_out)


def _route_kernel(x_ref, g_ref, mod_ref, wr_ref, rb_ref, h_ref, gate_ref):
    h = _norm_mod(x_ref[0], g_ref[...], mod_ref[0, 4:5, :], mod_ref[0, 3:4, :])
    h_ref[0] = h.astype(BF)
    tm = h.shape[0]
    logits = lax.dot_general(wr_ref[...], h, (((1,), (1,)), ((), ())), precision=HIGHEST, preferred_element_type=jnp.float32)
    scores = jax.nn.sigmoid(logits)
    sel = scores + rb_ref[...]
    gsz = N_EXPERTS // N_GROUPS
    grp = sel.reshape(N_GROUPS, gsz, tm)
    iota_in = lax.broadcasted_iota(jnp.int32, grp.shape, 1)
    m1 = jnp.max(grp, axis=1, keepdims=True)
    first = jnp.min(jnp.where(grp == m1, iota_in, gsz), axis=1, keepdims=True)
    m2 = jnp.max(jnp.where(iota_in == first, -jnp.inf, grp), axis=1, keepdims=True)
    gscore = (m1 + m2).reshape(N_GROUPS, tm)
    gi = lax.broadcasted_iota(jnp.int32, (N_GROUPS, tm), 0)
    rank = jnp.zeros((N_GROUPS, tm), jnp.int32)
    for g2 in range(N_GROUPS):
        other = gscore[g2:g2 + 1, :]
        rank = rank + ((other > gscore) | ((other == gscore) & (g2 < gi))).astype(jnp.int32)
    gmask = rank < TOPK_GROUPS
    emask = jnp.broadcast_to(gmask.reshape(N_GROUPS, 1, tm), (N_GROUPS, gsz, tm)).reshape(N_EXPERTS, tm)
    cand = jnp.where(emask, sel, NEG_INF)
    ei = lax.broadcasted_iota(jnp.int32, (N_EXPERTS, tm), 0)
    chosen = jnp.zeros((N_EXPERTS, tm), jnp.bool_)
    for _ in range(TOP_K):
        mx = jnp.max(cand, axis=0, keepdims=True)
        idx = jnp.min(jnp.where(cand == mx, ei, N_EXPERTS), axis=0, keepdims=True)
        hit = ei == idx
        chosen = chosen | hit
        cand = jnp.where(hit, -jnp.inf, cand)
    w = jnp.where(chosen, scores, 0.0)
    w = w / jnp.sum(w, axis=0, keepdims=True) * ROUTED_SCALE
    gate_ref[0] = jnp.concatenate([w, jnp.zeros((V7X_LANES - N_EXPERTS, tm), jnp.float32)], axis=0).T


def moe_route(x, g, mod, w_router, router_bias, tm):
    B, T, D = x.shape
    tok = pl.BlockSpec((1, tm, D), lambda b, i: (b, i, 0))
    return pl.pallas_call(
        _route_kernel,
        out_shape=(jax.ShapeDtypeStruct((B, T, D), BF), jax.ShapeDtypeStruct((B, T, V7X_LANES), jnp.float32)),
        grid=(B, T // tm),
        in_specs=[tok, pl.BlockSpec((1, D), lambda b, i: (0, 0)), pl.BlockSpec((1, 6, D), lambda b, i: (b, 0, 0)),
                  pl.BlockSpec((N_EXPERTS, D), lambda b, i: (0, 0)), pl.BlockSpec((N_EXPERTS, 1), lambda b, i: (0, 0))],
        out_specs=(tok, pl.BlockSpec((1, tm, V7X_LANES), lambda b, i: (b, i, 0))),
        compiler_params=_cparams("parallel", "parallel"),
        name="moe_route",
    )(x, g.reshape(1, D), mod, w_router.T, router_bias.reshape(N_EXPERTS, 1))


def _moe_kernel(h_ref, gate_ref, x_ref, mod_ref, sel_ref, wg_ref, wu_ref, wd_ref, sg_ref, su_ref, sd_ref, o_ref, *, tm):
    j = pl.program_id(1)
    rows = pl.ds(pl.multiple_of(pl.program_id(2) * tm, tm), tm)
    h = h_ref[0]

    @pl.when(j == 0)
    def _():
        sh = jax.nn.silu(_bdot(h, sg_ref[...])) * _bdot(h, su_ref[...])
        o_ref[0, rows, :] = _bdot(sh, sd_ref[...])

    g8 = _split_dot(gate_ref[0], sel_ref[0], 2)
    act = jax.nn.silu(_bdot(h, wg_ref[...])) * _bdot(h, wu_ref[...])
    act = jnp.concatenate([act[:, e * D_EXPERT:(e + 1) * D_EXPERT] * g8[:, e:e + 1] for e in range(MOE_STEP_EXPERTS)],
                          axis=1)
    o_ref[0, rows, :] += _bdot(act, wd_ref[...])

    @pl.when(j == pl.num_programs(1) - 1)
    def _():
        o_ref[0, rows, :] = x_ref[0] + mod_ref[0, 5:6, :] * o_ref[0, rows, :]


def _moe_select_table():
    se = MOE_STEP_EXPERTS
    t = np.zeros((N_EXPERTS // se, V7X_LANES, V7X_LANES), np.float32)
    for j in range(N_EXPERTS // se):
        for e in range(se):
            t[j, j * se + e, e] = 1.0
    return jnp.asarray(t, BF)


def moe_apply(h2, gates, x, mod, wg, wu, wd, sg, su, sd, tm):
    B, T, D = x.shape
    sw = MOE_STEP_EXPERTS * D_EXPERT
    n_groups = N_EXPERTS // MOE_STEP_EXPERTS
    tok = pl.BlockSpec((1, tm, D), lambda b, j, i: (b, i, 0))
    x_last = pl.BlockSpec((1, tm, D), lambda b, j, i: (b, jnp.where(j == n_groups - 1, i, 0), 0))
    full = lambda a: pl.BlockSpec(a.shape, lambda b, j, i: (0,) * a.ndim)
    return pl.pallas_call(
        functools.partial(_moe_kernel, tm=tm), out_shape=jax.ShapeDtypeStruct((B, T, D), jnp.float32),
        grid=(B, n_groups, T // tm),
        in_specs=[tok, pl.BlockSpec((1, tm, V7X_LANES), lambda b, j, i: (b, i, 0)), x_last,
                  pl.BlockSpec((1, 6, D), lambda b, j, i: (b, 0, 0)),
                  pl.BlockSpec((1, V7X_LANES, V7X_LANES), lambda b, j, i: (j, 0, 0)),
                  pl.BlockSpec((D, sw), lambda b, j, i: (0, j)), pl.BlockSpec((D, sw), lambda b, j, i: (0, j)),
                  pl.BlockSpec((sw, D), lambda b, j, i: (j, 0)), full(sg), full(su), full(sd)],
        out_specs=pl.BlockSpec((1, T, D), lambda b, j, i: (b, 0, 0)),
        compiler_params=pltpu.CompilerParams(dimension_semantics=("parallel", "arbitrary", "arbitrary"),
                                             vmem_limit_bytes=MOE_VMEM_LIMIT_BYTES),
        name="moe_experts",
    )(h2, gates, x, mod, _moe_select_table(), wg, wu, wd, sg, su, sd)


def _reorder_w_in(w):
    o_na = RWKV_COLS
    o_ml = o_na + NA_COLS
    o_mg = o_ml + MLSTM_MAIN_COLS
    o_gate = o_mg + MLSTM_GATES
    pad = jnp.zeros((w.shape[0], V7X_LANES - MLSTM_GATES), w.dtype)
    return jnp.concatenate([w[:, o_na:o_ml], w[:, o_ml:o_mg], w[:, o_gate:], w[:, :RWKV_COLS], w[:, o_mg:o_gate], pad],
                           axis=1).astype(BF)


def kernel(x, c, ctx, c_ctx, w_ada, b_ada, norm1_g, norm2_g, w_in, rw_mu, rw_w0, rw_w2, rw_a0, rw_a2, rw_k_k, rw_k_a, rw_r_k, rw_g2, rw_lnx_g, rw_lnx_b, rw_v0, rw_v1, rw_v2, na_qn_g, na_kn_g, na_rpb, ml_i_bias, ml_f_bias, ml_norm_g, w_branch, w_out, moe_router, moe_bias, moe_w_gate, moe_w_up, moe_w_down, sh_w_gate, sh_w_up, sh_w_down):
    B, S, D = x.shape
    n_ctx = ctx.shape[1]
    tm = 256
    assert S % tm == 0 and n_ctx % tm == 0 and PROJ_COLS == IN_COLS + V7X_LANES - MLSTM_GATES
    rope = rope_tables(S)
    n_cond = B + 1
    cond_pad = (-n_cond) % V7X_SUBLANES
    s_cond = jnp.pad(jnp.concatenate([jax.nn.silu(c), jax.nn.silu(c_ctx)[None]], axis=0), ((0, cond_pad), (0, 0)))
    vf_x = vf_c = None
    for l in range(DEPTH):
        need_ctx = l < DEPTH - 1
        mod = pmm(s_cond, w_ada[l]) + b_ada[l]
        mod_x = mod[:B].reshape(B, 6, D)
        mod_c = jnp.broadcast_to(mod[B].reshape(1, 6, D), (B, 6, D))
        w_proj = _reorder_w_in(w_in[l])
        hx = norm_mod(x, norm1_g[l], mod_x, 0, tm)
        hc = norm_mod(ctx, norm1_g[l], mod_c, 0, tm)
        proj_x = pmm(hx.reshape(B * S, D), w_proj).reshape(B, S, PROJ_COLS)
        proj_c = pmm(hc.reshape(B * n_ctx, D), w_proj).reshape(B, n_ctx, PROJ_COLS)
        vres = None if l == 0 else (rw_v0[l - 1], rw_v1[l - 1], rw_v2[l - 1])
        ya_x, ya_c, vf_x, vf_c = rwkv_mix(proj_x, proj_c, vf_x, vf_c, rw_mu[l], rw_w0[l], rw_w2[l], rw_a0[l], rw_a2[l],
                                          rw_k_k[l], rw_k_a[l], rw_r_k[l], rw_g2[l], rw_lnx_g[l], rw_lnx_b[l],
                                          vres, need_ctx, tm)
        yb_x, yb_c = na_mix(proj_x, proj_c, na_qn_g[l], na_kn_g[l], na_rpb[l], need_ctx, tm)
        yc_x, yc_c = mlstm_mix(proj_x, proj_c, ml_i_bias[l], ml_f_bias[l], ml_norm_g[l], rope, need_ctx, tm)
        wb = w_branch[l].astype(BF)
        wo = w_out[l].astype(BF)
        wg = moe_w_gate[l].transpose(1, 0, 2).reshape(D, N_EXPERTS * D_EXPERT).astype(BF)
        wu = moe_w_up[l].transpose(1, 0, 2).reshape(D, N_EXPERTS * D_EXPERT).astype(BF)
        wd = moe_w_down[l].reshape(N_EXPERTS * D_EXPERT, D).astype(BF)
        shared = (sh_w_gate[l].astype(BF), sh_w_up[l].astype(BF), sh_w_down[l].astype(BF))
        x = merge_apply(ya_x, yb_x, yc_x, proj_x, x, mod_x, wb, wo, tm)
        h2, gates = moe_route(x, norm2_g[l], mod_x, moe_router[l], moe_bias[l], tm)
        x = moe_apply(h2, gates, x, mod_x, wg, wu, wd, *shared, MOE_TOKEN_TILE)
        if need_ctx:
            ctx = merge_apply(ya_c, yb_c, yc_c, proj_c, ctx, mod_c, wb, wo, tm)
            h2, gates = moe_route(ctx, norm2_g[l], mod_c, moe_router[l], moe_bias[l], tm)
            ctx = moe_apply(h2, gates, ctx, mod_c, wg, wu, wd, *shared, tm)
    return x
```

```python
import functools

import numpy as np
import jax
import jax.numpy as jnp
from jax import lax
from jax.experimental import pallas as pl
from jax.experimental.pallas import tpu as pltpu

D_MODEL = 1024
DEPTH = 2
GRID_W = 64
N_DIR = 2
N_BRANCH = 3
BRANCH_WIDTH = 512
RMS_EPS = 1e-6
NEG_INF = -1e30

RWKV_HEAD = 64
RWKV_DECAY_LORA = 64
RWKV_LORA_COLS = 384
RWKV_COLS = 3 * BRANCH_WIDTH + RWKV_LORA_COLS
RWKV_LNX_EPS = 64e-5
RWKV_CHUNK = 64
WKV_SEQS_PER_STEP = 4

NA_HEAD = 64
NA_HEADS = BRANCH_WIDTH // NA_HEAD
NA_WIN_R = 8
NA_WIN_C = 16
NA_COLS = 3 * BRANCH_WIDTH
NA_ROWS_PER_STEP = 2

MLSTM_HEADS = 4
MLSTM_DQK = 64
MLSTM_DV = BRANCH_WIDTH // MLSTM_HEADS
MLSTM_QK = MLSTM_HEADS * MLSTM_DQK
MLSTM_CHUNK = 64
MLSTM_SEQS_PER_STEP = 4
MLSTM_MAIN_COLS = 2 * MLSTM_QK + 2 * BRANCH_WIDTH
MLSTM_GATES = 2 * N_DIR * MLSTM_HEADS
GATE_CAP = 15.0
ROPE_ROT = MLSTM_DQK // 4
ROPE_BASE = 10000.0

N_EXPERTS = 64
TOP_K = 8
N_GROUPS = 8
TOPK_GROUPS = 4
D_EXPERT = 128
ROUTED_SCALE = 2.5
MOE_STEP_EXPERTS = 8
MOE_TOKEN_TILE = 512

GATE_COLS = N_BRANCH * D_MODEL
IN_COLS = RWKV_COLS + NA_COLS + MLSTM_MAIN_COLS + MLSTM_GATES + GATE_COLS

V7X_LANES = 128
V7X_SUBLANES = 8
HP = 2 * RWKV_HEAD
VMEM_LIMIT_BYTES = 48 * 1024 * 1024
MOE_VMEM_LIMIT_BYTES = 56 * 1024 * 1024

PROJ_COLS = 8192
NA_BLK = 0
ML_BLK = 1
ML_OGATE_BLK = (NA_COLS + 2 * MLSTM_QK + BRANCH_WIDTH) // BRANCH_WIDTH
GATE_BLK0 = (NA_COLS + MLSTM_MAIN_COLS) // D_MODEL
RWKV_COL0 = NA_COLS + MLSTM_MAIN_COLS + GATE_COLS
RWKV_BLK0 = RWKV_COL0 // BRANCH_WIDTH
RWKV_LORA_BLK = (RWKV_COL0 + 3 * BRANCH_WIDTH) // RWKV_LORA_COLS
ML_GATE_BLK = (RWKV_COL0 + RWKV_COLS) // V7X_LANES

BF = jnp.bfloat16
HIGHEST = lax.Precision.HIGHEST


def _cparams(*sem):
    return pltpu.CompilerParams(dimension_semantics=sem, vmem_limit_bytes=VMEM_LIMIT_BYTES)


def _bdot(a, b):
    return jnp.dot(a.astype(BF), b.astype(BF), preferred_element_type=jnp.float32)


def _dot_nt(a, b):
    return lax.dot_general(a.astype(BF), b.astype(BF), (((1,), (1,)), ((), ())), preferred_element_type=jnp.float32)


def _dot_tn(a, b):
    return lax.dot_general(a.astype(BF), b.astype(BF), (((0,), (0,)), ((), ())), preferred_element_type=jnp.float32)


def _split_dot(x, w, parts):
    out = None
    rem = x
    for _ in range(parts):
        piece = rem.astype(BF)
        rem = rem - piece.astype(jnp.float32)
        t = jnp.dot(piece, w, preferred_element_type=jnp.float32)
        out = t if out is None else out + t
    return out


def _head_ones(width, head):
    i = np.arange(width) // head
    return jnp.asarray(i[:, None] == i[None, :], BF)


def _stack_pair(x, lo):
    zero = jnp.zeros_like(x)
    return jnp.concatenate([jnp.where(lo, x, zero), jnp.where(lo, zero, x)], axis=0)


def _lo_lanes(n):
    return lax.broadcasted_iota(jnp.int32, (n, HP), 1) < (HP // 2)


def _mm_kernel(x_ref, w_ref, o_ref):
    part = _bdot(x_ref[...], w_ref[...])

    @pl.when(pl.program_id(2) == 0)
    def _():
        o_ref[...] = part

    @pl.when(pl.program_id(2) > 0)
    def _():
        o_ref[...] += part


def _pick_tile(n, cands):
    for c in cands:
        if n % c == 0:
            return c
    return n


def pmm(x, w):
    M, K = x.shape
    N = w.shape[1]
    tm = _pick_tile(M, (1024, 512, 256, 128, 64, 32, 16, 8))
    tn = _pick_tile(N, (1024, 512, 384, 256, 128))
    tk = _pick_tile(K, (1024,)) if K > 1024 else K
    return pl.pallas_call(
        _mm_kernel,
        out_shape=jax.ShapeDtypeStruct((M, N), jnp.float32),
        grid=(M // tm, N // tn, K // tk),
        in_specs=[pl.BlockSpec((tm, tk), lambda i, j, k: (i, k)),
                  pl.BlockSpec((tk, tn), lambda i, j, k: (k, j))],
        out_specs=pl.BlockSpec((tm, tn), lambda i, j, k: (i, j)),
        compiler_params=_cparams("parallel", "parallel", "arbitrary"),
        name="tiled_matmul",
    )(x, w)


def _norm_mod(x, g, scale, shift):
    xn = x * lax.rsqrt(jnp.mean(x * x, axis=-1, keepdims=True) + RMS_EPS)
    return xn * g * (1.0 + scale) + shift


def _norm_proj_kernel(x_ref, g_ref, mod_ref, w_ref, o_ref, h_scr):
    @pl.when(pl.program_id(2) == 0)
    def _():
        h_scr[...] = _norm_mod(x_ref[0], g_ref[...], mod_ref[0, 1:2, :], mod_ref[0, 0:1, :]).astype(BF)

    o_ref[0] = jnp.dot(h_scr[...], w_ref[...], preferred_element_type=jnp.float32)


def norm_proj(x, g, mod, w):
    B, T, D = x.shape
    N = w.shape[1]
    tm = _pick_tile(T, (1024, 512, 256))
    tn = _pick_tile(N, (1024, 512, 256, 128))
    return pl.pallas_call(
        _norm_proj_kernel,
        out_shape=jax.ShapeDtypeStruct((B, T, N), jnp.float32), grid=(B, T // tm, N // tn),
        in_specs=[pl.BlockSpec((1, tm, D), lambda b, i, j: (b, i, 0)), pl.BlockSpec((1, D), lambda b, i, j: (0, 0)),
                  pl.BlockSpec((1, 6, D), lambda b, i, j: (b, 0, 0)), pl.BlockSpec((D, tn), lambda b, i, j: (0, j))],
        out_specs=pl.BlockSpec((1, tm, tn), lambda b, i, j: (b, i, j)),
        scratch_shapes=[pltpu.VMEM((tm, D), BF)],
        compiler_params=_cparams("parallel", "parallel", "arbitrary"),
        name="norm_proj",
    )(x, g.reshape(1, D), mod, w)


def _shifted(z, prev_row, next_row):
    n = z.shape[0]
    row = lax.broadcasted_iota(jnp.int32, z.shape, 0)
    zp = jnp.where(row == 0, prev_row, pltpu.roll(z, 1, axis=0))
    zn = jnp.where(row == n - 1, next_row, pltpu.roll(z, n - 1, axis=0))
    return zp, zn


def _rwkv_feat_kernel(zr_ref, zk_ref, zv_ref, zl_ref, pr_ref, pk_ref, pv_ref, pl_ref, nr_ref, nk_ref, nv_ref, nl_ref,
                      mu_ref, w0_ref, w2_ref, a0_ref, a2_ref, kk_ref, g2_ref, ones_ref, *rest, has_vres):
    if has_vres:
        v0_ref, v1_ref, v2_ref, vf_ref = rest[:4]
        outs = rest[4:]
    else:
        outs = rest
    r_ref, k_ref, v_ref, kn_ref, g_ref, lw0_ref, lw1_ref, ag0_ref, ag1_ref = outs
    C = BRANCH_WIDTH
    first = pl.program_id(1) == 0
    last = pl.program_id(1) == pl.num_programs(1) - 1

    def shift(z_ref, p_ref, n_ref, c0, c1):
        z = z_ref[0]
        prev_row = jnp.where(first, 0.0, p_ref[0, V7X_SUBLANES - 1:V7X_SUBLANES, :])
        next_row = jnp.where(last, 0.0, n_ref[0, 0:1, :])
        zp, zn = _shifted(z, prev_row, next_row)
        return z + mu_ref[:, c0:c1] * (0.5 * (zp + zn) - z)

    r = shift(zr_ref, pr_ref, nr_ref, 0, C)
    k = shift(zk_ref, pk_ref, nk_ref, C, 2 * C)
    v = shift(zv_ref, pv_ref, nv_ref, 2 * C, 3 * C)
    zl = shift(zl_ref, pl_ref, nl_ref, 3 * C, RWKV_COLS)
    wd = jnp.tanh(zl[:, 0:2 * RWKV_DECAY_LORA])
    ad = zl[:, 2 * RWKV_DECAY_LORA:4 * RWKV_DECAY_LORA]
    gd = zl[:, 4 * RWKV_DECAY_LORA:]
    for d, (lw_ref, ag_ref) in enumerate(((lw0_ref, ag0_ref), (lw1_ref, ag1_ref))):
        wl = -jax.nn.softplus(-(w0_ref[d:d + 1, :] + _bdot(wd, w2_ref[d]))) - 0.5
        lw_ref[0] = -jnp.exp(wl)
        ag_ref[0] = jax.nn.sigmoid(a0_ref[d:d + 1, :] + _bdot(ad, a2_ref[d]))
    kq = k * kk_ref[...]
    ss = _split_dot(kq * kq, ones_ref[...], 2)
    kn_ref[0] = kq / jnp.maximum(jnp.sqrt(ss), 1e-12)
    if has_vres:
        lora = _bdot(_bdot(v, v1_ref[...]), v2_ref[...])
        v = v + (vf_ref[0] - v) * jax.nn.sigmoid(v0_ref[...] + lora)
    g_ref[0] = _bdot(jax.nn.sigmoid(gd), g2_ref[...])
    r_ref[0] = r
    k_ref[0] = k
    v_ref[0] = v


def rwkv_features(proj, mu, w0, w2, a0, a2, k_k, g2, vres, v_first, tm):
    B, T, _ = proj.shape
    C = BRANCH_WIDTH
    nt = T // tm
    zpad = jnp.zeros((RWKV_DECAY_LORA, C), jnp.float32)
    pad_dirs = lambda w: jnp.stack([jnp.concatenate([w[0], zpad], 0), jnp.concatenate([zpad, w[1]], 0)]).astype(BF)
    sub = V7X_SUBLANES
    blk = lambda w, j: pl.BlockSpec((1, tm, w), lambda b, i: (b, i, j))
    before = lambda w, j: pl.BlockSpec((1, sub, w), lambda b, i: (b, jnp.maximum(i * (tm // sub) - 1, 0), j))
    after = lambda w, j: pl.BlockSpec((1, sub, w), lambda b, i: (b, jnp.minimum((i + 1) * (tm // sub), T // sub - 1), j))
    tok = pl.BlockSpec((1, tm, C), lambda b, i: (b, i, 0))
    full = lambda a: pl.BlockSpec(a.shape, lambda b, i: (0,) * a.ndim)
    params = [mu.reshape(1, -1), w0, pad_dirs(w2), a0, pad_dirs(a2), k_k.reshape(1, -1), g2.astype(BF),
              _head_ones(C, RWKV_HEAD)]
    pieces = [(C, RWKV_BLK0), (C, RWKV_BLK0 + 1), (C, RWKV_BLK0 + 2), (RWKV_LORA_COLS, RWKV_LORA_BLK)]
    args = [proj] * (3 * len(pieces)) + params
    specs = ([blk(w, j) for w, j in pieces] + [before(w, j) for w, j in pieces] + [after(w, j) for w, j in pieces]
             + [full(a) for a in params])
    if vres is not None:
        v0, v1, v2 = vres
        extra = [v0.reshape(1, -1), jnp.pad(v1, ((0, 0), (0, V7X_LANES - v1.shape[1]))).astype(BF),
                 jnp.pad(v2, ((0, V7X_LANES - v2.shape[0]), (0, 0))).astype(BF)]
        args += extra + [v_first]
        specs += [full(a) for a in extra] + [tok]
    return pl.pallas_call(
        functools.partial(_rwkv_feat_kernel, has_vres=vres is not None),
        out_shape=(jax.ShapeDtypeStruct((B, T, C), jnp.float32),) * 9,
        grid=(B, nt), in_specs=specs, out_specs=(tok,) * 9,
        compiler_params=_cparams("parallel", "parallel"),
        name="rwkv_features",
    )(*args)


def _wkv_kernel(r_ref, lw_ref, kk_ref, a_ref, k_ref, v_ref, ka_ref, s0_ref, *rest, reverse, bb, n_pairs, has_prev):
    if has_prev:
        yprev_ref, y_ref, sout_ref, s_scr = rest
    else:
        y_ref, sout_ref, s_scr = rest
    C = RWKV_CHUNK
    c_idx = pl.program_id(1)

    @pl.when(c_idx == 0)
    def _():
        s_scr[...] = s0_ref[...]

    ti = lax.broadcasted_iota(jnp.int32, (C, C), 0)
    si = lax.broadcasted_iota(jnp.int32, (C, C), 1)
    tri = ((si >= ti) if reverse else (si <= ti)).astype(jnp.float32)
    tp = lax.broadcasted_iota(jnp.int32, (C, 2 * C), 0)
    sp = lax.broadcasted_iota(jnp.int32, (C, 2 * C), 1) % C
    m_strict = (sp > tp) if reverse else (sp < tp)
    m_incl = (sp >= tp) if reverse else (sp <= tp)
    eye = (tp == sp).astype(jnp.float32)
    t2 = lax.broadcasted_iota(jnp.int32, (2 * C, 2 * C), 0)
    s2 = lax.broadcasted_iota(jnp.int32, (2 * C, 2 * C), 1)
    same_head = (t2 // C) == (s2 // C)
    lo = _lo_lanes(C)
    units = [(bi, slice(p * HP, (p + 1) * HP), p) for bi in range(bb) for p in range(n_pairs)]
    n = len(units)
    cat = lambda xs: jnp.concatenate(xs, axis=0)
    stack = lambda x: _stack_pair(x, lo)
    bdiag = lambda x: jnp.where(same_head, cat([x, x]), jnp.zeros((), x.dtype))

    ar, bk, bkh, v, e_tot = [], [], [], [], []
    for bi, sl, _ in units:
        lw = lw_ref[bi, :, sl]
        kk = kk_ref[bi, :, sl]
        ag = a_ref[bi, :, sl]
        kd = k_ref[bi, :, sl] * (1.0 + (ag - 1.0) * ka_ref[:, sl])
        cum = jnp.dot(tri, lw, precision=HIGHEST, preferred_element_type=jnp.float32)
        tot = jnp.sum(lw, axis=0, keepdims=True)
        e_neg = jnp.exp(-cum)
        e_end = jnp.exp(tot - cum)
        b = kk * ag
        ar.append(cat([-kk * jnp.exp(cum - lw), r_ref[bi, :, sl] * jnp.exp(cum)]).astype(BF))
        bk.append(cat([stack(b * e_neg), stack(kd * e_neg)]).astype(BF))
        bkh.append(cat([b * e_end, kd * e_end]).astype(BF))
        v.append(v_ref[bi, :, sl].astype(BF))
        e_tot.append(jnp.exp(tot))
    gram = [_dot_nt(ar[i], bk[i]) for i in range(n)]
    l_ab = [jnp.where(m_strict, g[:C, :2 * C], 0.0) for g in gram]
    l_ak = [jnp.where(m_strict, g[:C, 2 * C:], 0.0).astype(BF) for g in gram]
    l_rbk = [jnp.concatenate([jnp.where(m_incl, g[C:, :2 * C], 0.0), jnp.where(m_incl, g[C:, 2 * C:], 0.0)],
                             axis=1).astype(BF) for g in gram]
    vs = [stack(x) for x in v]
    s0 = [s_scr[bi, p] for bi, _, p in units]
    proj = [_dot_nt(ar[i], s0[i]) for i in range(n)]
    lv = [_bdot(l_ak[i], vs[i]) for i in range(n)]
    tinv = [eye + m for m in l_ab]
    pw_bd = [bdiag(m.astype(BF)) for m in l_ab]
    pw = [_bdot(l_ab[i], pw_bd[i]).astype(BF) for i in range(n)]
    levels = 5
    for lvl in range(1, levels + 1):
        pw_bd = [bdiag(m) for m in pw]
        if lvl < levels:
            both = [_bdot(cat([pw[i], tinv[i].astype(BF)]), pw_bd[i]) for i in range(n)]
            pw = [x[:C].astype(BF) for x in both]
            tinv = [tinv[i] + both[i][C:] for i in range(n)]
        else:
            tinv = [tinv[i] + _bdot(tinv[i], pw_bd[i]) for i in range(n)]
    u = [_bdot(tinv[i], stack((proj[i][:C] + lv[i]).astype(BF))) for i in range(n)]
    ub = [x.astype(BF) for x in u]
    ys = [proj[i][C:] + _bdot(l_rbk[i], cat([stack(ub[i]), vs[i]])) for i in range(n)]
    upd = [_dot_tn(cat([ub[i], v[i]]), bkh[i]) for i in range(n)]
    for i, (bi, sl, p) in enumerate(units):
        y = ys[i]
        if has_prev:
            y = y + yprev_ref[bi, :, sl]
        y_ref[bi, :, sl] = y
        s_scr[bi, p] = s0[i] * e_tot[i] + jnp.where(same_head, upd[i], 0.0)

    @pl.when(c_idx == pl.num_programs(1) - 1)
    def _():
        sout_ref[...] = s_scr[...]


def wkv_chunked(r, lw, kk, ag, k, v, k_a, s0, y_prev, reverse):
    B, T, W = r.shape
    C = RWKV_CHUNK
    bb = WKV_SEQS_PER_STEP
    nc = T // C
    n_pairs = W // HP
    cmap = (lambda b, c: (b, nc - 1 - c, 0)) if reverse else (lambda b, c: (b, c, 0))
    tok = pl.BlockSpec((bb, C, W), cmap)
    st = pl.BlockSpec((bb, n_pairs, HP, HP), lambda b, c: (b, 0, 0, 0))
    has_prev = y_prev is not None
    args = [r, lw, kk, ag, k, v, k_a, s0] + ([y_prev] if has_prev else [])
    return pl.pallas_call(
        functools.partial(_wkv_kernel, reverse=reverse, bb=bb, n_pairs=n_pairs, has_prev=has_prev),
        out_shape=(jax.ShapeDtypeStruct((B, T, W), jnp.float32), jax.ShapeDtypeStruct(s0.shape, jnp.float32)),
        grid=(B // bb, nc),
        in_specs=[tok] * 6 + [pl.BlockSpec((1, W), lambda b, c: (0, 0)), st] + ([tok] if has_prev else []),
        out_specs=(tok, st),
        scratch_shapes=[pltpu.VMEM((bb, n_pairs, HP, HP), jnp.float32)],
        compiler_params=_cparams("parallel", "arbitrary"),
        name="wkv_chunked",
    )(*args)


def _rwkv_readout_kernel(y_ref, r_ref, k_ref, v_ref, g_ref, ag0_ref, ag1_ref, ka_ref, rk_ref, lg_ref, lb_ref,
                         ones_ref, o_ref):
    ones_bd = ones_ref[...]
    y = y_ref[0]
    mean = _split_dot(y, ones_bd, 2) * (1.0 / RWKV_HEAD)
    yc = y - mean
    var = _split_dot(yc * yc, ones_bd, 2) * (1.0 / RWKV_HEAD)
    yn = yc * lax.rsqrt(var + RWKV_LNX_EPS) * lg_ref[...] + lb_ref[...]
    ksum = k_ref[0] * (2.0 + (ag0_ref[0] + ag1_ref[0] - 2.0) * ka_ref[...])
    bonus = _split_dot(r_ref[0] * ksum * rk_ref[...], ones_bd, 2) * v_ref[0]
    o_ref[0] = (yn + bonus) * g_ref[0]


def rwkv_readout(y, r, k, v, g, ag0, ag1, k_a, r_k, lnx_g, lnx_b, tm):
    B, T, C = y.shape
    tok = pl.BlockSpec((1, tm, C), lambda b, i: (b, i, 0))
    par = pl.BlockSpec((1, C), lambda b, i: (0, 0))
    return pl.pallas_call(
        _rwkv_readout_kernel,
        out_shape=jax.ShapeDtypeStruct((B, T, C), jnp.float32), grid=(B, T // tm),
        in_specs=[tok] * 7 + [par] * 4 + [pl.BlockSpec((C, C), lambda b, i: (0, 0))],
        out_specs=tok,
        compiler_params=_cparams("parallel", "parallel"),
        name="rwkv_readout",
    )(y, r, k, v, g, ag0, ag1, k_a, r_k.reshape(1, -1), lnx_g.reshape(1, -1), lnx_b.reshape(1, -1),
      _head_ones(C, RWKV_HEAD))


def rwkv_mix(proj_x, proj_c, vf_x, vf_c, mu, w0, w2, a0, a2, k_k, k_a, r_k, g2, lnx_g, lnx_b, vres, need_ctx, tm):
    B = proj_x.shape[0]
    fx = rwkv_features(proj_x, mu, w0, w2, a0, a2, k_k, g2, vres, vf_x, tm)
    fc = rwkv_features(proj_c, mu, w0, w2, a0, a2, k_k, g2, vres, vf_c, tm)
    ka = k_a.reshape(1, -1)
    y_x = y_c = None
    for d in range(N_DIR):
        s0 = jnp.zeros((B, BRANCH_WIDTH // HP, HP, HP), jnp.float32)
        y_c, s_ctx = wkv_chunked(fc[0], fc[5 + d], fc[3], fc[7 + d], fc[1], fc[2], ka, s0, y_c, d == 1)
        y_x, _ = wkv_chunked(fx[0], fx[5 + d], fx[3], fx[7 + d], fx[1], fx[2], ka, s_ctx, y_x, d == 1)
    out_x = rwkv_readout(y_x, fx[0], fx[1], fx[2], fx[4], fx[7], fx[8], ka, r_k, lnx_g, lnx_b, tm)
    out_c = rwkv_readout(y_c, fc[0], fc[1], fc[2], fc[4], fc[7], fc[8], ka, r_k, lnx_g, lnx_b, tm) if need_ctx else None
    vf_x = fx[2] if vres is None else vf_x
    vf_c = fc[2] if vres is None else vf_c
    return out_x, out_c, vf_x, vf_c


def _qknorm_kernel(z_ref, qg_ref, kg_ref, ones_ref, q_ref, k_ref, v_ref):
    ones_bd = ones_ref[...]
    C = BRANCH_WIDTH
    z = z_ref[0]
    q = z[:, 0:C]
    k = z[:, C:2 * C]
    qn = q * lax.rsqrt(_split_dot(q * q, ones_bd, 2) * (1.0 / NA_HEAD) + RMS_EPS) * qg_ref[...]
    kn = k * lax.rsqrt(_split_dot(k * k, ones_bd, 2) * (1.0 / NA_HEAD) + RMS_EPS) * kg_ref[...]
    q_ref[0] = (qn * NA_HEAD ** -0.5).astype(BF)
    k_ref[0] = kn.astype(BF)
    v_ref[0] = z[:, 2 * C:3 * C].astype(BF)


def na_qknorm(proj, qn_g, kn_g, tm):
    B, T, _ = proj.shape
    C = BRANCH_WIDTH
    tok = pl.BlockSpec((1, tm, C), lambda b, i: (b, i, 0))
    par = pl.BlockSpec((1, C), lambda b, i: (0, 0))
    sd = jax.ShapeDtypeStruct((B, T, C), BF)
    return pl.pallas_call(
        _qknorm_kernel, out_shape=(sd, sd, sd), grid=(B, T // tm),
        in_specs=[pl.BlockSpec((1, tm, NA_COLS), lambda b, i: (b, i, NA_BLK)), par, par,
                  pl.BlockSpec((C, C), lambda b, i: (0, 0))],
        out_specs=(tok, tok, tok),
        compiler_params=_cparams("parallel", "parallel"),
        name="na_qknorm",
    )(proj, jnp.tile(qn_g, NA_HEADS).reshape(1, C), jnp.tile(kn_g, NA_HEADS).reshape(1, C), _head_ones(C, NA_HEAD))


def na_bias_table(rpb):
    qc = np.arange(GRID_W)[:, None]
    kc = np.arange(GRID_W)[None, :]
    cs = np.clip(qc - NA_WIN_C // 2, 0, GRID_W - NA_WIN_C)
    valid = (kc >= cs) & (kc < cs + NA_WIN_C)
    cidx = np.clip(kc - qc + NA_WIN_C - 1, 0, 2 * NA_WIN_C - 2)
    t = jnp.where(valid[None, None], rpb[:, :, cidx], NEG_INF)
    t2 = jnp.concatenate([t[:, :-1], t[:, 1:]], axis=-1)
    H = rpb.shape[0]
    t2 = t2.reshape(H // 2, 2, 2 * NA_WIN_R - 2, GRID_W, 2 * GRID_W).transpose(0, 2, 1, 3, 4)
    return t2.reshape(H // 2, 2 * NA_WIN_R - 2, 2 * GRID_W, 2 * GRID_W)


def _na_kernel(q_ref, k_ref, v_ref, kc_ref, vc_ref, bias_ref, o_ref, *, rows):
    nwin = NA_WIN_R * GRID_W
    lo = _lo_lanes(GRID_W)
    units = []
    for rr in range(NA_ROWS_PER_STEP):
        r = pl.program_id(1) * NA_ROWS_PER_STEP + rr
        rs = jnp.clip(r - NA_WIN_R // 2, 0, rows - NA_WIN_R)
        k0 = pl.multiple_of(rs * GRID_W, GRID_W)
        for p in range(BRANCH_WIDTH // HP):
            units.append((rr, p, slice(p * HP, (p + 1) * HP), rs - r + NA_WIN_R - 1, k0))
    qs = [_stack_pair(q_ref[0, rr * GRID_W:(rr + 1) * GRID_W, sl], lo) for rr, _, sl, _, _ in units]
    s_loc = [_dot_nt(qs[i], k_ref[0, pl.ds(u[4], nwin), u[2]]) for i, u in enumerate(units)]
    s_ctx = [_dot_nt(qs[i], kc_ref[0, :, u[2]]) for i, u in enumerate(units)]
    p_loc, p_ctx, den = [], [], []
    for i, (_, p, _, base, _) in enumerate(units):
        sl_b = s_loc[i] + jnp.concatenate([bias_ref[p, base + 2 * j] for j in range(NA_WIN_R // 2)], axis=1)
        m = jnp.maximum(jnp.max(sl_b, axis=1, keepdims=True), jnp.max(s_ctx[i], axis=1, keepdims=True))
        el = jnp.exp(sl_b - m)
        ec = jnp.exp(s_ctx[i] - m)
        den.append(jnp.sum(el, axis=1, keepdims=True) + jnp.sum(ec, axis=1, keepdims=True))
        p_loc.append(el.astype(BF))
        p_ctx.append(ec.astype(BF))
    o_loc = [_bdot(p_loc[i], v_ref[0, pl.ds(u[4], nwin), u[2]]) for i, u in enumerate(units)]
    o_ctx = [_bdot(p_ctx[i], vc_ref[0, :, u[2]]) for i, u in enumerate(units)]
    for i, (rr, _, sl, _, _) in enumerate(units):
        o = (o_loc[i] + o_ctx[i]) / den[i]
        o_ref[0, rr * GRID_W:(rr + 1) * GRID_W, sl] = jnp.where(lo, o[:GRID_W], o[GRID_W:])


def na_attention(q, k, v, kc, vc, bias_tab):
    B, S, C = q.shape
    rows = S // GRID_W
    n_ctx = kc.shape[1]
    seq = pl.BlockSpec((1, S, C), lambda b, r: (b, 0, 0))
    cx = pl.BlockSpec((1, n_ctx, C), lambda b, r: (b, 0, 0))
    row = pl.BlockSpec((1, NA_ROWS_PER_STEP * GRID_W, C), lambda b, r: (b, r, 0))
    return pl.pallas_call(
        functools.partial(_na_kernel, rows=rows),
        out_shape=jax.ShapeDtypeStruct((B, S, C), jnp.float32),
        grid=(B, rows // NA_ROWS_PER_STEP),
        in_specs=[row, seq, seq, cx, cx, pl.BlockSpec(bias_tab.shape, lambda b, r: (0, 0, 0, 0))],
        out_specs=row,
        compiler_params=_cparams("parallel", "arbitrary"),
        name="na_attention",
    )(q, k, v, kc, vc, bias_tab)


def _ctx_attn_kernel(q_ref, k_ref, v_ref, o_ref):
    n = q_ref.shape[1]
    lo = _lo_lanes(n)
    sls = [slice(p * HP, (p + 1) * HP) for p in range(BRANCH_WIDTH // HP)]
    sc = [_dot_nt(_stack_pair(q_ref[0, :, sl], lo), k_ref[0, :, sl]) for sl in sls]
    e = [jnp.exp(x - jnp.max(x, axis=1, keepdims=True)) for x in sc]
    o = [_bdot(e[p], v_ref[0, :, sl]) / jnp.sum(e[p], axis=1, keepdims=True) for p, sl in enumerate(sls)]
    for p, sl in enumerate(sls):
        o_ref[0, :, sl] = jnp.where(lo, o[p][:n], o[p][n:])


def ctx_attention(q, k, v):
    B, n, C = q.shape
    blk = pl.BlockSpec((1, n, C), lambda b: (b, 0, 0))
    return pl.pallas_call(
        _ctx_attn_kernel, out_shape=jax.ShapeDtypeStruct((B, n, C), jnp.float32), grid=(B,),
        in_specs=[blk, blk, blk], out_specs=blk,
        compiler_params=_cparams("parallel"),
        name="ctx_attention",
    )(q, k, v)


def na_mix(proj_x, proj_c, qn_g, kn_g, rpb, need_ctx, tm):
    q, k, v = na_qknorm(proj_x, qn_g, kn_g, tm)
    qc, kc, vc = na_qknorm(proj_c, qn_g, kn_g, tm)
    out_x = na_attention(q, k, v, kc, vc, na_bias_table(rpb))
    out_c = ctx_attention(qc, kc, vc) if need_ctx else None
    return out_x, out_c


def rope_tables(n_tokens):
    t = jnp.arange(n_tokens)
    pos = jnp.stack([t // GRID_W, t % GRID_W], axis=-1).astype(jnp.float32)
    inv = ROPE_BASE ** (-jnp.arange(ROPE_ROT, dtype=jnp.float32) / ROPE_ROT)
    ang = pos[:, :, None] * inv
    cos_h = jnp.concatenate([jnp.cos(ang), jnp.cos(ang)], axis=-1).reshape(n_tokens, MLSTM_DQK)
    sin_h = jnp.concatenate([-jnp.sin(ang), jnp.sin(ang)], axis=-1).reshape(n_tokens, MLSTM_DQK)
    col = np.arange(MLSTM_QK)
    partner = np.where((col % (2 * ROPE_ROT)) < ROPE_ROT, col + ROPE_ROT, col - ROPE_ROT)
    perm = np.zeros((MLSTM_QK, MLSTM_QK), np.float32)
    perm[partner, col] = 1.0
    return jnp.tile(cos_h, (1, MLSTM_HEADS)), jnp.tile(sin_h, (1, MLSTM_HEADS)), jnp.asarray(perm, BF)


def _mlstm_prep_kernel(z_ref, gp_ref, ib_ref, fb_ref, *rest, rope):
    if rope:
        cos_ref, sin_ref, perm_ref, q_ref, k_ref, v_ref, g_ref = rest
    else:
        q_ref, k_ref, v_ref, g_ref = rest
    z = z_ref[0]
    q = z[:, 0:MLSTM_QK]
    k = z[:, MLSTM_QK:2 * MLSTM_QK]
    if rope:
        perm = perm_ref[...]
        q = q * cos_ref[...] + _split_dot(q, perm, 3) * sin_ref[...]
        k = k * cos_ref[...] + _split_dot(k, perm, 3) * sin_ref[...]
    q_ref[0] = (q * MLSTM_DQK ** -0.5).astype(BF)
    k_ref[0] = k.astype(BF)
    v_ref[0] = z[:, 2 * MLSTM_QK:2 * MLSTM_QK + BRANCH_WIDTH].astype(BF)
    gp = gp_ref[0]
    lane = lax.broadcasted_iota(jnp.int32, gp.shape, 1)
    ig = GATE_CAP * jnp.tanh((gp + ib_ref[...]) / GATE_CAP)
    fg = GATE_CAP * jnp.tanh((gp + fb_ref[...]) / GATE_CAP)
    g_ref[0] = jnp.where(lane < N_DIR * MLSTM_HEADS, ig, jax.nn.log_sigmoid(fg))


def mlstm_prep(proj, i_bias, f_bias, rope, tm):
    B, T, _ = proj.shape
    ng = N_DIR * MLSTM_HEADS
    ib = jnp.zeros((1, V7X_LANES), jnp.float32).at[0, 0:ng].set(i_bias.reshape(-1))
    fb = jnp.zeros((1, V7X_LANES), jnp.float32).at[0, ng:2 * ng].set(f_bias.reshape(-1))
    tok = lambda w: pl.BlockSpec((1, tm, w), lambda b, i: (b, i, 0))
    par = pl.BlockSpec((1, V7X_LANES), lambda b, i: (0, 0))
    args = [proj, proj, ib, fb]
    specs = [pl.BlockSpec((1, tm, MLSTM_MAIN_COLS), lambda b, i: (b, i, ML_BLK)),
             pl.BlockSpec((1, tm, V7X_LANES), lambda b, i: (b, i, ML_GATE_BLK)), par, par]
    if rope is not None:
        args += list(rope)
        specs += [pl.BlockSpec((tm, MLSTM_QK), lambda b, i: (i, 0)), pl.BlockSpec((tm, MLSTM_QK), lambda b, i: (i, 0)),
                  pl.BlockSpec((MLSTM_QK, MLSTM_QK), lambda b, i: (0, 0))]
    return pl.pallas_call(
        functools.partial(_mlstm_prep_kernel, rope=rope is not None),
        out_shape=(jax.ShapeDtypeStruct((B, T, MLSTM_QK), BF), jax.ShapeDtypeStruct((B, T, MLSTM_QK), BF),
                   jax.ShapeDtypeStruct((B, T, BRANCH_WIDTH), BF), jax.ShapeDtypeStruct((B, T, V7X_LANES), jnp.float32)),
        grid=(B, T // tm), in_specs=specs,
        out_specs=(tok(MLSTM_QK), tok(MLSTM_QK), tok(BRANCH_WIDTH), tok(V7X_LANES)),
        compiler_params=_cparams("parallel", "parallel"),
        name="mlstm_prep",
    )(*args)


def _mlstm_kernel(q_ref, k_ref, v_ref, gc_ref, gr_ref, c0_ref, n0_ref, m0_ref, *rest, reverse, direction, bb, has_prev):
    if has_prev:
        hprev_ref, h_ref, cout_ref, nout_ref, mout_ref, c_scr, n_scr, m_scr = rest
    else:
        h_ref, cout_ref, nout_ref, mout_ref, c_scr, n_scr, m_scr = rest
    L = MLSTM_CHUNK
    DV = MLSTM_DV
    H = MLSTM_HEADS
    cidx = pl.program_id(1)

    @pl.when(cidx == 0)
    def _():
        c_scr[...] = c0_ref[...]
        n_scr[...] = n0_ref[...]
        m_scr[...] = m0_ref[...]

    ti = lax.broadcasted_iota(jnp.int32, (L, L), 0)
    si = lax.broadcasted_iota(jnp.int32, (L, L), 1)
    before = (si >= ti) if reverse else (si <= ti)
    tri = before.astype(jnp.float32)
    lo = _lo_lanes(L)
    lane1 = lax.broadcasted_iota(jnp.int32, (1, HP), 1)
    lo_row = lane1 < (HP // 2)
    lo_col = lax.broadcasted_iota(jnp.int32, (HP, 1), 0) < (HP // 2)
    col = lambda a, c: a[:, c:c + 1]
    rowv = lambda a, c: a[c:c + 1, :]
    ci = lambda h: direction * H + h
    cf = lambda h: N_DIR * H + direction * H + h
    units = [(bi, p) for bi in range(bb) for p in range(H // 2)]
    blocks = [(bi, h) for bi in range(bb) for h in range(H)]
    cat = lambda xs: jnp.concatenate(xs, axis=0)
    rows_of = lambda col11: jnp.broadcast_to(col11, (L, 1))

    bcol, icol, brow, irow, tot_b, m_b = [], [], [], [], [], []
    for bi in range(bb):
        gc = gc_ref[bi]
        gr = gr_ref[bi, 0]
        bcol_all = jnp.dot(tri, gc, precision=HIGHEST, preferred_element_type=jnp.float32)
        brow_all = lax.dot_general(gr, tri, (((1,), (1,)), ((), ())), precision=HIGHEST, preferred_element_type=jnp.float32)
        tot_all = jnp.sum(gc, axis=0, keepdims=True)
        m_all = m_scr[bi, 0]
        for h in range(H):
            bcol.append(col(bcol_all, cf(h)))
            icol.append(col(gc, ci(h)))
            brow.append(jnp.broadcast_to(rowv(brow_all, cf(h)), (L, L)))
            irow.append(jnp.broadcast_to(rowv(gr, ci(h)), (L, L)))
            tot_b.append(col(tot_all, cf(h)))
            m_b.append(col(m_all, h))
    nb = len(blocks)
    bcol_r, icol_r = cat(bcol), cat(icol)
    tot_r = cat([rows_of(t) for t in tot_b])
    mprev_r = cat([rows_of(m) for m in m_b])
    causal_r = cat([before] * nb)
    dmat = jnp.where(causal_r, bcol_r - cat(brow) + cat(irow), NEG_INF)
    inter = bcol_r + mprev_r
    m_t = jnp.maximum(inter, jnp.max(dmat, axis=1, keepdims=True))
    decay = jnp.exp(dmat - m_t)
    sc_inter = jnp.exp(inter - m_t)
    inv_floor = jnp.exp(-m_t)

    qs = [_stack_pair(q_ref[bi, :, p * HP:(p + 1) * HP], lo) for bi, p in units]
    kp = [k_ref[bi, :, p * HP:(p + 1) * HP] for bi, p in units]
    vb = [v_ref[bi, :, h * DV:(h + 1) * DV] for bi, h in blocks]
    c_pair = [c_scr[bi, p] for bi, p in units]
    n_pair = [n_scr[bi, p] for bi, p in units]
    qk = cat([_dot_nt(qs[u], kp[u]) for u in range(len(units))])
    qc = cat([_bdot(qs[u], c_pair[u]) for u in range(len(units))])
    qn = cat([jnp.sum(qs[u].astype(jnp.float32) * n_pair[u], axis=1, keepdims=True) for u in range(len(units))])
    smat = qk * decay
    den = sc_inter * qn + jnp.sum(smat, axis=1, keepdims=True)
    sb = smat.astype(BF)
    sv = cat([_bdot(sb[j * L:(j + 1) * L], vb[j]) for j in range(nb)])
    hout = (sc_inter * qc + sv) / jnp.maximum(jnp.abs(den), inv_floor)
    for j, (bi, h) in enumerate(blocks):
        hs = slice(h * DV, (h + 1) * DV)
        part = hout[j * L:(j + 1) * L]
        h_ref[bi, :, hs] = (part + hprev_ref[bi, :, hs]) if has_prev else part

    wlog = tot_r - bcol_r + icol_r
    m_new = [jnp.maximum(tot_b[j] + m_b[j], jnp.max(wlog[j * L:(j + 1) * L], axis=0, keepdims=True)) for j in range(nb)]
    dec = [jnp.exp(tot_b[j] + m_b[j] - m_new[j]) for j in range(nb)]
    wexp = jnp.exp(wlog - cat([rows_of(m) for m in m_new]))
    for u, (bi, p) in enumerate(units):
        j0 = bi * H + 2 * p
        kws = _stack_pair(kp[u], lo).astype(jnp.float32) * wexp[2 * u * L:2 * (u + 1) * L]
        upd = _dot_tn(kws, cat([vb[j0], vb[j0 + 1]]))
        c_scr[bi, p] = jnp.where(lo_col, dec[j0], dec[j0 + 1]) * c_pair[u] + upd
        n_scr[bi, p] = jnp.where(lo_row, dec[j0], dec[j0 + 1]) * n_pair[u] + jnp.sum(kws, axis=0, keepdims=True)
    for bi in range(bb):
        m_all = m_scr[bi, 0]
        for h in range(H):
            m_all = jnp.where(lane1 == h, m_new[bi * H + h], m_all)
        m_scr[bi, 0] = m_all

    @pl.when(cidx == pl.num_programs(1) - 1)
    def _():
        cout_ref[...] = c_scr[...]
        nout_ref[...] = n_scr[...]
        mout_ref[...] = m_scr[...]


def mlstm_chunked(q, k, v, gates, state, h_prev, direction):
    B, T, _ = q.shape
    L = MLSTM_CHUNK
    bb = MLSTM_SEQS_PER_STEP
    nc = T // L
    reverse = direction == 1
    g_rows = gates[:, :, :MLSTM_GATES].reshape(B, nc, L, MLSTM_GATES).transpose(0, 1, 3, 2)
    cm = (lambda c: nc - 1 - c) if reverse else (lambda c: c)
    tok = lambda w: pl.BlockSpec((bb, L, w), lambda b, c: (b, cm(c), 0))
    st = lambda a: pl.BlockSpec((bb,) + a.shape[1:], lambda b, c: (b,) + (0,) * (a.ndim - 1))
    c0, n0, m0 = state
    has_prev = h_prev is not None
    args = [q, k, v, gates, g_rows, c0, n0, m0] + ([h_prev] if has_prev else [])
    outs = pl.pallas_call(
        functools.partial(_mlstm_kernel, reverse=reverse, direction=direction, bb=bb, has_prev=has_prev),
        out_shape=(jax.ShapeDtypeStruct((B, T, BRANCH_WIDTH), jnp.float32),) + tuple(
            jax.ShapeDtypeStruct(a.shape, jnp.float32) for a in state),
        grid=(B // bb, nc),
        in_specs=[tok(MLSTM_QK), tok(MLSTM_QK), tok(BRANCH_WIDTH), tok(V7X_LANES),
                  pl.BlockSpec((bb, 1, MLSTM_GATES, L), lambda b, c: (b, cm(c), 0, 0)), st(c0), st(n0), st(m0)]
                 + ([tok(BRANCH_WIDTH)] if has_prev else []),
        out_specs=(tok(BRANCH_WIDTH), st(c0), st(n0), st(m0)),
        scratch_shapes=[pltpu.VMEM((bb,) + a.shape[1:], jnp.float32) for a in state],
        compiler_params=_cparams("parallel", "arbitrary"),
        name="mlstm_chunked",
    )(*args)
    return outs[0], outs[1:]


def _mlstm_readout_kernel(h_ref, o_ref, g_ref, out_ref):
    for h in range(MLSTM_HEADS):
        sl = slice(h * MLSTM_DV, (h + 1) * MLSTM_DV)
        x = h_ref[0, :, sl]
        xn = x * lax.rsqrt(jnp.mean(x * x, axis=1, keepdims=True) + RMS_EPS)
        out_ref[0, :, sl] = xn * g_ref[:, sl] * jax.nn.sigmoid(o_ref[0, :, sl])


def mlstm_readout(h, proj, norm_g, tm):
    B, T, C = h.shape
    tok = pl.BlockSpec((1, tm, C), lambda b, i: (b, i, 0))
    return pl.pallas_call(
        _mlstm_readout_kernel, out_shape=jax.ShapeDtypeStruct((B, T, C), jnp.float32), grid=(B, T // tm),
        in_specs=[tok, pl.BlockSpec((1, tm, C), lambda b, i: (b, i, ML_OGATE_BLK)), pl.BlockSpec((1, C), lambda b, i: (0, 0))],
        out_specs=tok,
        compiler_params=_cparams("parallel", "parallel"),
        name="mlstm_readout",
    )(h, proj, norm_g.reshape(1, C))


def mlstm_mix(proj_x, proj_c, i_bias, f_bias, norm_g, rope, need_ctx, tm):
    B = proj_x.shape[0]
    qx, kx, vx, gx = mlstm_prep(proj_x, i_bias, f_bias, rope, tm)
    qc, kc, vc, gc = mlstm_prep(proj_c, i_bias, f_bias, None, tm)
    h_x = h_c = None
    for d in range(N_DIR):
        st0 = (jnp.zeros((B, MLSTM_HEADS // 2, HP, HP), jnp.float32), jnp.zeros((B, MLSTM_HEADS // 2, 1, HP), jnp.float32),
               jnp.zeros((B, 1, 1, HP), jnp.float32))
        h_c, st_ctx = mlstm_chunked(qc, kc, vc, gc, st0, h_c, d)
        h_x, _ = mlstm_chunked(qx, kx, vx, gx, st_ctx, h_x, d)
    out_x = mlstm_readout(h_x, proj_x, norm_g, tm)
    out_c = mlstm_readout(h_c, proj_c, norm_g, tm) if need_ctx else None
    return out_x, out_c


def _merge_kernel(ya_ref, yb_ref, yc_ref, ga_ref, gb_ref, gc_ref, x_ref, mod_ref, wb_ref, wo_ref, o_ref):
    merged = None
    for i, (y_ref, g_ref) in enumerate(((ya_ref, ga_ref), (yb_ref, gb_ref), (yc_ref, gc_ref))):
        t = jax.nn.sigmoid(g_ref[0]) * _bdot(y_ref[0], wb_ref[i])
        merged = t if merged is None else merged + t
    o_ref[0] = x_ref[0] + mod_ref[0, 2:3, :] * _bdot(merged, wo_ref[...])


def merge_apply(ya, yb, yc, proj, x, mod, w_branch, w_out, tm):
    B, T, D = x.shape
    tok = lambda w: pl.BlockSpec((1, tm, w), lambda b, i: (b, i, 0))
    gate = lambda k: pl.BlockSpec((1, tm, D), lambda b, i: (b, i, GATE_BLK0 + k))
    return pl.pallas_call(
        _merge_kernel, out_shape=jax.ShapeDtypeStruct((B, T, D), jnp.float32), grid=(B, T // tm),
        in_specs=[tok(BRANCH_WIDTH)] * 3 + [gate(0), gate(1), gate(2), tok(D), pl.BlockSpec((1, 6, D), lambda b, i: (b, 0, 0)),
                  pl.BlockSpec(w_branch.shape, lambda b, i: (0, 0, 0)), pl.BlockSpec(w_out.shape, lambda b, i: (0, 0))],
        out_specs=tok(D),
        compiler_params=_cparams("parallel", "parallel"),
        name="merge_branches",
    )(ya, yb, yc, proj, proj, proj, x, mod, w_branch, w_out)


def _route_kernel(x_ref, g_ref, mod_ref, wr_ref, rb_ref, h_ref, gate_ref):
    h = _norm_mod(x_ref[0], g_ref[...], mod_ref[0, 4:5, :], mod_ref[0, 3:4, :])
    h_ref[0] = h.astype(BF)
    tm = h.shape[0]
    logits = lax.dot_general(wr_ref[...], h, (((1,), (1,)), ((), ())), precision=HIGHEST, preferred_element_type=jnp.float32)
    scores = jax.nn.sigmoid(logits)
    sel = scores + rb_ref[...]
    gsz = N_EXPERTS // N_GROUPS
    grp = sel.reshape(N_GROUPS, gsz, tm)
    iota_in = lax.broadcasted_iota(jnp.int32, grp.shape, 1)
    m1 = jnp.max(grp, axis=1, keepdims=True)
    first = jnp.min(jnp.where(grp == m1, iota_in, gsz), axis=1, keepdims=True)
    m2 = jnp.max(jnp.where(iota_in == first, -jnp.inf, grp), axis=1, keepdims=True)
    gscore = (m1 + m2).reshape(N_GROUPS, tm)
    gi = lax.broadcasted_iota(jnp.int32, (N_GROUPS, tm), 0)
    rank = jnp.zeros((N_GROUPS, tm), jnp.int32)
    for g2 in range(N_GROUPS):
        other = gscore[g2:g2 + 1, :]
        rank = rank + ((other > gscore) | ((other == gscore) & (g2 < gi))).astype(jnp.int32)
    gmask = rank < TOPK_GROUPS
    emask = jnp.broadcast_to(gmask.reshape(N_GROUPS, 1, tm), (N_GROUPS, gsz, tm)).reshape(N_EXPERTS, tm)
    cand = jnp.where(emask, sel, NEG_INF)
    ei = lax.broadcasted_iota(jnp.int32, (N_EXPERTS, tm), 0)
    chosen = jnp.zeros((N_EXPERTS, tm), jnp.bool_)
    for _ in range(TOP_K):
        mx = jnp.max(cand, axis=0, keepdims=True)
        idx = jnp.min(jnp.where(cand == mx, ei, N_EXPERTS), axis=0, keepdims=True)
        hit = ei == idx
        chosen = chosen | hit
        cand = jnp.where(hit, -jnp.inf, cand)
    w = jnp.where(chosen, scores, 0.0)
    w = w / jnp.sum(w, axis=0, keepdims=True) * ROUTED_SCALE
    gate_ref[0] = jnp.concatenate([w, jnp.zeros((V7X_LANES - N_EXPERTS, tm), jnp.float32)], axis=0).T


def moe_route(x, g, mod, w_router, router_bias, tm):
    B, T, D = x.shape
    tok = pl.BlockSpec((1, tm, D), lambda b, i: (b, i, 0))
    return pl.pallas_call(
        _route_kernel,
        out_shape=(jax.ShapeDtypeStruct((B, T, D), BF), jax.ShapeDtypeStruct((B, T, V7X_LANES), jnp.float32)),
        grid=(B, T // tm),
        in_specs=[tok, pl.BlockSpec((1, D), lambda b, i: (0, 0)), pl.BlockSpec((1, 6, D), lambda b, i: (b, 0, 0)),
                  pl.BlockSpec((N_EXPERTS, D), lambda b, i: (0, 0)), pl.BlockSpec((N_EXPERTS, 1), lambda b, i: (0, 0))],
        out_specs=(tok, pl.BlockSpec((1, tm, V7X_LANES), lambda b, i: (b, i, 0))),
        compiler_params=_cparams("parallel", "parallel"),
        name="moe_route",
    )(x, g.reshape(1, D), mod, w_router.T, router_bias.reshape(N_EXPERTS, 1))


def _moe_kernel(h_ref, gate_ref, x_ref, mod_ref, sel_ref, wg_ref, wu_ref, wd_ref, sg_ref, su_ref, sd_ref, o_ref, *, tm):
    j = pl.program_id(1)
    rows = pl.ds(pl.multiple_of(pl.program_id(2) * tm, tm), tm)
    h = h_ref[0]

    @pl.when(j == 0)
    def _():
        sh = jax.nn.silu(_bdot(h, sg_ref[...])) * _bdot(h, su_ref[...])
        o_ref[0, rows, :] = _bdot(sh, sd_ref[...])

    g8 = _split_dot(gate_ref[0], sel_ref[0], 2)
    act = jax.nn.silu(_bdot(h, wg_ref[...])) * _bdot(h, wu_ref[...])
    act = jnp.concatenate([act[:, e * D_EXPERT:(e + 1) * D_EXPERT] * g8[:, e:e + 1] for e in range(MOE_STEP_EXPERTS)],
                          axis=1)
    o_ref[0, rows, :] += _bdot(act, wd_ref[...])

    @pl.when(j == pl.num_programs(1) - 1)
    def _():
        o_ref[0, rows, :] = x_ref[0] + mod_ref[0, 5:6, :] * o_ref[0, rows, :]


def _moe_select_table():
    se = MOE_STEP_EXPERTS
    t = np.zeros((N_EXPERTS // se, V7X_LANES, V7X_LANES), np.float32)
    for j in range(N_EXPERTS // se):
        for e in range(se):
            t[j, j * se + e, e] = 1.0
    return jnp.asarray(t, BF)


def moe_apply(h2, gates, x, mod, wg, wu, wd, sg, su, sd, tm):
    B, T, D = x.shape
    sw = MOE_STEP_EXPERTS * D_EXPERT
    n_groups = N_EXPERTS // MOE_STEP_EXPERTS
    tok = pl.BlockSpec((1, tm, D), lambda b, j, i: (b, i, 0))
    x_last = pl.BlockSpec((1, tm, D), lambda b, j, i: (b, jnp.where(j == n_groups - 1, i, 0), 0))
    full = lambda a: pl.BlockSpec(a.shape, lambda b, j, i: (0,) * a.ndim)
    return pl.pallas_call(
        functools.partial(_moe_kernel, tm=tm), out_shape=jax.ShapeDtypeStruct((B, T, D), jnp.float32),
        grid=(B, n_groups, T // tm),
        in_specs=[tok, pl.BlockSpec((1, tm, V7X_LANES), lambda b, j, i: (b, i, 0)), x_last,
                  pl.BlockSpec((1, 6, D), lambda b, j, i: (b, 0, 0)),
                  pl.BlockSpec((1, V7X_LANES, V7X_LANES), lambda b, j, i: (j, 0, 0)),
                  pl.BlockSpec((D, sw), lambda b, j, i: (0, j)), pl.BlockSpec((D, sw), lambda b, j, i: (0, j)),
                  pl.BlockSpec((sw, D), lambda b, j, i: (j, 0)), full(sg), full(su), full(sd)],
        out_specs=pl.BlockSpec((1, T, D), lambda b, j, i: (b, 0, 0)),
        compiler_params=pltpu.CompilerParams(dimension_semantics=("parallel", "arbitrary", "arbitrary"),
                                             vmem_limit_bytes=MOE_VMEM_LIMIT_BYTES),
        name="moe_experts",
    )(h2, gates, x, mod, _moe_select_table(), wg, wu, wd, sg, su, sd)


def _reorder_w_in(w):
    o_na = RWKV_COLS
    o_ml = o_na + NA_COLS
    o_mg = o_ml + MLSTM_MAIN_COLS
    o_gate = o_mg + MLSTM_GATES
    pad = jnp.zeros((w.shape[0], V7X_LANES - MLSTM_GATES), w.dtype)
    return jnp.concatenate([w[:, o_na:o_ml], w[:, o_ml:o_mg], w[:, o_gate:], w[:, :RWKV_COLS], w[:, o_mg:o_gate], pad],
                           axis=1).astype(BF)


def kernel(x, c, ctx, c_ctx, w_ada, b_ada, norm1_g, norm2_g, w_in, rw_mu, rw_w0, rw_w2, rw_a0, rw_a2, rw_k_k, rw_k_a, rw_r_k, rw_g2, rw_lnx_g, rw_lnx_b, rw_v0, rw_v1, rw_v2, na_qn_g, na_kn_g, na_rpb, ml_i_bias, ml_f_bias, ml_norm_g, w_branch, w_out, moe_router, moe_bias, moe_w_gate, moe_w_up, moe_w_down, sh_w_gate, sh_w_up, sh_w_down):
    B, S, D = x.shape
    n_ctx = ctx.shape[1]
    tm = 256
    assert S % tm == 0 and n_ctx % tm == 0 and PROJ_COLS == IN_COLS + V7X_LANES - MLSTM_GATES
    rope = rope_tables(S)
    n_cond = B + 1
    cond_pad = (-n_cond) % V7X_SUBLANES
    s_cond = jnp.pad(jnp.concatenate([jax.nn.silu(c), jax.nn.silu(c_ctx)[None]], axis=0), ((0, cond_pad), (0, 0)))
    vf_x = vf_c = None
    for l in range(DEPTH):
        need_ctx = l < DEPTH - 1
        mod = pmm(s_cond, w_ada[l]) + b_ada[l]
        mod_x = mod[:B].reshape(B, 6, D)
        mod_c = jnp.broadcast_to(mod[B].reshape(1, 6, D), (B, 6, D))
        w_proj = _reorder_w_in(w_in[l])
        proj_x = norm_proj(x, norm1_g[l], mod_x, w_proj)
        proj_c = norm_proj(ctx, norm1_g[l], mod_c, w_proj)
        vres = None if l == 0 else (rw_v0[l - 1], rw_v1[l - 1], rw_v2[l - 1])
        ya_x, ya_c, vf_x, vf_c = rwkv_mix(proj_x, proj_c, vf_x, vf_c, rw_mu[l], rw_w0[l], rw_w2[l], rw_a0[l], rw_a2[l],
                                          rw_k_k[l], rw_k_a[l], rw_r_k[l], rw_g2[l], rw_lnx_g[l], rw_lnx_b[l],
                                          vres, need_ctx, tm)
        yb_x, yb_c = na_mix(proj_x, proj_c, na_qn_g[l], na_kn_g[l], na_rpb[l], need_ctx, tm)
        yc_x, yc_c = mlstm_mix(proj_x, proj_c, ml_i_bias[l], ml_f_bias[l], ml_norm_g[l], rope, need_ctx, tm)
        wb = w_branch[l].astype(BF)
        wo = w_out[l].astype(BF)
        wg = moe_w_gate[l].transpose(1, 0, 2).reshape(D, N_EXPERTS * D_EXPERT).astype(BF)
        wu = moe_w_up[l].transpose(1, 0, 2).reshape(D, N_EXPERTS * D_EXPERT).astype(BF)
        wd = moe_w_down[l].reshape(N_EXPERTS * D_EXPERT, D).astype(BF)
        shared = (sh_w_gate[l].astype(BF), sh_w_up[l].astype(BF), sh_w_down[l].astype(BF))
        x = merge_apply(ya_x, yb_x, yc_x, proj_x, x, mod_x, wb, wo, tm)
        h2, gates = moe_route(x, norm2_g[l], mod_x, moe_router[l], moe_bias[l], tm)
        x = moe_apply(h2, gates, x, mod_x, wg, wu, wd, *shared, MOE_TOKEN_TILE)
        if need_ctx:
            ctx = merge_apply(ya_c, yb_c, yc_c, proj_c, ctx, mod_c, wb, wo, tm)
            h2, gates = moe_route(ctx, norm2_g[l], mod_c, moe_router[l], moe_bias[l], tm)
            ctx = moe_apply(h2, gates, ctx, mod_c, wg, wu, wd, *shared, tm)
    return x
```

```python
import functools

import numpy as np
import jax
import jax.numpy as jnp
from jax import lax
from jax.experimental import pallas as pl
from jax.experimental.pallas import tpu as pltpu

D_MODEL = 1024
DEPTH = 2
GRID_W = 64
N_DIR = 2
N_BRANCH = 3
BRANCH_WIDTH = 512
RMS_EPS = 1e-6
NEG_INF = -1e30

RWKV_HEAD = 64
RWKV_DECAY_LORA = 64
RWKV_LORA_COLS = 384
RWKV_COLS = 3 * BRANCH_WIDTH + RWKV_LORA_COLS
RWKV_LNX_EPS = 64e-5
RWKV_CHUNK = 64
WKV_SEQS_PER_STEP = 4

NA_HEAD = 64
NA_HEADS = BRANCH_WIDTH // NA_HEAD
NA_WIN_R = 8
NA_WIN_C = 16
NA_COLS = 3 * BRANCH_WIDTH
NA_ROWS_PER_STEP = 2

MLSTM_HEADS = 4
MLSTM_DQK = 64
MLSTM_DV = BRANCH_WIDTH // MLSTM_HEADS
MLSTM_QK = MLSTM_HEADS * MLSTM_DQK
MLSTM_CHUNK = 64
MLSTM_SEQS_PER_STEP = 4
MLSTM_MAIN_COLS = 2 * MLSTM_QK + 2 * BRANCH_WIDTH
MLSTM_GATES = 2 * N_DIR * MLSTM_HEADS
GATE_CAP = 15.0
ROPE_ROT = MLSTM_DQK // 4
ROPE_BASE = 10000.0

N_EXPERTS = 64
TOP_K = 8
N_GROUPS = 8
TOPK_GROUPS = 4
D_EXPERT = 128
ROUTED_SCALE = 2.5
MOE_STEP_EXPERTS = 8
MOE_TOKEN_TILE = 512

GATE_COLS = N_BRANCH * D_MODEL
IN_COLS = RWKV_COLS + NA_COLS + MLSTM_MAIN_COLS + MLSTM_GATES + GATE_COLS

V7X_LANES = 128
V7X_SUBLANES = 8
HP = 2 * RWKV_HEAD
VMEM_LIMIT_BYTES = 48 * 1024 * 1024
MOE_VMEM_LIMIT_BYTES = 56 * 1024 * 1024

PROJ_COLS = 8192
NA_BLK = 0
ML_BLK = 1
ML_OGATE_BLK = (NA_COLS + 2 * MLSTM_QK + BRANCH_WIDTH) // BRANCH_WIDTH
GATE_BLK0 = (NA_COLS + MLSTM_MAIN_COLS) // D_MODEL
RWKV_COL0 = NA_COLS + MLSTM_MAIN_COLS + GATE_COLS
RWKV_BLK0 = RWKV_COL0 // BRANCH_WIDTH
RWKV_LORA_BLK = (RWKV_COL0 + 3 * BRANCH_WIDTH) // RWKV_LORA_COLS
ML_GATE_BLK = (RWKV_COL0 + RWKV_COLS) // V7X_LANES

BF = jnp.bfloat16
ACT_DTYPE = BF
HALO_ROWS = 16
HIGHEST = lax.Precision.HIGHEST


def _cparams(*sem):
    return pltpu.CompilerParams(dimension_semantics=sem, vmem_limit_bytes=VMEM_LIMIT_BYTES)


def _bdot(a, b):
    return jnp.dot(a.astype(BF), b.astype(BF), preferred_element_type=jnp.float32)


def _dot_nt(a, b):
    return lax.dot_general(a.astype(BF), b.astype(BF), (((1,), (1,)), ((), ())), preferred_element_type=jnp.float32)


def _dot_tn(a, b):
    return lax.dot_general(a.astype(BF), b.astype(BF), (((0,), (0,)), ((), ())), preferred_element_type=jnp.float32)


def _split_dot(x, w, parts):
    out = None
    rem = x
    for _ in range(parts):
        piece = rem.astype(BF)
        rem = rem - piece.astype(jnp.float32)
        t = jnp.dot(piece, w, preferred_element_type=jnp.float32)
        out = t if out is None else out + t
    return out


def _head_ones(width, head):
    i = np.arange(width) // head
    return jnp.asarray(i[:, None] == i[None, :], BF)


def _stack_pair(x, lo):
    zero = jnp.zeros_like(x)
    return jnp.concatenate([jnp.where(lo, x, zero), jnp.where(lo, zero, x)], axis=0)


def _lo_lanes(n):
    return lax.broadcasted_iota(jnp.int32, (n, HP), 1) < (HP // 2)


def _mm_kernel(x_ref, w_ref, o_ref):
    part = _bdot(x_ref[...], w_ref[...])

    @pl.when(pl.program_id(2) == 0)
    def _():
        o_ref[...] = part

    @pl.when(pl.program_id(2) > 0)
    def _():
        o_ref[...] += part


def _pick_tile(n, cands):
    for c in cands:
        if n % c == 0:
            return c
    return n


def pmm(x, w):
    M, K = x.shape
    N = w.shape[1]
    tm = _pick_tile(M, (1024, 512, 256, 128, 64, 32, 16, 8))
    tn = _pick_tile(N, (1024, 512, 384, 256, 128))
    tk = _pick_tile(K, (1024,)) if K > 1024 else K
    return pl.pallas_call(
        _mm_kernel,
        out_shape=jax.ShapeDtypeStruct((M, N), jnp.float32),
        grid=(M // tm, N // tn, K // tk),
        in_specs=[pl.BlockSpec((tm, tk), lambda i, j, k: (i, k)),
                  pl.BlockSpec((tk, tn), lambda i, j, k: (k, j))],
        out_specs=pl.BlockSpec((tm, tn), lambda i, j, k: (i, j)),
        compiler_params=_cparams("parallel", "parallel", "arbitrary"),
        name="tiled_matmul",
    )(x, w)


def _norm_mod(x, g, scale, shift):
    xn = x * lax.rsqrt(jnp.mean(x * x, axis=-1, keepdims=True) + RMS_EPS)
    return xn * g * (1.0 + scale) + shift


def _norm_proj_kernel(x_ref, g_ref, mod_ref, w_ref, o_ref, h_scr):
    @pl.when(pl.program_id(2) == 0)
    def _():
        h_scr[...] = _norm_mod(x_ref[0], g_ref[...], mod_ref[0, 1:2, :], mod_ref[0, 0:1, :]).astype(BF)

    o_ref[0] = jnp.dot(h_scr[...], w_ref[...], preferred_element_type=jnp.float32).astype(o_ref.dtype)


def norm_proj(x, g, mod, w):
    B, T, D = x.shape
    N = w.shape[1]
    tm = _pick_tile(T, (1024, 512, 256))
    tn = _pick_tile(N, (1024, 512, 256, 128))
    return pl.pallas_call(
        _norm_proj_kernel,
        out_shape=jax.ShapeDtypeStruct((B, T, N), ACT_DTYPE), grid=(B, T // tm, N // tn),
        in_specs=[pl.BlockSpec((1, tm, D), lambda b, i, j: (b, i, 0)), pl.BlockSpec((1, D), lambda b, i, j: (0, 0)),
                  pl.BlockSpec((1, 6, D), lambda b, i, j: (b, 0, 0)), pl.BlockSpec((D, tn), lambda b, i, j: (0, j))],
        out_specs=pl.BlockSpec((1, tm, tn), lambda b, i, j: (b, i, j)),
        scratch_shapes=[pltpu.VMEM((tm, D), BF)],
        compiler_params=_cparams("parallel", "parallel", "arbitrary"),
        name="norm_proj",
    )(x, g.reshape(1, D), mod, w)


def _shifted(z, prev_row, next_row):
    n = z.shape[0]
    row = lax.broadcasted_iota(jnp.int32, z.shape, 0)
    zp = jnp.where(row == 0, prev_row, pltpu.roll(z, 1, axis=0))
    zn = jnp.where(row == n - 1, next_row, pltpu.roll(z, n - 1, axis=0))
    return zp, zn


def _rwkv_feat_kernel(zr_ref, zk_ref, zv_ref, zl_ref, pr_ref, pk_ref, pv_ref, pl_ref, nr_ref, nk_ref, nv_ref, nl_ref,
                      mu_ref, w0_ref, w2_ref, a0_ref, a2_ref, kk_ref, g2_ref, ones_ref, *rest, has_vres):
    if has_vres:
        v0_ref, v1_ref, v2_ref, vf_ref = rest[:4]
        outs = rest[4:]
    else:
        outs = rest
    r_ref, k_ref, v_ref, kn_ref, g_ref, lw0_ref, lw1_ref, ag0_ref, ag1_ref = outs
    C = BRANCH_WIDTH
    first = pl.program_id(1) == 0
    last = pl.program_id(1) == pl.num_programs(1) - 1

    def shift(z_ref, p_ref, n_ref, c0, c1):
        z = z_ref[0].astype(jnp.float32)
        prev_row = jnp.where(first, 0.0, p_ref[0].astype(jnp.float32)[HALO_ROWS - 1:HALO_ROWS, :])
        next_row = jnp.where(last, 0.0, n_ref[0].astype(jnp.float32)[0:1, :])
        zp, zn = _shifted(z, prev_row, next_row)
        return z + mu_ref[:, c0:c1] * (0.5 * (zp + zn) - z)

    r = shift(zr_ref, pr_ref, nr_ref, 0, C)
    k = shift(zk_ref, pk_ref, nk_ref, C, 2 * C)
    v = shift(zv_ref, pv_ref, nv_ref, 2 * C, 3 * C)
    zl = shift(zl_ref, pl_ref, nl_ref, 3 * C, RWKV_COLS)
    wd = jnp.tanh(zl[:, 0:2 * RWKV_DECAY_LORA])
    ad = zl[:, 2 * RWKV_DECAY_LORA:4 * RWKV_DECAY_LORA]
    gd = zl[:, 4 * RWKV_DECAY_LORA:]
    for d, (lw_ref, ag_ref) in enumerate(((lw0_ref, ag0_ref), (lw1_ref, ag1_ref))):
        wl = -jax.nn.softplus(-(w0_ref[d:d + 1, :] + _bdot(wd, w2_ref[d]))) - 0.5
        lw_ref[0] = -jnp.exp(wl)
        ag_ref[0] = jax.nn.sigmoid(a0_ref[d:d + 1, :] + _bdot(ad, a2_ref[d])).astype(ag_ref.dtype)
    kq = k * kk_ref[...]
    ss = _split_dot(kq * kq, ones_ref[...], 2)
    kn_ref[0] = (kq / jnp.maximum(jnp.sqrt(ss), 1e-12)).astype(kn_ref.dtype)
    if has_vres:
        lora = _bdot(_bdot(v, v1_ref[...]), v2_ref[...])
        v = v + (vf_ref[0].astype(jnp.float32) - v) * jax.nn.sigmoid(v0_ref[...] + lora)
    g_ref[0] = _bdot(jax.nn.sigmoid(gd), g2_ref[...]).astype(g_ref.dtype)
    r_ref[0] = r.astype(r_ref.dtype)
    k_ref[0] = k.astype(k_ref.dtype)
    v_ref[0] = v.astype(v_ref.dtype)


def rwkv_features(proj, mu, w0, w2, a0, a2, k_k, g2, vres, v_first, tm):
    B, T, _ = proj.shape
    C = BRANCH_WIDTH
    nt = T // tm
    zpad = jnp.zeros((RWKV_DECAY_LORA, C), jnp.float32)
    pad_dirs = lambda w: jnp.stack([jnp.concatenate([w[0], zpad], 0), jnp.concatenate([zpad, w[1]], 0)]).astype(BF)
    sub = HALO_ROWS
    blk = lambda w, j: pl.BlockSpec((1, tm, w), lambda b, i: (b, i, j))
    before = lambda w, j: pl.BlockSpec((1, sub, w), lambda b, i: (b, jnp.maximum(i * (tm // sub) - 1, 0), j))
    after = lambda w, j: pl.BlockSpec((1, sub, w), lambda b, i: (b, jnp.minimum((i + 1) * (tm // sub), T // sub - 1), j))
    tok = pl.BlockSpec((1, tm, C), lambda b, i: (b, i, 0))
    full = lambda a: pl.BlockSpec(a.shape, lambda b, i: (0,) * a.ndim)
    params = [mu.reshape(1, -1), w0, pad_dirs(w2), a0, pad_dirs(a2), k_k.reshape(1, -1), g2.astype(BF),
              _head_ones(C, RWKV_HEAD)]
    pieces = [(C, RWKV_BLK0), (C, RWKV_BLK0 + 1), (C, RWKV_BLK0 + 2), (RWKV_LORA_COLS, RWKV_LORA_BLK)]
    args = [proj] * (3 * len(pieces)) + params
    specs = ([blk(w, j) for w, j in pieces] + [before(w, j) for w, j in pieces] + [after(w, j) for w, j in pieces]
             + [full(a) for a in params])
    if vres is not None:
        v0, v1, v2 = vres
        extra = [v0.reshape(1, -1), jnp.pad(v1, ((0, 0), (0, V7X_LANES - v1.shape[1]))).astype(BF),
                 jnp.pad(v2, ((0, V7X_LANES - v2.shape[0]), (0, 0))).astype(BF)]
        args += extra + [v_first]
        specs += [full(a) for a in extra] + [tok]
    return pl.pallas_call(
        functools.partial(_rwkv_feat_kernel, has_vres=vres is not None),
        out_shape=(jax.ShapeDtypeStruct((B, T, C), ACT_DTYPE),) * 5 + (jax.ShapeDtypeStruct((B, T, C), jnp.float32),) * 2
                  + (jax.ShapeDtypeStruct((B, T, C), ACT_DTYPE),) * 2,
        grid=(B, nt), in_specs=specs, out_specs=(tok,) * 9,
        compiler_params=_cparams("parallel", "parallel"),
        name="rwkv_features",
    )(*args)


def _wkv_kernel(r_ref, lw_ref, kk_ref, a_ref, k_ref, v_ref, ka_ref, s0_ref, *rest, reverse, bb, n_pairs, has_prev):
    if has_prev:
        yprev_ref, y_ref, sout_ref, s_scr = rest
    else:
        y_ref, sout_ref, s_scr = rest
    C = RWKV_CHUNK
    c_idx = pl.program_id(1)

    @pl.when(c_idx == 0)
    def _():
        s_scr[...] = s0_ref[...]

    ti = lax.broadcasted_iota(jnp.int32, (C, C), 0)
    si = lax.broadcasted_iota(jnp.int32, (C, C), 1)
    tri = ((si >= ti) if reverse else (si <= ti)).astype(jnp.float32)
    tp = lax.broadcasted_iota(jnp.int32, (C, 2 * C), 0)
    sp = lax.broadcasted_iota(jnp.int32, (C, 2 * C), 1) % C
    m_strict = (sp > tp) if reverse else (sp < tp)
    m_incl = (sp >= tp) if reverse else (sp <= tp)
    eye = (tp == sp).astype(jnp.float32)
    t2 = lax.broadcasted_iota(jnp.int32, (2 * C, 2 * C), 0)
    s2 = lax.broadcasted_iota(jnp.int32, (2 * C, 2 * C), 1)
    same_head = (t2 // C) == (s2 // C)
    lo = _lo_lanes(C)
    units = [(bi, slice(p * HP, (p + 1) * HP), p) for bi in range(bb) for p in range(n_pairs)]
    n = len(units)
    cat = lambda xs: jnp.concatenate(xs, axis=0)
    stack = lambda x: _stack_pair(x, lo)
    bdiag = lambda x: jnp.where(same_head, cat([x, x]), jnp.zeros((), x.dtype))

    ar, bk, bkh, v, e_tot = [], [], [], [], []
    for bi, sl, _ in units:
        lw = lw_ref[bi, :, sl]
        kk = kk_ref[bi, :, sl].astype(jnp.float32)
        ag = a_ref[bi, :, sl].astype(jnp.float32)
        kd = k_ref[bi, :, sl].astype(jnp.float32) * (1.0 + (ag - 1.0) * ka_ref[:, sl])
        cum = jnp.dot(tri, lw, precision=HIGHEST, preferred_element_type=jnp.float32)
        tot = jnp.sum(lw, axis=0, keepdims=True)
        e_neg = jnp.exp(-cum)
        e_end = jnp.exp(tot - cum)
        b = kk * ag
        ar.append(cat([-kk * jnp.exp(cum - lw), r_ref[bi, :, sl].astype(jnp.float32) * jnp.exp(cum)]).astype(BF))
        bk.append(cat([stack(b * e_neg), stack(kd * e_neg)]).astype(BF))
        bkh.append(cat([b * e_end, kd * e_end]).astype(BF))
        v.append(v_ref[bi, :, sl].astype(BF))
        e_tot.append(jnp.exp(tot))
    gram = [_dot_nt(ar[i], bk[i]) for i in range(n)]
    l_ab = [jnp.where(m_strict, g[:C, :2 * C], 0.0) for g in gram]
    l_ak = [jnp.where(m_strict, g[:C, 2 * C:], 0.0).astype(BF) for g in gram]
    l_rbk = [jnp.concatenate([jnp.where(m_incl, g[C:, :2 * C], 0.0), jnp.where(m_incl, g[C:, 2 * C:], 0.0)],
                             axis=1).astype(BF) for g in gram]
    vs = [stack(x) for x in v]
    s0 = [s_scr[bi, p] for bi, _, p in units]
    proj = [_dot_nt(ar[i], s0[i]) for i in range(n)]
    lv = [_bdot(l_ak[i], vs[i]) for i in range(n)]
    tinv = [eye + m for m in l_ab]
    pw_bd = [bdiag(m.astype(BF)) for m in l_ab]
    pw = [_bdot(l_ab[i], pw_bd[i]).astype(BF) for i in range(n)]
    levels = 5
    for lvl in range(1, levels + 1):
        pw_bd = [bdiag(m) for m in pw]
        if lvl < levels:
            both = [_bdot(cat([pw[i], tinv[i].astype(BF)]), pw_bd[i]) for i in range(n)]
            pw = [x[:C].astype(BF) for x in both]
            tinv = [tinv[i] + both[i][C:] for i in range(n)]
        else:
            tinv = [tinv[i] + _bdot(tinv[i], pw_bd[i]) for i in range(n)]
    u = [_bdot(tinv[i], stack((proj[i][:C] + lv[i]).astype(BF))) for i in range(n)]
    ub = [x.astype(BF) for x in u]
    ys = [proj[i][C:] + _bdot(l_rbk[i], cat([stack(ub[i]), vs[i]])) for i in range(n)]
    upd = [_dot_tn(cat([ub[i], v[i]]), bkh[i]) for i in range(n)]
    for i, (bi, sl, p) in enumerate(units):
        y = ys[i]
        if has_prev:
            y = y + yprev_ref[bi, :, sl]
        y_ref[bi, :, sl] = y
        s_scr[bi, p] = s0[i] * e_tot[i] + jnp.where(same_head, upd[i], 0.0)

    @pl.when(c_idx == pl.num_programs(1) - 1)
    def _():
        sout_ref[...] = s_scr[...]


def wkv_chunked(r, lw, kk, ag, k, v, k_a, s0, y_prev, reverse):
    B, T, W = r.shape
    C = RWKV_CHUNK
    bb = WKV_SEQS_PER_STEP
    nc = T // C
    n_pairs = W // HP
    cmap = (lambda b, c: (b, nc - 1 - c, 0)) if reverse else (lambda b, c: (b, c, 0))
    tok = pl.BlockSpec((bb, C, W), cmap)
    st = pl.BlockSpec((bb, n_pairs, HP, HP), lambda b, c: (b, 0, 0, 0))
    has_prev = y_prev is not None
    args = [r, lw, kk, ag, k, v, k_a, s0] + ([y_prev] if has_prev else [])
    return pl.pallas_call(
        functools.partial(_wkv_kernel, reverse=reverse, bb=bb, n_pairs=n_pairs, has_prev=has_prev),
        out_shape=(jax.ShapeDtypeStruct((B, T, W), jnp.float32), jax.ShapeDtypeStruct(s0.shape, jnp.float32)),
        grid=(B // bb, nc),
        in_specs=[tok] * 6 + [pl.BlockSpec((1, W), lambda b, c: (0, 0)), st] + ([tok] if has_prev else []),
        out_specs=(tok, st),
        scratch_shapes=[pltpu.VMEM((bb, n_pairs, HP, HP), jnp.float32)],
        compiler_params=_cparams("parallel", "arbitrary"),
        name="wkv_chunked",
    )(*args)


def _rwkv_readout_kernel(y_ref, r_ref, k_ref, v_ref, g_ref, ag0_ref, ag1_ref, ka_ref, rk_ref, lg_ref, lb_ref,
                         ones_ref, o_ref):
    ones_bd = ones_ref[...]
    y = y_ref[0]
    mean = _split_dot(y, ones_bd, 2) * (1.0 / RWKV_HEAD)
    yc = y - mean
    var = _split_dot(yc * yc, ones_bd, 2) * (1.0 / RWKV_HEAD)
    yn = yc * lax.rsqrt(var + RWKV_LNX_EPS) * lg_ref[...] + lb_ref[...]
    f32 = lambda ref: ref[0].astype(jnp.float32)
    ksum = f32(k_ref) * (2.0 + (f32(ag0_ref) + f32(ag1_ref) - 2.0) * ka_ref[...])
    bonus = _split_dot(f32(r_ref) * ksum * rk_ref[...], ones_bd, 2) * f32(v_ref)
    o_ref[0] = (yn + bonus) * f32(g_ref)


def rwkv_readout(y, r, k, v, g, ag0, ag1, k_a, r_k, lnx_g, lnx_b, tm):
    B, T, C = y.shape
    tok = pl.BlockSpec((1, tm, C), lambda b, i: (b, i, 0))
    par = pl.BlockSpec((1, C), lambda b, i: (0, 0))
    return pl.pallas_call(
        _rwkv_readout_kernel,
        out_shape=jax.ShapeDtypeStruct((B, T, C), jnp.float32), grid=(B, T // tm),
        in_specs=[tok] * 7 + [par] * 4 + [pl.BlockSpec((C, C), lambda b, i: (0, 0))],
        out_specs=tok,
        compiler_params=_cparams("parallel", "parallel"),
        name="rwkv_readout",
    )(y, r, k, v, g, ag0, ag1, k_a, r_k.reshape(1, -1), lnx_g.reshape(1, -1), lnx_b.reshape(1, -1),
      _head_ones(C, RWKV_HEAD))


def rwkv_mix(proj_x, proj_c, vf_x, vf_c, mu, w0, w2, a0, a2, k_k, k_a, r_k, g2, lnx_g, lnx_b, vres, need_ctx, tm):
    B = proj_x.shape[0]
    fx = rwkv_features(proj_x, mu, w0, w2, a0, a2, k_k, g2, vres, vf_x, tm)
    fc = rwkv_features(proj_c, mu, w0, w2, a0, a2, k_k, g2, vres, vf_c, tm)
    ka = k_a.reshape(1, -1)
    y_x = y_c = None
    for d in range(N_DIR):
        s0 = jnp.zeros((B, BRANCH_WIDTH // HP, HP, HP), jnp.float32)
        y_c, s_ctx = wkv_chunked(fc[0], fc[5 + d], fc[3], fc[7 + d], fc[1], fc[2], ka, s0, y_c, d == 1)
        y_x, _ = wkv_chunked(fx[0], fx[5 + d], fx[3], fx[7 + d], fx[1], fx[2], ka, s_ctx, y_x, d == 1)
    out_x = rwkv_readout(y_x, fx[0], fx[1], fx[2], fx[4], fx[7], fx[8], ka, r_k, lnx_g, lnx_b, tm)
    out_c = rwkv_readout(y_c, fc[0], fc[1], fc[2], fc[4], fc[7], fc[8], ka, r_k, lnx_g, lnx_b, tm) if need_ctx else None
    vf_x = fx[2] if vres is None else vf_x
    vf_c = fc[2] if vres is None else vf_c
    return out_x, out_c, vf_x, vf_c


def _qknorm_kernel(z_ref, qg_ref, kg_ref, ones_ref, q_ref, k_ref, v_ref):
    ones_bd = ones_ref[...]
    C = BRANCH_WIDTH
    z = z_ref[0].astype(jnp.float32)
    q = z[:, 0:C]
    k = z[:, C:2 * C]
    qn = q * lax.rsqrt(_split_dot(q * q, ones_bd, 2) * (1.0 / NA_HEAD) + RMS_EPS) * qg_ref[...]
    kn = k * lax.rsqrt(_split_dot(k * k, ones_bd, 2) * (1.0 / NA_HEAD) + RMS_EPS) * kg_ref[...]
    q_ref[0] = (qn * NA_HEAD ** -0.5).astype(BF)
    k_ref[0] = kn.astype(BF)
    v_ref[0] = z[:, 2 * C:3 * C].astype(BF)


def na_qknorm(proj, qn_g, kn_g, tm):
    B, T, _ = proj.shape
    C = BRANCH_WIDTH
    tok = pl.BlockSpec((1, tm, C), lambda b, i: (b, i, 0))
    par = pl.BlockSpec((1, C), lambda b, i: (0, 0))
    sd = jax.ShapeDtypeStruct((B, T, C), BF)
    return pl.pallas_call(
        _qknorm_kernel, out_shape=(sd, sd, sd), grid=(B, T // tm),
        in_specs=[pl.BlockSpec((1, tm, NA_COLS), lambda b, i: (b, i, NA_BLK)), par, par,
                  pl.BlockSpec((C, C), lambda b, i: (0, 0))],
        out_specs=(tok, tok, tok),
        compiler_params=_cparams("parallel", "parallel"),
        name="na_qknorm",
    )(proj, jnp.tile(qn_g, NA_HEADS).reshape(1, C), jnp.tile(kn_g, NA_HEADS).reshape(1, C), _head_ones(C, NA_HEAD))


def na_bias_table(rpb):
    qc = np.arange(GRID_W)[:, None]
    kc = np.arange(GRID_W)[None, :]
    cs = np.clip(qc - NA_WIN_C // 2, 0, GRID_W - NA_WIN_C)
    valid = (kc >= cs) & (kc < cs + NA_WIN_C)
    cidx = np.clip(kc - qc + NA_WIN_C - 1, 0, 2 * NA_WIN_C - 2)
    t = jnp.where(valid[None, None], rpb[:, :, cidx], NEG_INF)
    t2 = jnp.concatenate([t[:, :-1], t[:, 1:]], axis=-1)
    H = rpb.shape[0]
    t2 = t2.reshape(H // 2, 2, 2 * NA_WIN_R - 2, GRID_W, 2 * GRID_W).transpose(0, 2, 1, 3, 4)
    return t2.reshape(H // 2, 2 * NA_WIN_R - 2, 2 * GRID_W, 2 * GRID_W)


def _na_kernel(q_ref, k_ref, v_ref, kc_ref, vc_ref, bias_ref, o_ref, *, rows):
    nwin = NA_WIN_R * GRID_W
    lo = _lo_lanes(GRID_W)
    units = []
    for rr in range(NA_ROWS_PER_STEP):
        r = pl.program_id(1) * NA_ROWS_PER_STEP + rr
        rs = jnp.clip(r - NA_WIN_R // 2, 0, rows - NA_WIN_R)
        k0 = pl.multiple_of(rs * GRID_W, GRID_W)
        for p in range(BRANCH_WIDTH // HP):
            units.append((rr, p, slice(p * HP, (p + 1) * HP), rs - r + NA_WIN_R - 1, k0))
    qs = [_stack_pair(q_ref[0, rr * GRID_W:(rr + 1) * GRID_W, sl], lo) for rr, _, sl, _, _ in units]
    s_loc = [_dot_nt(qs[i], k_ref[0, pl.ds(u[4], nwin), u[2]]) for i, u in enumerate(units)]
    s_ctx = [_dot_nt(qs[i], kc_ref[0, :, u[2]]) for i, u in enumerate(units)]
    p_loc, p_ctx, den = [], [], []
    for i, (_, p, _, base, _) in enumerate(units):
        sl_b = s_loc[i] + jnp.concatenate([bias_ref[p, base + 2 * j] for j in range(NA_WIN_R // 2)], axis=1)
        m = jnp.maximum(jnp.max(sl_b, axis=1, keepdims=True), jnp.max(s_ctx[i], axis=1, keepdims=True))
        el = jnp.exp(sl_b - m)
        ec = jnp.exp(s_ctx[i] - m)
        den.append(jnp.sum(el, axis=1, keepdims=True) + jnp.sum(ec, axis=1, keepdims=True))
        p_loc.append(el.astype(BF))
        p_ctx.append(ec.astype(BF))
    o_loc = [_bdot(p_loc[i], v_ref[0, pl.ds(u[4], nwin), u[2]]) for i, u in enumerate(units)]
    o_ctx = [_bdot(p_ctx[i], vc_ref[0, :, u[2]]) for i, u in enumerate(units)]
    for i, (rr, _, sl, _, _) in enumerate(units):
        o = (o_loc[i] + o_ctx[i]) / den[i]
        o_ref[0, rr * GRID_W:(rr + 1) * GRID_W, sl] = jnp.where(lo, o[:GRID_W], o[GRID_W:])


def na_attention(q, k, v, kc, vc, bias_tab):
    B, S, C = q.shape
    rows = S // GRID_W
    n_ctx = kc.shape[1]
    seq = pl.BlockSpec((1, S, C), lambda b, r: (b, 0, 0))
    cx = pl.BlockSpec((1, n_ctx, C), lambda b, r: (b, 0, 0))
    row = pl.BlockSpec((1, NA_ROWS_PER_STEP * GRID_W, C), lambda b, r: (b, r, 0))
    return pl.pallas_call(
        functools.partial(_na_kernel, rows=rows),
        out_shape=jax.ShapeDtypeStruct((B, S, C), jnp.float32),
        grid=(B, rows // NA_ROWS_PER_STEP),
        in_specs=[row, seq, seq, cx, cx, pl.BlockSpec(bias_tab.shape, lambda b, r: (0, 0, 0, 0))],
        out_specs=row,
        compiler_params=_cparams("parallel", "arbitrary"),
        name="na_attention",
    )(q, k, v, kc, vc, bias_tab)


def _ctx_attn_kernel(q_ref, k_ref, v_ref, o_ref):
    n = q_ref.shape[1]
    lo = _lo_lanes(n)
    sls = [slice(p * HP, (p + 1) * HP) for p in range(BRANCH_WIDTH // HP)]
    sc = [_dot_nt(_stack_pair(q_ref[0, :, sl], lo), k_ref[0, :, sl]) for sl in sls]
    e = [jnp.exp(x - jnp.max(x, axis=1, keepdims=True)) for x in sc]
    o = [_bdot(e[p], v_ref[0, :, sl]) / jnp.sum(e[p], axis=1, keepdims=True) for p, sl in enumerate(sls)]
    for p, sl in enumerate(sls):
        o_ref[0, :, sl] = jnp.where(lo, o[p][:n], o[p][n:])


def ctx_attention(q, k, v):
    B, n, C = q.shape
    blk = pl.BlockSpec((1, n, C), lambda b: (b, 0, 0))
    return pl.pallas_call(
        _ctx_attn_kernel, out_shape=jax.ShapeDtypeStruct((B, n, C), jnp.float32), grid=(B,),
        in_specs=[blk, blk, blk], out_specs=blk,
        compiler_params=_cparams("parallel"),
        name="ctx_attention",
    )(q, k, v)


def na_mix(proj_x, proj_c, qn_g, kn_g, rpb, need_ctx, tm):
    q, k, v = na_qknorm(proj_x, qn_g, kn_g, tm)
    qc, kc, vc = na_qknorm(proj_c, qn_g, kn_g, tm)
    out_x = na_attention(q, k, v, kc, vc, na_bias_table(rpb))
    out_c = ctx_attention(qc, kc, vc) if need_ctx else None
    return out_x, out_c


def rope_tables(n_tokens):
    t = jnp.arange(n_tokens)
    pos = jnp.stack([t // GRID_W, t % GRID_W], axis=-1).astype(jnp.float32)
    inv = ROPE_BASE ** (-jnp.arange(ROPE_ROT, dtype=jnp.float32) / ROPE_ROT)
    ang = pos[:, :, None] * inv
    cos_h = jnp.concatenate([jnp.cos(ang), jnp.cos(ang)], axis=-1).reshape(n_tokens, MLSTM_DQK)
    sin_h = jnp.concatenate([-jnp.sin(ang), jnp.sin(ang)], axis=-1).reshape(n_tokens, MLSTM_DQK)
    col = np.arange(MLSTM_QK)
    partner = np.where((col % (2 * ROPE_ROT)) < ROPE_ROT, col + ROPE_ROT, col - ROPE_ROT)
    perm = np.zeros((MLSTM_QK, MLSTM_QK), np.float32)
    perm[partner, col] = 1.0
    return jnp.tile(cos_h, (1, MLSTM_HEADS)), jnp.tile(sin_h, (1, MLSTM_HEADS)), jnp.asarray(perm, BF)


def _mlstm_prep_kernel(z_ref, gp_ref, ib_ref, fb_ref, *rest, rope):
    if rope:
        cos_ref, sin_ref, perm_ref, q_ref, k_ref, v_ref, g_ref = rest
    else:
        q_ref, k_ref, v_ref, g_ref = rest
    z = z_ref[0].astype(jnp.float32)
    q = z[:, 0:MLSTM_QK]
    k = z[:, MLSTM_QK:2 * MLSTM_QK]
    if rope:
        perm = perm_ref[...]
        q = q * cos_ref[...] + _split_dot(q, perm, 3) * sin_ref[...]
        k = k * cos_ref[...] + _split_dot(k, perm, 3) * sin_ref[...]
    q_ref[0] = (q * MLSTM_DQK ** -0.5).astype(BF)
    k_ref[0] = k.astype(BF)
    v_ref[0] = z[:, 2 * MLSTM_QK:2 * MLSTM_QK + BRANCH_WIDTH].astype(BF)
    gp = gp_ref[0].astype(jnp.float32)
    lane = lax.broadcasted_iota(jnp.int32, gp.shape, 1)
    ig = GATE_CAP * jnp.tanh((gp + ib_ref[...]) / GATE_CAP)
    fg = GATE_CAP * jnp.tanh((gp + fb_ref[...]) / GATE_CAP)
    g_ref[0] = jnp.where(lane < N_DIR * MLSTM_HEADS, ig, jax.nn.log_sigmoid(fg))


def mlstm_prep(proj, i_bias, f_bias, rope, tm):
    B, T, _ = proj.shape
    ng = N_DIR * MLSTM_HEADS
    ib = jnp.zeros((1, V7X_LANES), jnp.float32).at[0, 0:ng].set(i_bias.reshape(-1))
    fb = jnp.zeros((1, V7X_LANES), jnp.float32).at[0, ng:2 * ng].set(f_bias.reshape(-1))
    tok = lambda w: pl.BlockSpec((1, tm, w), lambda b, i: (b, i, 0))
    par = pl.BlockSpec((1, V7X_LANES), lambda b, i: (0, 0))
    args = [proj, proj, ib, fb]
    specs = [pl.BlockSpec((1, tm, MLSTM_MAIN_COLS), lambda b, i: (b, i, ML_BLK)),
             pl.BlockSpec((1, tm, V7X_LANES), lambda b, i: (b, i, ML_GATE_BLK)), par, par]
    if rope is not None:
        args += list(rope)
        specs += [pl.BlockSpec((tm, MLSTM_QK), lambda b, i: (i, 0)), pl.BlockSpec((tm, MLSTM_QK), lambda b, i: (i, 0)),
                  pl.BlockSpec((MLSTM_QK, MLSTM_QK), lambda b, i: (0, 0))]
    return pl.pallas_call(
        functools.partial(_mlstm_prep_kernel, rope=rope is not None),
        out_shape=(jax.ShapeDtypeStruct((B, T, MLSTM_QK), BF), jax.ShapeDtypeStruct((B, T, MLSTM_QK), BF),
                   jax.ShapeDtypeStruct((B, T, BRANCH_WIDTH), BF), jax.ShapeDtypeStruct((B, T, V7X_LANES), jnp.float32)),
        grid=(B, T // tm), in_specs=specs,
        out_specs=(tok(MLSTM_QK), tok(MLSTM_QK), tok(BRANCH_WIDTH), tok(V7X_LANES)),
        compiler_params=_cparams("parallel", "parallel"),
        name="mlstm_prep",
    )(*args)


def _mlstm_kernel(q_ref, k_ref, v_ref, gc_ref, gr_ref, c0_ref, n0_ref, m0_ref, *rest, reverse, direction, bb, has_prev):
    if has_prev:
        hprev_ref, h_ref, cout_ref, nout_ref, mout_ref, c_scr, n_scr, m_scr = rest
    else:
        h_ref, cout_ref, nout_ref, mout_ref, c_scr, n_scr, m_scr = rest
    L = MLSTM_CHUNK
    DV = MLSTM_DV
    H = MLSTM_HEADS
    cidx = pl.program_id(1)

    @pl.when(cidx == 0)
    def _():
        c_scr[...] = c0_ref[...]
        n_scr[...] = n0_ref[...]
        m_scr[...] = m0_ref[...]

    ti = lax.broadcasted_iota(jnp.int32, (L, L), 0)
    si = lax.broadcasted_iota(jnp.int32, (L, L), 1)
    before = (si >= ti) if reverse else (si <= ti)
    tri = before.astype(jnp.float32)
    lo = _lo_lanes(L)
    lane1 = lax.broadcasted_iota(jnp.int32, (1, HP), 1)
    lo_row = lane1 < (HP // 2)
    lo_col = lax.broadcasted_iota(jnp.int32, (HP, 1), 0) < (HP // 2)
    col = lambda a, c: a[:, c:c + 1]
    rowv = lambda a, c: a[c:c + 1, :]
    ci = lambda h: direction * H + h
    cf = lambda h: N_DIR * H + direction * H + h
    units = [(bi, p) for bi in range(bb) for p in range(H // 2)]
    blocks = [(bi, h) for bi in range(bb) for h in range(H)]
    cat = lambda xs: jnp.concatenate(xs, axis=0)
    rows_of = lambda col11: jnp.broadcast_to(col11, (L, 1))

    bcol, icol, brow, irow, tot_b, m_b = [], [], [], [], [], []
    for bi in range(bb):
        gc = gc_ref[bi]
        gr = gr_ref[bi, 0]
        bcol_all = jnp.dot(tri, gc, precision=HIGHEST, preferred_element_type=jnp.float32)
        brow_all = lax.dot_general(gr, tri, (((1,), (1,)), ((), ())), precision=HIGHEST, preferred_element_type=jnp.float32)
        tot_all = jnp.sum(gc, axis=0, keepdims=True)
        m_all = m_scr[bi, 0]
        for h in range(H):
            bcol.append(col(bcol_all, cf(h)))
            icol.append(col(gc, ci(h)))
            brow.append(jnp.broadcast_to(rowv(brow_all, cf(h)), (L, L)))
            irow.append(jnp.broadcast_to(rowv(gr, ci(h)), (L, L)))
            tot_b.append(col(tot_all, cf(h)))
            m_b.append(col(m_all, h))
    nb = len(blocks)
    bcol_r, icol_r = cat(bcol), cat(icol)
    tot_r = cat([rows_of(t) for t in tot_b])
    mprev_r = cat([rows_of(m) for m in m_b])
    causal_r = cat([before] * nb)
    dmat = jnp.where(causal_r, bcol_r - cat(brow) + cat(irow), NEG_INF)
    inter = bcol_r + mprev_r
    m_t = jnp.maximum(inter, jnp.max(dmat, axis=1, keepdims=True))
    decay = jnp.exp(dmat - m_t)
    sc_inter = jnp.exp(inter - m_t)
    inv_floor = jnp.exp(-m_t)

    qs = [_stack_pair(q_ref[bi, :, p * HP:(p + 1) * HP], lo) for bi, p in units]
    kp = [k_ref[bi, :, p * HP:(p + 1) * HP] for bi, p in units]
    vb = [v_ref[bi, :, h * DV:(h + 1) * DV] for bi, h in blocks]
    c_pair = [c_scr[bi, p] for bi, p in units]
    n_pair = [n_scr[bi, p] for bi, p in units]
    qk = cat([_dot_nt(qs[u], kp[u]) for u in range(len(units))])
    qc = cat([_bdot(qs[u], c_pair[u]) for u in range(len(units))])
    qn = cat([jnp.sum(qs[u].astype(jnp.float32) * n_pair[u], axis=1, keepdims=True) for u in range(len(units))])
    smat = qk * decay
    den = sc_inter * qn + jnp.sum(smat, axis=1, keepdims=True)
    sb = smat.astype(BF)
    sv = cat([_bdot(sb[j * L:(j + 1) * L], vb[j]) for j in range(nb)])
    hout = (sc_inter * qc + sv) / jnp.maximum(jnp.abs(den), inv_floor)
    for j, (bi, h) in enumerate(blocks):
        hs = slice(h * DV, (h + 1) * DV)
        part = hout[j * L:(j + 1) * L]
        h_ref[bi, :, hs] = (part + hprev_ref[bi, :, hs]) if has_prev else part

    wlog = tot_r - bcol_r + icol_r
    m_new = [jnp.maximum(tot_b[j] + m_b[j], jnp.max(wlog[j * L:(j + 1) * L], axis=0, keepdims=True)) for j in range(nb)]
    dec = [jnp.exp(tot_b[j] + m_b[j] - m_new[j]) for j in range(nb)]
    wexp = jnp.exp(wlog - cat([rows_of(m) for m in m_new]))
    for u, (bi, p) in enumerate(units):
        j0 = bi * H + 2 * p
        kws = _stack_pair(kp[u], lo).astype(jnp.float32) * wexp[2 * u * L:2 * (u + 1) * L]
        upd = _dot_tn(kws, cat([vb[j0], vb[j0 + 1]]))
        c_scr[bi, p] = jnp.where(lo_col, dec[j0], dec[j0 + 1]) * c_pair[u] + upd
        n_scr[bi, p] = jnp.where(lo_row, dec[j0], dec[j0 + 1]) * n_pair[u] + jnp.sum(kws, axis=0, keepdims=True)
    for bi in range(bb):
        m_all = m_scr[bi, 0]
        for h in range(H):
            m_all = jnp.where(lane1 == h, m_new[bi * H + h], m_all)
        m_scr[bi, 0] = m_all

    @pl.when(cidx == pl.num_programs(1) - 1)
    def _():
        cout_ref[...] = c_scr[...]
        nout_ref[...] = n_scr[...]
        mout_ref[...] = m_scr[...]


def mlstm_chunked(q, k, v, gates, state, h_prev, direction):
    B, T, _ = q.shape
    L = MLSTM_CHUNK
    bb = MLSTM_SEQS_PER_STEP
    nc = T // L
    reverse = direction == 1
    g_rows = gates[:, :, :MLSTM_GATES].reshape(B, nc, L, MLSTM_GATES).transpose(0, 1, 3, 2)
    cm = (lambda c: nc - 1 - c) if reverse else (lambda c: c)
    tok = lambda w: pl.BlockSpec((bb, L, w), lambda b, c: (b, cm(c), 0))
    st = lambda a: pl.BlockSpec((bb,) + a.shape[1:], lambda b, c: (b,) + (0,) * (a.ndim - 1))
    c0, n0, m0 = state
    has_prev = h_prev is not None
    args = [q, k, v, gates, g_rows, c0, n0, m0] + ([h_prev] if has_prev else [])
    outs = pl.pallas_call(
        functools.partial(_mlstm_kernel, reverse=reverse, direction=direction, bb=bb, has_prev=has_prev),
        out_shape=(jax.ShapeDtypeStruct((B, T, BRANCH_WIDTH), jnp.float32),) + tuple(
            jax.ShapeDtypeStruct(a.shape, jnp.float32) for a in state),
        grid=(B // bb, nc),
        in_specs=[tok(MLSTM_QK), tok(MLSTM_QK), tok(BRANCH_WIDTH), tok(V7X_LANES),
                  pl.BlockSpec((bb, 1, MLSTM_GATES, L), lambda b, c: (b, cm(c), 0, 0)), st(c0), st(n0), st(m0)]
                 + ([tok(BRANCH_WIDTH)] if has_prev else []),
        out_specs=(tok(BRANCH_WIDTH), st(c0), st(n0), st(m0)),
        scratch_shapes=[pltpu.VMEM((bb,) + a.shape[1:], jnp.float32) for a in state],
        compiler_params=_cparams("parallel", "arbitrary"),
        name="mlstm_chunked",
    )(*args)
    return outs[0], outs[1:]


def _mlstm_readout_kernel(h_ref, o_ref, g_ref, out_ref):
    for h in range(MLSTM_HEADS):
        sl = slice(h * MLSTM_DV, (h + 1) * MLSTM_DV)
        x = h_ref[0, :, sl]
        xn = x * lax.rsqrt(jnp.mean(x * x, axis=1, keepdims=True) + RMS_EPS)
        out_ref[0, :, sl] = xn * g_ref[:, sl] * jax.nn.sigmoid(o_ref[0, :, sl].astype(jnp.float32))


def mlstm_readout(h, proj, norm_g, tm):
    B, T, C = h.shape
    tok = pl.BlockSpec((1, tm, C), lambda b, i: (b, i, 0))
    return pl.pallas_call(
        _mlstm_readout_kernel, out_shape=jax.ShapeDtypeStruct((B, T, C), jnp.float32), grid=(B, T // tm),
        in_specs=[tok, pl.BlockSpec((1, tm, C), lambda b, i: (b, i, ML_OGATE_BLK)), pl.BlockSpec((1, C), lambda b, i: (0, 0))],
        out_specs=tok,
        compiler_params=_cparams("parallel", "parallel"),
        name="mlstm_readout",
    )(h, proj, norm_g.reshape(1, C))


def mlstm_mix(proj_x, proj_c, i_bias, f_bias, norm_g, rope, need_ctx, tm):
    B = proj_x.shape[0]
    qx, kx, vx, gx = mlstm_prep(proj_x, i_bias, f_bias, rope, tm)
    qc, kc, vc, gc = mlstm_prep(proj_c, i_bias, f_bias, None, tm)
    h_x = h_c = None
    for d in range(N_DIR):
        st0 = (jnp.zeros((B, MLSTM_HEADS // 2, HP, HP), jnp.float32), jnp.zeros((B, MLSTM_HEADS // 2, 1, HP), jnp.float32),
               jnp.zeros((B, 1, 1, HP), jnp.float32))
        h_c, st_ctx = mlstm_chunked(qc, kc, vc, gc, st0, h_c, d)
        h_x, _ = mlstm_chunked(qx, kx, vx, gx, st_ctx, h_x, d)
    out_x = mlstm_readout(h_x, proj_x, norm_g, tm)
    out_c = mlstm_readout(h_c, proj_c, norm_g, tm) if need_ctx else None
    return out_x, out_c


def _merge_kernel(ya_ref, yb_ref, yc_ref, ga_ref, gb_ref, gc_ref, x_ref, mod_ref, wb_ref, wo_ref, o_ref):
    merged = None
    for i, (y_ref, g_ref) in enumerate(((ya_ref, ga_ref), (yb_ref, gb_ref), (yc_ref, gc_ref))):
        t = jax.nn.sigmoid(g_ref[0].astype(jnp.float32)) * _bdot(y_ref[0], wb_ref[i])
        merged = t if merged is None else merged + t
    o_ref[0] = x_ref[0] + mod_ref[0, 2:3, :] * _bdot(merged, wo_ref[...])


def merge_apply(ya, yb, yc, proj, x, mod, w_branch, w_out, tm):
    B, T, D = x.shape
    tok = lambda w: pl.BlockSpec((1, tm, w), lambda b, i: (b, i, 0))
    gate = lambda k: pl.BlockSpec((1, tm, D), lambda b, i: (b, i, GATE_BLK0 + k))
    return pl.pallas_call(
        _merge_kernel, out_shape=jax.ShapeDtypeStruct((B, T, D), jnp.float32), grid=(B, T // tm),
        in_specs=[tok(BRANCH_WIDTH)] * 3 + [gate(0), gate(1), gate(2), tok(D), pl.BlockSpec((1, 6, D), lambda b, i: (b, 0, 0)),
                  pl.BlockSpec(w_branch.shape, lambda b, i: (0, 0, 0)), pl.BlockSpec(w_out.shape, lambda b, i: (0, 0))],
        out_specs=tok(D),
        compiler_params=_cparams("parallel", "parallel"),
        name="merge_branches",
    )(ya, yb, yc, proj, proj, proj, x, mod, w_branch, w_out)


def _route_kernel(x_ref, g_ref, mod_ref, wr_ref, rb_ref, h_ref, gate_ref):
    h = _norm_mod(x_ref[0], g_ref[...], mod_ref[0, 4:5, :], mod_ref[0, 3:4, :])
    h_ref[0] = h.astype(BF)
    tm = h.shape[0]
    logits = lax.dot_general(wr_ref[...], h, (((1,), (1,)), ((), ())), precision=HIGHEST, preferred_element_type=jnp.float32)
    scores = jax.nn.sigmoid(logits)
    sel = scores + rb_ref[...]
    gsz = N_EXPERTS // N_GROUPS
    grp = sel.reshape(N_GROUPS, gsz, tm)
    iota_in = lax.broadcasted_iota(jnp.int32, grp.shape, 1)
    m1 = jnp.max(grp, axis=1, keepdims=True)
    first = jnp.min(jnp.where(grp == m1, iota_in, gsz), axis=1, keepdims=True)
    m2 = jnp.max(jnp.where(iota_in == first, -jnp.inf, grp), axis=1, keepdims=True)
    gscore = (m1 + m2).reshape(N_GROUPS, tm)
    gi = lax.broadcasted_iota(jnp.int32, (N_GROUPS, tm), 0)
    rank = jnp.zeros((N_GROUPS, tm), jnp.int32)
    for g2 in range(N_GROUPS):
        other = gscore[g2:g2 + 1, :]
        rank = rank + ((other > gscore) | ((other == gscore) & (g2 < gi))).astype(jnp.int32)
    gmask = rank < TOPK_GROUPS
    emask = jnp.broadcast_to(gmask.reshape(N_GROUPS, 1, tm), (N_GROUPS, gsz, tm)).reshape(N_EXPERTS, tm)
    cand = jnp.where(emask, sel, NEG_INF)
    ei = lax.broadcasted_iota(jnp.int32, (N_EXPERTS, tm), 0)
    chosen = jnp.zeros((N_EXPERTS, tm), jnp.bool_)
    for _ in range(TOP_K):
        mx = jnp.max(cand, axis=0, keepdims=True)
        idx = jnp.min(jnp.where(cand == mx, ei, N_EXPERTS), axis=0, keepdims=True)
        hit = ei == idx
        chosen = chosen | hit
        cand = jnp.where(hit, -jnp.inf, cand)
    w = jnp.where(chosen, scores, 0.0)
    w = w / jnp.sum(w, axis=0, keepdims=True) * ROUTED_SCALE
    gate_ref[0] = jnp.concatenate([w, jnp.zeros((V7X_LANES - N_EXPERTS, tm), jnp.float32)], axis=0).T


def moe_route(x, g, mod, w_router, router_bias, tm):
    B, T, D = x.shape
    tok = pl.BlockSpec((1, tm, D), lambda b, i: (b, i, 0))
    return pl.pallas_call(
        _route_kernel,
        out_shape=(jax.ShapeDtypeStruct((B, T, D), BF), jax.ShapeDtypeStruct((B, T, V7X_LANES), jnp.float32)),
        grid=(B, T // tm),
        in_specs=[tok, pl.BlockSpec((1, D), lambda b, i: (0, 0)), pl.BlockSpec((1, 6, D), lambda b, i: (b, 0, 0)),
                  pl.BlockSpec((N_EXPERTS, D), lambda b, i: (0, 0)), pl.BlockSpec((N_EXPERTS, 1), lambda b, i: (0, 0))],
        out_specs=(tok, pl.BlockSpec((1, tm, V7X_LANES), lambda b, i: (b, i, 0))),
        compiler_params=_cparams("parallel", "parallel"),
        name="moe_route",
    )(x, g.reshape(1, D), mod, w_router.T, router_bias.reshape(N_EXPERTS, 1))


def _moe_kernel(h_ref, gate_ref, x_ref, mod_ref, sel_ref, wg_ref, wu_ref, wd_ref, sg_ref, su_ref, sd_ref, o_ref, *, tm):
    j = pl.program_id(1)
    rows = pl.ds(pl.multiple_of(pl.program_id(2) * tm, tm), tm)
    h = h_ref[0]

    @pl.when(j == 0)
    def _():
        sh = jax.nn.silu(_bdot(h, sg_ref[...])) * _bdot(h, su_ref[...])
        o_ref[0, rows, :] = _bdot(sh, sd_ref[...])

    g8 = _split_dot(gate_ref[0], sel_ref[0], 2)
    act = jax.nn.silu(_bdot(h, wg_ref[...])) * _bdot(h, wu_ref[...])
    act = jnp.concatenate([act[:, e * D_EXPERT:(e + 1) * D_EXPERT] * g8[:, e:e + 1] for e in range(MOE_STEP_EXPERTS)],
                          axis=1)
    o_ref[0, rows, :] += _bdot(act, wd_ref[...])

    @pl.when(j == pl.num_programs(1) - 1)
    def _():
        o_ref[0, rows, :] = x_ref[0] + mod_ref[0, 5:6, :] * o_ref[0, rows, :]


def _moe_select_table():
    se = MOE_STEP_EXPERTS
    t = np.zeros((N_EXPERTS // se, V7X_LANES, V7X_LANES), np.float32)
    for j in range(N_EXPERTS // se):
        for e in range(se):
            t[j, j * se + e, e] = 1.0
    return jnp.asarray(t, BF)


def moe_apply(h2, gates, x, mod, wg, wu, wd, sg, su, sd, tm):
    B, T, D = x.shape
    sw = MOE_STEP_EXPERTS * D_EXPERT
    n_groups = N_EXPERTS // MOE_STEP_EXPERTS
    tok = pl.BlockSpec((1, tm, D), lambda b, j, i: (b, i, 0))
    x_last = pl.BlockSpec((1, tm, D), lambda b, j, i: (b, jnp.where(j == n_groups - 1, i, 0), 0))
    full = lambda a: pl.BlockSpec(a.shape, lambda b, j, i: (0,) * a.ndim)
    return pl.pallas_call(
        functools.partial(_moe_kernel, tm=tm), out_shape=jax.ShapeDtypeStruct((B, T, D), jnp.float32),
        grid=(B, n_groups, T // tm),
        in_specs=[tok, pl.BlockSpec((1, tm, V7X_LANES), lambda b, j, i: (b, i, 0)), x_last,
                  pl.BlockSpec((1, 6, D), lambda b, j, i: (b, 0, 0)),
                  pl.BlockSpec((1, V7X_LANES, V7X_LANES), lambda b, j, i: (j, 0, 0)),
                  pl.BlockSpec((D, sw), lambda b, j, i: (0, j)), pl.BlockSpec((D, sw), lambda b, j, i: (0, j)),
                  pl.BlockSpec((sw, D), lambda b, j, i: (j, 0)), full(sg), full(su), full(sd)],
        out_specs=pl.BlockSpec((1, T, D), lambda b, j, i: (b, 0, 0)),
        compiler_params=pltpu.CompilerParams(dimension_semantics=("parallel", "arbitrary", "arbitrary"),
                                             vmem_limit_bytes=MOE_VMEM_LIMIT_BYTES),
        name="moe_experts",
    )(h2, gates, x, mod, _moe_select_table(), wg, wu, wd, sg, su, sd)


def _reorder_w_in(w):
    o_na = RWKV_COLS
    o_ml = o_na + NA_COLS
    o_mg = o_ml + MLSTM_MAIN_COLS
    o_gate = o_mg + MLSTM_GATES
    pad = jnp.zeros((w.shape[0], V7X_LANES - MLSTM_GATES), w.dtype)
    return jnp.concatenate([w[:, o_na:o_ml], w[:, o_ml:o_mg], w[:, o_gate:], w[:, :RWKV_COLS], w[:, o_mg:o_gate], pad],
                           axis=1).astype(BF)


def kernel(x, c, ctx, c_ctx, w_ada, b_ada, norm1_g, norm2_g, w_in, rw_mu, rw_w0, rw_w2, rw_a0, rw_a2, rw_k_k, rw_k_a, rw_r_k, rw_g2, rw_lnx_g, rw_lnx_b, rw_v0, rw_v1, rw_v2, na_qn_g, na_kn_g, na_rpb, ml_i_bias, ml_f_bias, ml_norm_g, w_branch, w_out, moe_router, moe_bias, moe_w_gate, moe_w_up, moe_w_down, sh_w_gate, sh_w_up, sh_w_down):
    B, S, D = x.shape
    n_ctx = ctx.shape[1]
    tm = 256
    assert S % tm == 0 and n_ctx % tm == 0 and PROJ_COLS == IN_COLS + V7X_LANES - MLSTM_GATES
    rope = rope_tables(S)
    n_cond = B + 1
    cond_pad = (-n_cond) % V7X_SUBLANES
    s_cond = jnp.pad(jnp.concatenate([jax.nn.silu(c), jax.nn.silu(c_ctx)[None]], axis=0), ((0, cond_pad), (0, 0)))
    vf_x = vf_c = None
    for l in range(DEPTH):
        need_ctx = l < DEPTH - 1
        mod = pmm(s_cond, w_ada[l]) + b_ada[l]
        mod_x = mod[:B].reshape(B, 6, D)
        mod_c = jnp.broadcast_to(mod[B].reshape(1, 6, D), (B, 6, D))
        w_proj = _reorder_w_in(w_in[l])
        proj_x = norm_proj(x, norm1_g[l], mod_x, w_proj)
        proj_c = norm_proj(ctx, norm1_g[l], mod_c, w_proj)
        vres = None if l == 0 else (rw_v0[l - 1], rw_v1[l - 1], rw_v2[l - 1])
        ya_x, ya_c, vf_x, vf_c = rwkv_mix(proj_x, proj_c, vf_x, vf_c, rw_mu[l], rw_w0[l], rw_w2[l], rw_a0[l], rw_a2[l],
                                          rw_k_k[l], rw_k_a[l], rw_r_k[l], rw_g2[l], rw_lnx_g[l], rw_lnx_b[l],
                                          vres, need_ctx, tm)
        yb_x, yb_c = na_mix(proj_x, proj_c, na_qn_g[l], na_kn_g[l], na_rpb[l], need_ctx, tm)
        yc_x, yc_c = mlstm_mix(proj_x, proj_c, ml_i_bias[l], ml_f_bias[l], ml_norm_g[l], rope, need_ctx, tm)
        wb = w_branch[l].astype(BF)
        wo = w_out[l].astype(BF)
        wg = moe_w_gate[l].transpose(1, 0, 2).reshape(D, N_EXPERTS * D_EXPERT).astype(BF)
        wu = moe_w_up[l].transpose(1, 0, 2).reshape(D, N_EXPERTS * D_EXPERT).astype(BF)
        wd = moe_w_down[l].reshape(N_EXPERTS * D_EXPERT, D).astype(BF)
        shared = (sh_w_gate[l].astype(BF), sh_w_up[l].astype(BF), sh_w_down[l].astype(BF))
        x = merge_apply(ya_x, yb_x, yc_x, proj_x, x, mod_x, wb, wo, tm)
        h2, gates = moe_route(x, norm2_g[l], mod_x, moe_router[l], moe_bias[l], tm)
        x = moe_apply(h2, gates, x, mod_x, wg, wu, wd, *shared, MOE_TOKEN_TILE)
        if need_ctx:
            ctx = merge_apply(ya_c, yb_c, yc_c, proj_c, ctx, mod_c, wb, wo, tm)
            h2, gates = moe_route(ctx, norm2_g[l], mod_c, moe_router[l], moe_bias[l], tm)
            ctx = moe_apply(h2, gates, ctx, mod_c, wg, wu, wd, *shared, tm)
    return x
```

```python
import functools

import numpy as np
import jax
import jax.numpy as jnp
from jax import lax
from jax.experimental import pallas as pl
from jax.experimental.pallas import tpu as pltpu

D_MODEL = 1024
DEPTH = 2
GRID_W = 64
N_DIR = 2
N_BRANCH = 3
BRANCH_WIDTH = 512
RMS_EPS = 1e-6
NEG_INF = -1e30

RWKV_HEAD = 64
RWKV_DECAY_LORA = 64
RWKV_LORA_COLS = 384
RWKV_COLS = 3 * BRANCH_WIDTH + RWKV_LORA_COLS
RWKV_LNX_EPS = 64e-5
RWKV_CHUNK = 64
WKV_SEQS_PER_STEP = 4

NA_HEAD = 64
NA_HEADS = BRANCH_WIDTH // NA_HEAD
NA_WIN_R = 8
NA_WIN_C = 16
NA_COLS = 3 * BRANCH_WIDTH
NA_ROWS_PER_STEP = 2

MLSTM_HEADS = 4
MLSTM_DQK = 64
MLSTM_DV = BRANCH_WIDTH // MLSTM_HEADS
MLSTM_QK = MLSTM_HEADS * MLSTM_DQK
MLSTM_CHUNK = 64
MLSTM_SEQS_PER_STEP = 4
MLSTM_MAIN_COLS = 2 * MLSTM_QK + 2 * BRANCH_WIDTH
MLSTM_GATES = 2 * N_DIR * MLSTM_HEADS
GATE_CAP = 15.0
ROPE_ROT = MLSTM_DQK // 4
ROPE_BASE = 10000.0

N_EXPERTS = 64
TOP_K = 8
N_GROUPS = 8
TOPK_GROUPS = 4
D_EXPERT = 128
ROUTED_SCALE = 2.5
MOE_STEP_EXPERTS = 8
MOE_TOKEN_TILE = 1024

GATE_COLS = N_BRANCH * D_MODEL
IN_COLS = RWKV_COLS + NA_COLS + MLSTM_MAIN_COLS + MLSTM_GATES + GATE_COLS

V7X_LANES = 128
V7X_SUBLANES = 8
HP = 2 * RWKV_HEAD
VMEM_LIMIT_BYTES = 48 * 1024 * 1024
MOE_VMEM_LIMIT_BYTES = 56 * 1024 * 1024

PROJ_COLS = 8192
NA_BLK = 0
ML_BLK = 1
ML_OGATE_BLK = (NA_COLS + 2 * MLSTM_QK + BRANCH_WIDTH) // BRANCH_WIDTH
GATE_BLK0 = (NA_COLS + MLSTM_MAIN_COLS) // D_MODEL
RWKV_COL0 = NA_COLS + MLSTM_MAIN_COLS + GATE_COLS
RWKV_BLK0 = RWKV_COL0 // BRANCH_WIDTH
RWKV_LORA_BLK = (RWKV_COL0 + 3 * BRANCH_WIDTH) // RWKV_LORA_COLS
ML_GATE_BLK = (RWKV_COL0 + RWKV_COLS) // V7X_LANES

BF = jnp.bfloat16
ACT_DTYPE = BF
HALO_ROWS = 16
HIGHEST = lax.Precision.HIGHEST


def _cparams(*sem):
    return pltpu.CompilerParams(dimension_semantics=sem, vmem_limit_bytes=VMEM_LIMIT_BYTES)


def _bdot(a, b):
    return jnp.dot(a.astype(BF), b.astype(BF), preferred_element_type=jnp.float32)


def _dot_nt(a, b):
    return lax.dot_general(a.astype(BF), b.astype(BF), (((1,), (1,)), ((), ())), preferred_element_type=jnp.float32)


def _dot_tn(a, b):
    return lax.dot_general(a.astype(BF), b.astype(BF), (((0,), (0,)), ((), ())), preferred_element_type=jnp.float32)


def _split_dot(x, w, parts):
    out = None
    rem = x
    for _ in range(parts):
        piece = rem.astype(BF)
        rem = rem - piece.astype(jnp.float32)
        t = jnp.dot(piece, w, preferred_element_type=jnp.float32)
        out = t if out is None else out + t
    return out


def _head_ones(width, head):
    i = np.arange(width) // head
    return jnp.asarray(i[:, None] == i[None, :], BF)


def _stack_pair(x, lo):
    zero = jnp.zeros_like(x)
    return jnp.concatenate([jnp.where(lo, x, zero), jnp.where(lo, zero, x)], axis=0)


def _lo_lanes(n):
    return lax.broadcasted_iota(jnp.int32, (n, HP), 1) < (HP // 2)


def _mm_kernel(x_ref, w_ref, o_ref):
    part = _bdot(x_ref[...], w_ref[...])

    @pl.when(pl.program_id(2) == 0)
    def _():
        o_ref[...] = part

    @pl.when(pl.program_id(2) > 0)
    def _():
        o_ref[...] += part


def _pick_tile(n, cands):
    for c in cands:
        if n % c == 0:
            return c
    return n


def pmm(x, w):
    M, K = x.shape
    N = w.shape[1]
    tm = _pick_tile(M, (1024, 512, 256, 128, 64, 32, 16, 8))
    tn = _pick_tile(N, (1024, 512, 384, 256, 128))
    tk = _pick_tile(K, (1024,)) if K > 1024 else K
    return pl.pallas_call(
        _mm_kernel,
        out_shape=jax.ShapeDtypeStruct((M, N), jnp.float32),
        grid=(M // tm, N // tn, K // tk),
        in_specs=[pl.BlockSpec((tm, tk), lambda i, j, k: (i, k)),
                  pl.BlockSpec((tk, tn), lambda i, j, k: (k, j))],
        out_specs=pl.BlockSpec((tm, tn), lambda i, j, k: (i, j)),
        compiler_params=_cparams("parallel", "parallel", "arbitrary"),
        name="tiled_matmul",
    )(x, w)


def _norm_mod(x, g, scale, shift):
    xn = x * lax.rsqrt(jnp.mean(x * x, axis=-1, keepdims=True) + RMS_EPS)
    return xn * g * (1.0 + scale) + shift


def _norm_proj_kernel(x_ref, g_ref, mod_ref, w_ref, o_ref, h_scr):
    @pl.when(pl.program_id(2) == 0)
    def _():
        h_scr[...] = _norm_mod(x_ref[0], g_ref[...], mod_ref[0, 1:2, :], mod_ref[0, 0:1, :]).astype(BF)

    o_ref[0] = jnp.dot(h_scr[...], w_ref[...], preferred_element_type=jnp.float32).astype(o_ref.dtype)


def norm_proj(x, g, mod, w):
    B, T, D = x.shape
    N = w.shape[1]
    tm = _pick_tile(T, (1024, 512, 256))
    tn = _pick_tile(N, (1024, 512, 256, 128))
    return pl.pallas_call(
        _norm_proj_kernel,
        out_shape=jax.ShapeDtypeStruct((B, T, N), ACT_DTYPE), grid=(B, T // tm, N // tn),
        in_specs=[pl.BlockSpec((1, tm, D), lambda b, i, j: (b, i, 0)), pl.BlockSpec((1, D), lambda b, i, j: (0, 0)),
                  pl.BlockSpec((1, 6, D), lambda b, i, j: (b, 0, 0)), pl.BlockSpec((D, tn), lambda b, i, j: (0, j))],
        out_specs=pl.BlockSpec((1, tm, tn), lambda b, i, j: (b, i, j)),
        scratch_shapes=[pltpu.VMEM((tm, D), BF)],
        compiler_params=_cparams("parallel", "parallel", "arbitrary"),
        name="norm_proj",
    )(x, g.reshape(1, D), mod, w)


def _shifted(z, prev_row, next_row):
    n = z.shape[0]
    row = lax.broadcasted_iota(jnp.int32, z.shape, 0)
    zp = jnp.where(row == 0, prev_row, pltpu.roll(z, 1, axis=0))
    zn = jnp.where(row == n - 1, next_row, pltpu.roll(z, n - 1, axis=0))
    return zp, zn


def _rwkv_feat_kernel(zr_ref, zk_ref, zv_ref, zl_ref, pr_ref, pk_ref, pv_ref, pl_ref, nr_ref, nk_ref, nv_ref, nl_ref,
                      mu_ref, w0_ref, w2_ref, a0_ref, a2_ref, kk_ref, g2_ref, ones_ref, *rest, has_vres):
    if has_vres:
        v0_ref, v1_ref, v2_ref, vf_ref = rest[:4]
        outs = rest[4:]
    else:
        outs = rest
    r_ref, k_ref, v_ref, kn_ref, g_ref, lw0_ref, lw1_ref, ag0_ref, ag1_ref = outs
    C = BRANCH_WIDTH
    first = pl.program_id(1) == 0
    last = pl.program_id(1) == pl.num_programs(1) - 1

    def shift(z_ref, p_ref, n_ref, c0, c1):
        z = z_ref[0].astype(jnp.float32)
        prev_row = jnp.where(first, 0.0, p_ref[0].astype(jnp.float32)[HALO_ROWS - 1:HALO_ROWS, :])
        next_row = jnp.where(last, 0.0, n_ref[0].astype(jnp.float32)[0:1, :])
        zp, zn = _shifted(z, prev_row, next_row)
        return z + mu_ref[:, c0:c1] * (0.5 * (zp + zn) - z)

    r = shift(zr_ref, pr_ref, nr_ref, 0, C)
    k = shift(zk_ref, pk_ref, nk_ref, C, 2 * C)
    v = shift(zv_ref, pv_ref, nv_ref, 2 * C, 3 * C)
    zl = shift(zl_ref, pl_ref, nl_ref, 3 * C, RWKV_COLS)
    wd = jnp.tanh(zl[:, 0:2 * RWKV_DECAY_LORA])
    ad = zl[:, 2 * RWKV_DECAY_LORA:4 * RWKV_DECAY_LORA]
    gd = zl[:, 4 * RWKV_DECAY_LORA:]
    for d, (lw_ref, ag_ref) in enumerate(((lw0_ref, ag0_ref), (lw1_ref, ag1_ref))):
        wl = -jax.nn.softplus(-(w0_ref[d:d + 1, :] + _bdot(wd, w2_ref[d]))) - 0.5
        lw_ref[0] = -jnp.exp(wl)
        ag_ref[0] = jax.nn.sigmoid(a0_ref[d:d + 1, :] + _bdot(ad, a2_ref[d])).astype(ag_ref.dtype)
    kq = k * kk_ref[...]
    ss = _split_dot(kq * kq, ones_ref[...], 2)
    kn_ref[0] = (kq / jnp.maximum(jnp.sqrt(ss), 1e-12)).astype(kn_ref.dtype)
    if has_vres:
        lora = _bdot(_bdot(v, v1_ref[...]), v2_ref[...])
        v = v + (vf_ref[0].astype(jnp.float32) - v) * jax.nn.sigmoid(v0_ref[...] + lora)
    g_ref[0] = _bdot(jax.nn.sigmoid(gd), g2_ref[...]).astype(g_ref.dtype)
    r_ref[0] = r.astype(r_ref.dtype)
    k_ref[0] = k.astype(k_ref.dtype)
    v_ref[0] = v.astype(v_ref.dtype)


def rwkv_features(proj, mu, w0, w2, a0, a2, k_k, g2, vres, v_first, tm):
    B, T, _ = proj.shape
    C = BRANCH_WIDTH
    nt = T // tm
    zpad = jnp.zeros((RWKV_DECAY_LORA, C), jnp.float32)
    pad_dirs = lambda w: jnp.stack([jnp.concatenate([w[0], zpad], 0), jnp.concatenate([zpad, w[1]], 0)]).astype(BF)
    sub = HALO_ROWS
    blk = lambda w, j: pl.BlockSpec((1, tm, w), lambda b, i: (b, i, j))
    before = lambda w, j: pl.BlockSpec((1, sub, w), lambda b, i: (b, jnp.maximum(i * (tm // sub) - 1, 0), j))
    after = lambda w, j: pl.BlockSpec((1, sub, w), lambda b, i: (b, jnp.minimum((i + 1) * (tm // sub), T // sub - 1), j))
    tok = pl.BlockSpec((1, tm, C), lambda b, i: (b, i, 0))
    full = lambda a: pl.BlockSpec(a.shape, lambda b, i: (0,) * a.ndim)
    params = [mu.reshape(1, -1), w0, pad_dirs(w2), a0, pad_dirs(a2), k_k.reshape(1, -1), g2.astype(BF),
              _head_ones(C, RWKV_HEAD)]
    pieces = [(C, RWKV_BLK0), (C, RWKV_BLK0 + 1), (C, RWKV_BLK0 + 2), (RWKV_LORA_COLS, RWKV_LORA_BLK)]
    args = [proj] * (3 * len(pieces)) + params
    specs = ([blk(w, j) for w, j in pieces] + [before(w, j) for w, j in pieces] + [after(w, j) for w, j in pieces]
             + [full(a) for a in params])
    if vres is not None:
        v0, v1, v2 = vres
        extra = [v0.reshape(1, -1), jnp.pad(v1, ((0, 0), (0, V7X_LANES - v1.shape[1]))).astype(BF),
                 jnp.pad(v2, ((0, V7X_LANES - v2.shape[0]), (0, 0))).astype(BF)]
        args += extra + [v_first]
        specs += [full(a) for a in extra] + [tok]
    return pl.pallas_call(
        functools.partial(_rwkv_feat_kernel, has_vres=vres is not None),
        out_shape=(jax.ShapeDtypeStruct((B, T, C), ACT_DTYPE),) * 5 + (jax.ShapeDtypeStruct((B, T, C), jnp.float32),) * 2
                  + (jax.ShapeDtypeStruct((B, T, C), ACT_DTYPE),) * 2,
        grid=(B, nt), in_specs=specs, out_specs=(tok,) * 9,
        compiler_params=_cparams("parallel", "parallel"),
        name="rwkv_features",
    )(*args)


def _wkv_kernel(r_ref, lw_ref, kk_ref, a_ref, k_ref, v_ref, ka_ref, s0_ref, *rest, reverse, bb, n_pairs, has_prev):
    if has_prev:
        yprev_ref, y_ref, sout_ref, s_scr = rest
    else:
        y_ref, sout_ref, s_scr = rest
    C = RWKV_CHUNK
    c_idx = pl.program_id(1)

    @pl.when(c_idx == 0)
    def _():
        s_scr[...] = s0_ref[...]

    ti = lax.broadcasted_iota(jnp.int32, (C, C), 0)
    si = lax.broadcasted_iota(jnp.int32, (C, C), 1)
    tri = ((si >= ti) if reverse else (si <= ti)).astype(jnp.float32)
    tp = lax.broadcasted_iota(jnp.int32, (C, 2 * C), 0)
    sp = lax.broadcasted_iota(jnp.int32, (C, 2 * C), 1) % C
    m_strict = (sp > tp) if reverse else (sp < tp)
    m_incl = (sp >= tp) if reverse else (sp <= tp)
    eye = (tp == sp).astype(jnp.float32)
    t2 = lax.broadcasted_iota(jnp.int32, (2 * C, 2 * C), 0)
    s2 = lax.broadcasted_iota(jnp.int32, (2 * C, 2 * C), 1)
    same_head = (t2 // C) == (s2 // C)
    lo = _lo_lanes(C)
    units = [(bi, slice(p * HP, (p + 1) * HP), p) for bi in range(bb) for p in range(n_pairs)]
    n = len(units)
    cat = lambda xs: jnp.concatenate(xs, axis=0)
    stack = lambda x: _stack_pair(x, lo)
    bdiag = lambda x: jnp.where(same_head, cat([x, x]), jnp.zeros((), x.dtype))

    ar, bk, bkh, v, e_tot = [], [], [], [], []
    for bi, sl, _ in units:
        lw = lw_ref[bi, :, sl]
        kk = kk_ref[bi, :, sl].astype(jnp.float32)
        ag = a_ref[bi, :, sl].astype(jnp.float32)
        kd = k_ref[bi, :, sl].astype(jnp.float32) * (1.0 + (ag - 1.0) * ka_ref[:, sl])
        cum = jnp.dot(tri, lw, precision=HIGHEST, preferred_element_type=jnp.float32)
        tot = jnp.sum(lw, axis=0, keepdims=True)
        e_neg = jnp.exp(-cum)
        e_end = jnp.exp(tot - cum)
        b = kk * ag
        ar.append(cat([-kk * jnp.exp(cum - lw), r_ref[bi, :, sl].astype(jnp.float32) * jnp.exp(cum)]).astype(BF))
        bk.append(cat([stack(b * e_neg), stack(kd * e_neg)]).astype(BF))
        bkh.append(cat([b * e_end, kd * e_end]).astype(BF))
        v.append(v_ref[bi, :, sl].astype(BF))
        e_tot.append(jnp.exp(tot))
    gram = [_dot_nt(ar[i], bk[i]) for i in range(n)]
    l_ab = [jnp.where(m_strict, g[:C, :2 * C], 0.0) for g in gram]
    l_ak = [jnp.where(m_strict, g[:C, 2 * C:], 0.0).astype(BF) for g in gram]
    l_rbk = [jnp.concatenate([jnp.where(m_incl, g[C:, :2 * C], 0.0), jnp.where(m_incl, g[C:, 2 * C:], 0.0)],
                             axis=1).astype(BF) for g in gram]
    vs = [stack(x) for x in v]
    s0 = [s_scr[bi, p] for bi, _, p in units]
    proj = [_dot_nt(ar[i], s0[i]) for i in range(n)]
    lv = [_bdot(l_ak[i], vs[i]) for i in range(n)]
    tinv = [eye + m for m in l_ab]
    pw_bd = [bdiag(m.astype(BF)) for m in l_ab]
    pw = [_bdot(l_ab[i], pw_bd[i]).astype(BF) for i in range(n)]
    levels = 5
    for lvl in range(1, levels + 1):
        pw_bd = [bdiag(m) for m in pw]
        if lvl < levels:
            both = [_bdot(cat([pw[i], tinv[i].astype(BF)]), pw_bd[i]) for i in range(n)]
            pw = [x[:C].astype(BF) for x in both]
            tinv = [tinv[i] + both[i][C:] for i in range(n)]
        else:
            tinv = [tinv[i] + _bdot(tinv[i], pw_bd[i]) for i in range(n)]
    u = [_bdot(tinv[i], stack((proj[i][:C] + lv[i]).astype(BF))) for i in range(n)]
    ub = [x.astype(BF) for x in u]
    ys = [proj[i][C:] + _bdot(l_rbk[i], cat([stack(ub[i]), vs[i]])) for i in range(n)]
    upd = [_dot_tn(cat([ub[i], v[i]]), bkh[i]) for i in range(n)]
    for i, (bi, sl, p) in enumerate(units):
        y = ys[i]
        if has_prev:
            y = y + yprev_ref[bi, :, sl]
        y_ref[bi, :, sl] = y
        s_scr[bi, p] = s0[i] * e_tot[i] + jnp.where(same_head, upd[i], 0.0)

    @pl.when(c_idx == pl.num_programs(1) - 1)
    def _():
        sout_ref[...] = s_scr[...]


def wkv_chunked(r, lw, kk, ag, k, v, k_a, s0, y_prev, reverse):
    B, T, W = r.shape
    C = RWKV_CHUNK
    bb = WKV_SEQS_PER_STEP
    nc = T // C
    n_pairs = W // HP
    cmap = (lambda b, c: (b, nc - 1 - c, 0)) if reverse else (lambda b, c: (b, c, 0))
    tok = pl.BlockSpec((bb, C, W), cmap)
    st = pl.BlockSpec((bb, n_pairs, HP, HP), lambda b, c: (b, 0, 0, 0))
    has_prev = y_prev is not None
    args = [r, lw, kk, ag, k, v, k_a, s0] + ([y_prev] if has_prev else [])
    return pl.pallas_call(
        functools.partial(_wkv_kernel, reverse=reverse, bb=bb, n_pairs=n_pairs, has_prev=has_prev),
        out_shape=(jax.ShapeDtypeStruct((B, T, W), jnp.float32), jax.ShapeDtypeStruct(s0.shape, jnp.float32)),
        grid=(B // bb, nc),
        in_specs=[tok] * 6 + [pl.BlockSpec((1, W), lambda b, c: (0, 0)), st] + ([tok] if has_prev else []),
        out_specs=(tok, st),
        scratch_shapes=[pltpu.VMEM((bb, n_pairs, HP, HP), jnp.float32)],
        compiler_params=_cparams("parallel", "arbitrary"),
        name="wkv_chunked",
    )(*args)


def _rwkv_readout_tile(y, r, k, v, g, ag0, ag1, ka, rk, lnx_g, lnx_b, ones_bd):
    mean = _split_dot(y, ones_bd, 2) * (1.0 / RWKV_HEAD)
    yc = y - mean
    var = _split_dot(yc * yc, ones_bd, 2) * (1.0 / RWKV_HEAD)
    yn = yc * lax.rsqrt(var + RWKV_LNX_EPS) * lnx_g + lnx_b
    ksum = k * (2.0 + (ag0 + ag1 - 2.0) * ka)
    bonus = _split_dot(r * ksum * rk, ones_bd, 2) * v
    return (yn + bonus) * g


def rwkv_mix(proj_x, proj_c, vf_x, vf_c, mu, w0, w2, a0, a2, k_k, k_a, g2, vres, tm):
    B = proj_x.shape[0]
    fx = rwkv_features(proj_x, mu, w0, w2, a0, a2, k_k, g2, vres, vf_x, tm)
    fc = rwkv_features(proj_c, mu, w0, w2, a0, a2, k_k, g2, vres, vf_c, tm)
    ka = k_a.reshape(1, -1)
    y_x = y_c = None
    for d in range(N_DIR):
        s0 = jnp.zeros((B, BRANCH_WIDTH // HP, HP, HP), jnp.float32)
        y_c, s_ctx = wkv_chunked(fc[0], fc[5 + d], fc[3], fc[7 + d], fc[1], fc[2], ka, s0, y_c, d == 1)
        y_x, _ = wkv_chunked(fx[0], fx[5 + d], fx[3], fx[7 + d], fx[1], fx[2], ka, s_ctx, y_x, d == 1)
    pick = lambda y, f: (y, f[0], f[1], f[2], f[4], f[7], f[8])
    vf_x = fx[2] if vres is None else vf_x
    vf_c = fc[2] if vres is None else vf_c
    return pick(y_x, fx), pick(y_c, fc), vf_x, vf_c


def _qknorm_kernel(z_ref, qg_ref, kg_ref, ones_ref, q_ref, k_ref, v_ref):
    ones_bd = ones_ref[...]
    C = BRANCH_WIDTH
    z = z_ref[0].astype(jnp.float32)
    q = z[:, 0:C]
    k = z[:, C:2 * C]
    qn = q * lax.rsqrt(_split_dot(q * q, ones_bd, 2) * (1.0 / NA_HEAD) + RMS_EPS) * qg_ref[...]
    kn = k * lax.rsqrt(_split_dot(k * k, ones_bd, 2) * (1.0 / NA_HEAD) + RMS_EPS) * kg_ref[...]
    q_ref[0] = (qn * NA_HEAD ** -0.5).astype(BF)
    k_ref[0] = kn.astype(BF)
    v_ref[0] = z[:, 2 * C:3 * C].astype(BF)


def na_qknorm(proj, qn_g, kn_g, tm):
    B, T, _ = proj.shape
    C = BRANCH_WIDTH
    tok = pl.BlockSpec((1, tm, C), lambda b, i: (b, i, 0))
    par = pl.BlockSpec((1, C), lambda b, i: (0, 0))
    sd = jax.ShapeDtypeStruct((B, T, C), BF)
    return pl.pallas_call(
        _qknorm_kernel, out_shape=(sd, sd, sd), grid=(B, T // tm),
        in_specs=[pl.BlockSpec((1, tm, NA_COLS), lambda b, i: (b, i, NA_BLK)), par, par,
                  pl.BlockSpec((C, C), lambda b, i: (0, 0))],
        out_specs=(tok, tok, tok),
        compiler_params=_cparams("parallel", "parallel"),
        name="na_qknorm",
    )(proj, jnp.tile(qn_g, NA_HEADS).reshape(1, C), jnp.tile(kn_g, NA_HEADS).reshape(1, C), _head_ones(C, NA_HEAD))


def na_bias_table(rpb):
    qc = np.arange(GRID_W)[:, None]
    kc = np.arange(GRID_W)[None, :]
    cs = np.clip(qc - NA_WIN_C // 2, 0, GRID_W - NA_WIN_C)
    valid = (kc >= cs) & (kc < cs + NA_WIN_C)
    cidx = np.clip(kc - qc + NA_WIN_C - 1, 0, 2 * NA_WIN_C - 2)
    t = jnp.where(valid[None, None], rpb[:, :, cidx], NEG_INF)
    t2 = jnp.concatenate([t[:, :-1], t[:, 1:]], axis=-1)
    H = rpb.shape[0]
    t2 = t2.reshape(H // 2, 2, 2 * NA_WIN_R - 2, GRID_W, 2 * GRID_W).transpose(0, 2, 1, 3, 4)
    return t2.reshape(H // 2, 2 * NA_WIN_R - 2, 2 * GRID_W, 2 * GRID_W)


def _na_kernel(q_ref, k_ref, v_ref, kc_ref, vc_ref, bias_ref, o_ref, *, rows):
    nwin = NA_WIN_R * GRID_W
    lo = _lo_lanes(GRID_W)
    units = []
    for rr in range(NA_ROWS_PER_STEP):
        r = pl.program_id(1) * NA_ROWS_PER_STEP + rr
        rs = jnp.clip(r - NA_WIN_R // 2, 0, rows - NA_WIN_R)
        k0 = pl.multiple_of(rs * GRID_W, GRID_W)
        for p in range(BRANCH_WIDTH // HP):
            units.append((rr, p, slice(p * HP, (p + 1) * HP), rs - r + NA_WIN_R - 1, k0))
    qs = [_stack_pair(q_ref[0, rr * GRID_W:(rr + 1) * GRID_W, sl], lo) for rr, _, sl, _, _ in units]
    s_loc = [_dot_nt(qs[i], k_ref[0, pl.ds(u[4], nwin), u[2]]) for i, u in enumerate(units)]
    s_ctx = [_dot_nt(qs[i], kc_ref[0, :, u[2]]) for i, u in enumerate(units)]
    p_loc, p_ctx, den = [], [], []
    for i, (_, p, _, base, _) in enumerate(units):
        sl_b = s_loc[i] + jnp.concatenate([bias_ref[p, base + 2 * j] for j in range(NA_WIN_R // 2)], axis=1)
        m = jnp.maximum(jnp.max(sl_b, axis=1, keepdims=True), jnp.max(s_ctx[i], axis=1, keepdims=True))
        el = jnp.exp(sl_b - m)
        ec = jnp.exp(s_ctx[i] - m)
        den.append(jnp.sum(el, axis=1, keepdims=True) + jnp.sum(ec, axis=1, keepdims=True))
        p_loc.append(el.astype(BF))
        p_ctx.append(ec.astype(BF))
    o_loc = [_bdot(p_loc[i], v_ref[0, pl.ds(u[4], nwin), u[2]]) for i, u in enumerate(units)]
    o_ctx = [_bdot(p_ctx[i], vc_ref[0, :, u[2]]) for i, u in enumerate(units)]
    for i, (rr, _, sl, _, _) in enumerate(units):
        o = (o_loc[i] + o_ctx[i]) / den[i]
        o_ref[0, rr * GRID_W:(rr + 1) * GRID_W, sl] = jnp.where(lo, o[:GRID_W], o[GRID_W:])


def na_attention(q, k, v, kc, vc, bias_tab):
    B, S, C = q.shape
    rows = S // GRID_W
    n_ctx = kc.shape[1]
    seq = pl.BlockSpec((1, S, C), lambda b, r: (b, 0, 0))
    cx = pl.BlockSpec((1, n_ctx, C), lambda b, r: (b, 0, 0))
    row = pl.BlockSpec((1, NA_ROWS_PER_STEP * GRID_W, C), lambda b, r: (b, r, 0))
    return pl.pallas_call(
        functools.partial(_na_kernel, rows=rows),
        out_shape=jax.ShapeDtypeStruct((B, S, C), jnp.float32),
        grid=(B, rows // NA_ROWS_PER_STEP),
        in_specs=[row, seq, seq, cx, cx, pl.BlockSpec(bias_tab.shape, lambda b, r: (0, 0, 0, 0))],
        out_specs=row,
        compiler_params=_cparams("parallel", "arbitrary"),
        name="na_attention",
    )(q, k, v, kc, vc, bias_tab)


def _ctx_attn_kernel(q_ref, k_ref, v_ref, o_ref):
    n = q_ref.shape[1]
    lo = _lo_lanes(n)
    sls = [slice(p * HP, (p + 1) * HP) for p in range(BRANCH_WIDTH // HP)]
    sc = [_dot_nt(_stack_pair(q_ref[0, :, sl], lo), k_ref[0, :, sl]) for sl in sls]
    e = [jnp.exp(x - jnp.max(x, axis=1, keepdims=True)) for x in sc]
    o = [_bdot(e[p], v_ref[0, :, sl]) / jnp.sum(e[p], axis=1, keepdims=True) for p, sl in enumerate(sls)]
    for p, sl in enumerate(sls):
        o_ref[0, :, sl] = jnp.where(lo, o[p][:n], o[p][n:])


def ctx_attention(q, k, v):
    B, n, C = q.shape
    blk = pl.BlockSpec((1, n, C), lambda b: (b, 0, 0))
    return pl.pallas_call(
        _ctx_attn_kernel, out_shape=jax.ShapeDtypeStruct((B, n, C), jnp.float32), grid=(B,),
        in_specs=[blk, blk, blk], out_specs=blk,
        compiler_params=_cparams("parallel"),
        name="ctx_attention",
    )(q, k, v)


def na_mix(proj_x, proj_c, qn_g, kn_g, rpb, need_ctx, tm):
    q, k, v = na_qknorm(proj_x, qn_g, kn_g, tm)
    qc, kc, vc = na_qknorm(proj_c, qn_g, kn_g, tm)
    out_x = na_attention(q, k, v, kc, vc, na_bias_table(rpb))
    out_c = ctx_attention(qc, kc, vc) if need_ctx else None
    return out_x, out_c


def rope_tables(n_tokens):
    t = jnp.arange(n_tokens)
    pos = jnp.stack([t // GRID_W, t % GRID_W], axis=-1).astype(jnp.float32)
    inv = ROPE_BASE ** (-jnp.arange(ROPE_ROT, dtype=jnp.float32) / ROPE_ROT)
    ang = pos[:, :, None] * inv
    cos_h = jnp.concatenate([jnp.cos(ang), jnp.cos(ang)], axis=-1).reshape(n_tokens, MLSTM_DQK)
    sin_h = jnp.concatenate([-jnp.sin(ang), jnp.sin(ang)], axis=-1).reshape(n_tokens, MLSTM_DQK)
    col = np.arange(MLSTM_QK)
    partner = np.where((col % (2 * ROPE_ROT)) < ROPE_ROT, col + ROPE_ROT, col - ROPE_ROT)
    perm = np.zeros((MLSTM_QK, MLSTM_QK), np.float32)
    perm[partner, col] = 1.0
    return jnp.tile(cos_h, (1, MLSTM_HEADS)), jnp.tile(sin_h, (1, MLSTM_HEADS)), jnp.asarray(perm, BF)


def _mlstm_prep_kernel(z_ref, gp_ref, ib_ref, fb_ref, *rest, rope):
    if rope:
        cos_ref, sin_ref, perm_ref, q_ref, k_ref, v_ref, g_ref = rest
    else:
        q_ref, k_ref, v_ref, g_ref = rest
    z = z_ref[0].astype(jnp.float32)
    q = z[:, 0:MLSTM_QK]
    k = z[:, MLSTM_QK:2 * MLSTM_QK]
    if rope:
        perm = perm_ref[...]
        q = q * cos_ref[...] + _split_dot(q, perm, 3) * sin_ref[...]
        k = k * cos_ref[...] + _split_dot(k, perm, 3) * sin_ref[...]
    q_ref[0] = (q * MLSTM_DQK ** -0.5).astype(BF)
    k_ref[0] = k.astype(BF)
    v_ref[0] = z[:, 2 * MLSTM_QK:2 * MLSTM_QK + BRANCH_WIDTH].astype(BF)
    gp = gp_ref[0].astype(jnp.float32)
    lane = lax.broadcasted_iota(jnp.int32, gp.shape, 1)
    ig = GATE_CAP * jnp.tanh((gp + ib_ref[...]) / GATE_CAP)
    fg = GATE_CAP * jnp.tanh((gp + fb_ref[...]) / GATE_CAP)
    g_ref[0] = jnp.where(lane < N_DIR * MLSTM_HEADS, ig, jax.nn.log_sigmoid(fg))


def mlstm_prep(proj, i_bias, f_bias, rope, tm):
    B, T, _ = proj.shape
    ng = N_DIR * MLSTM_HEADS
    ib = jnp.zeros((1, V7X_LANES), jnp.float32).at[0, 0:ng].set(i_bias.reshape(-1))
    fb = jnp.zeros((1, V7X_LANES), jnp.float32).at[0, ng:2 * ng].set(f_bias.reshape(-1))
    tok = lambda w: pl.BlockSpec((1, tm, w), lambda b, i: (b, i, 0))
    par = pl.BlockSpec((1, V7X_LANES), lambda b, i: (0, 0))
    args = [proj, proj, ib, fb]
    specs = [pl.BlockSpec((1, tm, MLSTM_MAIN_COLS), lambda b, i: (b, i, ML_BLK)),
             pl.BlockSpec((1, tm, V7X_LANES), lambda b, i: (b, i, ML_GATE_BLK)), par, par]
    if rope is not None:
        args += list(rope)
        specs += [pl.BlockSpec((tm, MLSTM_QK), lambda b, i: (i, 0)), pl.BlockSpec((tm, MLSTM_QK), lambda b, i: (i, 0)),
                  pl.BlockSpec((MLSTM_QK, MLSTM_QK), lambda b, i: (0, 0))]
    return pl.pallas_call(
        functools.partial(_mlstm_prep_kernel, rope=rope is not None),
        out_shape=(jax.ShapeDtypeStruct((B, T, MLSTM_QK), BF), jax.ShapeDtypeStruct((B, T, MLSTM_QK), BF),
                   jax.ShapeDtypeStruct((B, T, BRANCH_WIDTH), BF), jax.ShapeDtypeStruct((B, T, V7X_LANES), jnp.float32)),
        grid=(B, T // tm), in_specs=specs,
        out_specs=(tok(MLSTM_QK), tok(MLSTM_QK), tok(BRANCH_WIDTH), tok(V7X_LANES)),
        compiler_params=_cparams("parallel", "parallel"),
        name="mlstm_prep",
    )(*args)


def _mlstm_kernel(q_ref, k_ref, v_ref, gc_ref, gr_ref, c0_ref, n0_ref, m0_ref, *rest, reverse, direction, bb, has_prev):
    if has_prev:
        hprev_ref, h_ref, cout_ref, nout_ref, mout_ref, c_scr, n_scr, m_scr = rest
    else:
        h_ref, cout_ref, nout_ref, mout_ref, c_scr, n_scr, m_scr = rest
    L = MLSTM_CHUNK
    DV = MLSTM_DV
    H = MLSTM_HEADS
    cidx = pl.program_id(1)

    @pl.when(cidx == 0)
    def _():
        c_scr[...] = c0_ref[...]
        n_scr[...] = n0_ref[...]
        m_scr[...] = m0_ref[...]

    ti = lax.broadcasted_iota(jnp.int32, (L, L), 0)
    si = lax.broadcasted_iota(jnp.int32, (L, L), 1)
    before = (si >= ti) if reverse else (si <= ti)
    tri = before.astype(jnp.float32)
    lo = _lo_lanes(L)
    lane1 = lax.broadcasted_iota(jnp.int32, (1, HP), 1)
    lo_row = lane1 < (HP // 2)
    lo_col = lax.broadcasted_iota(jnp.int32, (HP, 1), 0) < (HP // 2)
    col = lambda a, c: a[:, c:c + 1]
    rowv = lambda a, c: a[c:c + 1, :]
    ci = lambda h: direction * H + h
    cf = lambda h: N_DIR * H + direction * H + h
    units = [(bi, p) for bi in range(bb) for p in range(H // 2)]
    blocks = [(bi, h) for bi in range(bb) for h in range(H)]
    cat = lambda xs: jnp.concatenate(xs, axis=0)
    rows_of = lambda col11: jnp.broadcast_to(col11, (L, 1))

    bcol, icol, brow, irow, tot_b, m_b = [], [], [], [], [], []
    for bi in range(bb):
        gc = gc_ref[bi]
        gr = gr_ref[bi, 0]
        bcol_all = jnp.dot(tri, gc, precision=HIGHEST, preferred_element_type=jnp.float32)
        brow_all = lax.dot_general(gr, tri, (((1,), (1,)), ((), ())), precision=HIGHEST, preferred_element_type=jnp.float32)
        tot_all = jnp.sum(gc, axis=0, keepdims=True)
        m_all = m_scr[bi, 0]
        for h in range(H):
            bcol.append(col(bcol_all, cf(h)))
            icol.append(col(gc, ci(h)))
            brow.append(jnp.broadcast_to(rowv(brow_all, cf(h)), (L, L)))
            irow.append(jnp.broadcast_to(rowv(gr, ci(h)), (L, L)))
            tot_b.append(col(tot_all, cf(h)))
            m_b.append(col(m_all, h))
    nb = len(blocks)
    bcol_r, icol_r = cat(bcol), cat(icol)
    tot_r = cat([rows_of(t) for t in tot_b])
    mprev_r = cat([rows_of(m) for m in m_b])
    causal_r = cat([before] * nb)
    dmat = jnp.where(causal_r, bcol_r - cat(brow) + cat(irow), NEG_INF)
    inter = bcol_r + mprev_r
    m_t = jnp.maximum(inter, jnp.max(dmat, axis=1, keepdims=True))
    decay = jnp.exp(dmat - m_t)
    sc_inter = jnp.exp(inter - m_t)
    inv_floor = jnp.exp(-m_t)

    qs = [_stack_pair(q_ref[bi, :, p * HP:(p + 1) * HP], lo) for bi, p in units]
    kp = [k_ref[bi, :, p * HP:(p + 1) * HP] for bi, p in units]
    vb = [v_ref[bi, :, h * DV:(h + 1) * DV] for bi, h in blocks]
    c_pair = [c_scr[bi, p] for bi, p in units]
    n_pair = [n_scr[bi, p] for bi, p in units]
    qk = cat([_dot_nt(qs[u], kp[u]) for u in range(len(units))])
    qc = cat([_bdot(qs[u], c_pair[u]) for u in range(len(units))])
    qn = cat([jnp.sum(qs[u].astype(jnp.float32) * n_pair[u], axis=1, keepdims=True) for u in range(len(units))])
    smat = qk * decay
    den = sc_inter * qn + jnp.sum(smat, axis=1, keepdims=True)
    sb = smat.astype(BF)
    sv = cat([_bdot(sb[j * L:(j + 1) * L], vb[j]) for j in range(nb)])
    hout = (sc_inter * qc + sv) / jnp.maximum(jnp.abs(den), inv_floor)
    for j, (bi, h) in enumerate(blocks):
        hs = slice(h * DV, (h + 1) * DV)
        part = hout[j * L:(j + 1) * L]
        h_ref[bi, :, hs] = (part + hprev_ref[bi, :, hs]) if has_prev else part

    wlog = tot_r - bcol_r + icol_r
    m_new = [jnp.maximum(tot_b[j] + m_b[j], jnp.max(wlog[j * L:(j + 1) * L], axis=0, keepdims=True)) for j in range(nb)]
    dec = [jnp.exp(tot_b[j] + m_b[j] - m_new[j]) for j in range(nb)]
    wexp = jnp.exp(wlog - cat([rows_of(m) for m in m_new]))
    for u, (bi, p) in enumerate(units):
        j0 = bi * H + 2 * p
        kws = _stack_pair(kp[u], lo).astype(jnp.float32) * wexp[2 * u * L:2 * (u + 1) * L]
        upd = _dot_tn(kws, cat([vb[j0], vb[j0 + 1]]))
        c_scr[bi, p] = jnp.where(lo_col, dec[j0], dec[j0 + 1]) * c_pair[u] + upd
        n_scr[bi, p] = jnp.where(lo_row, dec[j0], dec[j0 + 1]) * n_pair[u] + jnp.sum(kws, axis=0, keepdims=True)
    for bi in range(bb):
        m_all = m_scr[bi, 0]
        for h in range(H):
            m_all = jnp.where(lane1 == h, m_new[bi * H + h], m_all)
        m_scr[bi, 0] = m_all

    @pl.when(cidx == pl.num_programs(1) - 1)
    def _():
        cout_ref[...] = c_scr[...]
        nout_ref[...] = n_scr[...]
        mout_ref[...] = m_scr[...]


def mlstm_chunked(q, k, v, gates, state, h_prev, direction):
    B, T, _ = q.shape
    L = MLSTM_CHUNK
    bb = MLSTM_SEQS_PER_STEP
    nc = T // L
    reverse = direction == 1
    g_rows = gates[:, :, :MLSTM_GATES].reshape(B, nc, L, MLSTM_GATES).transpose(0, 1, 3, 2)
    cm = (lambda c: nc - 1 - c) if reverse else (lambda c: c)
    tok = lambda w: pl.BlockSpec((bb, L, w), lambda b, c: (b, cm(c), 0))
    st = lambda a: pl.BlockSpec((bb,) + a.shape[1:], lambda b, c: (b,) + (0,) * (a.ndim - 1))
    c0, n0, m0 = state
    has_prev = h_prev is not None
    args = [q, k, v, gates, g_rows, c0, n0, m0] + ([h_prev] if has_prev else [])
    outs = pl.pallas_call(
        functools.partial(_mlstm_kernel, reverse=reverse, direction=direction, bb=bb, has_prev=has_prev),
        out_shape=(jax.ShapeDtypeStruct((B, T, BRANCH_WIDTH), jnp.float32),) + tuple(
            jax.ShapeDtypeStruct(a.shape, jnp.float32) for a in state),
        grid=(B // bb, nc),
        in_specs=[tok(MLSTM_QK), tok(MLSTM_QK), tok(BRANCH_WIDTH), tok(V7X_LANES),
                  pl.BlockSpec((bb, 1, MLSTM_GATES, L), lambda b, c: (b, cm(c), 0, 0)), st(c0), st(n0), st(m0)]
                 + ([tok(BRANCH_WIDTH)] if has_prev else []),
        out_specs=(tok(BRANCH_WIDTH), st(c0), st(n0), st(m0)),
        scratch_shapes=[pltpu.VMEM((bb,) + a.shape[1:], jnp.float32) for a in state],
        compiler_params=_cparams("parallel", "arbitrary"),
        name="mlstm_chunked",
    )(*args)
    return outs[0], outs[1:]


def _mlstm_readout_tile(h, o, norm_g):
    parts = []
    for hd in range(MLSTM_HEADS):
        x = h[:, hd * MLSTM_DV:(hd + 1) * MLSTM_DV]
        parts.append(x * lax.rsqrt(jnp.mean(x * x, axis=1, keepdims=True) + RMS_EPS))
    return jnp.concatenate(parts, axis=1) * norm_g * jax.nn.sigmoid(o)


def mlstm_mix(proj_x, proj_c, i_bias, f_bias, rope, tm):
    B = proj_x.shape[0]
    qx, kx, vx, gx = mlstm_prep(proj_x, i_bias, f_bias, rope, tm)
    qc, kc, vc, gc = mlstm_prep(proj_c, i_bias, f_bias, None, tm)
    h_x = h_c = None
    for d in range(N_DIR):
        st0 = (jnp.zeros((B, MLSTM_HEADS // 2, HP, HP), jnp.float32), jnp.zeros((B, MLSTM_HEADS // 2, 1, HP), jnp.float32),
               jnp.zeros((B, 1, 1, HP), jnp.float32))
        h_c, st_ctx = mlstm_chunked(qc, kc, vc, gc, st0, h_c, d)
        h_x, _ = mlstm_chunked(qx, kx, vx, gx, st_ctx, h_x, d)
    return h_x, h_c


def _merge_kernel(y_ref, r_ref, k_ref, v_ref, g_ref, ag0_ref, ag1_ref, yb_ref, h_ref, og_ref, ga_ref, gb_ref, gc_ref,
                  x_ref, mod_ref, ka_ref, rk_ref, lg_ref, lb_ref, ones_ref, ng_ref, wb_ref, wo_ref, o_ref):
    f32 = lambda ref: ref[0].astype(jnp.float32)
    ya = _rwkv_readout_tile(y_ref[0], f32(r_ref), f32(k_ref), f32(v_ref), f32(g_ref), f32(ag0_ref), f32(ag1_ref),
                            ka_ref[...], rk_ref[...], lg_ref[...], lb_ref[...], ones_ref[...])
    yc = _mlstm_readout_tile(h_ref[0], f32(og_ref), ng_ref[...])
    merged = None
    for i, (y, gate_ref) in enumerate(((ya, ga_ref), (yb_ref[0], gb_ref), (yc, gc_ref))):
        t = jax.nn.sigmoid(f32(gate_ref)) * _bdot(y, wb_ref[i])
        merged = t if merged is None else merged + t
    o_ref[0] = x_ref[0] + mod_ref[0, 2:3, :] * _bdot(merged, wo_ref[...])


def merge_apply(rw, yb, h_ml, proj, x, mod, rw_params, ml_norm_g, w_branch, w_out, tm):
    B, T, D = x.shape
    C = BRANCH_WIDTH
    tok = lambda w: pl.BlockSpec((1, tm, w), lambda b, i: (b, i, 0))
    gate = lambda k: pl.BlockSpec((1, tm, D), lambda b, i: (b, i, GATE_BLK0 + k))
    par = pl.BlockSpec((1, C), lambda b, i: (0, 0))
    params = [p.reshape(1, C) for p in rw_params]
    return pl.pallas_call(
        _merge_kernel, out_shape=jax.ShapeDtypeStruct((B, T, D), jnp.float32), grid=(B, T // tm),
        in_specs=[tok(C)] * 9 + [pl.BlockSpec((1, tm, C), lambda b, i: (b, i, ML_OGATE_BLK)), gate(0), gate(1), gate(2), tok(D),
                  pl.BlockSpec((1, 6, D), lambda b, i: (b, 0, 0))] + [par] * 4
                 + [pl.BlockSpec((C, C), lambda b, i: (0, 0)), par,
                    pl.BlockSpec(w_branch.shape, lambda b, i: (0, 0, 0)), pl.BlockSpec(w_out.shape, lambda b, i: (0, 0))],
        out_specs=tok(D),
        compiler_params=_cparams("parallel", "parallel"),
        name="merge_branches",
    )(*rw, yb, h_ml, proj, proj, proj, proj, x, mod, *params, _head_ones(C, RWKV_HEAD), ml_norm_g.reshape(1, C),
      w_branch, w_out)


def _route_kernel(x_ref, g_ref, mod_ref, wr_ref, rb_ref, h_ref, gate_ref):
    h = _norm_mod(x_ref[0], g_ref[...], mod_ref[0, 4:5, :], mod_ref[0, 3:4, :])
    h_ref[0] = h.astype(BF)
    tm = h.shape[0]
    logits = lax.dot_general(wr_ref[...], h, (((1,), (1,)), ((), ())), precision=HIGHEST, preferred_element_type=jnp.float32)
    scores = jax.nn.sigmoid(logits)
    sel = scores + rb_ref[...]
    gsz = N_EXPERTS // N_GROUPS
    grp = sel.reshape(N_GROUPS, gsz, tm)
    iota_in = lax.broadcasted_iota(jnp.int32, grp.shape, 1)
    m1 = jnp.max(grp, axis=1, keepdims=True)
    first = jnp.min(jnp.where(grp == m1, iota_in, gsz), axis=1, keepdims=True)
    m2 = jnp.max(jnp.where(iota_in == first, -jnp.inf, grp), axis=1, keepdims=True)
    gscore = (m1 + m2).reshape(N_GROUPS, tm)
    gi = lax.broadcasted_iota(jnp.int32, (N_GROUPS, tm), 0)
    rank = jnp.zeros((N_GROUPS, tm), jnp.int32)
    for g2 in range(N_GROUPS):
        other = gscore[g2:g2 + 1, :]
        rank = rank + ((other > gscore) | ((other == gscore) & (g2 < gi))).astype(jnp.int32)
    gmask = rank < TOPK_GROUPS
    emask = jnp.broadcast_to(gmask.reshape(N_GROUPS, 1, tm), (N_GROUPS, gsz, tm)).reshape(N_EXPERTS, tm)
    cand = jnp.where(emask, sel, NEG_INF)
    ei = lax.broadcasted_iota(jnp.int32, (N_EXPERTS, tm), 0)
    chosen = jnp.zeros((N_EXPERTS, tm), jnp.bool_)
    for _ in range(TOP_K):
        mx = jnp.max(cand, axis=0, keepdims=True)
        idx = jnp.min(jnp.where(cand == mx, ei, N_EXPERTS), axis=0, keepdims=True)
        hit = ei == idx
        chosen = chosen | hit
        cand = jnp.where(hit, -jnp.inf, cand)
    w = jnp.where(chosen, scores, 0.0)
    w = w / jnp.sum(w, axis=0, keepdims=True) * ROUTED_SCALE
    gate_ref[0] = jnp.concatenate([w, jnp.zeros((V7X_LANES - N_EXPERTS, tm), jnp.float32)], axis=0).T


def moe_route(x, g, mod, w_router, router_bias, tm):
    B, T, D = x.shape
    tok = pl.BlockSpec((1, tm, D), lambda b, i: (b, i, 0))
    return pl.pallas_call(
        _route_kernel,
        out_shape=(jax.ShapeDtypeStruct((B, T, D), BF), jax.ShapeDtypeStruct((B, T, V7X_LANES), jnp.float32)),
        grid=(B, T // tm),
        in_specs=[tok, pl.BlockSpec((1, D), lambda b, i: (0, 0)), pl.BlockSpec((1, 6, D), lambda b, i: (b, 0, 0)),
                  pl.BlockSpec((N_EXPERTS, D), lambda b, i: (0, 0)), pl.BlockSpec((N_EXPERTS, 1), lambda b, i: (0, 0))],
        out_specs=(tok, pl.BlockSpec((1, tm, V7X_LANES), lambda b, i: (b, i, 0))),
        compiler_params=_cparams("parallel", "parallel"),
        name="moe_route",
    )(x, g.reshape(1, D), mod, w_router.T, router_bias.reshape(N_EXPERTS, 1))


def _moe_kernel(h_ref, gate_ref, x_ref, mod_ref, sel_ref, wg_ref, wu_ref, wd_ref, sg_ref, su_ref, sd_ref, o_ref, *, tm):
    j = pl.program_id(1)
    rows = pl.ds(pl.multiple_of(pl.program_id(2) * tm, tm), tm)
    h = h_ref[0]

    @pl.when(j == 0)
    def _():
        sh = jax.nn.silu(_bdot(h, sg_ref[...])) * _bdot(h, su_ref[...])
        o_ref[0, rows, :] = _bdot(sh, sd_ref[...])

    g8 = _split_dot(gate_ref[0], sel_ref[0], 2)
    act = jax.nn.silu(_bdot(h, wg_ref[...])) * _bdot(h, wu_ref[...])
    act = jnp.concatenate([act[:, e * D_EXPERT:(e + 1) * D_EXPERT] * g8[:, e:e + 1] for e in range(MOE_STEP_EXPERTS)],
                          axis=1)
    o_ref[0, rows, :] += _bdot(act, wd_ref[...])

    @pl.when(j == pl.num_programs(1) - 1)
    def _():
        o_ref[0, rows, :] = x_ref[0] + mod_ref[0, 5:6, :] * o_ref[0, rows, :]


def _moe_select_table():
    se = MOE_STEP_EXPERTS
    t = np.zeros((N_EXPERTS // se, V7X_LANES, V7X_LANES), np.float32)
    for j in range(N_EXPERTS // se):
        for e in range(se):
            t[j, j * se + e, e] = 1.0
    return jnp.asarray(t, BF)


def moe_apply(h2, gates, x, mod, wg, wu, wd, sg, su, sd, tm):
    B, T, D = x.shape
    sw = MOE_STEP_EXPERTS * D_EXPERT
    n_groups = N_EXPERTS // MOE_STEP_EXPERTS
    tok = pl.BlockSpec((1, tm, D), lambda b, j, i: (b, i, 0))
    x_last = pl.BlockSpec((1, tm, D), lambda b, j, i: (b, jnp.where(j == n_groups - 1, i, 0), 0))
    full = lambda a: pl.BlockSpec(a.shape, lambda b, j, i: (0,) * a.ndim)
    return pl.pallas_call(
        functools.partial(_moe_kernel, tm=tm), out_shape=jax.ShapeDtypeStruct((B, T, D), jnp.float32),
        grid=(B, n_groups, T // tm),
        in_specs=[tok, pl.BlockSpec((1, tm, V7X_LANES), lambda b, j, i: (b, i, 0)), x_last,
                  pl.BlockSpec((1, 6, D), lambda b, j, i: (b, 0, 0)),
                  pl.BlockSpec((1, V7X_LANES, V7X_LANES), lambda b, j, i: (j, 0, 0)),
                  pl.BlockSpec((D, sw), lambda b, j, i: (0, j)), pl.BlockSpec((D, sw), lambda b, j, i: (0, j)),
                  pl.BlockSpec((sw, D), lambda b, j, i: (j, 0)), full(sg), full(su), full(sd)],
        out_specs=pl.BlockSpec((1, T, D), lambda b, j, i: (b, 0, 0)),
        compiler_params=pltpu.CompilerParams(dimension_semantics=("parallel", "arbitrary", "arbitrary"),
                                             vmem_limit_bytes=MOE_VMEM_LIMIT_BYTES),
        name="moe_experts",
    )(h2, gates, x, mod, _moe_select_table(), wg, wu, wd, sg, su, sd)


def _reorder_w_in(w):
    o_na = RWKV_COLS
    o_ml = o_na + NA_COLS
    o_mg = o_ml + MLSTM_MAIN_COLS
    o_gate = o_mg + MLSTM_GATES
    pad = jnp.zeros((w.shape[0], V7X_LANES - MLSTM_GATES), w.dtype)
    return jnp.concatenate([w[:, o_na:o_ml], w[:, o_ml:o_mg], w[:, o_gate:], w[:, :RWKV_COLS], w[:, o_mg:o_gate], pad],
                           axis=1).astype(BF)


def kernel(x, c, ctx, c_ctx, w_ada, b_ada, norm1_g, norm2_g, w_in, rw_mu, rw_w0, rw_w2, rw_a0, rw_a2, rw_k_k, rw_k_a, rw_r_k, rw_g2, rw_lnx_g, rw_lnx_b, rw_v0, rw_v1, rw_v2, na_qn_g, na_kn_g, na_rpb, ml_i_bias, ml_f_bias, ml_norm_g, w_branch, w_out, moe_router, moe_bias, moe_w_gate, moe_w_up, moe_w_down, sh_w_gate, sh_w_up, sh_w_down):
    B, S, D = x.shape
    n_ctx = ctx.shape[1]
    tm = 256
    assert S % tm == 0 and n_ctx % tm == 0 and PROJ_COLS == IN_COLS + V7X_LANES - MLSTM_GATES
    rope = rope_tables(S)
    n_cond = B + 1
    cond_pad = (-n_cond) % V7X_SUBLANES
    s_cond = jnp.pad(jnp.concatenate([jax.nn.silu(c), jax.nn.silu(c_ctx)[None]], axis=0), ((0, cond_pad), (0, 0)))
    vf_x = vf_c = None
    for l in range(DEPTH):
        need_ctx = l < DEPTH - 1
        mod = pmm(s_cond, w_ada[l]) + b_ada[l]
        mod_x = mod[:B].reshape(B, 6, D)
        mod_c = jnp.broadcast_to(mod[B].reshape(1, 6, D), (B, 6, D))
        w_proj = _reorder_w_in(w_in[l])
        proj_x = norm_proj(x, norm1_g[l], mod_x, w_proj)
        proj_c = norm_proj(ctx.reshape(1, B * n_ctx, D), norm1_g[l], mod_c[:1], w_proj).reshape(B, n_ctx, PROJ_COLS)
        vres = None if l == 0 else (rw_v0[l - 1], rw_v1[l - 1], rw_v2[l - 1])
        rw_x, rw_c, vf_x, vf_c = rwkv_mix(proj_x, proj_c, vf_x, vf_c, rw_mu[l], rw_w0[l], rw_w2[l], rw_a0[l], rw_a2[l],
                                          rw_k_k[l], rw_k_a[l], rw_g2[l], vres, tm)
        rw_params = (rw_k_a[l], rw_r_k[l], rw_lnx_g[l], rw_lnx_b[l])
        yb_x, yb_c = na_mix(proj_x, proj_c, na_qn_g[l], na_kn_g[l], na_rpb[l], need_ctx, tm)
        hm_x, hm_c = mlstm_mix(proj_x, proj_c, ml_i_bias[l], ml_f_bias[l], rope, tm)
        wb = w_branch[l].astype(BF)
        wo = w_out[l].astype(BF)
        wg = moe_w_gate[l].transpose(1, 0, 2).reshape(D, N_EXPERTS * D_EXPERT).astype(BF)
        wu = moe_w_up[l].transpose(1, 0, 2).reshape(D, N_EXPERTS * D_EXPERT).astype(BF)
        wd = moe_w_down[l].reshape(N_EXPERTS * D_EXPERT, D).astype(BF)
        shared = (sh_w_gate[l].astype(BF), sh_w_up[l].astype(BF), sh_w_down[l].astype(BF))
        x = merge_apply(rw_x, yb_x, hm_x, proj_x, x, mod_x, rw_params, ml_norm_g[l], wb, wo, tm)
        h2, gates = moe_route(x, norm2_g[l], mod_x, moe_router[l], moe_bias[l], tm)
        x = moe_apply(h2, gates, x, mod_x, wg, wu, wd, *shared, MOE_TOKEN_TILE)
        if need_ctx:
            ctx = merge_apply(rw_c, yb_c, hm_c, proj_c, ctx, mod_c, rw_params, ml_norm_g[l], wb, wo, tm)
            h2, gates = moe_route(ctx, norm2_g[l], mod_c, moe_router[l], moe_bias[l], tm)
            ctx = moe_apply(h2, gates, ctx, mod_c, wg, wu, wd, *shared, tm)
    return x
```

```python
import functools

import numpy as np
import jax
import jax.numpy as jnp
from jax import lax
from jax.experimental import pallas as pl
from jax.experimental.pallas import tpu as pltpu

D_MODEL = 1024
DEPTH = 2
GRID_W = 64
N_DIR = 2
N_BRANCH = 3
BRANCH_WIDTH = 512
RMS_EPS = 1e-6
NEG_INF = -1e30

RWKV_HEAD = 64
RWKV_DECAY_LORA = 64
RWKV_LORA_COLS = 384
RWKV_COLS = 3 * BRANCH_WIDTH + RWKV_LORA_COLS
RWKV_LNX_EPS = 64e-5
RWKV_CHUNK = 64
WKV_SEQS_PER_STEP = 8

NA_HEAD = 64
NA_HEADS = BRANCH_WIDTH // NA_HEAD
NA_WIN_R = 8
NA_WIN_C = 16
NA_COLS = 3 * BRANCH_WIDTH
NA_ROWS_PER_STEP = 2

MLSTM_HEADS = 4
MLSTM_DQK = 64
MLSTM_DV = BRANCH_WIDTH // MLSTM_HEADS
MLSTM_QK = MLSTM_HEADS * MLSTM_DQK
MLSTM_CHUNK = 64
MLSTM_SEQS_PER_STEP = 8
MLSTM_GATE_SLOTS = 8
MLSTM_MAIN_COLS = 2 * MLSTM_QK + 2 * BRANCH_WIDTH
MLSTM_GATES = 2 * N_DIR * MLSTM_HEADS
GATE_CAP = 15.0
ROPE_ROT = MLSTM_DQK // 4
ROPE_BASE = 10000.0

N_EXPERTS = 64
TOP_K = 8
N_GROUPS = 8
TOPK_GROUPS = 4
D_EXPERT = 128
ROUTED_SCALE = 2.5
MOE_STEP_EXPERTS = 8
MOE_TOKEN_TILE = 1024

GATE_COLS = N_BRANCH * D_MODEL
IN_COLS = RWKV_COLS + NA_COLS + MLSTM_MAIN_COLS + MLSTM_GATES + GATE_COLS

V7X_LANES = 128
V7X_SUBLANES = 8
HP = 2 * RWKV_HEAD
VMEM_LIMIT_BYTES = 48 * 1024 * 1024
MOE_VMEM_LIMIT_BYTES = 56 * 1024 * 1024

PROJ_COLS = 8192
NA_BLK = 0
ML_BLK = 1
ML_OGATE_BLK = (NA_COLS + 2 * MLSTM_QK + BRANCH_WIDTH) // BRANCH_WIDTH
GATE_BLK0 = (NA_COLS + MLSTM_MAIN_COLS) // D_MODEL
RWKV_COL0 = NA_COLS + MLSTM_MAIN_COLS + GATE_COLS
RWKV_BLK0 = RWKV_COL0 // BRANCH_WIDTH
RWKV_LORA_BLK = (RWKV_COL0 + 3 * BRANCH_WIDTH) // RWKV_LORA_COLS
ML_GATE_BLK = (RWKV_COL0 + RWKV_COLS) // V7X_LANES

BF = jnp.bfloat16
ACT_DTYPE = BF
HALO_ROWS = 16
HIGHEST = lax.Precision.HIGHEST


def _cparams(*sem):
    return pltpu.CompilerParams(dimension_semantics=sem, vmem_limit_bytes=VMEM_LIMIT_BYTES)


def _bdot(a, b):
    return jnp.dot(a.astype(BF), b.astype(BF), preferred_element_type=jnp.float32)


def _dot_nt(a, b):
    return lax.dot_general(a.astype(BF), b.astype(BF), (((1,), (1,)), ((), ())), preferred_element_type=jnp.float32)


def _dot_tn(a, b):
    return lax.dot_general(a.astype(BF), b.astype(BF), (((0,), (0,)), ((), ())), preferred_element_type=jnp.float32)


def _split_dot(x, w, parts):
    out = None
    rem = x
    for _ in range(parts):
        piece = rem.astype(BF)
        rem = rem - piece.astype(jnp.float32)
        t = jnp.dot(piece, w, preferred_element_type=jnp.float32)
        out = t if out is None else out + t
    return out


def _head_ones(width, head):
    i = np.arange(width) // head
    return jnp.asarray(i[:, None] == i[None, :], BF)


def _stack_pair(x, lo):
    zero = jnp.zeros_like(x)
    return jnp.concatenate([jnp.where(lo, x, zero), jnp.where(lo, zero, x)], axis=0)


def _lo_lanes(n):
    return lax.broadcasted_iota(jnp.int32, (n, HP), 1) < (HP // 2)


def _mm_kernel(x_ref, w_ref, o_ref):
    part = _bdot(x_ref[...], w_ref[...])

    @pl.when(pl.program_id(2) == 0)
    def _():
        o_ref[...] = part

    @pl.when(pl.program_id(2) > 0)
    def _():
        o_ref[...] += part


def _pick_tile(n, cands):
    for c in cands:
        if n % c == 0:
            return c
    return n


def pmm(x, w):
    M, K = x.shape
    N = w.shape[1]
    tm = _pick_tile(M, (1024, 512, 256, 128, 64, 32, 16, 8))
    tn = _pick_tile(N, (1024, 512, 384, 256, 128))
    tk = _pick_tile(K, (1024,)) if K > 1024 else K
    return pl.pallas_call(
        _mm_kernel,
        out_shape=jax.ShapeDtypeStruct((M, N), jnp.float32),
        grid=(M // tm, N // tn, K // tk),
        in_specs=[pl.BlockSpec((tm, tk), lambda i, j, k: (i, k)),
                  pl.BlockSpec((tk, tn), lambda i, j, k: (k, j))],
        out_specs=pl.BlockSpec((tm, tn), lambda i, j, k: (i, j)),
        compiler_params=_cparams("parallel", "parallel", "arbitrary"),
        name="tiled_matmul",
    )(x, w)


def _norm_mod(x, g, scale, shift):
    xn = x * lax.rsqrt(jnp.mean(x * x, axis=-1, keepdims=True) + RMS_EPS)
    return xn * g * (1.0 + scale) + shift


def _norm_proj_kernel(x_ref, g_ref, mod_ref, w_ref, o_ref, h_scr):
    @pl.when(pl.program_id(2) == 0)
    def _():
        h_scr[...] = _norm_mod(x_ref[0], g_ref[...], mod_ref[0, 1:2, :], mod_ref[0, 0:1, :]).astype(BF)

    o_ref[0] = jnp.dot(h_scr[...], w_ref[...], preferred_element_type=jnp.float32).astype(o_ref.dtype)


def norm_proj(x, g, mod, w):
    B, T, D = x.shape
    N = w.shape[1]
    tm = _pick_tile(T, (1024, 512, 256))
    tn = _pick_tile(N, (1024, 512, 256, 128))
    return pl.pallas_call(
        _norm_proj_kernel,
        out_shape=jax.ShapeDtypeStruct((B, T, N), ACT_DTYPE), grid=(B, T // tm, N // tn),
        in_specs=[pl.BlockSpec((1, tm, D), lambda b, i, j: (b, i, 0)), pl.BlockSpec((1, D), lambda b, i, j: (0, 0)),
                  pl.BlockSpec((1, 6, D), lambda b, i, j: (b, 0, 0)), pl.BlockSpec((D, tn), lambda b, i, j: (0, j))],
        out_specs=pl.BlockSpec((1, tm, tn), lambda b, i, j: (b, i, j)),
        scratch_shapes=[pltpu.VMEM((tm, D), BF)],
        compiler_params=_cparams("parallel", "parallel", "arbitrary"),
        name="norm_proj",
    )(x, g.reshape(1, D), mod, w)


def _shifted(z, prev_row, next_row):
    n = z.shape[0]
    row = lax.broadcasted_iota(jnp.int32, z.shape, 0)
    zp = jnp.where(row == 0, prev_row, pltpu.roll(z, 1, axis=0))
    zn = jnp.where(row == n - 1, next_row, pltpu.roll(z, n - 1, axis=0))
    return zp, zn


def _rwkv_feat_kernel(zr_ref, zk_ref, zv_ref, zl_ref, pr_ref, pk_ref, pv_ref, pl_ref, nr_ref, nk_ref, nv_ref, nl_ref,
                      mu_ref, w0_ref, w2_ref, a0_ref, a2_ref, kk_ref, g2_ref, ones_ref, *rest, has_vres):
    if has_vres:
        v0_ref, v1_ref, v2_ref, vf_ref = rest[:4]
        outs = rest[4:]
    else:
        outs = rest
    r_ref, k_ref, v_ref, kn_ref, g_ref, lw0_ref, lw1_ref, ag0_ref, ag1_ref = outs
    C = BRANCH_WIDTH
    first = pl.program_id(1) == 0
    last = pl.program_id(1) == pl.num_programs(1) - 1

    def shift(z_ref, p_ref, n_ref, c0, c1):
        z = z_ref[0].astype(jnp.float32)
        prev_row = jnp.where(first, 0.0, p_ref[0].astype(jnp.float32)[HALO_ROWS - 1:HALO_ROWS, :])
        next_row = jnp.where(last, 0.0, n_ref[0].astype(jnp.float32)[0:1, :])
        zp, zn = _shifted(z, prev_row, next_row)
        return z + mu_ref[:, c0:c1] * (0.5 * (zp + zn) - z)

    r = shift(zr_ref, pr_ref, nr_ref, 0, C)
    k = shift(zk_ref, pk_ref, nk_ref, C, 2 * C)
    v = shift(zv_ref, pv_ref, nv_ref, 2 * C, 3 * C)
    zl = shift(zl_ref, pl_ref, nl_ref, 3 * C, RWKV_COLS)
    wd = jnp.tanh(zl[:, 0:2 * RWKV_DECAY_LORA])
    ad = zl[:, 2 * RWKV_DECAY_LORA:4 * RWKV_DECAY_LORA]
    gd = zl[:, 4 * RWKV_DECAY_LORA:]
    for d, (lw_ref, ag_ref) in enumerate(((lw0_ref, ag0_ref), (lw1_ref, ag1_ref))):
        wl = -jax.nn.softplus(-(w0_ref[d:d + 1, :] + _bdot(wd, w2_ref[d]))) - 0.5
        lw_ref[0] = -jnp.exp(wl)
        ag_ref[0] = jax.nn.sigmoid(a0_ref[d:d + 1, :] + _bdot(ad, a2_ref[d])).astype(ag_ref.dtype)
    kq = k * kk_ref[...]
    ss = _split_dot(kq * kq, ones_ref[...], 2)
    kn_ref[0] = (kq / jnp.maximum(jnp.sqrt(ss), 1e-12)).astype(kn_ref.dtype)
    if has_vres:
        lora = _bdot(_bdot(v, v1_ref[...]), v2_ref[...])
        v = v + (vf_ref[0].astype(jnp.float32) - v) * jax.nn.sigmoid(v0_ref[...] + lora)
    g_ref[0] = _bdot(jax.nn.sigmoid(gd), g2_ref[...]).astype(g_ref.dtype)
    r_ref[0] = r.astype(r_ref.dtype)
    k_ref[0] = k.astype(k_ref.dtype)
    v_ref[0] = v.astype(v_ref.dtype)


def rwkv_features(proj, mu, w0, w2, a0, a2, k_k, g2, vres, v_first, tm):
    B, T, _ = proj.shape
    C = BRANCH_WIDTH
    nt = T // tm
    zpad = jnp.zeros((RWKV_DECAY_LORA, C), jnp.float32)
    pad_dirs = lambda w: jnp.stack([jnp.concatenate([w[0], zpad], 0), jnp.concatenate([zpad, w[1]], 0)]).astype(BF)
    sub = HALO_ROWS
    blk = lambda w, j: pl.BlockSpec((1, tm, w), lambda b, i: (b, i, j))
    before = lambda w, j: pl.BlockSpec((1, sub, w), lambda b, i: (b, jnp.maximum(i * (tm // sub) - 1, 0), j))
    after = lambda w, j: pl.BlockSpec((1, sub, w), lambda b, i: (b, jnp.minimum((i + 1) * (tm // sub), T // sub - 1), j))
    tok = pl.BlockSpec((1, tm, C), lambda b, i: (b, i, 0))
    full = lambda a: pl.BlockSpec(a.shape, lambda b, i: (0,) * a.ndim)
    params = [mu.reshape(1, -1), w0, pad_dirs(w2), a0, pad_dirs(a2), k_k.reshape(1, -1), g2.astype(BF),
              _head_ones(C, RWKV_HEAD)]
    pieces = [(C, RWKV_BLK0), (C, RWKV_BLK0 + 1), (C, RWKV_BLK0 + 2), (RWKV_LORA_COLS, RWKV_LORA_BLK)]
    args = [proj] * (3 * len(pieces)) + params
    specs = ([blk(w, j) for w, j in pieces] + [before(w, j) for w, j in pieces] + [after(w, j) for w, j in pieces]
             + [full(a) for a in params])
    if vres is not None:
        v0, v1, v2 = vres
        extra = [v0.reshape(1, -1), jnp.pad(v1, ((0, 0), (0, V7X_LANES - v1.shape[1]))).astype(BF),
                 jnp.pad(v2, ((0, V7X_LANES - v2.shape[0]), (0, 0))).astype(BF)]
        args += extra + [v_first]
        specs += [full(a) for a in extra] + [tok]
    return pl.pallas_call(
        functools.partial(_rwkv_feat_kernel, has_vres=vres is not None),
        out_shape=(jax.ShapeDtypeStruct((B, T, C), ACT_DTYPE),) * 5 + (jax.ShapeDtypeStruct((B, T, C), jnp.float32),) * 2
                  + (jax.ShapeDtypeStruct((B, T, C), ACT_DTYPE),) * 2,
        grid=(B, nt), in_specs=specs, out_specs=(tok,) * 9,
        compiler_params=_cparams("parallel", "parallel"),
        name="rwkv_features",
    )(*args)


def _wkv_kernel(r_ref, lw_ref, kk_ref, a_ref, k_ref, v_ref, ka_ref, s0_ref, *rest, reverse, bb, n_pairs, has_prev):
    if has_prev:
        yprev_ref, y_ref, sout_ref, s_scr = rest
    else:
        y_ref, sout_ref, s_scr = rest
    C = RWKV_CHUNK
    c_idx = pl.program_id(1)

    @pl.when(c_idx == 0)
    def _():
        s_scr[...] = s0_ref[...]

    ti = lax.broadcasted_iota(jnp.int32, (C, C), 0)
    si = lax.broadcasted_iota(jnp.int32, (C, C), 1)
    tri = ((si >= ti) if reverse else (si <= ti)).astype(jnp.float32)
    tp = lax.broadcasted_iota(jnp.int32, (C, 2 * C), 0)
    sp = lax.broadcasted_iota(jnp.int32, (C, 2 * C), 1) % C
    m_strict = (sp > tp) if reverse else (sp < tp)
    m_incl = (sp >= tp) if reverse else (sp <= tp)
    eye = (tp == sp).astype(jnp.float32)
    t2 = lax.broadcasted_iota(jnp.int32, (2 * C, 2 * C), 0)
    s2 = lax.broadcasted_iota(jnp.int32, (2 * C, 2 * C), 1)
    same_head = (t2 // C) == (s2 // C)
    lo = _lo_lanes(C)
    units = [(bi, slice(p * HP, (p + 1) * HP), p) for bi in range(bb) for p in range(n_pairs)]
    n = len(units)
    cat = lambda xs: jnp.concatenate(xs, axis=0)
    stack = lambda x: _stack_pair(x, lo)
    bdiag = lambda x: jnp.where(same_head, cat([x, x]), jnp.zeros((), x.dtype))

    ar, bk, bkh, v, e_tot = [], [], [], [], []
    for bi, sl, _ in units:
        lw = lw_ref[bi, :, sl]
        kk = kk_ref[bi, :, sl].astype(jnp.float32)
        ag = a_ref[bi, :, sl].astype(jnp.float32)
        kd = k_ref[bi, :, sl].astype(jnp.float32) * (1.0 + (ag - 1.0) * ka_ref[:, sl])
        cum = jnp.dot(tri, lw, precision=HIGHEST, preferred_element_type=jnp.float32)
        tot = jnp.sum(lw, axis=0, keepdims=True)
        e_neg = jnp.exp(-cum)
        e_end = jnp.exp(tot - cum)
        b = kk * ag
        ar.append(cat([-kk * jnp.exp(cum - lw), r_ref[bi, :, sl].astype(jnp.float32) * jnp.exp(cum)]).astype(BF))
        bk.append(cat([stack(b * e_neg), stack(kd * e_neg)]).astype(BF))
        bkh.append(cat([b * e_end, kd * e_end]).astype(BF))
        v.append(v_ref[bi, :, sl].astype(BF))
        e_tot.append(jnp.exp(tot))
    gram = [_dot_nt(ar[i], bk[i]) for i in range(n)]
    l_ab = [jnp.where(m_strict, g[:C, :2 * C], 0.0) for g in gram]
    l_ak = [jnp.where(m_strict, g[:C, 2 * C:], 0.0).astype(BF) for g in gram]
    l_rbk = [jnp.concatenate([jnp.where(m_incl, g[C:, :2 * C], 0.0), jnp.where(m_incl, g[C:, 2 * C:], 0.0)],
                             axis=1).astype(BF) for g in gram]
    vs = [stack(x) for x in v]
    s0 = [s_scr[bi, p] for bi, _, p in units]
    proj = [_dot_nt(ar[i], s0[i]) for i in range(n)]
    lv = [_bdot(l_ak[i], vs[i]) for i in range(n)]
    tinv = [eye + m for m in l_ab]
    pw_bd = [bdiag(m.astype(BF)) for m in l_ab]
    pw = [_bdot(l_ab[i], pw_bd[i]).astype(BF) for i in range(n)]
    levels = 5
    for lvl in range(1, levels + 1):
        pw_bd = [bdiag(m) for m in pw]
        if lvl < levels:
            both = [_bdot(cat([pw[i], tinv[i].astype(BF)]), pw_bd[i]) for i in range(n)]
            pw = [x[:C].astype(BF) for x in both]
            tinv = [tinv[i] + both[i][C:] for i in range(n)]
        else:
            tinv = [tinv[i] + _bdot(tinv[i], pw_bd[i]) for i in range(n)]
    u = [_bdot(tinv[i], stack((proj[i][:C] + lv[i]).astype(BF))) for i in range(n)]
    ub = [x.astype(BF) for x in u]
    ys = [proj[i][C:] + _bdot(l_rbk[i], cat([stack(ub[i]), vs[i]])) for i in range(n)]
    upd = [_dot_tn(cat([ub[i], v[i]]), bkh[i]) for i in range(n)]
    for i, (bi, sl, p) in enumerate(units):
        y = ys[i]
        if has_prev:
            y = y + yprev_ref[bi, :, sl]
        y_ref[bi, :, sl] = y
        s_scr[bi, p] = s0[i] * e_tot[i] + jnp.where(same_head, upd[i], 0.0)

    @pl.when(c_idx == pl.num_programs(1) - 1)
    def _():
        sout_ref[...] = s_scr[...]


def wkv_chunked(r, lw, kk, ag, k, v, k_a, s0, y_prev, reverse):
    B, T, W = r.shape
    C = RWKV_CHUNK
    bb = WKV_SEQS_PER_STEP
    nc = T // C
    n_pairs = W // HP
    cmap = (lambda b, c: (b, nc - 1 - c, 0)) if reverse else (lambda b, c: (b, c, 0))
    tok = pl.BlockSpec((bb, C, W), cmap)
    st = pl.BlockSpec((bb, n_pairs, HP, HP), lambda b, c: (b, 0, 0, 0))
    has_prev = y_prev is not None
    args = [r, lw, kk, ag, k, v, k_a, s0] + ([y_prev] if has_prev else [])
    return pl.pallas_call(
        functools.partial(_wkv_kernel, reverse=reverse, bb=bb, n_pairs=n_pairs, has_prev=has_prev),
        out_shape=(jax.ShapeDtypeStruct((B, T, W), jnp.float32), jax.ShapeDtypeStruct(s0.shape, jnp.float32)),
        grid=(B // bb, nc),
        in_specs=[tok] * 6 + [pl.BlockSpec((1, W), lambda b, c: (0, 0)), st] + ([tok] if has_prev else []),
        out_specs=(tok, st),
        scratch_shapes=[pltpu.VMEM((bb, n_pairs, HP, HP), jnp.float32)],
        compiler_params=_cparams("parallel", "arbitrary"),
        name="wkv_chunked",
    )(*args)


def _rwkv_readout_tile(y, r, k, v, g, ag0, ag1, ka, rk, lnx_g, lnx_b, ones_bd):
    mean = _split_dot(y, ones_bd, 2) * (1.0 / RWKV_HEAD)
    yc = y - mean
    var = _split_dot(yc * yc, ones_bd, 2) * (1.0 / RWKV_HEAD)
    yn = yc * lax.rsqrt(var + RWKV_LNX_EPS) * lnx_g + lnx_b
    ksum = k * (2.0 + (ag0 + ag1 - 2.0) * ka)
    bonus = _split_dot(r * ksum * rk, ones_bd, 2) * v
    return (yn + bonus) * g


def rwkv_mix(proj_x, proj_c, vf_x, vf_c, mu, w0, w2, a0, a2, k_k, k_a, g2, vres, tm):
    B = proj_x.shape[0]
    fx = rwkv_features(proj_x, mu, w0, w2, a0, a2, k_k, g2, vres, vf_x, tm)
    fc = rwkv_features(proj_c, mu, w0, w2, a0, a2, k_k, g2, vres, vf_c, tm)
    ka = k_a.reshape(1, -1)
    y_x = y_c = None
    for d in range(N_DIR):
        s0 = jnp.zeros((B, BRANCH_WIDTH // HP, HP, HP), jnp.float32)
        y_c, s_ctx = wkv_chunked(fc[0], fc[5 + d], fc[3], fc[7 + d], fc[1], fc[2], ka, s0, y_c, d == 1)
        y_x, _ = wkv_chunked(fx[0], fx[5 + d], fx[3], fx[7 + d], fx[1], fx[2], ka, s_ctx, y_x, d == 1)
    pick = lambda y, f: (y, f[0], f[1], f[2], f[4], f[7], f[8])
    vf_x = fx[2] if vres is None else vf_x
    vf_c = fc[2] if vres is None else vf_c
    return pick(y_x, fx), pick(y_c, fc), vf_x, vf_c


def _qknorm_kernel(z_ref, qg_ref, kg_ref, ones_ref, q_ref, k_ref, v_ref):
    ones_bd = ones_ref[...]
    C = BRANCH_WIDTH
    z = z_ref[0].astype(jnp.float32)
    q = z[:, 0:C]
    k = z[:, C:2 * C]
    qn = q * lax.rsqrt(_split_dot(q * q, ones_bd, 2) * (1.0 / NA_HEAD) + RMS_EPS) * qg_ref[...]
    kn = k * lax.rsqrt(_split_dot(k * k, ones_bd, 2) * (1.0 / NA_HEAD) + RMS_EPS) * kg_ref[...]
    q_ref[0] = (qn * NA_HEAD ** -0.5).astype(BF)
    k_ref[0] = kn.astype(BF)
    v_ref[0] = z[:, 2 * C:3 * C].astype(BF)


def na_qknorm(proj, qn_g, kn_g, tm):
    B, T, _ = proj.shape
    C = BRANCH_WIDTH
    tok = pl.BlockSpec((1, tm, C), lambda b, i: (b, i, 0))
    par = pl.BlockSpec((1, C), lambda b, i: (0, 0))
    sd = jax.ShapeDtypeStruct((B, T, C), BF)
    return pl.pallas_call(
        _qknorm_kernel, out_shape=(sd, sd, sd), grid=(B, T // tm),
        in_specs=[pl.BlockSpec((1, tm, NA_COLS), lambda b, i: (b, i, NA_BLK)), par, par,
                  pl.BlockSpec((C, C), lambda b, i: (0, 0))],
        out_specs=(tok, tok, tok),
        compiler_params=_cparams("parallel", "parallel"),
        name="na_qknorm",
    )(proj, jnp.tile(qn_g, NA_HEADS).reshape(1, C), jnp.tile(kn_g, NA_HEADS).reshape(1, C), _head_ones(C, NA_HEAD))


def na_bias_table(rpb):
    qc = np.arange(GRID_W)[:, None]
    kc = np.arange(GRID_W)[None, :]
    cs = np.clip(qc - NA_WIN_C // 2, 0, GRID_W - NA_WIN_C)
    valid = (kc >= cs) & (kc < cs + NA_WIN_C)
    cidx = np.clip(kc - qc + NA_WIN_C - 1, 0, 2 * NA_WIN_C - 2)
    t = jnp.where(valid[None, None], rpb[:, :, cidx], NEG_INF)
    t2 = jnp.concatenate([t[:, :-1], t[:, 1:]], axis=-1)
    H = rpb.shape[0]
    t2 = t2.reshape(H // 2, 2, 2 * NA_WIN_R - 2, GRID_W, 2 * GRID_W).transpose(0, 2, 1, 3, 4)
    return t2.reshape(H // 2, 2 * NA_WIN_R - 2, 2 * GRID_W, 2 * GRID_W)


def _na_kernel(q_ref, k_ref, v_ref, kc_ref, vc_ref, bias_ref, o_ref, *, rows):
    nwin = NA_WIN_R * GRID_W
    lo = _lo_lanes(GRID_W)
    units = []
    for rr in range(NA_ROWS_PER_STEP):
        r = pl.program_id(1) * NA_ROWS_PER_STEP + rr
        rs = jnp.clip(r - NA_WIN_R // 2, 0, rows - NA_WIN_R)
        k0 = pl.multiple_of(rs * GRID_W, GRID_W)
        for p in range(BRANCH_WIDTH // HP):
            units.append((rr, p, slice(p * HP, (p + 1) * HP), rs - r + NA_WIN_R - 1, k0))
    qs = [_stack_pair(q_ref[0, rr * GRID_W:(rr + 1) * GRID_W, sl], lo) for rr, _, sl, _, _ in units]
    s_loc = [_dot_nt(qs[i], k_ref[0, pl.ds(u[4], nwin), u[2]]) for i, u in enumerate(units)]
    s_ctx = [_dot_nt(qs[i], kc_ref[0, :, u[2]]) for i, u in enumerate(units)]
    p_loc, p_ctx, den = [], [], []
    for i, (_, p, _, base, _) in enumerate(units):
        sl_b = s_loc[i] + jnp.concatenate([bias_ref[p, base + 2 * j] for j in range(NA_WIN_R // 2)], axis=1)
        m = jnp.maximum(jnp.max(sl_b, axis=1, keepdims=True), jnp.max(s_ctx[i], axis=1, keepdims=True))
        el = jnp.exp(sl_b - m)
        ec = jnp.exp(s_ctx[i] - m)
        den.append(jnp.sum(el, axis=1, keepdims=True) + jnp.sum(ec, axis=1, keepdims=True))
        p_loc.append(el.astype(BF))
        p_ctx.append(ec.astype(BF))
    o_loc = [_bdot(p_loc[i], v_ref[0, pl.ds(u[4], nwin), u[2]]) for i, u in enumerate(units)]
    o_ctx = [_bdot(p_ctx[i], vc_ref[0, :, u[2]]) for i, u in enumerate(units)]
    for i, (rr, _, sl, _, _) in enumerate(units):
        o = (o_loc[i] + o_ctx[i]) / den[i]
        o_ref[0, rr * GRID_W:(rr + 1) * GRID_W, sl] = jnp.where(lo, o[:GRID_W], o[GRID_W:])


def na_attention(q, k, v, kc, vc, bias_tab):
    B, S, C = q.shape
    rows = S // GRID_W
    n_ctx = kc.shape[1]
    seq = pl.BlockSpec((1, S, C), lambda b, r: (b, 0, 0))
    cx = pl.BlockSpec((1, n_ctx, C), lambda b, r: (b, 0, 0))
    row = pl.BlockSpec((1, NA_ROWS_PER_STEP * GRID_W, C), lambda b, r: (b, r, 0))
    return pl.pallas_call(
        functools.partial(_na_kernel, rows=rows),
        out_shape=jax.ShapeDtypeStruct((B, S, C), jnp.float32),
        grid=(B, rows // NA_ROWS_PER_STEP),
        in_specs=[row, seq, seq, cx, cx, pl.BlockSpec(bias_tab.shape, lambda b, r: (0, 0, 0, 0))],
        out_specs=row,
        compiler_params=_cparams("parallel", "arbitrary"),
        name="na_attention",
    )(q, k, v, kc, vc, bias_tab)


def _ctx_attn_kernel(q_ref, k_ref, v_ref, o_ref):
    n = q_ref.shape[1]
    lo = _lo_lanes(n)
    sls = [slice(p * HP, (p + 1) * HP) for p in range(BRANCH_WIDTH // HP)]
    sc = [_dot_nt(_stack_pair(q_ref[0, :, sl], lo), k_ref[0, :, sl]) for sl in sls]
    e = [jnp.exp(x - jnp.max(x, axis=1, keepdims=True)) for x in sc]
    o = [_bdot(e[p], v_ref[0, :, sl]) / jnp.sum(e[p], axis=1, keepdims=True) for p, sl in enumerate(sls)]
    for p, sl in enumerate(sls):
        o_ref[0, :, sl] = jnp.where(lo, o[p][:n], o[p][n:])


def ctx_attention(q, k, v):
    B, n, C = q.shape
    blk = pl.BlockSpec((1, n, C), lambda b: (b, 0, 0))
    return pl.pallas_call(
        _ctx_attn_kernel, out_shape=jax.ShapeDtypeStruct((B, n, C), jnp.float32), grid=(B,),
        in_specs=[blk, blk, blk], out_specs=blk,
        compiler_params=_cparams("parallel"),
        name="ctx_attention",
    )(q, k, v)


def na_mix(proj_x, proj_c, qn_g, kn_g, rpb, need_ctx, tm):
    q, k, v = na_qknorm(proj_x, qn_g, kn_g, tm)
    qc, kc, vc = na_qknorm(proj_c, qn_g, kn_g, tm)
    out_x = na_attention(q, k, v, kc, vc, na_bias_table(rpb))
    out_c = ctx_attention(qc, kc, vc) if need_ctx else None
    return out_x, out_c


def rope_tables(n_tokens):
    t = jnp.arange(n_tokens)
    pos = jnp.stack([t // GRID_W, t % GRID_W], axis=-1).astype(jnp.float32)
    inv = ROPE_BASE ** (-jnp.arange(ROPE_ROT, dtype=jnp.float32) / ROPE_ROT)
    ang = pos[:, :, None] * inv
    cos_h = jnp.concatenate([jnp.cos(ang), jnp.cos(ang)], axis=-1).reshape(n_tokens, MLSTM_DQK)
    sin_h = jnp.concatenate([-jnp.sin(ang), jnp.sin(ang)], axis=-1).reshape(n_tokens, MLSTM_DQK)
    col = np.arange(MLSTM_QK)
    partner = np.where((col % (2 * ROPE_ROT)) < ROPE_ROT, col + ROPE_ROT, col - ROPE_ROT)
    perm = np.zeros((MLSTM_QK, MLSTM_QK), np.float32)
    perm[partner, col] = 1.0
    return jnp.tile(cos_h, (1, MLSTM_HEADS)), jnp.tile(sin_h, (1, MLSTM_HEADS)), jnp.asarray(perm, BF)


def _mlstm_prep_kernel(z_ref, gp_ref, ib_ref, fb_ref, *rest, rope):
    if rope:
        cos_ref, sin_ref, perm_ref, q_ref, k_ref, v_ref, g_ref = rest
    else:
        q_ref, k_ref, v_ref, g_ref = rest
    z = z_ref[0].astype(jnp.float32)
    q = z[:, 0:MLSTM_QK]
    k = z[:, MLSTM_QK:2 * MLSTM_QK]
    if rope:
        perm = perm_ref[...]
        q = q * cos_ref[...] + _split_dot(q, perm, 3) * sin_ref[...]
        k = k * cos_ref[...] + _split_dot(k, perm, 3) * sin_ref[...]
    q_ref[0] = (q * MLSTM_DQK ** -0.5).astype(BF)
    k_ref[0] = k.astype(BF)
    v_ref[0] = z[:, 2 * MLSTM_QK:2 * MLSTM_QK + BRANCH_WIDTH].astype(BF)
    gp = gp_ref[0].astype(jnp.float32)
    lane = lax.broadcasted_iota(jnp.int32, gp.shape, 1)
    ig = GATE_CAP * jnp.tanh((gp + ib_ref[...]) / GATE_CAP)
    fg = GATE_CAP * jnp.tanh((gp + fb_ref[...]) / GATE_CAP)
    g_ref[0] = jnp.where(lane < N_DIR * MLSTM_HEADS, ig, jax.nn.log_sigmoid(fg))


def mlstm_prep(proj, i_bias, f_bias, rope, tm):
    B, T, _ = proj.shape
    ng = N_DIR * MLSTM_HEADS
    ib = jnp.zeros((1, V7X_LANES), jnp.float32).at[0, 0:ng].set(i_bias.reshape(-1))
    fb = jnp.zeros((1, V7X_LANES), jnp.float32).at[0, ng:2 * ng].set(f_bias.reshape(-1))
    tok = lambda w: pl.BlockSpec((1, tm, w), lambda b, i: (b, i, 0))
    par = pl.BlockSpec((1, V7X_LANES), lambda b, i: (0, 0))
    args = [proj, proj, ib, fb]
    specs = [pl.BlockSpec((1, tm, MLSTM_MAIN_COLS), lambda b, i: (b, i, ML_BLK)),
             pl.BlockSpec((1, tm, V7X_LANES), lambda b, i: (b, i, ML_GATE_BLK)), par, par]
    if rope is not None:
        args += list(rope)
        specs += [pl.BlockSpec((tm, MLSTM_QK), lambda b, i: (i, 0)), pl.BlockSpec((tm, MLSTM_QK), lambda b, i: (i, 0)),
                  pl.BlockSpec((MLSTM_QK, MLSTM_QK), lambda b, i: (0, 0))]
    return pl.pallas_call(
        functools.partial(_mlstm_prep_kernel, rope=rope is not None),
        out_shape=(jax.ShapeDtypeStruct((B, T, MLSTM_QK), BF), jax.ShapeDtypeStruct((B, T, MLSTM_QK), BF),
                   jax.ShapeDtypeStruct((B, T, BRANCH_WIDTH), BF), jax.ShapeDtypeStruct((B, T, V7X_LANES), jnp.float32)),
        grid=(B, T // tm), in_specs=specs,
        out_specs=(tok(MLSTM_QK), tok(MLSTM_QK), tok(BRANCH_WIDTH), tok(V7X_LANES)),
        compiler_params=_cparams("parallel", "parallel"),
        name="mlstm_prep",
    )(*args)


def _mlstm_kernel(q_ref, k_ref, v_ref, gr_ref, c0_ref, n0_ref, m0_ref, *rest, reverse, bb, has_prev):
    if has_prev:
        hprev_ref, h_ref, cout_ref, nout_ref, mout_ref, c_scr, n_scr, m_scr = rest
    else:
        h_ref, cout_ref, nout_ref, mout_ref, c_scr, n_scr, m_scr = rest
    L = MLSTM_CHUNK
    DV = MLSTM_DV
    H = MLSTM_HEADS
    SL = MLSTM_GATE_SLOTS
    cidx = pl.program_id(1)

    @pl.when(cidx == 0)
    def _():
        c_scr[...] = c0_ref[...]
        n_scr[...] = n0_ref[...]
        m_scr[...] = m0_ref[...]

    ti = lax.broadcasted_iota(jnp.int32, (L, L), 0)
    si = lax.broadcasted_iota(jnp.int32, (L, L), 1)
    before = (si >= ti) if reverse else (si <= ti)
    tri = before.astype(jnp.float32)
    last = 0 if reverse else L - 1
    lo = _lo_lanes(L)
    lo_row = lax.broadcasted_iota(jnp.int32, (1, HP), 1) < (HP // 2)
    lo_col = lax.broadcasted_iota(jnp.int32, (HP, 1), 0) < (HP // 2)
    units = [(bi, p) for bi in range(bb) for p in range(H // 2)]
    blocks = [(bi, h) for bi in range(bb) for h in range(H)]
    nb = len(blocks)
    cat = lambda xs: jnp.concatenate(xs, axis=0)
    row_of = lambda bi, h: bi * SL + h

    i_rows = cat([gr_ref[bi, 0, 0:SL, :] for bi in range(bb)])
    lf_rows = cat([gr_ref[bi, 0, SL:2 * SL, :] for bi in range(bb)])
    m_prev = cat([m_scr[bi] for bi in range(bb)])
    b_rows = lax.dot_general(lf_rows, tri, (((1,), (1,)), ((), ())), precision=HIGHEST, preferred_element_type=jnp.float32)
    g_rows = i_rows - b_rows
    run = g_rows
    neg = jnp.full_like(g_rows, -jnp.inf)
    step = 1
    while step < L:
        shifted = (jnp.concatenate([run[:, step:], neg[:, :step]], axis=1) if reverse
                   else jnp.concatenate([neg[:, :step], run[:, :L - step]], axis=1))
        run = jnp.maximum(run, shifted)
        step *= 2
    m_rows = jnp.maximum(m_prev, run)
    m_end = jnp.broadcast_to(m_rows[:, last:last + 1], m_rows.shape)
    b_end = jnp.broadcast_to(b_rows[:, last:last + 1], b_rows.shape)
    sc_rows = jnp.exp(m_prev - m_rows)
    floor_rows = jnp.exp(-(b_rows + m_rows))
    wexp_rows = jnp.exp(g_rows - m_end)
    dec_rows = jnp.exp(m_prev - m_end)
    cols = cat([m_rows, sc_rows, floor_rows, wexp_rows]).T
    nr = bb * SL
    col_of = lambda kind, bi, h: cols[:, kind * nr + row_of(bi, h):kind * nr + row_of(bi, h) + 1]
    m_col = cat([col_of(0, bi, h) for bi, h in blocks])
    sc_col = cat([col_of(1, bi, h) for bi, h in blocks])
    floor_col = cat([col_of(2, bi, h) for bi, h in blocks])
    wexp_col = cat([col_of(3, bi, h) for bi, h in blocks])
    g_bcast = cat([jnp.broadcast_to(g_rows[row_of(bi, h):row_of(bi, h) + 1, :], (L, L)) for bi, h in blocks])
    decay = jnp.where(cat([before] * nb), jnp.exp(g_bcast - m_col), 0.0)

    qs = [_stack_pair(q_ref[bi, :, p * HP:(p + 1) * HP], lo) for bi, p in units]
    kp = [k_ref[bi, :, p * HP:(p + 1) * HP] for bi, p in units]
    vb = [v_ref[bi, :, h * DV:(h + 1) * DV] for bi, h in blocks]
    c_pair = [c_scr[bi, p] for bi, p in units]
    n_pair = [n_scr[bi, p] for bi, p in units]
    qk = cat([_dot_nt(qs[u], kp[u]) for u in range(len(units))])
    qc = cat([_bdot(qs[u], c_pair[u]) for u in range(len(units))])
    qn = cat([jnp.sum(qs[u].astype(jnp.float32) * n_pair[u], axis=1, keepdims=True) for u in range(len(units))])
    smat = qk * decay
    den = sc_col * qn + jnp.sum(smat, axis=1, keepdims=True)
    sb = smat.astype(BF)
    sv = cat([_bdot(sb[j * L:(j + 1) * L], vb[j]) for j in range(nb)])
    hout = (sc_col * qc + sv) / jnp.maximum(jnp.abs(den), floor_col)
    for j, (bi, h) in enumerate(blocks):
        hs = slice(h * DV, (h + 1) * DV)
        part = hout[j * L:(j + 1) * L]
        h_ref[bi, :, hs] = (part + hprev_ref[bi, :, hs]) if has_prev else part

    dec = lambda bi, h: dec_rows[row_of(bi, h):row_of(bi, h) + 1, 0:1]
    for u, (bi, p) in enumerate(units):
        j0 = bi * H + 2 * p
        kws = _stack_pair(kp[u], lo).astype(jnp.float32) * wexp_col[2 * u * L:2 * (u + 1) * L]
        upd = _dot_tn(kws, cat([vb[j0], vb[j0 + 1]]))
        c_scr[bi, p] = jnp.where(lo_col, dec(bi, 2 * p), dec(bi, 2 * p + 1)) * c_pair[u] + upd
        n_scr[bi, p] = jnp.where(lo_row, dec(bi, 2 * p), dec(bi, 2 * p + 1)) * n_pair[u] + jnp.sum(kws, axis=0, keepdims=True)
    m_next = b_end + m_end
    for bi in range(bb):
        m_scr[bi] = m_next[bi * SL:(bi + 1) * SL]

    @pl.when(cidx == pl.num_programs(1) - 1)
    def _():
        cout_ref[...] = c_scr[...]
        nout_ref[...] = n_scr[...]
        mout_ref[...] = m_scr[...]


def mlstm_chunked(q, k, v, gates, state, h_prev, direction):
    B, T, _ = q.shape
    L = MLSTM_CHUNK
    H = MLSTM_HEADS
    SL = MLSTM_GATE_SLOTS
    bb = MLSTM_SEQS_PER_STEP
    nc = T // L
    reverse = direction == 1
    ig = gates[:, :, direction * H:(direction + 1) * H]
    fg = gates[:, :, N_DIR * H + direction * H:N_DIR * H + (direction + 1) * H]
    zpad = jnp.zeros((B, T, SL - H), jnp.float32)
    g_rows = jnp.concatenate([ig, zpad, fg, zpad], axis=-1).reshape(B, nc, L, 2 * SL).transpose(0, 1, 3, 2)
    cm = (lambda c: nc - 1 - c) if reverse else (lambda c: c)
    tok = lambda w: pl.BlockSpec((bb, L, w), lambda b, c: (b, cm(c), 0))
    st = lambda a: pl.BlockSpec((bb,) + a.shape[1:], lambda b, c: (b,) + (0,) * (a.ndim - 1))
    c0, n0, m0 = state
    has_prev = h_prev is not None
    args = [q, k, v, g_rows, c0, n0, m0] + ([h_prev] if has_prev else [])
    outs = pl.pallas_call(
        functools.partial(_mlstm_kernel, reverse=reverse, bb=bb, has_prev=has_prev),
        out_shape=(jax.ShapeDtypeStruct((B, T, BRANCH_WIDTH), jnp.float32),) + tuple(
            jax.ShapeDtypeStruct(a.shape, jnp.float32) for a in state),
        grid=(B // bb, nc),
        in_specs=[tok(MLSTM_QK), tok(MLSTM_QK), tok(BRANCH_WIDTH),
                  pl.BlockSpec((bb, 1, 2 * SL, L), lambda b, c: (b, cm(c), 0, 0)), st(c0), st(n0), st(m0)]
                 + ([tok(BRANCH_WIDTH)] if has_prev else []),
        out_specs=(tok(BRANCH_WIDTH), st(c0), st(n0), st(m0)),
        scratch_shapes=[pltpu.VMEM((bb,) + a.shape[1:], jnp.float32) for a in state],
        compiler_params=_cparams("parallel", "arbitrary"),
        name="mlstm_chunked",
    )(*args)
    return outs[0], outs[1:]


def _mlstm_readout_tile(h, o, norm_g):
    parts = []
    for hd in range(MLSTM_HEADS):
        x = h[:, hd * MLSTM_DV:(hd + 1) * MLSTM_DV]
        parts.append(x * lax.rsqrt(jnp.mean(x * x, axis=1, keepdims=True) + RMS_EPS))
    return jnp.concatenate(parts, axis=1) * norm_g * jax.nn.sigmoid(o)


def mlstm_mix(proj_x, proj_c, i_bias, f_bias, rope, tm):
    B = proj_x.shape[0]
    qx, kx, vx, gx = mlstm_prep(proj_x, i_bias, f_bias, rope, tm)
    qc, kc, vc, gc = mlstm_prep(proj_c, i_bias, f_bias, None, tm)
    h_x = h_c = None
    for d in range(N_DIR):
        st0 = (jnp.zeros((B, MLSTM_HEADS // 2, HP, HP), jnp.float32), jnp.zeros((B, MLSTM_HEADS // 2, 1, HP), jnp.float32),
               jnp.zeros((B, MLSTM_GATE_SLOTS, MLSTM_CHUNK), jnp.float32))
        h_c, st_ctx = mlstm_chunked(qc, kc, vc, gc, st0, h_c, d)
        h_x, _ = mlstm_chunked(qx, kx, vx, gx, st_ctx, h_x, d)
    return h_x, h_c


def _merge_kernel(y_ref, r_ref, k_ref, v_ref, g_ref, ag0_ref, ag1_ref, yb_ref, h_ref, og_ref, ga_ref, gb_ref, gc_ref,
                  x_ref, mod_ref, ka_ref, rk_ref, lg_ref, lb_ref, ones_ref, ng_ref, wb_ref, wo_ref, o_ref):
    f32 = lambda ref: ref[0].astype(jnp.float32)
    ya = _rwkv_readout_tile(y_ref[0], f32(r_ref), f32(k_ref), f32(v_ref), f32(g_ref), f32(ag0_ref), f32(ag1_ref),
                            ka_ref[...], rk_ref[...], lg_ref[...], lb_ref[...], ones_ref[...])
    yc = _mlstm_readout_tile(h_ref[0], f32(og_ref), ng_ref[...])
    merged = None
    for i, (y, gate_ref) in enumerate(((ya, ga_ref), (yb_ref[0], gb_ref), (yc, gc_ref))):
        t = jax.nn.sigmoid(f32(gate_ref)) * _bdot(y, wb_ref[i])
        merged = t if merged is None else merged + t
    o_ref[0] = x_ref[0] + mod_ref[0, 2:3, :] * _bdot(merged, wo_ref[...])


def merge_apply(rw, yb, h_ml, proj, x, mod, rw_params, ml_norm_g, w_branch, w_out, tm):
    B, T, D = x.shape
    C = BRANCH_WIDTH
    tok = lambda w: pl.BlockSpec((1, tm, w), lambda b, i: (b, i, 0))
    gate = lambda k: pl.BlockSpec((1, tm, D), lambda b, i: (b, i, GATE_BLK0 + k))
    par = pl.BlockSpec((1, C), lambda b, i: (0, 0))
    params = [p.reshape(1, C) for p in rw_params]
    return pl.pallas_call(
        _merge_kernel, out_shape=jax.ShapeDtypeStruct((B, T, D), jnp.float32), grid=(B, T // tm),
        in_specs=[tok(C)] * 9 + [pl.BlockSpec((1, tm, C), lambda b, i: (b, i, ML_OGATE_BLK)), gate(0), gate(1), gate(2), tok(D),
                  pl.BlockSpec((1, 6, D), lambda b, i: (b, 0, 0))] + [par] * 4
                 + [pl.BlockSpec((C, C), lambda b, i: (0, 0)), par,
                    pl.BlockSpec(w_branch.shape, lambda b, i: (0, 0, 0)), pl.BlockSpec(w_out.shape, lambda b, i: (0, 0))],
        out_specs=tok(D),
        compiler_params=_cparams("parallel", "parallel"),
        name="merge_branches",
    )(*rw, yb, h_ml, proj, proj, proj, proj, x, mod, *params, _head_ones(C, RWKV_HEAD), ml_norm_g.reshape(1, C),
      w_branch, w_out)


def _route_kernel(x_ref, g_ref, mod_ref, wr_ref, rb_ref, h_ref, gate_ref):
    h = _norm_mod(x_ref[0], g_ref[...], mod_ref[0, 4:5, :], mod_ref[0, 3:4, :])
    h_ref[0] = h.astype(BF)
    tm = h.shape[0]
    logits = lax.dot_general(wr_ref[...], h, (((1,), (1,)), ((), ())), precision=HIGHEST, preferred_element_type=jnp.float32)
    scores = jax.nn.sigmoid(logits)
    sel = scores + rb_ref[...]
    gsz = N_EXPERTS // N_GROUPS
    grp = sel.reshape(N_GROUPS, gsz, tm)
    iota_in = lax.broadcasted_iota(jnp.int32, grp.shape, 1)
    m1 = jnp.max(grp, axis=1, keepdims=True)
    first = jnp.min(jnp.where(grp == m1, iota_in, gsz), axis=1, keepdims=True)
    m2 = jnp.max(jnp.where(iota_in == first, -jnp.inf, grp), axis=1, keepdims=True)
    gscore = (m1 + m2).reshape(N_GROUPS, tm)
    gi = lax.broadcasted_iota(jnp.int32, (N_GROUPS, tm), 0)
    rank = jnp.zeros((N_GROUPS, tm), jnp.int32)
    for g2 in range(N_GROUPS):
        other = gscore[g2:g2 + 1, :]
        rank = rank + ((other > gscore) | ((other == gscore) & (g2 < gi))).astype(jnp.int32)
    gmask = rank < TOPK_GROUPS
    emask = jnp.broadcast_to(gmask.reshape(N_GROUPS, 1, tm), (N_GROUPS, gsz, tm)).reshape(N_EXPERTS, tm)
    cand = jnp.where(emask, sel, NEG_INF)
    ei = lax.broadcasted_iota(jnp.int32, (N_EXPERTS, tm), 0)
    chosen = jnp.zeros((N_EXPERTS, tm), jnp.bool_)
    for _ in range(TOP_K):
        mx = jnp.max(cand, axis=0, keepdims=True)
        idx = jnp.min(jnp.where(cand == mx, ei, N_EXPERTS), axis=0, keepdims=True)
        hit = ei == idx
        chosen = chosen | hit
        cand = jnp.where(hit, -jnp.inf, cand)
    w = jnp.where(chosen, scores, 0.0)
    w = w / jnp.sum(w, axis=0, keepdims=True) * ROUTED_SCALE
    gate_ref[0] = jnp.concatenate([w, jnp.zeros((V7X_LANES - N_EXPERTS, tm), jnp.float32)], axis=0).T


def moe_route(x, g, mod, w_router, router_bias, tm):
    B, T, D = x.shape
    tok = pl.BlockSpec((1, tm, D), lambda b, i: (b, i, 0))
    return pl.pallas_call(
        _route_kernel,
        out_shape=(jax.ShapeDtypeStruct((B, T, D), BF), jax.ShapeDtypeStruct((B, T, V7X_LANES), jnp.float32)),
        grid=(B, T // tm),
        in_specs=[tok, pl.BlockSpec((1, D), lambda b, i: (0, 0)), pl.BlockSpec((1, 6, D), lambda b, i: (b, 0, 0)),
                  pl.BlockSpec((N_EXPERTS, D), lambda b, i: (0, 0)), pl.BlockSpec((N_EXPERTS, 1), lambda b, i: (0, 0))],
        out_specs=(tok, pl.BlockSpec((1, tm, V7X_LANES), lambda b, i: (b, i, 0))),
        compiler_params=_cparams("parallel", "parallel"),
        name="moe_route",
    )(x, g.reshape(1, D), mod, w_router.T, router_bias.reshape(N_EXPERTS, 1))


def _moe_kernel(h_ref, gate_ref, x_ref, mod_ref, sel_ref, wg_ref, wu_ref, wd_ref, sg_ref, su_ref, sd_ref, o_ref, *, tm):
    j = pl.program_id(1)
    rows = pl.ds(pl.multiple_of(pl.program_id(2) * tm, tm), tm)
    h = h_ref[0]

    @pl.when(j == 0)
    def _():
        sh = jax.nn.silu(_bdot(h, sg_ref[...])) * _bdot(h, su_ref[...])
        o_ref[0, rows, :] = _bdot(sh, sd_ref[...])

    g8 = _split_dot(gate_ref[0], sel_ref[0], 2)
    act = jax.nn.silu(_bdot(h, wg_ref[...])) * _bdot(h, wu_ref[...])
    act = jnp.concatenate([act[:, e * D_EXPERT:(e + 1) * D_EXPERT] * g8[:, e:e + 1] for e in range(MOE_STEP_EXPERTS)],
                          axis=1)
    o_ref[0, rows, :] += _bdot(act, wd_ref[...])

    @pl.when(j == pl.num_programs(1) - 1)
    def _():
        o_ref[0, rows, :] = x_ref[0] + mod_ref[0, 5:6, :] * o_ref[0, rows, :]


def _moe_select_table():
    se = MOE_STEP_EXPERTS
    t = np.zeros((N_EXPERTS // se, V7X_LANES, V7X_LANES), np.float32)
    for j in range(N_EXPERTS // se):
        for e in range(se):
            t[j, j * se + e, e] = 1.0
    return jnp.asarray(t, BF)


def moe_apply(h2, gates, x, mod, wg, wu, wd, sg, su, sd, tm):
    B, T, D = x.shape
    sw = MOE_STEP_EXPERTS * D_EXPERT
    n_groups = N_EXPERTS // MOE_STEP_EXPERTS
    tok = pl.BlockSpec((1, tm, D), lambda b, j, i: (b, i, 0))
    x_last = pl.BlockSpec((1, tm, D), lambda b, j, i: (b, jnp.where(j == n_groups - 1, i, 0), 0))
    full = lambda a: pl.BlockSpec(a.shape, lambda b, j, i: (0,) * a.ndim)
    return pl.pallas_call(
        functools.partial(_moe_kernel, tm=tm), out_shape=jax.ShapeDtypeStruct((B, T, D), jnp.float32),
        grid=(B, n_groups, T // tm),
        in_specs=[tok, pl.BlockSpec((1, tm, V7X_LANES), lambda b, j, i: (b, i, 0)), x_last,
                  pl.BlockSpec((1, 6, D), lambda b, j, i: (b, 0, 0)),
                  pl.BlockSpec((1, V7X_LANES, V7X_LANES), lambda b, j, i: (j, 0, 0)),
                  pl.BlockSpec((D, sw), lambda b, j, i: (0, j)), pl.BlockSpec((D, sw), lambda b, j, i: (0, j)),
                  pl.BlockSpec((sw, D), lambda b, j, i: (j, 0)), full(sg), full(su), full(sd)],
        out_specs=pl.BlockSpec((1, T, D), lambda b, j, i: (b, 0, 0)),
        compiler_params=pltpu.CompilerParams(dimension_semantics=("parallel", "arbitrary", "arbitrary"),
                                             vmem_limit_bytes=MOE_VMEM_LIMIT_BYTES),
        name="moe_experts",
    )(h2, gates, x, mod, _moe_select_table(), wg, wu, wd, sg, su, sd)


def _reorder_w_in(w):
    o_na = RWKV_COLS
    o_ml = o_na + NA_COLS
    o_mg = o_ml + MLSTM_MAIN_COLS
    o_gate = o_mg + MLSTM_GATES
    pad = jnp.zeros((w.shape[0], V7X_LANES - MLSTM_GATES), w.dtype)
    return jnp.concatenate([w[:, o_na:o_ml], w[:, o_ml:o_mg], w[:, o_gate:], w[:, :RWKV_COLS], w[:, o_mg:o_gate], pad],
                           axis=1).astype(BF)


def kernel(x, c, ctx, c_ctx, w_ada, b_ada, norm1_g, norm2_g, w_in, rw_mu, rw_w0, rw_w2, rw_a0, rw_a2, rw_k_k, rw_k_a, rw_r_k, rw_g2, rw_lnx_g, rw_lnx_b, rw_v0, rw_v1, rw_v2, na_qn_g, na_kn_g, na_rpb, ml_i_bias, ml_f_bias, ml_norm_g, w_branch, w_out, moe_router, moe_bias, moe_w_gate, moe_w_up, moe_w_down, sh_w_gate, sh_w_up, sh_w_down):
    B, S, D = x.shape
    n_ctx = ctx.shape[1]
    tm = 256
    assert S % tm == 0 and n_ctx % tm == 0 and PROJ_COLS == IN_COLS + V7X_LANES - MLSTM_GATES
    rope = rope_tables(S)
    n_cond = B + 1
    cond_pad = (-n_cond) % V7X_SUBLANES
    s_cond = jnp.pad(jnp.concatenate([jax.nn.silu(c), jax.nn.silu(c_ctx)[None]], axis=0), ((0, cond_pad), (0, 0)))
    vf_x = vf_c = None
    for l in range(DEPTH):
        need_ctx = l < DEPTH - 1
        mod = pmm(s_cond, w_ada[l]) + b_ada[l]
        mod_x = mod[:B].reshape(B, 6, D)
        mod_c = jnp.broadcast_to(mod[B].reshape(1, 6, D), (B, 6, D))
        w_proj = _reorder_w_in(w_in[l])
        proj_x = norm_proj(x, norm1_g[l], mod_x, w_proj)
        proj_c = norm_proj(ctx.reshape(1, B * n_ctx, D), norm1_g[l], mod_c[:1], w_proj).reshape(B, n_ctx, PROJ_COLS)
        vres = None if l == 0 else (rw_v0[l - 1], rw_v1[l - 1], rw_v2[l - 1])
        rw_x, rw_c, vf_x, vf_c = rwkv_mix(proj_x, proj_c, vf_x, vf_c, rw_mu[l], rw_w0[l], rw_w2[l], rw_a0[l], rw_a2[l],
                                          rw_k_k[l], rw_k_a[l], rw_g2[l], vres, tm)
        rw_params = (rw_k_a[l], rw_r_k[l], rw_lnx_g[l], rw_lnx_b[l])
        yb_x, yb_c = na_mix(proj_x, proj_c, na_qn_g[l], na_kn_g[l], na_rpb[l], need_ctx, tm)
        hm_x, hm_c = mlstm_mix(proj_x, proj_c, ml_i_bias[l], ml_f_bias[l], rope, tm)
        wb = w_branch[l].astype(BF)
        wo = w_out[l].astype(BF)
        wg = moe_w_gate[l].transpose(1, 0, 2).reshape(D, N_EXPERTS * D_EXPERT).astype(BF)
        wu = moe_w_up[l].transpose(1, 0, 2).reshape(D, N_EXPERTS * D_EXPERT).astype(BF)
        wd = moe_w_down[l].reshape(N_EXPERTS * D_EXPERT, D).astype(BF)
        shared = (sh_w_gate[l].astype(BF), sh_w_up[l].astype(BF), sh_w_down[l].astype(BF))
        x = merge_apply(rw_x, yb_x, hm_x, proj_x, x, mod_x, rw_params, ml_norm_g[l], wb, wo, tm)
        h2, gates = moe_route(x, norm2_g[l], mod_x, moe_router[l], moe_bias[l], tm)
        x = moe_apply(h2, gates, x, mod_x, wg, wu, wd, *shared, MOE_TOKEN_TILE)
        if need_ctx:
            ctx = merge_apply(rw_c, yb_c, hm_c, proj_c, ctx, mod_c, rw_params, ml_norm_g[l], wb, wo, tm)
            h2, gates = moe_route(ctx, norm2_g[l], mod_c, moe_router[l], moe_bias[l], tm)
            ctx = moe_apply(h2, gates, ctx, mod_c, wg, wu, wd, *shared, tm)
    return x
```

```python
import functools

import numpy as np
import jax
import jax.numpy as jnp
from jax import lax
from jax.experimental import pallas as pl
from jax.experimental.pallas import tpu as pltpu

D_MODEL = 1024
DEPTH = 2
GRID_W = 64
N_DIR = 2
N_BRANCH = 3
BRANCH_WIDTH = 512
RMS_EPS = 1e-6
NEG_INF = -1e30

RWKV_HEAD = 64
RWKV_DECAY_LORA = 64
RWKV_LORA_COLS = 384
RWKV_COLS = 3 * BRANCH_WIDTH + RWKV_LORA_COLS
RWKV_LNX_EPS = 64e-5
RWKV_CHUNK = 64
WKV_SEQS_PER_STEP = 8

NA_HEAD = 64
NA_HEADS = BRANCH_WIDTH // NA_HEAD
NA_WIN_R = 8
NA_WIN_C = 16
NA_COLS = 3 * BRANCH_WIDTH
NA_ROWS_PER_STEP = 4

MLSTM_HEADS = 4
MLSTM_DQK = 64
MLSTM_DV = BRANCH_WIDTH // MLSTM_HEADS
MLSTM_QK = MLSTM_HEADS * MLSTM_DQK
MLSTM_CHUNK = 64
MLSTM_SEQS_PER_STEP = 8
MLSTM_GATE_SLOTS = 8
MLSTM_MAIN_COLS = 2 * MLSTM_QK + 2 * BRANCH_WIDTH
MLSTM_GATES = 2 * N_DIR * MLSTM_HEADS
GATE_CAP = 15.0
ROPE_ROT = MLSTM_DQK // 4
ROPE_BASE = 10000.0

N_EXPERTS = 64
TOP_K = 8
N_GROUPS = 8
TOPK_GROUPS = 4
D_EXPERT = 128
ROUTED_SCALE = 2.5
MOE_STEP_EXPERTS = 8
MOE_TOKEN_TILE = 1024

GATE_COLS = N_BRANCH * D_MODEL
IN_COLS = RWKV_COLS + NA_COLS + MLSTM_MAIN_COLS + MLSTM_GATES + GATE_COLS

TOKEN_TILE = 512
V7X_LANES = 128
V7X_SUBLANES = 8
HP = 2 * RWKV_HEAD
VMEM_LIMIT_BYTES = 48 * 1024 * 1024
MOE_VMEM_LIMIT_BYTES = 56 * 1024 * 1024

PROJ_COLS = 8192
NA_BLK = 0
ML_BLK = 1
ML_OGATE_BLK = (NA_COLS + 2 * MLSTM_QK + BRANCH_WIDTH) // BRANCH_WIDTH
GATE_BLK0 = (NA_COLS + MLSTM_MAIN_COLS) // D_MODEL
RWKV_COL0 = NA_COLS + MLSTM_MAIN_COLS + GATE_COLS
RWKV_BLK0 = RWKV_COL0 // BRANCH_WIDTH
RWKV_LORA_BLK = (RWKV_COL0 + 3 * BRANCH_WIDTH) // RWKV_LORA_COLS
ML_GATE_BLK = (RWKV_COL0 + RWKV_COLS) // V7X_LANES

BF = jnp.bfloat16
ACT_DTYPE = BF
HALO_ROWS = 16
HIGHEST = lax.Precision.HIGHEST


def _cparams(*sem):
    return pltpu.CompilerParams(dimension_semantics=sem, vmem_limit_bytes=VMEM_LIMIT_BYTES)


def _bdot(a, b):
    return jnp.dot(a.astype(BF), b.astype(BF), preferred_element_type=jnp.float32)


def _dot_nt(a, b):
    return lax.dot_general(a.astype(BF), b.astype(BF), (((1,), (1,)), ((), ())), preferred_element_type=jnp.float32)


def _dot_tn(a, b):
    return lax.dot_general(a.astype(BF), b.astype(BF), (((0,), (0,)), ((), ())), preferred_element_type=jnp.float32)


def _split_dot(x, w, parts):
    out = None
    rem = x
    for _ in range(parts):
        piece = rem.astype(BF)
        rem = rem - piece.astype(jnp.float32)
        t = jnp.dot(piece, w, preferred_element_type=jnp.float32)
        out = t if out is None else out + t
    return out


def _head_ones(width, head):
    i = np.arange(width) // head
    return jnp.asarray(i[:, None] == i[None, :], BF)


def _stack_pair(x, lo):
    zero = jnp.zeros_like(x)
    return jnp.concatenate([jnp.where(lo, x, zero), jnp.where(lo, zero, x)], axis=0)


def _lo_lanes(n):
    return lax.broadcasted_iota(jnp.int32, (n, HP), 1) < (HP // 2)


def _mm_kernel(x_ref, w_ref, o_ref):
    part = _bdot(x_ref[...], w_ref[...])

    @pl.when(pl.program_id(2) == 0)
    def _():
        o_ref[...] = part

    @pl.when(pl.program_id(2) > 0)
    def _():
        o_ref[...] += part


def _pick_tile(n, cands):
    for c in cands:
        if n % c == 0:
            return c
    return n


def pmm(x, w):
    M, K = x.shape
    N = w.shape[1]
    tm = _pick_tile(M, (1024, 512, 256, 128, 64, 32, 16, 8))
    tn = _pick_tile(N, (1024, 512, 384, 256, 128))
    tk = _pick_tile(K, (1024,)) if K > 1024 else K
    return pl.pallas_call(
        _mm_kernel,
        out_shape=jax.ShapeDtypeStruct((M, N), jnp.float32),
        grid=(M // tm, N // tn, K // tk),
        in_specs=[pl.BlockSpec((tm, tk), lambda i, j, k: (i, k)),
                  pl.BlockSpec((tk, tn), lambda i, j, k: (k, j))],
        out_specs=pl.BlockSpec((tm, tn), lambda i, j, k: (i, j)),
        compiler_params=_cparams("parallel", "parallel", "arbitrary"),
        name="tiled_matmul",
    )(x, w)


def _norm_mod(x, g, scale, shift):
    xn = x * lax.rsqrt(jnp.mean(x * x, axis=-1, keepdims=True) + RMS_EPS)
    return xn * g * (1.0 + scale) + shift


def _norm_proj_kernel(x_ref, g_ref, mod_ref, w_ref, o_ref, h_scr):
    @pl.when(pl.program_id(2) == 0)
    def _():
        h_scr[...] = _norm_mod(x_ref[0], g_ref[...], mod_ref[0, 1:2, :], mod_ref[0, 0:1, :]).astype(BF)

    o_ref[0] = jnp.dot(h_scr[...], w_ref[...], preferred_element_type=jnp.float32).astype(o_ref.dtype)


def norm_proj(x, g, mod, w):
    B, T, D = x.shape
    N = w.shape[1]
    tm = _pick_tile(T, (1024, 512, 256))
    tn = _pick_tile(N, (1024, 512, 256, 128))
    return pl.pallas_call(
        _norm_proj_kernel,
        out_shape=jax.ShapeDtypeStruct((B, T, N), ACT_DTYPE), grid=(B, T // tm, N // tn),
        in_specs=[pl.BlockSpec((1, tm, D), lambda b, i, j: (b, i, 0)), pl.BlockSpec((1, D), lambda b, i, j: (0, 0)),
                  pl.BlockSpec((1, 6, D), lambda b, i, j: (b, 0, 0)), pl.BlockSpec((D, tn), lambda b, i, j: (0, j))],
        out_specs=pl.BlockSpec((1, tm, tn), lambda b, i, j: (b, i, j)),
        scratch_shapes=[pltpu.VMEM((tm, D), BF)],
        compiler_params=_cparams("parallel", "parallel", "arbitrary"),
        name="norm_proj",
    )(x, g.reshape(1, D), mod, w)


def _shifted(z, prev_row, next_row):
    n = z.shape[0]
    row = lax.broadcasted_iota(jnp.int32, z.shape, 0)
    zp = jnp.where(row == 0, prev_row, pltpu.roll(z, 1, axis=0))
    zn = jnp.where(row == n - 1, next_row, pltpu.roll(z, n - 1, axis=0))
    return zp, zn


def _rwkv_feat_kernel(zr_ref, zk_ref, zv_ref, zl_ref, pr_ref, pk_ref, pv_ref, pl_ref, nr_ref, nk_ref, nv_ref, nl_ref,
                      mu_ref, w0_ref, w2_ref, a0_ref, a2_ref, kk_ref, g2_ref, ones_ref, *rest, has_vres):
    if has_vres:
        v0_ref, v1_ref, v2_ref, vf_ref = rest[:4]
        outs = rest[4:]
    else:
        outs = rest
    r_ref, k_ref, v_ref, kn_ref, g_ref, lw0_ref, lw1_ref, ag0_ref, ag1_ref = outs
    C = BRANCH_WIDTH
    first = pl.program_id(1) == 0
    last = pl.program_id(1) == pl.num_programs(1) - 1

    def shift(z_ref, p_ref, n_ref, c0, c1):
        z = z_ref[0].astype(jnp.float32)
        prev_row = jnp.where(first, 0.0, p_ref[0].astype(jnp.float32)[HALO_ROWS - 1:HALO_ROWS, :])
        next_row = jnp.where(last, 0.0, n_ref[0].astype(jnp.float32)[0:1, :])
        zp, zn = _shifted(z, prev_row, next_row)
        return z + mu_ref[:, c0:c1] * (0.5 * (zp + zn) - z)

    r = shift(zr_ref, pr_ref, nr_ref, 0, C)
    k = shift(zk_ref, pk_ref, nk_ref, C, 2 * C)
    v = shift(zv_ref, pv_ref, nv_ref, 2 * C, 3 * C)
    zl = shift(zl_ref, pl_ref, nl_ref, 3 * C, RWKV_COLS)
    wd = jnp.tanh(zl[:, 0:2 * RWKV_DECAY_LORA])
    ad = zl[:, 2 * RWKV_DECAY_LORA:4 * RWKV_DECAY_LORA]
    gd = zl[:, 4 * RWKV_DECAY_LORA:]
    for d, (lw_ref, ag_ref) in enumerate(((lw0_ref, ag0_ref), (lw1_ref, ag1_ref))):
        wl = -jax.nn.softplus(-(w0_ref[d:d + 1, :] + _bdot(wd, w2_ref[d]))) - 0.5
        lw_ref[0] = -jnp.exp(wl)
        ag_ref[0] = jax.nn.sigmoid(a0_ref[d:d + 1, :] + _bdot(ad, a2_ref[d])).astype(ag_ref.dtype)
    kq = k * kk_ref[...]
    ss = _split_dot(kq * kq, ones_ref[...], 2)
    kn_ref[0] = (kq / jnp.maximum(jnp.sqrt(ss), 1e-12)).astype(kn_ref.dtype)
    if has_vres:
        lora = _bdot(_bdot(v, v1_ref[...]), v2_ref[...])
        v = v + (vf_ref[0].astype(jnp.float32) - v) * jax.nn.sigmoid(v0_ref[...] + lora)
    g_ref[0] = _bdot(jax.nn.sigmoid(gd), g2_ref[...]).astype(g_ref.dtype)
    r_ref[0] = r.astype(r_ref.dtype)
    k_ref[0] = k.astype(k_ref.dtype)
    v_ref[0] = v.astype(v_ref.dtype)


def rwkv_features(proj, mu, w0, w2, a0, a2, k_k, g2, vres, v_first, tm):
    B, T, _ = proj.shape
    tm = min(tm, T)
    C = BRANCH_WIDTH
    nt = T // tm
    zpad = jnp.zeros((RWKV_DECAY_LORA, C), jnp.float32)
    pad_dirs = lambda w: jnp.stack([jnp.concatenate([w[0], zpad], 0), jnp.concatenate([zpad, w[1]], 0)]).astype(BF)
    sub = HALO_ROWS
    blk = lambda w, j: pl.BlockSpec((1, tm, w), lambda b, i: (b, i, j))
    before = lambda w, j: pl.BlockSpec((1, sub, w), lambda b, i: (b, jnp.maximum(i * (tm // sub) - 1, 0), j))
    after = lambda w, j: pl.BlockSpec((1, sub, w), lambda b, i: (b, jnp.minimum((i + 1) * (tm // sub), T // sub - 1), j))
    tok = pl.BlockSpec((1, tm, C), lambda b, i: (b, i, 0))
    full = lambda a: pl.BlockSpec(a.shape, lambda b, i: (0,) * a.ndim)
    params = [mu.reshape(1, -1), w0, pad_dirs(w2), a0, pad_dirs(a2), k_k.reshape(1, -1), g2.astype(BF),
              _head_ones(C, RWKV_HEAD)]
    pieces = [(C, RWKV_BLK0), (C, RWKV_BLK0 + 1), (C, RWKV_BLK0 + 2), (RWKV_LORA_COLS, RWKV_LORA_BLK)]
    args = [proj] * (3 * len(pieces)) + params
    specs = ([blk(w, j) for w, j in pieces] + [before(w, j) for w, j in pieces] + [after(w, j) for w, j in pieces]
             + [full(a) for a in params])
    if vres is not None:
        v0, v1, v2 = vres
        extra = [v0.reshape(1, -1), jnp.pad(v1, ((0, 0), (0, V7X_LANES - v1.shape[1]))).astype(BF),
                 jnp.pad(v2, ((0, V7X_LANES - v2.shape[0]), (0, 0))).astype(BF)]
        args += extra + [v_first]
        specs += [full(a) for a in extra] + [tok]
    return pl.pallas_call(
        functools.partial(_rwkv_feat_kernel, has_vres=vres is not None),
        out_shape=(jax.ShapeDtypeStruct((B, T, C), ACT_DTYPE),) * 5 + (jax.ShapeDtypeStruct((B, T, C), jnp.float32),) * 2
                  + (jax.ShapeDtypeStruct((B, T, C), ACT_DTYPE),) * 2,
        grid=(B, nt), in_specs=specs, out_specs=(tok,) * 9,
        compiler_params=_cparams("parallel", "parallel"),
        name="rwkv_features",
    )(*args)


def _wkv_kernel(r_ref, lw_ref, kk_ref, a_ref, k_ref, v_ref, ka_ref, s0_ref, *rest, reverse, bb, n_pairs, has_prev):
    if has_prev:
        yprev_ref, y_ref, sout_ref, s_scr = rest
    else:
        y_ref, sout_ref, s_scr = rest
    C = RWKV_CHUNK
    c_idx = pl.program_id(1)

    @pl.when(c_idx == 0)
    def _():
        s_scr[...] = s0_ref[...]

    ti = lax.broadcasted_iota(jnp.int32, (C, C), 0)
    si = lax.broadcasted_iota(jnp.int32, (C, C), 1)
    tri = ((si >= ti) if reverse else (si <= ti)).astype(jnp.float32)
    tp = lax.broadcasted_iota(jnp.int32, (C, 2 * C), 0)
    sp = lax.broadcasted_iota(jnp.int32, (C, 2 * C), 1) % C
    m_strict = (sp > tp) if reverse else (sp < tp)
    m_incl = (sp >= tp) if reverse else (sp <= tp)
    eye = (tp == sp).astype(jnp.float32)
    t2 = lax.broadcasted_iota(jnp.int32, (2 * C, 2 * C), 0)
    s2 = lax.broadcasted_iota(jnp.int32, (2 * C, 2 * C), 1)
    same_head = (t2 // C) == (s2 // C)
    lo = _lo_lanes(C)
    units = [(bi, slice(p * HP, (p + 1) * HP), p) for bi in range(bb) for p in range(n_pairs)]
    n = len(units)
    cat = lambda xs: jnp.concatenate(xs, axis=0)
    stack = lambda x: _stack_pair(x, lo)
    bdiag = lambda x: jnp.where(same_head, cat([x, x]), jnp.zeros((), x.dtype))

    ar, bk, bkh, v, e_tot = [], [], [], [], []
    for bi, sl, _ in units:
        lw = lw_ref[bi, :, sl]
        kk = kk_ref[bi, :, sl].astype(jnp.float32)
        ag = a_ref[bi, :, sl].astype(jnp.float32)
        kd = k_ref[bi, :, sl].astype(jnp.float32) * (1.0 + (ag - 1.0) * ka_ref[:, sl])
        cum = jnp.dot(tri, lw, precision=HIGHEST, preferred_element_type=jnp.float32)
        tot = jnp.sum(lw, axis=0, keepdims=True)
        e_neg = jnp.exp(-cum)
        e_end = jnp.exp(tot - cum)
        b = kk * ag
        ar.append(cat([-kk * jnp.exp(cum - lw), r_ref[bi, :, sl].astype(jnp.float32) * jnp.exp(cum)]).astype(BF))
        bk.append(cat([stack(b * e_neg), stack(kd * e_neg)]).astype(BF))
        bkh.append(cat([b * e_end, kd * e_end]).astype(BF))
        v.append(v_ref[bi, :, sl].astype(BF))
        e_tot.append(jnp.exp(tot))
    gram = [_dot_nt(ar[i], bk[i]) for i in range(n)]
    l_ab = [jnp.where(m_strict, g[:C, :2 * C], 0.0) for g in gram]
    l_ak = [jnp.where(m_strict, g[:C, 2 * C:], 0.0).astype(BF) for g in gram]
    l_rbk = [jnp.concatenate([jnp.where(m_incl, g[C:, :2 * C], 0.0), jnp.where(m_incl, g[C:, 2 * C:], 0.0)],
                             axis=1).astype(BF) for g in gram]
    vs = [stack(x) for x in v]
    s0 = [s_scr[bi, p] for bi, _, p in units]
    proj = [_dot_nt(ar[i], s0[i]) for i in range(n)]
    lv = [_bdot(l_ak[i], vs[i]) for i in range(n)]
    tinv = [eye + m for m in l_ab]
    pw_bd = [bdiag(m.astype(BF)) for m in l_ab]
    pw = [_bdot(l_ab[i], pw_bd[i]).astype(BF) for i in range(n)]
    levels = 5
    for lvl in range(1, levels + 1):
        pw_bd = [bdiag(m) for m in pw]
        if lvl < levels:
            both = [_bdot(cat([pw[i], tinv[i].astype(BF)]), pw_bd[i]) for i in range(n)]
            pw = [x[:C].astype(BF) for x in both]
            tinv = [tinv[i] + both[i][C:] for i in range(n)]
        else:
            tinv = [tinv[i] + _bdot(tinv[i], pw_bd[i]) for i in range(n)]
    u = [_bdot(tinv[i], stack((proj[i][:C] + lv[i]).astype(BF))) for i in range(n)]
    ub = [x.astype(BF) for x in u]
    ys = [proj[i][C:] + _bdot(l_rbk[i], cat([stack(ub[i]), vs[i]])) for i in range(n)]
    upd = [_dot_tn(cat([ub[i], v[i]]), bkh[i]) for i in range(n)]
    for i, (bi, sl, p) in enumerate(units):
        y = ys[i]
        if has_prev:
            y = y + yprev_ref[bi, :, sl]
        y_ref[bi, :, sl] = y
        s_scr[bi, p] = s0[i] * e_tot[i] + jnp.where(same_head, upd[i], 0.0)

    @pl.when(c_idx == pl.num_programs(1) - 1)
    def _():
        sout_ref[...] = s_scr[...]


def wkv_chunked(r, lw, kk, ag, k, v, k_a, s0, y_prev, reverse):
    B, T, W = r.shape
    C = RWKV_CHUNK
    bb = WKV_SEQS_PER_STEP
    nc = T // C
    n_pairs = W // HP
    cmap = (lambda b, c: (b, nc - 1 - c, 0)) if reverse else (lambda b, c: (b, c, 0))
    tok = pl.BlockSpec((bb, C, W), cmap)
    st = pl.BlockSpec((bb, n_pairs, HP, HP), lambda b, c: (b, 0, 0, 0))
    has_prev = y_prev is not None
    args = [r, lw, kk, ag, k, v, k_a, s0] + ([y_prev] if has_prev else [])
    return pl.pallas_call(
        functools.partial(_wkv_kernel, reverse=reverse, bb=bb, n_pairs=n_pairs, has_prev=has_prev),
        out_shape=(jax.ShapeDtypeStruct((B, T, W), jnp.float32), jax.ShapeDtypeStruct(s0.shape, jnp.float32)),
        grid=(B // bb, nc),
        in_specs=[tok] * 6 + [pl.BlockSpec((1, W), lambda b, c: (0, 0)), st] + ([tok] if has_prev else []),
        out_specs=(tok, st),
        scratch_shapes=[pltpu.VMEM((bb, n_pairs, HP, HP), jnp.float32)],
        compiler_params=_cparams("parallel", "arbitrary"),
        name="wkv_chunked",
    )(*args)


def _rwkv_readout_tile(y, r, k, v, g, ag0, ag1, ka, rk, lnx_g, lnx_b, ones_bd):
    mean = _split_dot(y, ones_bd, 2) * (1.0 / RWKV_HEAD)
    yc = y - mean
    var = _split_dot(yc * yc, ones_bd, 2) * (1.0 / RWKV_HEAD)
    yn = yc * lax.rsqrt(var + RWKV_LNX_EPS) * lnx_g + lnx_b
    ksum = k * (2.0 + (ag0 + ag1 - 2.0) * ka)
    bonus = _split_dot(r * ksum * rk, ones_bd, 2) * v
    return (yn + bonus) * g


def rwkv_mix(proj_x, proj_c, vf_x, vf_c, mu, w0, w2, a0, a2, k_k, k_a, g2, vres, tm):
    B = proj_x.shape[0]
    fx = rwkv_features(proj_x, mu, w0, w2, a0, a2, k_k, g2, vres, vf_x, tm)
    fc = rwkv_features(proj_c, mu, w0, w2, a0, a2, k_k, g2, vres, vf_c, tm)
    ka = k_a.reshape(1, -1)
    y_x = y_c = None
    for d in range(N_DIR):
        s0 = jnp.zeros((B, BRANCH_WIDTH // HP, HP, HP), jnp.float32)
        y_c, s_ctx = wkv_chunked(fc[0], fc[5 + d], fc[3], fc[7 + d], fc[1], fc[2], ka, s0, y_c, d == 1)
        y_x, _ = wkv_chunked(fx[0], fx[5 + d], fx[3], fx[7 + d], fx[1], fx[2], ka, s_ctx, y_x, d == 1)
    pick = lambda y, f: (y, f[0], f[1], f[2], f[4], f[7], f[8])
    vf_x = fx[2] if vres is None else vf_x
    vf_c = fc[2] if vres is None else vf_c
    return pick(y_x, fx), pick(y_c, fc), vf_x, vf_c


def _qknorm_kernel(z_ref, qg_ref, kg_ref, ones_ref, q_ref, k_ref, v_ref):
    ones_bd = ones_ref[...]
    C = BRANCH_WIDTH
    z = z_ref[0].astype(jnp.float32)
    q = z[:, 0:C]
    k = z[:, C:2 * C]
    qn = q * lax.rsqrt(_split_dot(q * q, ones_bd, 2) * (1.0 / NA_HEAD) + RMS_EPS) * qg_ref[...]
    kn = k * lax.rsqrt(_split_dot(k * k, ones_bd, 2) * (1.0 / NA_HEAD) + RMS_EPS) * kg_ref[...]
    q_ref[0] = (qn * NA_HEAD ** -0.5).astype(BF)
    k_ref[0] = kn.astype(BF)
    v_ref[0] = z[:, 2 * C:3 * C].astype(BF)


def na_qknorm(proj, qn_g, kn_g, tm):
    B, T, _ = proj.shape
    tm = min(tm, T)
    C = BRANCH_WIDTH
    tok = pl.BlockSpec((1, tm, C), lambda b, i: (b, i, 0))
    par = pl.BlockSpec((1, C), lambda b, i: (0, 0))
    sd = jax.ShapeDtypeStruct((B, T, C), BF)
    return pl.pallas_call(
        _qknorm_kernel, out_shape=(sd, sd, sd), grid=(B, T // tm),
        in_specs=[pl.BlockSpec((1, tm, NA_COLS), lambda b, i: (b, i, NA_BLK)), par, par,
                  pl.BlockSpec((C, C), lambda b, i: (0, 0))],
        out_specs=(tok, tok, tok),
        compiler_params=_cparams("parallel", "parallel"),
        name="na_qknorm",
    )(proj, jnp.tile(qn_g, NA_HEADS).reshape(1, C), jnp.tile(kn_g, NA_HEADS).reshape(1, C), _head_ones(C, NA_HEAD))


def na_bias_table(rpb):
    qc = np.arange(GRID_W)[:, None]
    kc = np.arange(GRID_W)[None, :]
    cs = np.clip(qc - NA_WIN_C // 2, 0, GRID_W - NA_WIN_C)
    valid = (kc >= cs) & (kc < cs + NA_WIN_C)
    cidx = np.clip(kc - qc + NA_WIN_C - 1, 0, 2 * NA_WIN_C - 2)
    t = jnp.where(valid[None, None], rpb[:, :, cidx], NEG_INF)
    t2 = jnp.concatenate([t[:, :-1], t[:, 1:]], axis=-1)
    H = rpb.shape[0]
    t2 = t2.reshape(H // 2, 2, 2 * NA_WIN_R - 2, GRID_W, 2 * GRID_W).transpose(0, 2, 1, 3, 4)
    return t2.reshape(H // 2, 2 * NA_WIN_R - 2, 2 * GRID_W, 2 * GRID_W)


def _na_kernel(q_ref, k_ref, v_ref, kc_ref, vc_ref, bias_ref, o_ref, *, rows):
    nwin = NA_WIN_R * GRID_W
    lo = _lo_lanes(GRID_W)
    units = []
    for rr in range(NA_ROWS_PER_STEP):
        r = pl.program_id(1) * NA_ROWS_PER_STEP + rr
        rs = jnp.clip(r - NA_WIN_R // 2, 0, rows - NA_WIN_R)
        k0 = pl.multiple_of(rs * GRID_W, GRID_W)
        for p in range(BRANCH_WIDTH // HP):
            units.append((rr, p, slice(p * HP, (p + 1) * HP), rs - r + NA_WIN_R - 1, k0))
    qs = [_stack_pair(q_ref[0, rr * GRID_W:(rr + 1) * GRID_W, sl], lo) for rr, _, sl, _, _ in units]
    s_loc = [_dot_nt(qs[i], k_ref[0, pl.ds(u[4], nwin), u[2]]) for i, u in enumerate(units)]
    s_ctx = [_dot_nt(qs[i], kc_ref[0, :, u[2]]) for i, u in enumerate(units)]
    p_loc, p_ctx, den = [], [], []
    for i, (_, p, _, base, _) in enumerate(units):
        sl_b = s_loc[i] + jnp.concatenate([bias_ref[p, base + 2 * j] for j in range(NA_WIN_R // 2)], axis=1)
        m = jnp.maximum(jnp.max(sl_b, axis=1, keepdims=True), jnp.max(s_ctx[i], axis=1, keepdims=True))
        el = jnp.exp(sl_b - m)
        ec = jnp.exp(s_ctx[i] - m)
        den.append(jnp.sum(el, axis=1, keepdims=True) + jnp.sum(ec, axis=1, keepdims=True))
        p_loc.append(el.astype(BF))
        p_ctx.append(ec.astype(BF))
    o_loc = [_bdot(p_loc[i], v_ref[0, pl.ds(u[4], nwin), u[2]]) for i, u in enumerate(units)]
    o_ctx = [_bdot(p_ctx[i], vc_ref[0, :, u[2]]) for i, u in enumerate(units)]
    for i, (rr, _, sl, _, _) in enumerate(units):
        o = (o_loc[i] + o_ctx[i]) / den[i]
        o_ref[0, rr * GRID_W:(rr + 1) * GRID_W, sl] = jnp.where(lo, o[:GRID_W], o[GRID_W:])


def na_attention(q, k, v, kc, vc, bias_tab):
    B, S, C = q.shape
    rows = S // GRID_W
    n_ctx = kc.shape[1]
    seq = pl.BlockSpec((1, S, C), lambda b, r: (b, 0, 0))
    cx = pl.BlockSpec((1, n_ctx, C), lambda b, r: (b, 0, 0))
    row = pl.BlockSpec((1, NA_ROWS_PER_STEP * GRID_W, C), lambda b, r: (b, r, 0))
    return pl.pallas_call(
        functools.partial(_na_kernel, rows=rows),
        out_shape=jax.ShapeDtypeStruct((B, S, C), jnp.float32),
        grid=(B, rows // NA_ROWS_PER_STEP),
        in_specs=[row, seq, seq, cx, cx, pl.BlockSpec(bias_tab.shape, lambda b, r: (0, 0, 0, 0))],
        out_specs=row,
        compiler_params=_cparams("parallel", "arbitrary"),
        name="na_attention",
    )(q, k, v, kc, vc, bias_tab)


def _ctx_attn_kernel(q_ref, k_ref, v_ref, o_ref):
    n = q_ref.shape[1]
    lo = _lo_lanes(n)
    sls = [slice(p * HP, (p + 1) * HP) for p in range(BRANCH_WIDTH // HP)]
    sc = [_dot_nt(_stack_pair(q_ref[0, :, sl], lo), k_ref[0, :, sl]) for sl in sls]
    e = [jnp.exp(x - jnp.max(x, axis=1, keepdims=True)) for x in sc]
    o = [_bdot(e[p], v_ref[0, :, sl]) / jnp.sum(e[p], axis=1, keepdims=True) for p, sl in enumerate(sls)]
    for p, sl in enumerate(sls):
        o_ref[0, :, sl] = jnp.where(lo, o[p][:n], o[p][n:])


def ctx_attention(q, k, v):
    B, n, C = q.shape
    blk = pl.BlockSpec((1, n, C), lambda b: (b, 0, 0))
    return pl.pallas_call(
        _ctx_attn_kernel, out_shape=jax.ShapeDtypeStruct((B, n, C), jnp.float32), grid=(B,),
        in_specs=[blk, blk, blk], out_specs=blk,
        compiler_params=_cparams("parallel"),
        name="ctx_attention",
    )(q, k, v)


def na_mix(proj_x, proj_c, qn_g, kn_g, rpb, need_ctx, tm):
    q, k, v = na_qknorm(proj_x, qn_g, kn_g, tm)
    qc, kc, vc = na_qknorm(proj_c, qn_g, kn_g, tm)
    out_x = na_attention(q, k, v, kc, vc, na_bias_table(rpb))
    out_c = ctx_attention(qc, kc, vc) if need_ctx else None
    return out_x, out_c


def rope_tables(n_tokens):
    t = jnp.arange(n_tokens)
    pos = jnp.stack([t // GRID_W, t % GRID_W], axis=-1).astype(jnp.float32)
    inv = ROPE_BASE ** (-jnp.arange(ROPE_ROT, dtype=jnp.float32) / ROPE_ROT)
    ang = pos[:, :, None] * inv
    cos_h = jnp.concatenate([jnp.cos(ang), jnp.cos(ang)], axis=-1).reshape(n_tokens, MLSTM_DQK)
    sin_h = jnp.concatenate([-jnp.sin(ang), jnp.sin(ang)], axis=-1).reshape(n_tokens, MLSTM_DQK)
    col = np.arange(MLSTM_QK)
    partner = np.where((col % (2 * ROPE_ROT)) < ROPE_ROT, col + ROPE_ROT, col - ROPE_ROT)
    perm = np.zeros((MLSTM_QK, MLSTM_QK), np.float32)
    perm[partner, col] = 1.0
    return jnp.tile(cos_h, (1, MLSTM_HEADS)), jnp.tile(sin_h, (1, MLSTM_HEADS)), jnp.asarray(perm, BF)


def _mlstm_prep_kernel(z_ref, gp_ref, ib_ref, fb_ref, *rest, rope):
    if rope:
        cos_ref, sin_ref, perm_ref, q_ref, k_ref, v_ref, g_ref = rest
    else:
        q_ref, k_ref, v_ref, g_ref = rest
    z = z_ref[0].astype(jnp.float32)
    q = z[:, 0:MLSTM_QK]
    k = z[:, MLSTM_QK:2 * MLSTM_QK]
    if rope:
        perm = perm_ref[...]
        q = q * cos_ref[...] + _split_dot(q, perm, 3) * sin_ref[...]
        k = k * cos_ref[...] + _split_dot(k, perm, 3) * sin_ref[...]
    q_ref[0] = (q * MLSTM_DQK ** -0.5).astype(BF)
    k_ref[0] = k.astype(BF)
    v_ref[0] = z[:, 2 * MLSTM_QK:2 * MLSTM_QK + BRANCH_WIDTH].astype(BF)
    gp = gp_ref[0].astype(jnp.float32)
    lane = lax.broadcasted_iota(jnp.int32, gp.shape, 1)
    ig = GATE_CAP * jnp.tanh((gp + ib_ref[...]) / GATE_CAP)
    fg = GATE_CAP * jnp.tanh((gp + fb_ref[...]) / GATE_CAP)
    g_ref[0] = jnp.where(lane < N_DIR * MLSTM_HEADS, ig, jax.nn.log_sigmoid(fg))


def mlstm_prep(proj, i_bias, f_bias, rope, tm):
    B, T, _ = proj.shape
    tm = min(tm, T)
    ng = N_DIR * MLSTM_HEADS
    ib = jnp.zeros((1, V7X_LANES), jnp.float32).at[0, 0:ng].set(i_bias.reshape(-1))
    fb = jnp.zeros((1, V7X_LANES), jnp.float32).at[0, ng:2 * ng].set(f_bias.reshape(-1))
    tok = lambda w: pl.BlockSpec((1, tm, w), lambda b, i: (b, i, 0))
    par = pl.BlockSpec((1, V7X_LANES), lambda b, i: (0, 0))
    args = [proj, proj, ib, fb]
    specs = [pl.BlockSpec((1, tm, MLSTM_MAIN_COLS), lambda b, i: (b, i, ML_BLK)),
             pl.BlockSpec((1, tm, V7X_LANES), lambda b, i: (b, i, ML_GATE_BLK)), par, par]
    if rope is not None:
        args += list(rope)
        specs += [pl.BlockSpec((tm, MLSTM_QK), lambda b, i: (i, 0)), pl.BlockSpec((tm, MLSTM_QK), lambda b, i: (i, 0)),
                  pl.BlockSpec((MLSTM_QK, MLSTM_QK), lambda b, i: (0, 0))]
    return pl.pallas_call(
        functools.partial(_mlstm_prep_kernel, rope=rope is not None),
        out_shape=(jax.ShapeDtypeStruct((B, T, MLSTM_QK), BF), jax.ShapeDtypeStruct((B, T, MLSTM_QK), BF),
                   jax.ShapeDtypeStruct((B, T, BRANCH_WIDTH), BF), jax.ShapeDtypeStruct((B, T, V7X_LANES), jnp.float32)),
        grid=(B, T // tm), in_specs=specs,
        out_specs=(tok(MLSTM_QK), tok(MLSTM_QK), tok(BRANCH_WIDTH), tok(V7X_LANES)),
        compiler_params=_cparams("parallel", "parallel"),
        name="mlstm_prep",
    )(*args)


def _mlstm_kernel(q_ref, k_ref, v_ref, gr_ref, c0_ref, n0_ref, m0_ref, *rest, reverse, bb, has_prev):
    if has_prev:
        hprev_ref, h_ref, cout_ref, nout_ref, mout_ref, c_scr, n_scr, m_scr = rest
    else:
        h_ref, cout_ref, nout_ref, mout_ref, c_scr, n_scr, m_scr = rest
    L = MLSTM_CHUNK
    DV = MLSTM_DV
    H = MLSTM_HEADS
    SL = MLSTM_GATE_SLOTS
    cidx = pl.program_id(1)

    @pl.when(cidx == 0)
    def _():
        c_scr[...] = c0_ref[...]
        n_scr[...] = n0_ref[...]
        m_scr[...] = m0_ref[...]

    ti = lax.broadcasted_iota(jnp.int32, (L, L), 0)
    si = lax.broadcasted_iota(jnp.int32, (L, L), 1)
    before = (si >= ti) if reverse else (si <= ti)
    tri = before.astype(jnp.float32)
    last = 0 if reverse else L - 1
    lo = _lo_lanes(L)
    lo_row = lax.broadcasted_iota(jnp.int32, (1, HP), 1) < (HP // 2)
    lo_col = lax.broadcasted_iota(jnp.int32, (HP, 1), 0) < (HP // 2)
    units = [(bi, p) for bi in range(bb) for p in range(H // 2)]
    blocks = [(bi, h) for bi in range(bb) for h in range(H)]
    nb = len(blocks)
    cat = lambda xs: jnp.concatenate(xs, axis=0)
    row_of = lambda bi, h: bi * SL + h

    i_rows = cat([gr_ref[bi, 0, 0:SL, :] for bi in range(bb)])
    lf_rows = cat([gr_ref[bi, 0, SL:2 * SL, :] for bi in range(bb)])
    m_prev = cat([m_scr[bi] for bi in range(bb)])
    b_rows = lax.dot_general(lf_rows, tri, (((1,), (1,)), ((), ())), precision=HIGHEST, preferred_element_type=jnp.float32)
    g_rows = i_rows - b_rows
    run = g_rows
    neg = jnp.full_like(g_rows, -jnp.inf)
    step = 1
    while step < L:
        shifted = (jnp.concatenate([run[:, step:], neg[:, :step]], axis=1) if reverse
                   else jnp.concatenate([neg[:, :step], run[:, :L - step]], axis=1))
        run = jnp.maximum(run, shifted)
        step *= 2
    m_rows = jnp.maximum(m_prev, run)
    m_end = jnp.broadcast_to(m_rows[:, last:last + 1], m_rows.shape)
    b_end = jnp.broadcast_to(b_rows[:, last:last + 1], b_rows.shape)
    sc_rows = jnp.exp(m_prev - m_rows)
    floor_rows = jnp.exp(-(b_rows + m_rows))
    wexp_rows = jnp.exp(g_rows - m_end)
    dec_rows = jnp.exp(m_prev - m_end)
    cols = cat([m_rows, sc_rows, floor_rows, wexp_rows]).T
    nr = bb * SL
    col_of = lambda kind, bi, h: cols[:, kind * nr + row_of(bi, h):kind * nr + row_of(bi, h) + 1]
    m_col = cat([col_of(0, bi, h) for bi, h in blocks])
    sc_col = cat([col_of(1, bi, h) for bi, h in blocks])
    floor_col = cat([col_of(2, bi, h) for bi, h in blocks])
    wexp_col = cat([col_of(3, bi, h) for bi, h in blocks])
    g_bcast = cat([jnp.broadcast_to(g_rows[row_of(bi, h):row_of(bi, h) + 1, :], (L, L)) for bi, h in blocks])
    decay = jnp.where(cat([before] * nb), jnp.exp(g_bcast - m_col), 0.0)

    qs = [_stack_pair(q_ref[bi, :, p * HP:(p + 1) * HP], lo) for bi, p in units]
    kp = [k_ref[bi, :, p * HP:(p + 1) * HP] for bi, p in units]
    vb = [v_ref[bi, :, h * DV:(h + 1) * DV] for bi, h in blocks]
    c_pair = [c_scr[bi, p] for bi, p in units]
    n_pair = [n_scr[bi, p] for bi, p in units]
    qk = cat([_dot_nt(qs[u], kp[u]) for u in range(len(units))])
    qc = cat([_bdot(qs[u], c_pair[u]) for u in range(len(units))])
    qn = cat([jnp.sum(qs[u].astype(jnp.float32) * n_pair[u], axis=1, keepdims=True) for u in range(len(units))])
    smat = qk * decay
    den = sc_col * qn + jnp.sum(smat, axis=1, keepdims=True)
    sb = smat.astype(BF)
    sv = cat([_bdot(sb[j * L:(j + 1) * L], vb[j]) for j in range(nb)])
    hout = (sc_col * qc + sv) / jnp.maximum(jnp.abs(den), floor_col)
    for j, (bi, h) in enumerate(blocks):
        hs = slice(h * DV, (h + 1) * DV)
        part = hout[j * L:(j + 1) * L]
        h_ref[bi, :, hs] = (part + hprev_ref[bi, :, hs]) if has_prev else part

    dec = lambda bi, h: dec_rows[row_of(bi, h):row_of(bi, h) + 1, 0:1]
    for u, (bi, p) in enumerate(units):
        j0 = bi * H + 2 * p
        kws = _stack_pair(kp[u], lo).astype(jnp.float32) * wexp_col[2 * u * L:2 * (u + 1) * L]
        upd = _dot_tn(kws, cat([vb[j0], vb[j0 + 1]]))
        c_scr[bi, p] = jnp.where(lo_col, dec(bi, 2 * p), dec(bi, 2 * p + 1)) * c_pair[u] + upd
        n_scr[bi, p] = jnp.where(lo_row, dec(bi, 2 * p), dec(bi, 2 * p + 1)) * n_pair[u] + jnp.sum(kws, axis=0, keepdims=True)
    m_next = b_end + m_end
    for bi in range(bb):
        m_scr[bi] = m_next[bi * SL:(bi + 1) * SL]

    @pl.when(cidx == pl.num_programs(1) - 1)
    def _():
        cout_ref[...] = c_scr[...]
        nout_ref[...] = n_scr[...]
        mout_ref[...] = m_scr[...]


def mlstm_chunked(q, k, v, gates, state, h_prev, direction):
    B, T, _ = q.shape
    L = MLSTM_CHUNK
    H = MLSTM_HEADS
    SL = MLSTM_GATE_SLOTS
    bb = MLSTM_SEQS_PER_STEP
    nc = T // L
    reverse = direction == 1
    ig = gates[:, :, direction * H:(direction + 1) * H]
    fg = gates[:, :, N_DIR * H + direction * H:N_DIR * H + (direction + 1) * H]
    zpad = jnp.zeros((B, T, SL - H), jnp.float32)
    g_rows = jnp.concatenate([ig, zpad, fg, zpad], axis=-1).reshape(B, nc, L, 2 * SL).transpose(0, 1, 3, 2)
    cm = (lambda c: nc - 1 - c) if reverse else (lambda c: c)
    tok = lambda w: pl.BlockSpec((bb, L, w), lambda b, c: (b, cm(c), 0))
    st = lambda a: pl.BlockSpec((bb,) + a.shape[1:], lambda b, c: (b,) + (0,) * (a.ndim - 1))
    c0, n0, m0 = state
    has_prev = h_prev is not None
    args = [q, k, v, g_rows, c0, n0, m0] + ([h_prev] if has_prev else [])
    outs = pl.pallas_call(
        functools.partial(_mlstm_kernel, reverse=reverse, bb=bb, has_prev=has_prev),
        out_shape=(jax.ShapeDtypeStruct((B, T, BRANCH_WIDTH), jnp.float32),) + tuple(
            jax.ShapeDtypeStruct(a.shape, jnp.float32) for a in state),
        grid=(B // bb, nc),
        in_specs=[tok(MLSTM_QK), tok(MLSTM_QK), tok(BRANCH_WIDTH),
                  pl.BlockSpec((bb, 1, 2 * SL, L), lambda b, c: (b, cm(c), 0, 0)), st(c0), st(n0), st(m0)]
                 + ([tok(BRANCH_WIDTH)] if has_prev else []),
        out_specs=(tok(BRANCH_WIDTH), st(c0), st(n0), st(m0)),
        scratch_shapes=[pltpu.VMEM((bb,) + a.shape[1:], jnp.float32) for a in state],
        compiler_params=_cparams("parallel", "arbitrary"),
        name="mlstm_chunked",
    )(*args)
    return outs[0], outs[1:]


def _mlstm_readout_tile(h, o, norm_g):
    parts = []
    for hd in range(MLSTM_HEADS):
        x = h[:, hd * MLSTM_DV:(hd + 1) * MLSTM_DV]
        parts.append(x * lax.rsqrt(jnp.mean(x * x, axis=1, keepdims=True) + RMS_EPS))
    return jnp.concatenate(parts, axis=1) * norm_g * jax.nn.sigmoid(o)


def mlstm_mix(proj_x, proj_c, i_bias, f_bias, rope, tm):
    B = proj_x.shape[0]
    qx, kx, vx, gx = mlstm_prep(proj_x, i_bias, f_bias, rope, tm)
    qc, kc, vc, gc = mlstm_prep(proj_c, i_bias, f_bias, None, tm)
    h_x = h_c = None
    for d in range(N_DIR):
        st0 = (jnp.zeros((B, MLSTM_HEADS // 2, HP, HP), jnp.float32), jnp.zeros((B, MLSTM_HEADS // 2, 1, HP), jnp.float32),
               jnp.zeros((B, MLSTM_GATE_SLOTS, MLSTM_CHUNK), jnp.float32))
        h_c, st_ctx = mlstm_chunked(qc, kc, vc, gc, st0, h_c, d)
        h_x, _ = mlstm_chunked(qx, kx, vx, gx, st_ctx, h_x, d)
    return h_x, h_c


def _merge_kernel(y_ref, r_ref, k_ref, v_ref, g_ref, ag0_ref, ag1_ref, yb_ref, h_ref, og_ref, ga_ref, gb_ref, gc_ref,
                  x_ref, mod_ref, ka_ref, rk_ref, lg_ref, lb_ref, ones_ref, ng_ref, wb_ref, wo_ref, o_ref):
    f32 = lambda ref: ref[0].astype(jnp.float32)
    ya = _rwkv_readout_tile(y_ref[0], f32(r_ref), f32(k_ref), f32(v_ref), f32(g_ref), f32(ag0_ref), f32(ag1_ref),
                            ka_ref[...], rk_ref[...], lg_ref[...], lb_ref[...], ones_ref[...])
    yc = _mlstm_readout_tile(h_ref[0], f32(og_ref), ng_ref[...])
    merged = None
    for i, (y, gate_ref) in enumerate(((ya, ga_ref), (yb_ref[0], gb_ref), (yc, gc_ref))):
        t = jax.nn.sigmoid(f32(gate_ref)) * _bdot(y, wb_ref[i])
        merged = t if merged is None else merged + t
    o_ref[0] = x_ref[0] + mod_ref[0, 2:3, :] * _bdot(merged, wo_ref[...])


def merge_apply(rw, yb, h_ml, proj, x, mod, rw_params, ml_norm_g, w_branch, w_out, tm):
    B, T, D = x.shape
    tm = min(tm, T)
    C = BRANCH_WIDTH
    tok = lambda w: pl.BlockSpec((1, tm, w), lambda b, i: (b, i, 0))
    gate = lambda k: pl.BlockSpec((1, tm, D), lambda b, i: (b, i, GATE_BLK0 + k))
    par = pl.BlockSpec((1, C), lambda b, i: (0, 0))
    params = [p.reshape(1, C) for p in rw_params]
    return pl.pallas_call(
        _merge_kernel, out_shape=jax.ShapeDtypeStruct((B, T, D), jnp.float32), grid=(B, T // tm),
        in_specs=[tok(C)] * 9 + [pl.BlockSpec((1, tm, C), lambda b, i: (b, i, ML_OGATE_BLK)), gate(0), gate(1), gate(2), tok(D),
                  pl.BlockSpec((1, 6, D), lambda b, i: (b, 0, 0))] + [par] * 4
                 + [pl.BlockSpec((C, C), lambda b, i: (0, 0)), par,
                    pl.BlockSpec(w_branch.shape, lambda b, i: (0, 0, 0)), pl.BlockSpec(w_out.shape, lambda b, i: (0, 0))],
        out_specs=tok(D),
        compiler_params=_cparams("parallel", "parallel"),
        name="merge_branches",
    )(*rw, yb, h_ml, proj, proj, proj, proj, x, mod, *params, _head_ones(C, RWKV_HEAD), ml_norm_g.reshape(1, C),
      w_branch, w_out)


def _route_kernel(x_ref, g_ref, mod_ref, wr_ref, rb_ref, h_ref, gate_ref):
    h = _norm_mod(x_ref[0], g_ref[...], mod_ref[0, 4:5, :], mod_ref[0, 3:4, :])
    h_ref[0] = h.astype(BF)
    tm = h.shape[0]
    logits = lax.dot_general(wr_ref[...], h, (((1,), (1,)), ((), ())), precision=HIGHEST, preferred_element_type=jnp.float32)
    scores = jax.nn.sigmoid(logits)
    sel = scores + rb_ref[...]
    gsz = N_EXPERTS // N_GROUPS
    grp = sel.reshape(N_GROUPS, gsz, tm)
    iota_in = lax.broadcasted_iota(jnp.int32, grp.shape, 1)
    m1 = jnp.max(grp, axis=1, keepdims=True)
    first = jnp.min(jnp.where(grp == m1, iota_in, gsz), axis=1, keepdims=True)
    m2 = jnp.max(jnp.where(iota_in == first, -jnp.inf, grp), axis=1, keepdims=True)
    gscore = (m1 + m2).reshape(N_GROUPS, tm)
    gi = lax.broadcasted_iota(jnp.int32, (N_GROUPS, tm), 0)
    rank = jnp.zeros((N_GROUPS, tm), jnp.int32)
    for g2 in range(N_GROUPS):
        other = gscore[g2:g2 + 1, :]
        rank = rank + ((other > gscore) | ((other == gscore) & (g2 < gi))).astype(jnp.int32)
    gmask = rank < TOPK_GROUPS
    emask = jnp.broadcast_to(gmask.reshape(N_GROUPS, 1, tm), (N_GROUPS, gsz, tm)).reshape(N_EXPERTS, tm)
    cand = jnp.where(emask, sel, NEG_INF)
    ei = lax.broadcasted_iota(jnp.int32, (N_EXPERTS, tm), 0)
    chosen = jnp.zeros((N_EXPERTS, tm), jnp.bool_)
    for _ in range(TOP_K):
        mx = jnp.max(cand, axis=0, keepdims=True)
        idx = jnp.min(jnp.where(cand == mx, ei, N_EXPERTS), axis=0, keepdims=True)
        hit = ei == idx
        chosen = chosen | hit
        cand = jnp.where(hit, -jnp.inf, cand)
    w = jnp.where(chosen, scores, 0.0)
    w = w / jnp.sum(w, axis=0, keepdims=True) * ROUTED_SCALE
    gate_ref[0] = jnp.concatenate([w, jnp.zeros((V7X_LANES - N_EXPERTS, tm), jnp.float32)], axis=0).T


def moe_route(x, g, mod, w_router, router_bias, tm):
    B, T, D = x.shape
    tm = min(tm, T)
    tok = pl.BlockSpec((1, tm, D), lambda b, i: (b, i, 0))
    return pl.pallas_call(
        _route_kernel,
        out_shape=(jax.ShapeDtypeStruct((B, T, D), BF), jax.ShapeDtypeStruct((B, T, V7X_LANES), jnp.float32)),
        grid=(B, T // tm),
        in_specs=[tok, pl.BlockSpec((1, D), lambda b, i: (0, 0)), pl.BlockSpec((1, 6, D), lambda b, i: (b, 0, 0)),
                  pl.BlockSpec((N_EXPERTS, D), lambda b, i: (0, 0)), pl.BlockSpec((N_EXPERTS, 1), lambda b, i: (0, 0))],
        out_specs=(tok, pl.BlockSpec((1, tm, V7X_LANES), lambda b, i: (b, i, 0))),
        compiler_params=_cparams("parallel", "parallel"),
        name="moe_route",
    )(x, g.reshape(1, D), mod, w_router.T, router_bias.reshape(N_EXPERTS, 1))


def _moe_kernel(h_ref, gate_ref, x_ref, mod_ref, sel_ref, wg_ref, wu_ref, wd_ref, sg_ref, su_ref, sd_ref, o_ref, *, tm):
    j = pl.program_id(1)
    rows = pl.ds(pl.multiple_of(pl.program_id(2) * tm, tm), tm)
    h = h_ref[0]

    @pl.when(j == 0)
    def _():
        sh = jax.nn.silu(_bdot(h, sg_ref[...])) * _bdot(h, su_ref[...])
        o_ref[0, rows, :] = _bdot(sh, sd_ref[...])

    g8 = _split_dot(gate_ref[0], sel_ref[0], 2)
    act = jax.nn.silu(_bdot(h, wg_ref[...])) * _bdot(h, wu_ref[...])
    act = jnp.concatenate([act[:, e * D_EXPERT:(e + 1) * D_EXPERT] * g8[:, e:e + 1] for e in range(MOE_STEP_EXPERTS)],
                          axis=1)
    o_ref[0, rows, :] += _bdot(act, wd_ref[...])

    @pl.when(j == pl.num_programs(1) - 1)
    def _():
        o_ref[0, rows, :] = x_ref[0] + mod_ref[0, 5:6, :] * o_ref[0, rows, :]


def _moe_select_table():
    se = MOE_STEP_EXPERTS
    t = np.zeros((N_EXPERTS // se, V7X_LANES, V7X_LANES), np.float32)
    for j in range(N_EXPERTS // se):
        for e in range(se):
            t[j, j * se + e, e] = 1.0
    return jnp.asarray(t, BF)


def moe_apply(h2, gates, x, mod, wg, wu, wd, sg, su, sd, tm):
    B, T, D = x.shape
    sw = MOE_STEP_EXPERTS * D_EXPERT
    n_groups = N_EXPERTS // MOE_STEP_EXPERTS
    tok = pl.BlockSpec((1, tm, D), lambda b, j, i: (b, i, 0))
    x_last = pl.BlockSpec((1, tm, D), lambda b, j, i: (b, jnp.where(j == n_groups - 1, i, 0), 0))
    full = lambda a: pl.BlockSpec(a.shape, lambda b, j, i: (0,) * a.ndim)
    return pl.pallas_call(
        functools.partial(_moe_kernel, tm=tm), out_shape=jax.ShapeDtypeStruct((B, T, D), jnp.float32),
        grid=(B, n_groups, T // tm),
        in_specs=[tok, pl.BlockSpec((1, tm, V7X_LANES), lambda b, j, i: (b, i, 0)), x_last,
                  pl.BlockSpec((1, 6, D), lambda b, j, i: (b, 0, 0)),
                  pl.BlockSpec((1, V7X_LANES, V7X_LANES), lambda b, j, i: (j, 0, 0)),
                  pl.BlockSpec((D, sw), lambda b, j, i: (0, j)), pl.BlockSpec((D, sw), lambda b, j, i: (0, j)),
                  pl.BlockSpec((sw, D), lambda b, j, i: (j, 0)), full(sg), full(su), full(sd)],
        out_specs=pl.BlockSpec((1, T, D), lambda b, j, i: (b, 0, 0)),
        compiler_params=pltpu.CompilerParams(dimension_semantics=("parallel", "arbitrary", "arbitrary"),
                                             vmem_limit_bytes=MOE_VMEM_LIMIT_BYTES),
        name="moe_experts",
    )(h2, gates, x, mod, _moe_select_table(), wg, wu, wd, sg, su, sd)


def _reorder_w_in(w):
    o_na = RWKV_COLS
    o_ml = o_na + NA_COLS
    o_mg = o_ml + MLSTM_MAIN_COLS
    o_gate = o_mg + MLSTM_GATES
    pad = jnp.zeros((w.shape[0], V7X_LANES - MLSTM_GATES), w.dtype)
    return jnp.concatenate([w[:, o_na:o_ml], w[:, o_ml:o_mg], w[:, o_gate:], w[:, :RWKV_COLS], w[:, o_mg:o_gate], pad],
                           axis=1).astype(BF)


def kernel(x, c, ctx, c_ctx, w_ada, b_ada, norm1_g, norm2_g, w_in, rw_mu, rw_w0, rw_w2, rw_a0, rw_a2, rw_k_k, rw_k_a, rw_r_k, rw_g2, rw_lnx_g, rw_lnx_b, rw_v0, rw_v1, rw_v2, na_qn_g, na_kn_g, na_rpb, ml_i_bias, ml_f_bias, ml_norm_g, w_branch, w_out, moe_router, moe_bias, moe_w_gate, moe_w_up, moe_w_down, sh_w_gate, sh_w_up, sh_w_down):
    B, S, D = x.shape
    n_ctx = ctx.shape[1]
    tm = TOKEN_TILE
    assert S % tm == 0 and n_ctx % min(tm, n_ctx) == 0 and PROJ_COLS == IN_COLS + V7X_LANES - MLSTM_GATES
    rope = rope_tables(S)
    n_cond = B + 1
    cond_pad = (-n_cond) % V7X_SUBLANES
    s_cond = jnp.pad(jnp.concatenate([jax.nn.silu(c), jax.nn.silu(c_ctx)[None]], axis=0), ((0, cond_pad), (0, 0)))
    vf_x = vf_c = None
    for l in range(DEPTH):
        need_ctx = l < DEPTH - 1
        mod = pmm(s_cond, w_ada[l]) + b_ada[l]
        mod_x = mod[:B].reshape(B, 6, D)
        mod_c = jnp.broadcast_to(mod[B].reshape(1, 6, D), (B, 6, D))
        w_proj = _reorder_w_in(w_in[l])
        proj_x = norm_proj(x, norm1_g[l], mod_x, w_proj)
        proj_c = norm_proj(ctx.reshape(1, B * n_ctx, D), norm1_g[l], mod_c[:1], w_proj).reshape(B, n_ctx, PROJ_COLS)
        vres = None if l == 0 else (rw_v0[l - 1], rw_v1[l - 1], rw_v2[l - 1])
        rw_x, rw_c, vf_x, vf_c = rwkv_mix(proj_x, proj_c, vf_x, vf_c, rw_mu[l], rw_w0[l], rw_w2[l], rw_a0[l], rw_a2[l],
                                          rw_k_k[l], rw_k_a[l], rw_g2[l], vres, tm)
        rw_params = (rw_k_a[l], rw_r_k[l], rw_lnx_g[l], rw_lnx_b[l])
        yb_x, yb_c = na_mix(proj_x, proj_c, na_qn_g[l], na_kn_g[l], na_rpb[l], need_ctx, tm)
        hm_x, hm_c = mlstm_mix(proj_x, proj_c, ml_i_bias[l], ml_f_bias[l], rope, tm)
        wb = w_branch[l].astype(BF)
        wo = w_out[l].astype(BF)
        wg = moe_w_gate[l].transpose(1, 0, 2).reshape(D, N_EXPERTS * D_EXPERT).astype(BF)
        wu = moe_w_up[l].transpose(1, 0, 2).reshape(D, N_EXPERTS * D_EXPERT).astype(BF)
        wd = moe_w_down[l].reshape(N_EXPERTS * D_EXPERT, D).astype(BF)
        shared = (sh_w_gate[l].astype(BF), sh_w_up[l].astype(BF), sh_w_down[l].astype(BF))
        x = merge_apply(rw_x, yb_x, hm_x, proj_x, x, mod_x, rw_params, ml_norm_g[l], wb, wo, tm)
        h2, gates = moe_route(x, norm2_g[l], mod_x, moe_router[l], moe_bias[l], tm)
        x = moe_apply(h2, gates, x, mod_x, wg, wu, wd, *shared, MOE_TOKEN_TILE)
        if need_ctx:
            ctx = merge_apply(rw_c, yb_c, hm_c, proj_c, ctx, mod_c, rw_params, ml_norm_g[l], wb, wo, tm)
            h2, gates = moe_route(ctx, norm2_g[l], mod_c, moe_router[l], moe_bias[l], tm)
            ctx = moe_apply(h2, gates, ctx, mod_c, wg, wu, wd, *shared, min(tm, n_ctx))
    return x
```

```python
import functools

import numpy as np
import jax
import jax.numpy as jnp
from jax import lax
from jax.experimental import pallas as pl
from jax.experimental.pallas import tpu as pltpu

D_MODEL = 1024
DEPTH = 2
GRID_W = 64
N_DIR = 2
N_BRANCH = 3
BRANCH_WIDTH = 512
RMS_EPS = 1e-6
NEG_INF = -1e30

RWKV_HEAD = 64
RWKV_DECAY_LORA = 64
RWKV_LORA_COLS = 384
RWKV_COLS = 3 * BRANCH_WIDTH + RWKV_LORA_COLS
RWKV_LNX_EPS = 64e-5
RWKV_DECAY_SCALE = float(np.exp(-0.5))
RWKV_CHUNK = 64
WKV_SEQS_PER_STEP = 8

NA_HEAD = 64
NA_HEADS = BRANCH_WIDTH // NA_HEAD
NA_WIN_R = 8
NA_WIN_C = 16
NA_COLS = 3 * BRANCH_WIDTH
NA_ROWS_PER_STEP = 4

MLSTM_HEADS = 4
MLSTM_DQK = 64
MLSTM_DV = BRANCH_WIDTH // MLSTM_HEADS
MLSTM_QK = MLSTM_HEADS * MLSTM_DQK
MLSTM_CHUNK = 64
MLSTM_SEQS_PER_STEP = 8
MLSTM_GATE_SLOTS = 8
MLSTM_MAIN_COLS = 2 * MLSTM_QK + 2 * BRANCH_WIDTH
MLSTM_GATES = 2 * N_DIR * MLSTM_HEADS
GATE_CAP = 15.0
ROPE_ROT = MLSTM_DQK // 4
ROPE_BASE = 10000.0

N_EXPERTS = 64
TOP_K = 8
N_GROUPS = 8
TOPK_GROUPS = 4
D_EXPERT = 128
ROUTED_SCALE = 2.5
MOE_STEP_EXPERTS = 8
MOE_TOKEN_TILE = 1024

GATE_COLS = N_BRANCH * D_MODEL
IN_COLS = RWKV_COLS + NA_COLS + MLSTM_MAIN_COLS + MLSTM_GATES + GATE_COLS

TOKEN_TILE = 512
V7X_LANES = 128
V7X_SUBLANES = 8
HP = 2 * RWKV_HEAD
VMEM_LIMIT_BYTES = 48 * 1024 * 1024
MOE_VMEM_LIMIT_BYTES = 56 * 1024 * 1024

PROJ_COLS = 8192
NA_BLK = 0
ML_BLK = 1
ML_OGATE_BLK = (NA_COLS + 2 * MLSTM_QK + BRANCH_WIDTH) // BRANCH_WIDTH
GATE_BLK0 = (NA_COLS + MLSTM_MAIN_COLS) // D_MODEL
RWKV_COL0 = NA_COLS + MLSTM_MAIN_COLS + GATE_COLS
RWKV_BLK0 = RWKV_COL0 // BRANCH_WIDTH
RWKV_LORA_BLK = (RWKV_COL0 + 3 * BRANCH_WIDTH) // RWKV_LORA_COLS
ML_GATE_BLK = (RWKV_COL0 + RWKV_COLS) // V7X_LANES

BF = jnp.bfloat16
ACT_DTYPE = BF
HALO_ROWS = 16
HIGHEST = lax.Precision.HIGHEST


def _cparams(*sem):
    return pltpu.CompilerParams(dimension_semantics=sem, vmem_limit_bytes=VMEM_LIMIT_BYTES)


def _bdot(a, b):
    return jnp.dot(a.astype(BF), b.astype(BF), preferred_element_type=jnp.float32)


def _dot_nt(a, b):
    return lax.dot_general(a.astype(BF), b.astype(BF), (((1,), (1,)), ((), ())), preferred_element_type=jnp.float32)


def _dot_tn(a, b):
    return lax.dot_general(a.astype(BF), b.astype(BF), (((0,), (0,)), ((), ())), preferred_element_type=jnp.float32)


def _split_dot(x, w, parts):
    out = None
    rem = x
    for _ in range(parts):
        piece = rem.astype(BF)
        rem = rem - piece.astype(jnp.float32)
        t = jnp.dot(piece, w, preferred_element_type=jnp.float32)
        out = t if out is None else out + t
    return out


def _head_ones(width, head):
    i = np.arange(width) // head
    return jnp.asarray(i[:, None] == i[None, :], BF)


def _stack_pair(x, lo):
    zero = jnp.zeros_like(x)
    return jnp.concatenate([jnp.where(lo, x, zero), jnp.where(lo, zero, x)], axis=0)


def _lo_lanes(n):
    return lax.broadcasted_iota(jnp.int32, (n, HP), 1) < (HP // 2)


def _mm_kernel(x_ref, w_ref, o_ref):
    part = _bdot(x_ref[...], w_ref[...])

    @pl.when(pl.program_id(2) == 0)
    def _():
        o_ref[...] = part

    @pl.when(pl.program_id(2) > 0)
    def _():
        o_ref[...] += part


def _pick_tile(n, cands):
    for c in cands:
        if n % c == 0:
            return c
    return n


def pmm(x, w):
    M, K = x.shape
    N = w.shape[1]
    tm = _pick_tile(M, (1024, 512, 256, 128, 64, 32, 16, 8))
    tn = _pick_tile(N, (1024, 512, 384, 256, 128))
    tk = _pick_tile(K, (1024,)) if K > 1024 else K
    return pl.pallas_call(
        _mm_kernel,
        out_shape=jax.ShapeDtypeStruct((M, N), jnp.float32),
        grid=(M // tm, N // tn, K // tk),
        in_specs=[pl.BlockSpec((tm, tk), lambda i, j, k: (i, k)),
                  pl.BlockSpec((tk, tn), lambda i, j, k: (k, j))],
        out_specs=pl.BlockSpec((tm, tn), lambda i, j, k: (i, j)),
        compiler_params=_cparams("parallel", "parallel", "arbitrary"),
        name="tiled_matmul",
    )(x, w)


def _norm_mod(x, g, scale, shift):
    xn = x * lax.rsqrt(jnp.mean(x * x, axis=-1, keepdims=True) + RMS_EPS)
    return xn * g * (1.0 + scale) + shift


def _norm_proj_kernel(x_ref, g_ref, mod_ref, w_ref, o_ref, h_scr):
    @pl.when(pl.program_id(2) == 0)
    def _():
        h_scr[...] = _norm_mod(x_ref[0], g_ref[...], mod_ref[0, 1:2, :], mod_ref[0, 0:1, :]).astype(BF)

    o_ref[0] = jnp.dot(h_scr[...], w_ref[...], preferred_element_type=jnp.float32).astype(o_ref.dtype)


def norm_proj(x, g, mod, w):
    B, T, D = x.shape
    N = w.shape[1]
    tm = _pick_tile(T, (1024, 512, 256))
    tn = _pick_tile(N, (1024, 512, 256, 128))
    return pl.pallas_call(
        _norm_proj_kernel,
        out_shape=jax.ShapeDtypeStruct((B, T, N), ACT_DTYPE), grid=(B, T // tm, N // tn),
        in_specs=[pl.BlockSpec((1, tm, D), lambda b, i, j: (b, i, 0)), pl.BlockSpec((1, D), lambda b, i, j: (0, 0)),
                  pl.BlockSpec((1, 6, D), lambda b, i, j: (b, 0, 0)), pl.BlockSpec((D, tn), lambda b, i, j: (0, j))],
        out_specs=pl.BlockSpec((1, tm, tn), lambda b, i, j: (b, i, j)),
        scratch_shapes=[pltpu.VMEM((tm, D), BF)],
        compiler_params=_cparams("parallel", "parallel", "arbitrary"),
        name="norm_proj",
    )(x, g.reshape(1, D), mod, w)


def _shifted(z, prev_row, next_row):
    n = z.shape[0]
    s = V7X_SUBLANES
    row = lax.broadcasted_iota(jnp.int32, (s, z.shape[1]), 0)
    zp = pltpu.roll(z, 1, axis=0)
    zn = pltpu.roll(z, n - 1, axis=0)
    zp = jnp.concatenate([jnp.where(row == 0, prev_row, zp[:s]), zp[s:]], axis=0)
    zn = jnp.concatenate([zn[:n - s], jnp.where(row == s - 1, next_row, zn[n - s:])], axis=0)
    return zp, zn


def _rwkv_feat_kernel(zr_ref, zk_ref, zv_ref, zl_ref, pr_ref, pk_ref, pv_ref, pl_ref, nr_ref, nk_ref, nv_ref, nl_ref,
                      mu_ref, w0_ref, w2_ref, a0_ref, a2_ref, kk_ref, g2_ref, ones_ref, *rest, has_vres):
    if has_vres:
        v0_ref, v1_ref, v2_ref, vf_ref = rest[:4]
        outs = rest[4:]
    else:
        outs = rest
    r_ref, k_ref, v_ref, kn_ref, g_ref, lw0_ref, lw1_ref, ag0_ref, ag1_ref = outs
    C = BRANCH_WIDTH
    first = pl.program_id(1) == 0
    last = pl.program_id(1) == pl.num_programs(1) - 1

    def shift(z_ref, p_ref, n_ref, c0, c1):
        z = z_ref[0].astype(jnp.float32)
        prev_row = jnp.where(first, 0.0, p_ref[0].astype(jnp.float32)[HALO_ROWS - 1:HALO_ROWS, :])
        next_row = jnp.where(last, 0.0, n_ref[0].astype(jnp.float32)[0:1, :])
        zp, zn = _shifted(z, prev_row, next_row)
        return z + mu_ref[:, c0:c1] * (0.5 * (zp + zn) - z)

    r = shift(zr_ref, pr_ref, nr_ref, 0, C)
    k = shift(zk_ref, pk_ref, nk_ref, C, 2 * C)
    v = shift(zv_ref, pv_ref, nv_ref, 2 * C, 3 * C)
    zl = shift(zl_ref, pl_ref, nl_ref, 3 * C, RWKV_COLS)
    wd = jnp.tanh(zl[:, 0:2 * RWKV_DECAY_LORA])
    ad = zl[:, 2 * RWKV_DECAY_LORA:4 * RWKV_DECAY_LORA]
    gd = zl[:, 4 * RWKV_DECAY_LORA:]
    for d, (lw_ref, ag_ref) in enumerate(((lw0_ref, ag0_ref), (lw1_ref, ag1_ref))):
        u = w0_ref[d:d + 1, :] + _bdot(wd, w2_ref[d])
        lw_ref[0] = -RWKV_DECAY_SCALE * jax.nn.sigmoid(u)
        ag_ref[0] = jax.nn.sigmoid(a0_ref[d:d + 1, :] + _bdot(ad, a2_ref[d])).astype(ag_ref.dtype)
    kq = k * kk_ref[...]
    ss = _split_dot(kq * kq, ones_ref[...], 2)
    kn_ref[0] = (kq * lax.rsqrt(jnp.maximum(ss, 1e-24))).astype(kn_ref.dtype)
    if has_vres:
        lora = _bdot(_bdot(v, v1_ref[...]), v2_ref[...])
        v = v + (vf_ref[0].astype(jnp.float32) - v) * jax.nn.sigmoid(v0_ref[...] + lora)
    g_ref[0] = _bdot(jax.nn.sigmoid(gd), g2_ref[...]).astype(g_ref.dtype)
    r_ref[0] = r.astype(r_ref.dtype)
    k_ref[0] = k.astype(k_ref.dtype)
    v_ref[0] = v.astype(v_ref.dtype)


def rwkv_features(proj, mu, w0, w2, a0, a2, k_k, g2, vres, v_first, tm):
    B, T, _ = proj.shape
    tm = min(tm, T)
    C = BRANCH_WIDTH
    nt = T // tm
    zpad = jnp.zeros((RWKV_DECAY_LORA, C), jnp.float32)
    pad_dirs = lambda w: jnp.stack([jnp.concatenate([w[0], zpad], 0), jnp.concatenate([zpad, w[1]], 0)]).astype(BF)
    sub = HALO_ROWS
    blk = lambda w, j: pl.BlockSpec((1, tm, w), lambda b, i: (b, i, j))
    before = lambda w, j: pl.BlockSpec((1, sub, w), lambda b, i: (b, jnp.maximum(i * (tm // sub) - 1, 0), j))
    after = lambda w, j: pl.BlockSpec((1, sub, w), lambda b, i: (b, jnp.minimum((i + 1) * (tm // sub), T // sub - 1), j))
    tok = pl.BlockSpec((1, tm, C), lambda b, i: (b, i, 0))
    full = lambda a: pl.BlockSpec(a.shape, lambda b, i: (0,) * a.ndim)
    params = [mu.reshape(1, -1), w0, pad_dirs(w2), a0, pad_dirs(a2), k_k.reshape(1, -1), g2.astype(BF),
              _head_ones(C, RWKV_HEAD)]
    pieces = [(C, RWKV_BLK0), (C, RWKV_BLK0 + 1), (C, RWKV_BLK0 + 2), (RWKV_LORA_COLS, RWKV_LORA_BLK)]
    args = [proj] * (3 * len(pieces)) + params
    specs = ([blk(w, j) for w, j in pieces] + [before(w, j) for w, j in pieces] + [after(w, j) for w, j in pieces]
             + [full(a) for a in params])
    if vres is not None:
        v0, v1, v2 = vres
        extra = [v0.reshape(1, -1), jnp.pad(v1, ((0, 0), (0, V7X_LANES - v1.shape[1]))).astype(BF),
                 jnp.pad(v2, ((0, V7X_LANES - v2.shape[0]), (0, 0))).astype(BF)]
        args += extra + [v_first]
        specs += [full(a) for a in extra] + [tok]
    return pl.pallas_call(
        functools.partial(_rwkv_feat_kernel, has_vres=vres is not None),
        out_shape=(jax.ShapeDtypeStruct((B, T, C), ACT_DTYPE),) * 5 + (jax.ShapeDtypeStruct((B, T, C), jnp.float32),) * 2
                  + (jax.ShapeDtypeStruct((B, T, C), ACT_DTYPE),) * 2,
        grid=(B, nt), in_specs=specs, out_specs=(tok,) * 9,
        compiler_params=_cparams("parallel", "parallel"),
        name="rwkv_features",
    )(*args)


def _wkv_kernel(r_ref, lw_ref, kk_ref, a_ref, k_ref, v_ref, ka_ref, s0_ref, *rest, reverse, bb, n_pairs, has_prev):
    if has_prev:
        yprev_ref, y_ref, sout_ref, s_scr = rest
    else:
        y_ref, sout_ref, s_scr = rest
    C = RWKV_CHUNK
    c_idx = pl.program_id(1)

    @pl.when(c_idx == 0)
    def _():
        s_scr[...] = s0_ref[...]

    ti = lax.broadcasted_iota(jnp.int32, (C, C), 0)
    si = lax.broadcasted_iota(jnp.int32, (C, C), 1)
    tri = ((si >= ti) if reverse else (si <= ti)).astype(jnp.float32)
    tp = lax.broadcasted_iota(jnp.int32, (C, 2 * C), 0)
    sp = lax.broadcasted_iota(jnp.int32, (C, 2 * C), 1) % C
    m_strict = (sp > tp) if reverse else (sp < tp)
    m_incl = (sp >= tp) if reverse else (sp <= tp)
    eye = (tp == sp).astype(jnp.float32)
    t2 = lax.broadcasted_iota(jnp.int32, (2 * C, 2 * C), 0)
    s2 = lax.broadcasted_iota(jnp.int32, (2 * C, 2 * C), 1)
    same_head = (t2 // C) == (s2 // C)
    lo = _lo_lanes(C)
    units = [(bi, slice(p * HP, (p + 1) * HP), p) for bi in range(bb) for p in range(n_pairs)]
    n = len(units)
    cat = lambda xs: jnp.concatenate(xs, axis=0)
    stack = lambda x: _stack_pair(x, lo)
    bdiag = lambda x: jnp.where(same_head, cat([x, x]), jnp.zeros((), x.dtype))

    ar, bk, bkh, v, e_tot = [], [], [], [], []
    for bi, sl, _ in units:
        lw = lw_ref[bi, :, sl]
        kk = kk_ref[bi, :, sl].astype(jnp.float32)
        ag = a_ref[bi, :, sl].astype(jnp.float32)
        kd = k_ref[bi, :, sl].astype(jnp.float32) * (1.0 + (ag - 1.0) * ka_ref[:, sl])
        cum = jnp.dot(tri, lw, precision=HIGHEST, preferred_element_type=jnp.float32)
        tot = jnp.sum(lw, axis=0, keepdims=True)
        e_neg = jnp.exp(-cum)
        e_end = jnp.exp(tot - cum)
        b = kk * ag
        ar.append(cat([-kk * jnp.exp(cum - lw), r_ref[bi, :, sl].astype(jnp.float32) * jnp.exp(cum)]).astype(BF))
        bk.append(cat([stack(b * e_neg), stack(kd * e_neg)]).astype(BF))
        bkh.append(cat([b * e_end, kd * e_end]).astype(BF))
        v.append(v_ref[bi, :, sl].astype(BF))
        e_tot.append(jnp.exp(tot))
    gram = [_dot_nt(ar[i], bk[i]) for i in range(n)]
    l_ab = [jnp.where(m_strict, g[:C, :2 * C], 0.0) for g in gram]
    l_ak = [jnp.where(m_strict, g[:C, 2 * C:], 0.0).astype(BF) for g in gram]
    l_rbk = [jnp.concatenate([jnp.where(m_incl, g[C:, :2 * C], 0.0), jnp.where(m_incl, g[C:, 2 * C:], 0.0)],
                             axis=1).astype(BF) for g in gram]
    vs = [stack(x) for x in v]
    s0 = [s_scr[bi, p] for bi, _, p in units]
    proj = [_dot_nt(ar[i], s0[i]) for i in range(n)]
    lv = [_bdot(l_ak[i], vs[i]) for i in range(n)]
    tinv = [eye + m for m in l_ab]
    pw_bd = [bdiag(m.astype(BF)) for m in l_ab]
    pw = [_bdot(l_ab[i], pw_bd[i]).astype(BF) for i in range(n)]
    levels = 5
    for lvl in range(1, levels + 1):
        pw_bd = [bdiag(m) for m in pw]
        if lvl < levels:
            both = [_bdot(cat([pw[i], tinv[i].astype(BF)]), pw_bd[i]) for i in range(n)]
            pw = [x[:C].astype(BF) for x in both]
            tinv = [tinv[i] + both[i][C:] for i in range(n)]
        else:
            tinv = [tinv[i] + _bdot(tinv[i], pw_bd[i]) for i in range(n)]
    u = [_bdot(tinv[i], stack((proj[i][:C] + lv[i]).astype(BF))) for i in range(n)]
    ub = [x.astype(BF) for x in u]
    ys = [proj[i][C:] + _bdot(l_rbk[i], cat([stack(ub[i]), vs[i]])) for i in range(n)]
    upd = [_dot_tn(cat([ub[i], v[i]]), bkh[i]) for i in range(n)]
    for i, (bi, sl, p) in enumerate(units):
        y = ys[i]
        if has_prev:
            y = y + yprev_ref[bi, :, sl]
        y_ref[bi, :, sl] = y
        s_scr[bi, p] = s0[i] * e_tot[i] + jnp.where(same_head, upd[i], 0.0)

    @pl.when(c_idx == pl.num_programs(1) - 1)
    def _():
        sout_ref[...] = s_scr[...]


def wkv_chunked(r, lw, kk, ag, k, v, k_a, s0, y_prev, reverse):
    B, T, W = r.shape
    C = RWKV_CHUNK
    bb = WKV_SEQS_PER_STEP
    nc = T // C
    n_pairs = W // HP
    cmap = (lambda b, c: (b, nc - 1 - c, 0)) if reverse else (lambda b, c: (b, c, 0))
    tok = pl.BlockSpec((bb, C, W), cmap)
    st = pl.BlockSpec((bb, n_pairs, HP, HP), lambda b, c: (b, 0, 0, 0))
    has_prev = y_prev is not None
    args = [r, lw, kk, ag, k, v, k_a, s0] + ([y_prev] if has_prev else [])
    return pl.pallas_call(
        functools.partial(_wkv_kernel, reverse=reverse, bb=bb, n_pairs=n_pairs, has_prev=has_prev),
        out_shape=(jax.ShapeDtypeStruct((B, T, W), jnp.float32), jax.ShapeDtypeStruct(s0.shape, jnp.float32)),
        grid=(B // bb, nc),
        in_specs=[tok] * 6 + [pl.BlockSpec((1, W), lambda b, c: (0, 0)), st] + ([tok] if has_prev else []),
        out_specs=(tok, st),
        scratch_shapes=[pltpu.VMEM((bb, n_pairs, HP, HP), jnp.float32)],
        compiler_params=_cparams("parallel", "arbitrary"),
        name="wkv_chunked",
    )(*args)


def _rwkv_readout_tile(y, r, k, v, g, ag0, ag1, ka, rk, lnx_g, lnx_b, ones_bd):
    mean = _split_dot(y, ones_bd, 2) * (1.0 / RWKV_HEAD)
    yc = y - mean
    var = _split_dot(yc * yc, ones_bd, 2) * (1.0 / RWKV_HEAD)
    yn = yc * lax.rsqrt(var + RWKV_LNX_EPS) * lnx_g + lnx_b
    ksum = k * (2.0 + (ag0 + ag1 - 2.0) * ka)
    bonus = _split_dot(r * ksum * rk, ones_bd, 2) * v
    return (yn + bonus) * g


def rwkv_mix(proj_x, proj_c, vf_x, vf_c, mu, w0, w2, a0, a2, k_k, k_a, g2, vres, tm):
    B = proj_x.shape[0]
    fx = rwkv_features(proj_x, mu, w0, w2, a0, a2, k_k, g2, vres, vf_x, tm)
    fc = rwkv_features(proj_c, mu, w0, w2, a0, a2, k_k, g2, vres, vf_c, tm)
    ka = k_a.reshape(1, -1)
    y_x = y_c = None
    for d in range(N_DIR):
        s0 = jnp.zeros((B, BRANCH_WIDTH // HP, HP, HP), jnp.float32)
        y_c, s_ctx = wkv_chunked(fc[0], fc[5 + d], fc[3], fc[7 + d], fc[1], fc[2], ka, s0, y_c, d == 1)
        y_x, _ = wkv_chunked(fx[0], fx[5 + d], fx[3], fx[7 + d], fx[1], fx[2], ka, s_ctx, y_x, d == 1)
    pick = lambda y, f: (y, f[0], f[1], f[2], f[4], f[7], f[8])
    vf_x = fx[2] if vres is None else vf_x
    vf_c = fc[2] if vres is None else vf_c
    return pick(y_x, fx), pick(y_c, fc), vf_x, vf_c


def _qknorm_kernel(z_ref, qg_ref, kg_ref, ones_ref, q_ref, k_ref, v_ref):
    ones_bd = ones_ref[...]
    C = BRANCH_WIDTH
    z = z_ref[0].astype(jnp.float32)
    q = z[:, 0:C]
    k = z[:, C:2 * C]
    qn = q * lax.rsqrt(_split_dot(q * q, ones_bd, 2) * (1.0 / NA_HEAD) + RMS_EPS) * qg_ref[...]
    kn = k * lax.rsqrt(_split_dot(k * k, ones_bd, 2) * (1.0 / NA_HEAD) + RMS_EPS) * kg_ref[...]
    q_ref[0] = (qn * NA_HEAD ** -0.5).astype(BF)
    k_ref[0] = kn.astype(BF)
    v_ref[0] = z[:, 2 * C:3 * C].astype(BF)


def na_qknorm(proj, qn_g, kn_g, tm):
    B, T, _ = proj.shape
    tm = min(tm, T)
    C = BRANCH_WIDTH
    tok = pl.BlockSpec((1, tm, C), lambda b, i: (b, i, 0))
    par = pl.BlockSpec((1, C), lambda b, i: (0, 0))
    sd = jax.ShapeDtypeStruct((B, T, C), BF)
    return pl.pallas_call(
        _qknorm_kernel, out_shape=(sd, sd, sd), grid=(B, T // tm),
        in_specs=[pl.BlockSpec((1, tm, NA_COLS), lambda b, i: (b, i, NA_BLK)), par, par,
                  pl.BlockSpec((C, C), lambda b, i: (0, 0))],
        out_specs=(tok, tok, tok),
        compiler_params=_cparams("parallel", "parallel"),
        name="na_qknorm",
    )(proj, jnp.tile(qn_g, NA_HEADS).reshape(1, C), jnp.tile(kn_g, NA_HEADS).reshape(1, C), _head_ones(C, NA_HEAD))


def na_bias_table(rpb):
    qc = np.arange(GRID_W)[:, None]
    kc = np.arange(GRID_W)[None, :]
    cs = np.clip(qc - NA_WIN_C // 2, 0, GRID_W - NA_WIN_C)
    valid = (kc >= cs) & (kc < cs + NA_WIN_C)
    cidx = np.clip(kc - qc + NA_WIN_C - 1, 0, 2 * NA_WIN_C - 2)
    t = jnp.where(valid[None, None], rpb[:, :, cidx], NEG_INF)
    t2 = jnp.concatenate([t[:, :-1], t[:, 1:]], axis=-1)
    H = rpb.shape[0]
    t2 = t2.reshape(H // 2, 2, 2 * NA_WIN_R - 2, GRID_W, 2 * GRID_W).transpose(0, 2, 1, 3, 4)
    return t2.reshape(H // 2, 2 * NA_WIN_R - 2, 2 * GRID_W, 2 * GRID_W)


def _na_kernel(q_ref, k_ref, v_ref, kc_ref, vc_ref, bias_ref, o_ref, *, rows):
    nwin = NA_WIN_R * GRID_W
    lo = _lo_lanes(GRID_W)
    units = []
    for rr in range(NA_ROWS_PER_STEP):
        r = pl.program_id(1) * NA_ROWS_PER_STEP + rr
        rs = jnp.clip(r - NA_WIN_R // 2, 0, rows - NA_WIN_R)
        k0 = pl.multiple_of(rs * GRID_W, GRID_W)
        for p in range(BRANCH_WIDTH // HP):
            units.append((rr, p, slice(p * HP, (p + 1) * HP), rs - r + NA_WIN_R - 1, k0))
    qs = [_stack_pair(q_ref[0, rr * GRID_W:(rr + 1) * GRID_W, sl], lo) for rr, _, sl, _, _ in units]
    s_loc = [_dot_nt(qs[i], k_ref[0, pl.ds(u[4], nwin), u[2]]) for i, u in enumerate(units)]
    s_ctx = [_dot_nt(qs[i], kc_ref[0, :, u[2]]) for i, u in enumerate(units)]
    p_loc, p_ctx, den = [], [], []
    for i, (_, p, _, base, _) in enumerate(units):
        sl_b = s_loc[i] + jnp.concatenate([bias_ref[p, base + 2 * j] for j in range(NA_WIN_R // 2)], axis=1)
        m = jnp.maximum(jnp.max(sl_b, axis=1, keepdims=True), jnp.max(s_ctx[i], axis=1, keepdims=True))
        el = jnp.exp(sl_b - m)
        ec = jnp.exp(s_ctx[i] - m)
        den.append(jnp.sum(el, axis=1, keepdims=True) + jnp.sum(ec, axis=1, keepdims=True))
        p_loc.append(el.astype(BF))
        p_ctx.append(ec.astype(BF))
    o_loc = [_bdot(p_loc[i], v_ref[0, pl.ds(u[4], nwin), u[2]]) for i, u in enumerate(units)]
    o_ctx = [_bdot(p_ctx[i], vc_ref[0, :, u[2]]) for i, u in enumerate(units)]
    for i, (rr, _, sl, _, _) in enumerate(units):
        o = (o_loc[i] + o_ctx[i]) / den[i]
        o_ref[0, rr * GRID_W:(rr + 1) * GRID_W, sl] = jnp.where(lo, o[:GRID_W], o[GRID_W:])


def na_attention(q, k, v, kc, vc, bias_tab):
    B, S, C = q.shape
    rows = S // GRID_W
    n_ctx = kc.shape[1]
    seq = pl.BlockSpec((1, S, C), lambda b, r: (b, 0, 0))
    cx = pl.BlockSpec((1, n_ctx, C), lambda b, r: (b, 0, 0))
    row = pl.BlockSpec((1, NA_ROWS_PER_STEP * GRID_W, C), lambda b, r: (b, r, 0))
    return pl.pallas_call(
        functools.partial(_na_kernel, rows=rows),
        out_shape=jax.ShapeDtypeStruct((B, S, C), jnp.float32),
        grid=(B, rows // NA_ROWS_PER_STEP),
        in_specs=[row, seq, seq, cx, cx, pl.BlockSpec(bias_tab.shape, lambda b, r: (0, 0, 0, 0))],
        out_specs=row,
        compiler_params=_cparams("parallel", "arbitrary"),
        name="na_attention",
    )(q, k, v, kc, vc, bias_tab)


def _ctx_attn_kernel(q_ref, k_ref, v_ref, o_ref):
    n = q_ref.shape[1]
    lo = _lo_lanes(n)
    sls = [slice(p * HP, (p + 1) * HP) for p in range(BRANCH_WIDTH // HP)]
    sc = [_dot_nt(_stack_pair(q_ref[0, :, sl], lo), k_ref[0, :, sl]) for sl in sls]
    e = [jnp.exp(x - jnp.max(x, axis=1, keepdims=True)) for x in sc]
    o = [_bdot(e[p], v_ref[0, :, sl]) / jnp.sum(e[p], axis=1, keepdims=True) for p, sl in enumerate(sls)]
    for p, sl in enumerate(sls):
        o_ref[0, :, sl] = jnp.where(lo, o[p][:n], o[p][n:])


def ctx_attention(q, k, v):
    B, n, C = q.shape
    blk = pl.BlockSpec((1, n, C), lambda b: (b, 0, 0))
    return pl.pallas_call(
        _ctx_attn_kernel, out_shape=jax.ShapeDtypeStruct((B, n, C), jnp.float32), grid=(B,),
        in_specs=[blk, blk, blk], out_specs=blk,
        compiler_params=_cparams("parallel"),
        name="ctx_attention",
    )(q, k, v)


def na_mix(proj_x, proj_c, qn_g, kn_g, rpb, need_ctx, tm):
    q, k, v = na_qknorm(proj_x, qn_g, kn_g, tm)
    qc, kc, vc = na_qknorm(proj_c, qn_g, kn_g, tm)
    out_x = na_attention(q, k, v, kc, vc, na_bias_table(rpb))
    out_c = ctx_attention(qc, kc, vc) if need_ctx else None
    return out_x, out_c


def rope_tables(n_tokens):
    t = jnp.arange(n_tokens)
    pos = jnp.stack([t // GRID_W, t % GRID_W], axis=-1).astype(jnp.float32)
    inv = ROPE_BASE ** (-jnp.arange(ROPE_ROT, dtype=jnp.float32) / ROPE_ROT)
    ang = pos[:, :, None] * inv
    cos_h = jnp.concatenate([jnp.cos(ang), jnp.cos(ang)], axis=-1).reshape(n_tokens, MLSTM_DQK)
    sin_h = jnp.concatenate([-jnp.sin(ang), jnp.sin(ang)], axis=-1).reshape(n_tokens, MLSTM_DQK)
    col = np.arange(MLSTM_QK)
    partner = np.where((col % (2 * ROPE_ROT)) < ROPE_ROT, col + ROPE_ROT, col - ROPE_ROT)
    perm = np.zeros((MLSTM_QK, MLSTM_QK), np.float32)
    perm[partner, col] = 1.0
    return jnp.tile(cos_h, (1, MLSTM_HEADS)), jnp.tile(sin_h, (1, MLSTM_HEADS)), jnp.asarray(perm, BF)


def _mlstm_prep_kernel(z_ref, gp_ref, ib_ref, fb_ref, *rest, rope):
    if rope:
        cos_ref, sin_ref, perm_ref, q_ref, k_ref, v_ref, g_ref = rest
    else:
        q_ref, k_ref, v_ref, g_ref = rest
    z = z_ref[0].astype(jnp.float32)
    q = z[:, 0:MLSTM_QK]
    k = z[:, MLSTM_QK:2 * MLSTM_QK]
    if rope:
        perm = perm_ref[...]
        q = q * cos_ref[...] + _split_dot(q, perm, 3) * sin_ref[...]
        k = k * cos_ref[...] + _split_dot(k, perm, 3) * sin_ref[...]
    q_ref[0] = (q * MLSTM_DQK ** -0.5).astype(BF)
    k_ref[0] = k.astype(BF)
    v_ref[0] = z[:, 2 * MLSTM_QK:2 * MLSTM_QK + BRANCH_WIDTH].astype(BF)
    gp = gp_ref[0].astype(jnp.float32)
    lane = lax.broadcasted_iota(jnp.int32, gp.shape, 1)
    ig = GATE_CAP * jnp.tanh((gp + ib_ref[...]) / GATE_CAP)
    fg = GATE_CAP * jnp.tanh((gp + fb_ref[...]) / GATE_CAP)
    g_ref[0] = jnp.where(lane < N_DIR * MLSTM_HEADS, ig, jax.nn.log_sigmoid(fg))


def mlstm_prep(proj, i_bias, f_bias, rope, tm):
    B, T, _ = proj.shape
    tm = min(tm, T)
    ng = N_DIR * MLSTM_HEADS
    ib = jnp.zeros((1, V7X_LANES), jnp.float32).at[0, 0:ng].set(i_bias.reshape(-1))
    fb = jnp.zeros((1, V7X_LANES), jnp.float32).at[0, ng:2 * ng].set(f_bias.reshape(-1))
    tok = lambda w: pl.BlockSpec((1, tm, w), lambda b, i: (b, i, 0))
    par = pl.BlockSpec((1, V7X_LANES), lambda b, i: (0, 0))
    args = [proj, proj, ib, fb]
    specs = [pl.BlockSpec((1, tm, MLSTM_MAIN_COLS), lambda b, i: (b, i, ML_BLK)),
             pl.BlockSpec((1, tm, V7X_LANES), lambda b, i: (b, i, ML_GATE_BLK)), par, par]
    if rope is not None:
        args += list(rope)
        specs += [pl.BlockSpec((tm, MLSTM_QK), lambda b, i: (i, 0)), pl.BlockSpec((tm, MLSTM_QK), lambda b, i: (i, 0)),
                  pl.BlockSpec((MLSTM_QK, MLSTM_QK), lambda b, i: (0, 0))]
    return pl.pallas_call(
        functools.partial(_mlstm_prep_kernel, rope=rope is not None),
        out_shape=(jax.ShapeDtypeStruct((B, T, MLSTM_QK), BF), jax.ShapeDtypeStruct((B, T, MLSTM_QK), BF),
                   jax.ShapeDtypeStruct((B, T, BRANCH_WIDTH), BF), jax.ShapeDtypeStruct((B, T, V7X_LANES), jnp.float32)),
        grid=(B, T // tm), in_specs=specs,
        out_specs=(tok(MLSTM_QK), tok(MLSTM_QK), tok(BRANCH_WIDTH), tok(V7X_LANES)),
        compiler_params=_cparams("parallel", "parallel"),
        name="mlstm_prep",
    )(*args)


def _mlstm_kernel(q_ref, k_ref, v_ref, gr_ref, c0_ref, n0_ref, m0_ref, *rest, reverse, bb, has_prev):
    if has_prev:
        hprev_ref, h_ref, cout_ref, nout_ref, mout_ref, c_scr, n_scr, m_scr = rest
    else:
        h_ref, cout_ref, nout_ref, mout_ref, c_scr, n_scr, m_scr = rest
    L = MLSTM_CHUNK
    DV = MLSTM_DV
    H = MLSTM_HEADS
    SL = MLSTM_GATE_SLOTS
    cidx = pl.program_id(1)

    @pl.when(cidx == 0)
    def _():
        c_scr[...] = c0_ref[...]
        n_scr[...] = n0_ref[...]
        m_scr[...] = m0_ref[...]

    ti = lax.broadcasted_iota(jnp.int32, (L, L), 0)
    si = lax.broadcasted_iota(jnp.int32, (L, L), 1)
    before = (si >= ti) if reverse else (si <= ti)
    tri = before.astype(jnp.float32)
    last = 0 if reverse else L - 1
    lo = _lo_lanes(L)
    lo_row = lax.broadcasted_iota(jnp.int32, (1, HP), 1) < (HP // 2)
    lo_col = lax.broadcasted_iota(jnp.int32, (HP, 1), 0) < (HP // 2)
    units = [(bi, p) for bi in range(bb) for p in range(H // 2)]
    blocks = [(bi, h) for bi in range(bb) for h in range(H)]
    nb = len(blocks)
    cat = lambda xs: jnp.concatenate(xs, axis=0)
    row_of = lambda bi, h: bi * SL + h

    i_rows = cat([gr_ref[bi, 0, 0:SL, :] for bi in range(bb)])
    lf_rows = cat([gr_ref[bi, 0, SL:2 * SL, :] for bi in range(bb)])
    m_prev = cat([m_scr[bi] for bi in range(bb)])
    b_rows = lax.dot_general(lf_rows, tri, (((1,), (1,)), ((), ())), precision=HIGHEST, preferred_element_type=jnp.float32)
    g_rows = i_rows - b_rows
    run = g_rows
    neg = jnp.full_like(g_rows, -jnp.inf)
    step = 1
    while step < L:
        shifted = (jnp.concatenate([run[:, step:], neg[:, :step]], axis=1) if reverse
                   else jnp.concatenate([neg[:, :step], run[:, :L - step]], axis=1))
        run = jnp.maximum(run, shifted)
        step *= 2
    m_rows = jnp.maximum(m_prev, run)
    m_end = jnp.broadcast_to(m_rows[:, last:last + 1], m_rows.shape)
    b_end = jnp.broadcast_to(b_rows[:, last:last + 1], b_rows.shape)
    sc_rows = jnp.exp(m_prev - m_rows)
    floor_rows = jnp.exp(-(b_rows + m_rows))
    wexp_rows = jnp.exp(g_rows - m_end)
    dec_rows = jnp.exp(m_prev - m_end)
    cols = cat([m_rows, sc_rows, floor_rows, wexp_rows]).T
    nr = bb * SL
    col_of = lambda kind, bi, h: cols[:, kind * nr + row_of(bi, h):kind * nr + row_of(bi, h) + 1]
    m_col = cat([col_of(0, bi, h) for bi, h in blocks])
    sc_col = cat([col_of(1, bi, h) for bi, h in blocks])
    floor_col = cat([col_of(2, bi, h) for bi, h in blocks])
    wexp_col = cat([col_of(3, bi, h) for bi, h in blocks])
    g_bcast = cat([jnp.broadcast_to(g_rows[row_of(bi, h):row_of(bi, h) + 1, :], (L, L)) for bi, h in blocks])
    decay = jnp.where(cat([before] * nb), jnp.exp(g_bcast - m_col), 0.0)

    qs = [_stack_pair(q_ref[bi, :, p * HP:(p + 1) * HP], lo) for bi, p in units]
    kp = [k_ref[bi, :, p * HP:(p + 1) * HP] for bi, p in units]
    vb = [v_ref[bi, :, h * DV:(h + 1) * DV] for bi, h in blocks]
    c_pair = [c_scr[bi, p] for bi, p in units]
    n_pair = [n_scr[bi, p] for bi, p in units]
    qk = cat([_dot_nt(qs[u], kp[u]) for u in range(len(units))])
    qc = cat([_bdot(qs[u], c_pair[u]) for u in range(len(units))])
    qn = cat([jnp.sum(qs[u].astype(jnp.float32) * n_pair[u], axis=1, keepdims=True) for u in range(len(units))])
    smat = qk * decay
    den = sc_col * qn + jnp.sum(smat, axis=1, keepdims=True)
    sb = smat.astype(BF)
    sv = cat([_bdot(sb[j * L:(j + 1) * L], vb[j]) for j in range(nb)])
    hout = (sc_col * qc + sv) / jnp.maximum(jnp.abs(den), floor_col)
    for j, (bi, h) in enumerate(blocks):
        hs = slice(h * DV, (h + 1) * DV)
        part = hout[j * L:(j + 1) * L]
        h_ref[bi, :, hs] = (part + hprev_ref[bi, :, hs]) if has_prev else part

    dec = lambda bi, h: dec_rows[row_of(bi, h):row_of(bi, h) + 1, 0:1]
    for u, (bi, p) in enumerate(units):
        j0 = bi * H + 2 * p
        kws = _stack_pair(kp[u], lo).astype(jnp.float32) * wexp_col[2 * u * L:2 * (u + 1) * L]
        upd = _dot_tn(kws, cat([vb[j0], vb[j0 + 1]]))
        c_scr[bi, p] = jnp.where(lo_col, dec(bi, 2 * p), dec(bi, 2 * p + 1)) * c_pair[u] + upd
        n_scr[bi, p] = jnp.where(lo_row, dec(bi, 2 * p), dec(bi, 2 * p + 1)) * n_pair[u] + jnp.sum(kws, axis=0, keepdims=True)
    m_next = b_end + m_end
    for bi in range(bb):
        m_scr[bi] = m_next[bi * SL:(bi + 1) * SL]

    @pl.when(cidx == pl.num_programs(1) - 1)
    def _():
        cout_ref[...] = c_scr[...]
        nout_ref[...] = n_scr[...]
        mout_ref[...] = m_scr[...]


def mlstm_chunked(q, k, v, gates, state, h_prev, direction):
    B, T, _ = q.shape
    L = MLSTM_CHUNK
    H = MLSTM_HEADS
    SL = MLSTM_GATE_SLOTS
    bb = MLSTM_SEQS_PER_STEP
    nc = T // L
    reverse = direction == 1
    ig = gates[:, :, direction * H:(direction + 1) * H]
    fg = gates[:, :, N_DIR * H + direction * H:N_DIR * H + (direction + 1) * H]
    zpad = jnp.zeros((B, T, SL - H), jnp.float32)
    g_rows = jnp.concatenate([ig, zpad, fg, zpad], axis=-1).reshape(B, nc, L, 2 * SL).transpose(0, 1, 3, 2)
    cm = (lambda c: nc - 1 - c) if reverse else (lambda c: c)
    tok = lambda w: pl.BlockSpec((bb, L, w), lambda b, c: (b, cm(c), 0))
    st = lambda a: pl.BlockSpec((bb,) + a.shape[1:], lambda b, c: (b,) + (0,) * (a.ndim - 1))
    c0, n0, m0 = state
    has_prev = h_prev is not None
    args = [q, k, v, g_rows, c0, n0, m0] + ([h_prev] if has_prev else [])
    outs = pl.pallas_call(
        functools.partial(_mlstm_kernel, reverse=reverse, bb=bb, has_prev=has_prev),
        out_shape=(jax.ShapeDtypeStruct((B, T, BRANCH_WIDTH), jnp.float32),) + tuple(
            jax.ShapeDtypeStruct(a.shape, jnp.float32) for a in state),
        grid=(B // bb, nc),
        in_specs=[tok(MLSTM_QK), tok(MLSTM_QK), tok(BRANCH_WIDTH),
                  pl.BlockSpec((bb, 1, 2 * SL, L), lambda b, c: (b, cm(c), 0, 0)), st(c0), st(n0), st(m0)]
                 + ([tok(BRANCH_WIDTH)] if has_prev else []),
        out_specs=(tok(BRANCH_WIDTH), st(c0), st(n0), st(m0)),
        scratch_shapes=[pltpu.VMEM((bb,) + a.shape[1:], jnp.float32) for a in state],
        compiler_params=_cparams("parallel", "arbitrary"),
        name="mlstm_chunked",
    )(*args)
    return outs[0], outs[1:]


def _mlstm_readout_tile(h, o, norm_g):
    parts = []
    for hd in range(MLSTM_HEADS):
        x = h[:, hd * MLSTM_DV:(hd + 1) * MLSTM_DV]
        parts.append(x * lax.rsqrt(jnp.mean(x * x, axis=1, keepdims=True) + RMS_EPS))
    return jnp.concatenate(parts, axis=1) * norm_g * jax.nn.sigmoid(o)


def mlstm_mix(proj_x, proj_c, i_bias, f_bias, rope, tm):
    B = proj_x.shape[0]
    qx, kx, vx, gx = mlstm_prep(proj_x, i_bias, f_bias, rope, tm)
    qc, kc, vc, gc = mlstm_prep(proj_c, i_bias, f_bias, None, tm)
    h_x = h_c = None
    for d in range(N_DIR):
        st0 = (jnp.zeros((B, MLSTM_HEADS // 2, HP, HP), jnp.float32), jnp.zeros((B, MLSTM_HEADS // 2, 1, HP), jnp.float32),
               jnp.zeros((B, MLSTM_GATE_SLOTS, MLSTM_CHUNK), jnp.float32))
        h_c, st_ctx = mlstm_chunked(qc, kc, vc, gc, st0, h_c, d)
        h_x, _ = mlstm_chunked(qx, kx, vx, gx, st_ctx, h_x, d)
    return h_x, h_c


def _merge_kernel(y_ref, r_ref, k_ref, v_ref, g_ref, ag0_ref, ag1_ref, yb_ref, h_ref, og_ref, ga_ref, gb_ref, gc_ref,
                  x_ref, mod_ref, ka_ref, rk_ref, lg_ref, lb_ref, ones_ref, ng_ref, wb_ref, wo_ref, o_ref):
    f32 = lambda ref: ref[0].astype(jnp.float32)
    ya = _rwkv_readout_tile(y_ref[0], f32(r_ref), f32(k_ref), f32(v_ref), f32(g_ref), f32(ag0_ref), f32(ag1_ref),
                            ka_ref[...], rk_ref[...], lg_ref[...], lb_ref[...], ones_ref[...])
    yc = _mlstm_readout_tile(h_ref[0], f32(og_ref), ng_ref[...])
    merged = None
    for i, (y, gate_ref) in enumerate(((ya, ga_ref), (yb_ref[0], gb_ref), (yc, gc_ref))):
        t = jax.nn.sigmoid(f32(gate_ref)) * _bdot(y, wb_ref[i])
        merged = t if merged is None else merged + t
    o_ref[0] = x_ref[0] + mod_ref[0, 2:3, :] * _bdot(merged, wo_ref[...])


def merge_apply(rw, yb, h_ml, proj, x, mod, rw_params, ml_norm_g, w_branch, w_out, tm):
    B, T, D = x.shape
    tm = min(tm, T)
    C = BRANCH_WIDTH
    tok = lambda w: pl.BlockSpec((1, tm, w), lambda b, i: (b, i, 0))
    gate = lambda k: pl.BlockSpec((1, tm, D), lambda b, i: (b, i, GATE_BLK0 + k))
    par = pl.BlockSpec((1, C), lambda b, i: (0, 0))
    params = [p.reshape(1, C) for p in rw_params]
    return pl.pallas_call(
        _merge_kernel, out_shape=jax.ShapeDtypeStruct((B, T, D), jnp.float32), grid=(B, T // tm),
        in_specs=[tok(C)] * 9 + [pl.BlockSpec((1, tm, C), lambda b, i: (b, i, ML_OGATE_BLK)), gate(0), gate(1), gate(2), tok(D),
                  pl.BlockSpec((1, 6, D), lambda b, i: (b, 0, 0))] + [par] * 4
                 + [pl.BlockSpec((C, C), lambda b, i: (0, 0)), par,
                    pl.BlockSpec(w_branch.shape, lambda b, i: (0, 0, 0)), pl.BlockSpec(w_out.shape, lambda b, i: (0, 0))],
        out_specs=tok(D),
        compiler_params=_cparams("parallel", "parallel"),
        name="merge_branches",
    )(*rw, yb, h_ml, proj, proj, proj, proj, x, mod, *params, _head_ones(C, RWKV_HEAD), ml_norm_g.reshape(1, C),
      w_branch, w_out)


def _route_kernel(x_ref, g_ref, mod_ref, wr_ref, rb_ref, h_ref, gate_ref):
    h = _norm_mod(x_ref[0], g_ref[...], mod_ref[0, 4:5, :], mod_ref[0, 3:4, :])
    h_ref[0] = h.astype(BF)
    tm = h.shape[0]
    logits = lax.dot_general(wr_ref[...], h, (((1,), (1,)), ((), ())), precision=HIGHEST, preferred_element_type=jnp.float32)
    scores = jax.nn.sigmoid(logits)
    sel = scores + rb_ref[...]
    gsz = N_EXPERTS // N_GROUPS
    grp = sel.reshape(N_GROUPS, gsz, tm)
    iota_in = lax.broadcasted_iota(jnp.int32, grp.shape, 1)
    m1 = jnp.max(grp, axis=1, keepdims=True)
    first = jnp.min(jnp.where(grp == m1, iota_in, gsz), axis=1, keepdims=True)
    m2 = jnp.max(jnp.where(iota_in == first, -jnp.inf, grp), axis=1, keepdims=True)
    gscore = (m1 + m2).reshape(N_GROUPS, tm)
    gi = lax.broadcasted_iota(jnp.int32, (N_GROUPS, tm), 0)
    rank = jnp.zeros((N_GROUPS, tm), jnp.int32)
    for g2 in range(N_GROUPS):
        other = gscore[g2:g2 + 1, :]
        rank = rank + ((other > gscore) | ((other == gscore) & (g2 < gi))).astype(jnp.int32)
    gmask = rank < TOPK_GROUPS
    emask = jnp.broadcast_to(gmask.reshape(N_GROUPS, 1, tm), (N_GROUPS, gsz, tm)).reshape(N_EXPERTS, tm)
    cand = jnp.where(emask, sel, NEG_INF)
    ei = lax.broadcasted_iota(jnp.int32, (N_EXPERTS, tm), 0)
    chosen = jnp.zeros((N_EXPERTS, tm), jnp.bool_)
    for _ in range(TOP_K):
        mx = jnp.max(cand, axis=0, keepdims=True)
        idx = jnp.min(jnp.where(cand == mx, ei, N_EXPERTS), axis=0, keepdims=True)
        hit = ei == idx
        chosen = chosen | hit
        cand = jnp.where(hit, -jnp.inf, cand)
    w = jnp.where(chosen, scores, 0.0)
    w = w / jnp.sum(w, axis=0, keepdims=True) * ROUTED_SCALE
    gate_ref[0] = jnp.concatenate([w, jnp.zeros((V7X_LANES - N_EXPERTS, tm), jnp.float32)], axis=0).T


def moe_route(x, g, mod, w_router, router_bias, tm):
    B, T, D = x.shape
    tm = min(tm, T)
    tok = pl.BlockSpec((1, tm, D), lambda b, i: (b, i, 0))
    return pl.pallas_call(
        _route_kernel,
        out_shape=(jax.ShapeDtypeStruct((B, T, D), BF), jax.ShapeDtypeStruct((B, T, V7X_LANES), jnp.float32)),
        grid=(B, T // tm),
        in_specs=[tok, pl.BlockSpec((1, D), lambda b, i: (0, 0)), pl.BlockSpec((1, 6, D), lambda b, i: (b, 0, 0)),
                  pl.BlockSpec((N_EXPERTS, D), lambda b, i: (0, 0)), pl.BlockSpec((N_EXPERTS, 1), lambda b, i: (0, 0))],
        out_specs=(tok, pl.BlockSpec((1, tm, V7X_LANES), lambda b, i: (b, i, 0))),
        compiler_params=_cparams("parallel", "parallel"),
        name="moe_route",
    )(x, g.reshape(1, D), mod, w_router.T, router_bias.reshape(N_EXPERTS, 1))


def _moe_kernel(h_ref, gate_ref, x_ref, mod_ref, sel_ref, wg_ref, wu_ref, wd_ref, sg_ref, su_ref, sd_ref, o_ref, *, tm):
    j = pl.program_id(1)
    rows = pl.ds(pl.multiple_of(pl.program_id(2) * tm, tm), tm)
    h = h_ref[0]

    @pl.when(j == 0)
    def _():
        sh = jax.nn.silu(_bdot(h, sg_ref[...])) * _bdot(h, su_ref[...])
        o_ref[0, rows, :] = _bdot(sh, sd_ref[...])

    g8 = _split_dot(gate_ref[0], sel_ref[0], 2)
    act = jax.nn.silu(_bdot(h, wg_ref[...])) * _bdot(h, wu_ref[...])
    act = jnp.concatenate([act[:, e * D_EXPERT:(e + 1) * D_EXPERT] * g8[:, e:e + 1] for e in range(MOE_STEP_EXPERTS)],
                          axis=1)
    o_ref[0, rows, :] += _bdot(act, wd_ref[...])

    @pl.when(j == pl.num_programs(1) - 1)
    def _():
        o_ref[0, rows, :] = x_ref[0] + mod_ref[0, 5:6, :] * o_ref[0, rows, :]


def _moe_select_table():
    se = MOE_STEP_EXPERTS
    t = np.zeros((N_EXPERTS // se, V7X_LANES, V7X_LANES), np.float32)
    for j in range(N_EXPERTS // se):
        for e in range(se):
            t[j, j * se + e, e] = 1.0
    return jnp.asarray(t, BF)


def moe_apply(h2, gates, x, mod, wg, wu, wd, sg, su, sd, tm):
    B, T, D = x.shape
    sw = MOE_STEP_EXPERTS * D_EXPERT
    n_groups = N_EXPERTS // MOE_STEP_EXPERTS
    tok = pl.BlockSpec((1, tm, D), lambda b, j, i: (b, i, 0))
    x_last = pl.BlockSpec((1, tm, D), lambda b, j, i: (b, jnp.where(j == n_groups - 1, i, 0), 0))
    full = lambda a: pl.BlockSpec(a.shape, lambda b, j, i: (0,) * a.ndim)
    return pl.pallas_call(
        functools.partial(_moe_kernel, tm=tm), out_shape=jax.ShapeDtypeStruct((B, T, D), jnp.float32),
        grid=(B, n_groups, T // tm),
        in_specs=[tok, pl.BlockSpec((1, tm, V7X_LANES), lambda b, j, i: (b, i, 0)), x_last,
                  pl.BlockSpec((1, 6, D), lambda b, j, i: (b, 0, 0)),
                  pl.BlockSpec((1, V7X_LANES, V7X_LANES), lambda b, j, i: (j, 0, 0)),
                  pl.BlockSpec((D, sw), lambda b, j, i: (0, j)), pl.BlockSpec((D, sw), lambda b, j, i: (0, j)),
                  pl.BlockSpec((sw, D), lambda b, j, i: (j, 0)), full(sg), full(su), full(sd)],
        out_specs=pl.BlockSpec((1, T, D), lambda b, j, i: (b, 0, 0)),
        compiler_params=pltpu.CompilerParams(dimension_semantics=("parallel", "arbitrary", "arbitrary"),
                                             vmem_limit_bytes=MOE_VMEM_LIMIT_BYTES),
        name="moe_experts",
    )(h2, gates, x, mod, _moe_select_table(), wg, wu, wd, sg, su, sd)


def _reorder_w_in(w):
    o_na = RWKV_COLS
    o_ml = o_na + NA_COLS
    o_mg = o_ml + MLSTM_MAIN_COLS
    o_gate = o_mg + MLSTM_GATES
    pad = jnp.zeros((w.shape[0], V7X_LANES - MLSTM_GATES), w.dtype)
    return jnp.concatenate([w[:, o_na:o_ml], w[:, o_ml:o_mg], w[:, o_gate:], w[:, :RWKV_COLS], w[:, o_mg:o_gate], pad],
                           axis=1).astype(BF)


def kernel(x, c, ctx, c_ctx, w_ada, b_ada, norm1_g, norm2_g, w_in, rw_mu, rw_w0, rw_w2, rw_a0, rw_a2, rw_k_k, rw_k_a, rw_r_k, rw_g2, rw_lnx_g, rw_lnx_b, rw_v0, rw_v1, rw_v2, na_qn_g, na_kn_g, na_rpb, ml_i_bias, ml_f_bias, ml_norm_g, w_branch, w_out, moe_router, moe_bias, moe_w_gate, moe_w_up, moe_w_down, sh_w_gate, sh_w_up, sh_w_down):
    B, S, D = x.shape
    n_ctx = ctx.shape[1]
    tm = TOKEN_TILE
    assert S % tm == 0 and n_ctx % min(tm, n_ctx) == 0 and PROJ_COLS == IN_COLS + V7X_LANES - MLSTM_GATES
    rope = rope_tables(S)
    n_cond = B + 1
    cond_pad = (-n_cond) % V7X_SUBLANES
    s_cond = jnp.pad(jnp.concatenate([jax.nn.silu(c), jax.nn.silu(c_ctx)[None]], axis=0), ((0, cond_pad), (0, 0)))
    vf_x = vf_c = None
    for l in range(DEPTH):
        need_ctx = l < DEPTH - 1
        mod = pmm(s_cond, w_ada[l]) + b_ada[l]
        mod_x = mod[:B].reshape(B, 6, D)
        mod_c = jnp.broadcast_to(mod[B].reshape(1, 6, D), (B, 6, D))
        w_proj = _reorder_w_in(w_in[l])
        proj_x = norm_proj(x, norm1_g[l], mod_x, w_proj)
        proj_c = norm_proj(ctx.reshape(1, B * n_ctx, D), norm1_g[l], mod_c[:1], w_proj).reshape(B, n_ctx, PROJ_COLS)
        vres = None if l == 0 else (rw_v0[l - 1], rw_v1[l - 1], rw_v2[l - 1])
        rw_x, rw_c, vf_x, vf_c = rwkv_mix(proj_x, proj_c, vf_x, vf_c, rw_mu[l], rw_w0[l], rw_w2[l], rw_a0[l], rw_a2[l],
                                          rw_k_k[l], rw_k_a[l], rw_g2[l], vres, tm)
        rw_params = (rw_k_a[l], rw_r_k[l], rw_lnx_g[l], rw_lnx_b[l])
        yb_x, yb_c = na_mix(proj_x, proj_c, na_qn_g[l], na_kn_g[l], na_rpb[l], need_ctx, tm)
        hm_x, hm_c = mlstm_mix(proj_x, proj_c, ml_i_bias[l], ml_f_bias[l], rope, tm)
        wb = w_branch[l].astype(BF)
        wo = w_out[l].astype(BF)
        wg = moe_w_gate[l].transpose(1, 0, 2).reshape(D, N_EXPERTS * D_EXPERT).astype(BF)
        wu = moe_w_up[l].transpose(1, 0, 2).reshape(D, N_EXPERTS * D_EXPERT).astype(BF)
        wd = moe_w_down[l].reshape(N_EXPERTS * D_EXPERT, D).astype(BF)
        shared = (sh_w_gate[l].astype(BF), sh_w_up[l].astype(BF), sh_w_down[l].astype(BF))
        x = merge_apply(rw_x, yb_x, hm_x, proj_x, x, mod_x, rw_params, ml_norm_g[l], wb, wo, tm)
        h2, gates = moe_route(x, norm2_g[l], mod_x, moe_router[l], moe_bias[l], tm)
        x = moe_apply(h2, gates, x, mod_x, wg, wu, wd, *shared, MOE_TOKEN_TILE)
        if need_ctx:
            ctx = merge_apply(rw_c, yb_c, hm_c, proj_c, ctx, mod_c, rw_params, ml_norm_g[l], wb, wo, tm)
            h2, gates = moe_route(ctx, norm2_g[l], mod_c, moe_router[l], moe_bias[l], tm)
            ctx = moe_apply(h2, gates, ctx, mod_c, wg, wu, wd, *shared, min(tm, n_ctx))
    return x
```

```python
import functools

import numpy as np
import jax
import jax.numpy as jnp
from jax import lax
from jax.experimental import pallas as pl
from jax.experimental.pallas import tpu as pltpu

D_MODEL = 1024
DEPTH = 2
GRID_W = 64
N_DIR = 2
N_BRANCH = 3
BRANCH_WIDTH = 512
RMS_EPS = 1e-6
NEG_INF = -1e30

RWKV_HEAD = 64
RWKV_DECAY_LORA = 64
RWKV_LORA_COLS = 384
RWKV_COLS = 3 * BRANCH_WIDTH + RWKV_LORA_COLS
RWKV_LNX_EPS = 64e-5
RWKV_DECAY_SCALE = float(np.exp(-0.5))
RWKV_CHUNK = 64
WKV_SEQS_PER_STEP = 8

NA_HEAD = 64
NA_HEADS = BRANCH_WIDTH // NA_HEAD
NA_WIN_R = 8
NA_WIN_C = 16
NA_COLS = 3 * BRANCH_WIDTH
NA_ROWS_PER_STEP = 4

MLSTM_HEADS = 4
MLSTM_DQK = 64
MLSTM_DV = BRANCH_WIDTH // MLSTM_HEADS
MLSTM_QK = MLSTM_HEADS * MLSTM_DQK
MLSTM_CHUNK = 64
MLSTM_SEQS_PER_STEP = 8
MLSTM_GATE_SLOTS = 8
MLSTM_MAIN_COLS = 2 * MLSTM_QK + 2 * BRANCH_WIDTH
MLSTM_GATES = 2 * N_DIR * MLSTM_HEADS
GATE_CAP = 15.0
ROPE_ROT = MLSTM_DQK // 4
ROPE_BASE = 10000.0

N_EXPERTS = 64
TOP_K = 8
N_GROUPS = 8
TOPK_GROUPS = 4
D_EXPERT = 128
ROUTED_SCALE = 2.5
MOE_STEP_EXPERTS = 8
MOE_TOKEN_TILE = 1024

GATE_COLS = N_BRANCH * D_MODEL
IN_COLS = RWKV_COLS + NA_COLS + MLSTM_MAIN_COLS + MLSTM_GATES + GATE_COLS

TOKEN_TILE = 512
V7X_LANES = 128
V7X_SUBLANES = 8
HP = 2 * RWKV_HEAD
VMEM_LIMIT_BYTES = 48 * 1024 * 1024
MOE_VMEM_LIMIT_BYTES = 56 * 1024 * 1024

PROJ_COLS = 8192
NA_BLK = 0
ML_BLK = 1
ML_OGATE_BLK = (NA_COLS + 2 * MLSTM_QK + BRANCH_WIDTH) // BRANCH_WIDTH
GATE_BLK0 = (NA_COLS + MLSTM_MAIN_COLS) // D_MODEL
RWKV_COL0 = NA_COLS + MLSTM_MAIN_COLS + GATE_COLS
RWKV_BLK0 = RWKV_COL0 // BRANCH_WIDTH
RWKV_LORA_BLK = (RWKV_COL0 + 3 * BRANCH_WIDTH) // RWKV_LORA_COLS
ML_GATE_BLK = (RWKV_COL0 + RWKV_COLS) // V7X_LANES

BF = jnp.bfloat16
ACT_DTYPE = BF
HALO_ROWS = 16
HIGHEST = lax.Precision.HIGHEST


def _cparams(*sem):
    return pltpu.CompilerParams(dimension_semantics=sem, vmem_limit_bytes=VMEM_LIMIT_BYTES)


def _bdot(a, b):
    return jnp.dot(a.astype(BF), b.astype(BF), preferred_element_type=jnp.float32)


def _dot_nt(a, b):
    return lax.dot_general(a.astype(BF), b.astype(BF), (((1,), (1,)), ((), ())), preferred_element_type=jnp.float32)


def _dot_tn(a, b):
    return lax.dot_general(a.astype(BF), b.astype(BF), (((0,), (0,)), ((), ())), preferred_element_type=jnp.float32)


def _split_dot(x, w, parts):
    out = None
    rem = x
    for _ in range(parts):
        piece = rem.astype(BF)
        rem = rem - piece.astype(jnp.float32)
        t = jnp.dot(piece, w, preferred_element_type=jnp.float32)
        out = t if out is None else out + t
    return out


def _head_ones(width, head):
    i = np.arange(width) // head
    return jnp.asarray(i[:, None] == i[None, :], BF)


def _stack_pair(x, lo):
    zero = jnp.zeros_like(x)
    return jnp.concatenate([jnp.where(lo, x, zero), jnp.where(lo, zero, x)], axis=0)


def _lo_lanes(n):
    return lax.broadcasted_iota(jnp.int32, (n, HP), 1) < (HP // 2)


def _mm_kernel(x_ref, w_ref, o_ref):
    part = _bdot(x_ref[...], w_ref[...])

    @pl.when(pl.program_id(2) == 0)
    def _():
        o_ref[...] = part

    @pl.when(pl.program_id(2) > 0)
    def _():
        o_ref[...] += part


def _pick_tile(n, cands):
    for c in cands:
        if n % c == 0:
            return c
    return n


def pmm(x, w):
    M, K = x.shape
    N = w.shape[1]
    tm = _pick_tile(M, (1024, 512, 256, 128, 64, 32, 16, 8))
    tn = _pick_tile(N, (1024, 512, 384, 256, 128))
    tk = _pick_tile(K, (1024,)) if K > 1024 else K
    return pl.pallas_call(
        _mm_kernel,
        out_shape=jax.ShapeDtypeStruct((M, N), jnp.float32),
        grid=(M // tm, N // tn, K // tk),
        in_specs=[pl.BlockSpec((tm, tk), lambda i, j, k: (i, k)),
                  pl.BlockSpec((tk, tn), lambda i, j, k: (k, j))],
        out_specs=pl.BlockSpec((tm, tn), lambda i, j, k: (i, j)),
        compiler_params=_cparams("parallel", "parallel", "arbitrary"),
        name="tiled_matmul",
    )(x, w)


def _norm_mod(x, g, scale, shift):
    xn = x * lax.rsqrt(jnp.mean(x * x, axis=-1, keepdims=True) + RMS_EPS)
    return xn * g * (1.0 + scale) + shift


def _norm_proj_kernel(x_ref, g_ref, mod_ref, w_ref, o_ref, h_scr):
    @pl.when(pl.program_id(2) == 0)
    def _():
        h_scr[...] = _norm_mod(x_ref[0], g_ref[...], mod_ref[0, 1:2, :], mod_ref[0, 0:1, :]).astype(BF)

    o_ref[0] = jnp.dot(h_scr[...], w_ref[...], preferred_element_type=jnp.float32).astype(o_ref.dtype)


def norm_proj(x, g, mod, w):
    B, T, D = x.shape
    N = w.shape[1]
    tm = _pick_tile(T, (1024, 512, 256))
    tn = _pick_tile(N, (1024, 512, 256, 128))
    return pl.pallas_call(
        _norm_proj_kernel,
        out_shape=jax.ShapeDtypeStruct((B, T, N), ACT_DTYPE), grid=(B, T // tm, N // tn),
        in_specs=[pl.BlockSpec((1, tm, D), lambda b, i, j: (b, i, 0)), pl.BlockSpec((1, D), lambda b, i, j: (0, 0)),
                  pl.BlockSpec((1, 6, D), lambda b, i, j: (b, 0, 0)), pl.BlockSpec((D, tn), lambda b, i, j: (0, j))],
        out_specs=pl.BlockSpec((1, tm, tn), lambda b, i, j: (b, i, j)),
        scratch_shapes=[pltpu.VMEM((tm, D), BF)],
        compiler_params=_cparams("parallel", "parallel", "arbitrary"),
        name="norm_proj",
    )(x, g.reshape(1, D), mod, w)


def _shifted(z, prev_row, next_row):
    n = z.shape[0]
    s = V7X_SUBLANES
    row = lax.broadcasted_iota(jnp.int32, (s, z.shape[1]), 0)
    zp = pltpu.roll(z, 1, axis=0)
    zn = pltpu.roll(z, n - 1, axis=0)
    zp = jnp.concatenate([jnp.where(row == 0, prev_row, zp[:s]), zp[s:]], axis=0)
    zn = jnp.concatenate([zn[:n - s], jnp.where(row == s - 1, next_row, zn[n - s:])], axis=0)
    return zp, zn


def _rwkv_feat_kernel(zr_ref, zk_ref, zv_ref, zl_ref, pr_ref, pk_ref, pv_ref, pl_ref, nr_ref, nk_ref, nv_ref, nl_ref,
                      mu_ref, w0_ref, w2_ref, a0_ref, a2_ref, kk_ref, g2_ref, ones_ref, *rest, has_vres):
    if has_vres:
        v0_ref, v1_ref, v2_ref, vf_ref = rest[:4]
        outs = rest[4:]
    else:
        outs = rest
    r_ref, k_ref, v_ref, kn_ref, g_ref, lw0_ref, lw1_ref, ag0_ref, ag1_ref = outs
    C = BRANCH_WIDTH
    first = pl.program_id(1) == 0
    last = pl.program_id(1) == pl.num_programs(1) - 1

    def shift(z_ref, p_ref, n_ref, c0, c1):
        z = z_ref[0].astype(jnp.float32)
        prev_row = jnp.where(first, 0.0, p_ref[0].astype(jnp.float32)[HALO_ROWS - 1:HALO_ROWS, :])
        next_row = jnp.where(last, 0.0, n_ref[0].astype(jnp.float32)[0:1, :])
        zp, zn = _shifted(z, prev_row, next_row)
        return z + mu_ref[:, c0:c1] * (0.5 * (zp + zn) - z)

    r = shift(zr_ref, pr_ref, nr_ref, 0, C)
    k = shift(zk_ref, pk_ref, nk_ref, C, 2 * C)
    v = shift(zv_ref, pv_ref, nv_ref, 2 * C, 3 * C)
    zl = shift(zl_ref, pl_ref, nl_ref, 3 * C, RWKV_COLS)
    wd = jnp.tanh(zl[:, 0:2 * RWKV_DECAY_LORA])
    ad = zl[:, 2 * RWKV_DECAY_LORA:4 * RWKV_DECAY_LORA]
    gd = zl[:, 4 * RWKV_DECAY_LORA:]
    for d, (lw_ref, ag_ref) in enumerate(((lw0_ref, ag0_ref), (lw1_ref, ag1_ref))):
        u = w0_ref[d:d + 1, :] + _bdot(wd, w2_ref[d])
        lw_ref[0] = -RWKV_DECAY_SCALE * jax.nn.sigmoid(u)
        ag_ref[0] = jax.nn.sigmoid(a0_ref[d:d + 1, :] + _bdot(ad, a2_ref[d])).astype(ag_ref.dtype)
    kq = k * kk_ref[...]
    ss = _split_dot(kq * kq, ones_ref[...], 2)
    kn_ref[0] = (kq * lax.rsqrt(jnp.maximum(ss, 1e-24))).astype(kn_ref.dtype)
    if has_vres:
        lora = _bdot(_bdot(v, v1_ref[...]), v2_ref[...])
        v = v + (vf_ref[0].astype(jnp.float32) - v) * jax.nn.sigmoid(v0_ref[...] + lora)
    g_ref[0] = _bdot(jax.nn.sigmoid(gd), g2_ref[...]).astype(g_ref.dtype)
    r_ref[0] = r.astype(r_ref.dtype)
    k_ref[0] = k.astype(k_ref.dtype)
    v_ref[0] = v.astype(v_ref.dtype)


def rwkv_features(proj, mu, w0, w2, a0, a2, k_k, g2, vres, v_first, tm):
    B, T, _ = proj.shape
    tm = min(tm, T)
    C = BRANCH_WIDTH
    nt = T // tm
    zpad = jnp.zeros((RWKV_DECAY_LORA, C), jnp.float32)
    pad_dirs = lambda w: jnp.stack([jnp.concatenate([w[0], zpad], 0), jnp.concatenate([zpad, w[1]], 0)]).astype(BF)
    sub = HALO_ROWS
    blk = lambda w, j: pl.BlockSpec((1, tm, w), lambda b, i: (b, i, j))
    before = lambda w, j: pl.BlockSpec((1, sub, w), lambda b, i: (b, jnp.maximum(i * (tm // sub) - 1, 0), j))
    after = lambda w, j: pl.BlockSpec((1, sub, w), lambda b, i: (b, jnp.minimum((i + 1) * (tm // sub), T // sub - 1), j))
    tok = pl.BlockSpec((1, tm, C), lambda b, i: (b, i, 0))
    full = lambda a: pl.BlockSpec(a.shape, lambda b, i: (0,) * a.ndim)
    params = [mu.reshape(1, -1), w0, pad_dirs(w2), a0, pad_dirs(a2), k_k.reshape(1, -1), g2.astype(BF),
              _head_ones(C, RWKV_HEAD)]
    pieces = [(C, RWKV_BLK0), (C, RWKV_BLK0 + 1), (C, RWKV_BLK0 + 2), (RWKV_LORA_COLS, RWKV_LORA_BLK)]
    args = [proj] * (3 * len(pieces)) + params
    specs = ([blk(w, j) for w, j in pieces] + [before(w, j) for w, j in pieces] + [after(w, j) for w, j in pieces]
             + [full(a) for a in params])
    if vres is not None:
        v0, v1, v2 = vres
        extra = [v0.reshape(1, -1), jnp.pad(v1, ((0, 0), (0, V7X_LANES - v1.shape[1]))).astype(BF),
                 jnp.pad(v2, ((0, V7X_LANES - v2.shape[0]), (0, 0))).astype(BF)]
        args += extra + [v_first]
        specs += [full(a) for a in extra] + [tok]
    return pl.pallas_call(
        functools.partial(_rwkv_feat_kernel, has_vres=vres is not None),
        out_shape=(jax.ShapeDtypeStruct((B, T, C), ACT_DTYPE),) * 5 + (jax.ShapeDtypeStruct((B, T, C), jnp.float32),) * 2
                  + (jax.ShapeDtypeStruct((B, T, C), ACT_DTYPE),) * 2,
        grid=(B, nt), in_specs=specs, out_specs=(tok,) * 9,
        compiler_params=_cparams("parallel", "parallel"),
        name="rwkv_features",
    )(*args)


def _wkv_kernel(r_ref, lw_ref, kk_ref, a_ref, k_ref, v_ref, ka_ref, s0_ref, *rest, reverse, bb, n_pairs, has_prev):
    if has_prev:
        yprev_ref, y_ref, sout_ref, s_scr = rest
    else:
        y_ref, sout_ref, s_scr = rest
    C = RWKV_CHUNK
    c_idx = pl.program_id(1)

    @pl.when(c_idx == 0)
    def _():
        s_scr[...] = s0_ref[...]

    ti = lax.broadcasted_iota(jnp.int32, (C, C), 0)
    si = lax.broadcasted_iota(jnp.int32, (C, C), 1)
    tri = ((si >= ti) if reverse else (si <= ti)).astype(jnp.float32)
    tp = lax.broadcasted_iota(jnp.int32, (C, 2 * C), 0)
    sp = lax.broadcasted_iota(jnp.int32, (C, 2 * C), 1) % C
    m_strict = (sp > tp) if reverse else (sp < tp)
    m_incl = (sp >= tp) if reverse else (sp <= tp)
    eye = (tp == sp).astype(jnp.float32)
    t2 = lax.broadcasted_iota(jnp.int32, (2 * C, 2 * C), 0)
    s2 = lax.broadcasted_iota(jnp.int32, (2 * C, 2 * C), 1)
    same_head = (t2 // C) == (s2 // C)
    lo = _lo_lanes(C)
    units = [(bi, slice(p * HP, (p + 1) * HP), p) for bi in range(bb) for p in range(n_pairs)]
    n = len(units)
    cat = lambda xs: jnp.concatenate(xs, axis=0)
    stack = lambda x: _stack_pair(x, lo)
    bdiag = lambda x: jnp.where(same_head, cat([x, x]), jnp.zeros((), x.dtype))

    ar, bk, bkh, v, e_tot = [], [], [], [], []
    for bi, sl, _ in units:
        lw = lw_ref[bi, :, sl]
        kk = kk_ref[bi, :, sl].astype(jnp.float32)
        ag = a_ref[bi, :, sl].astype(jnp.float32)
        kd = k_ref[bi, :, sl].astype(jnp.float32) * (1.0 + (ag - 1.0) * ka_ref[:, sl])
        cum = jnp.dot(tri, lw, precision=HIGHEST, preferred_element_type=jnp.float32)
        tot = jnp.sum(lw, axis=0, keepdims=True)
        e_neg = jnp.exp(-cum)
        e_end = jnp.exp(tot - cum)
        b = kk * ag
        ar.append(cat([-kk * jnp.exp(cum - lw), r_ref[bi, :, sl].astype(jnp.float32) * jnp.exp(cum)]).astype(BF))
        bk.append(cat([stack(b * e_neg), stack(kd * e_neg)]).astype(BF))
        bkh.append(cat([b * e_end, kd * e_end]).astype(BF))
        v.append(v_ref[bi, :, sl].astype(BF))
        e_tot.append(jnp.exp(tot))
    gram = [_dot_nt(ar[i], bk[i]) for i in range(n)]
    l_ab = [jnp.where(m_strict, g[:C, :2 * C], 0.0) for g in gram]
    l_ak = [jnp.where(m_strict, g[:C, 2 * C:], 0.0).astype(BF) for g in gram]
    l_rbk = [jnp.concatenate([jnp.where(m_incl, g[C:, :2 * C], 0.0), jnp.where(m_incl, g[C:, 2 * C:], 0.0)],
                             axis=1).astype(BF) for g in gram]
    vs = [stack(x) for x in v]
    s0 = [s_scr[bi, p] for bi, _, p in units]
    proj = [_dot_nt(ar[i], s0[i]) for i in range(n)]
    lv = [_bdot(l_ak[i], vs[i]) for i in range(n)]
    tinv = [eye + m for m in l_ab]
    pw_bd = [bdiag(m.astype(BF)) for m in l_ab]
    pw = [_bdot(l_ab[i], pw_bd[i]).astype(BF) for i in range(n)]
    levels = 5
    for lvl in range(1, levels + 1):
        pw_bd = [bdiag(m) for m in pw]
        if lvl < levels:
            both = [_bdot(cat([pw[i], tinv[i].astype(BF)]), pw_bd[i]) for i in range(n)]
            pw = [x[:C].astype(BF) for x in both]
            tinv = [tinv[i] + both[i][C:] for i in range(n)]
        else:
            tinv = [tinv[i] + _bdot(tinv[i], pw_bd[i]) for i in range(n)]
    u = [_bdot(tinv[i], stack((proj[i][:C] + lv[i]).astype(BF))) for i in range(n)]
    ub = [x.astype(BF) for x in u]
    ys = [proj[i][C:] + _bdot(l_rbk[i], cat([stack(ub[i]), vs[i]])) for i in range(n)]
    upd = [_dot_tn(cat([ub[i], v[i]]), bkh[i]) for i in range(n)]
    for i, (bi, sl, p) in enumerate(units):
        y = ys[i]
        if has_prev:
            y = y + yprev_ref[bi, :, sl]
        y_ref[bi, :, sl] = y
        s_scr[bi, p] = s0[i] * e_tot[i] + jnp.where(same_head, upd[i], 0.0)

    @pl.when(c_idx == pl.num_programs(1) - 1)
    def _():
        sout_ref[...] = s_scr[...]


def wkv_chunked(r, lw, kk, ag, k, v, k_a, s0, y_prev, reverse):
    B, T, W = r.shape
    C = RWKV_CHUNK
    bb = WKV_SEQS_PER_STEP
    nc = T // C
    n_pairs = W // HP
    cmap = (lambda b, c: (b, nc - 1 - c, 0)) if reverse else (lambda b, c: (b, c, 0))
    tok = pl.BlockSpec((bb, C, W), cmap)
    st = pl.BlockSpec((bb, n_pairs, HP, HP), lambda b, c: (b, 0, 0, 0))
    has_prev = y_prev is not None
    args = [r, lw, kk, ag, k, v, k_a, s0] + ([y_prev] if has_prev else [])
    return pl.pallas_call(
        functools.partial(_wkv_kernel, reverse=reverse, bb=bb, n_pairs=n_pairs, has_prev=has_prev),
        out_shape=(jax.ShapeDtypeStruct((B, T, W), jnp.float32), jax.ShapeDtypeStruct(s0.shape, jnp.float32)),
        grid=(B // bb, nc),
        in_specs=[tok] * 6 + [pl.BlockSpec((1, W), lambda b, c: (0, 0)), st] + ([tok] if has_prev else []),
        out_specs=(tok, st),
        scratch_shapes=[pltpu.VMEM((bb, n_pairs, HP, HP), jnp.float32)],
        compiler_params=_cparams("parallel", "arbitrary"),
        name="wkv_chunked",
    )(*args)


def _rwkv_readout_tile(y, r, k, v, g, ag0, ag1, ka, rk, lnx_g, lnx_b, ones_bd):
    mean = _split_dot(y, ones_bd, 2) * (1.0 / RWKV_HEAD)
    yc = y - mean
    var = _split_dot(yc * yc, ones_bd, 2) * (1.0 / RWKV_HEAD)
    yn = yc * lax.rsqrt(var + RWKV_LNX_EPS) * lnx_g + lnx_b
    ksum = k * (2.0 + (ag0 + ag1 - 2.0) * ka)
    bonus = _split_dot(r * ksum * rk, ones_bd, 2) * v
    return (yn + bonus) * g


def rwkv_mix(proj_x, proj_c, vf_x, vf_c, mu, w0, w2, a0, a2, k_k, k_a, g2, vres, tm):
    B = proj_x.shape[0]
    fx = rwkv_features(proj_x, mu, w0, w2, a0, a2, k_k, g2, vres, vf_x, tm)
    fc = rwkv_features(proj_c, mu, w0, w2, a0, a2, k_k, g2, vres, vf_c, tm)
    ka = k_a.reshape(1, -1)
    y_x = y_c = None
    for d in range(N_DIR):
        s0 = jnp.zeros((B, BRANCH_WIDTH // HP, HP, HP), jnp.float32)
        y_c, s_ctx = wkv_chunked(fc[0], fc[5 + d], fc[3], fc[7 + d], fc[1], fc[2], ka, s0, y_c, d == 1)
        y_x, _ = wkv_chunked(fx[0], fx[5 + d], fx[3], fx[7 + d], fx[1], fx[2], ka, s_ctx, y_x, d == 1)
    pick = lambda y, f: (y, f[0], f[1], f[2], f[4], f[7], f[8])
    vf_x = fx[2] if vres is None else vf_x
    vf_c = fc[2] if vres is None else vf_c
    return pick(y_x, fx), pick(y_c, fc), vf_x, vf_c


def _qknorm_kernel(z_ref, qg_ref, kg_ref, ones_ref, q_ref, k_ref, v_ref):
    ones_bd = ones_ref[...]
    C = BRANCH_WIDTH
    z = z_ref[0].astype(jnp.float32)
    q = z[:, 0:C]
    k = z[:, C:2 * C]
    qn = q * lax.rsqrt(_split_dot(q * q, ones_bd, 2) * (1.0 / NA_HEAD) + RMS_EPS) * qg_ref[...]
    kn = k * lax.rsqrt(_split_dot(k * k, ones_bd, 2) * (1.0 / NA_HEAD) + RMS_EPS) * kg_ref[...]
    q_ref[0] = (qn * NA_HEAD ** -0.5).astype(BF)
    k_ref[0] = kn.astype(BF)
    v_ref[0] = z[:, 2 * C:3 * C].astype(BF)


def na_qknorm(proj, qn_g, kn_g, tm):
    B, T, _ = proj.shape
    tm = min(tm, T)
    C = BRANCH_WIDTH
    tok = pl.BlockSpec((1, tm, C), lambda b, i: (b, i, 0))
    par = pl.BlockSpec((1, C), lambda b, i: (0, 0))
    sd = jax.ShapeDtypeStruct((B, T, C), BF)
    return pl.pallas_call(
        _qknorm_kernel, out_shape=(sd, sd, sd), grid=(B, T // tm),
        in_specs=[pl.BlockSpec((1, tm, NA_COLS), lambda b, i: (b, i, NA_BLK)), par, par,
                  pl.BlockSpec((C, C), lambda b, i: (0, 0))],
        out_specs=(tok, tok, tok),
        compiler_params=_cparams("parallel", "parallel"),
        name="na_qknorm",
    )(proj, jnp.tile(qn_g, NA_HEADS).reshape(1, C), jnp.tile(kn_g, NA_HEADS).reshape(1, C), _head_ones(C, NA_HEAD))


def na_bias_table(rpb):
    qc = np.arange(GRID_W)[:, None]
    kc = np.arange(GRID_W)[None, :]
    cs = np.clip(qc - NA_WIN_C // 2, 0, GRID_W - NA_WIN_C)
    valid = (kc >= cs) & (kc < cs + NA_WIN_C)
    cidx = np.clip(kc - qc + NA_WIN_C - 1, 0, 2 * NA_WIN_C - 2)
    t = jnp.where(valid[None, None], rpb[:, :, cidx], NEG_INF)
    t2 = jnp.concatenate([t[:, :-1], t[:, 1:]], axis=-1)
    H = rpb.shape[0]
    t2 = t2.reshape(H // 2, 2, 2 * NA_WIN_R - 2, GRID_W, 2 * GRID_W).transpose(0, 2, 1, 3, 4)
    return t2.reshape(H // 2, 2 * NA_WIN_R - 2, 2 * GRID_W, 2 * GRID_W)


def _na_kernel(q_ref, k_ref, v_ref, kc_ref, vc_ref, bias_ref, o_ref, *, rows):
    nwin = NA_WIN_R * GRID_W
    lo = _lo_lanes(GRID_W)
    units = []
    for rr in range(NA_ROWS_PER_STEP):
        r = pl.program_id(1) * NA_ROWS_PER_STEP + rr
        rs = jnp.clip(r - NA_WIN_R // 2, 0, rows - NA_WIN_R)
        k0 = pl.multiple_of(rs * GRID_W, GRID_W)
        for p in range(BRANCH_WIDTH // HP):
            units.append((rr, p, slice(p * HP, (p + 1) * HP), rs - r + NA_WIN_R - 1, k0))
    qs = [_stack_pair(q_ref[0, rr * GRID_W:(rr + 1) * GRID_W, sl], lo) for rr, _, sl, _, _ in units]
    s_loc = [_dot_nt(qs[i], k_ref[0, pl.ds(u[4], nwin), u[2]]) for i, u in enumerate(units)]
    s_ctx = [_dot_nt(qs[i], kc_ref[0, :, u[2]]) for i, u in enumerate(units)]
    p_loc, p_ctx, den = [], [], []
    for i, (_, p, _, base, _) in enumerate(units):
        sl_b = s_loc[i] + jnp.concatenate([bias_ref[p, base + 2 * j] for j in range(NA_WIN_R // 2)], axis=1)
        m = jnp.maximum(jnp.max(sl_b, axis=1, keepdims=True), jnp.max(s_ctx[i], axis=1, keepdims=True))
        el = jnp.exp(sl_b - m)
        ec = jnp.exp(s_ctx[i] - m)
        den.append(jnp.sum(el, axis=1, keepdims=True) + jnp.sum(ec, axis=1, keepdims=True))
        p_loc.append(el.astype(BF))
        p_ctx.append(ec.astype(BF))
    o_loc = [_bdot(p_loc[i], v_ref[0, pl.ds(u[4], nwin), u[2]]) for i, u in enumerate(units)]
    o_ctx = [_bdot(p_ctx[i], vc_ref[0, :, u[2]]) for i, u in enumerate(units)]
    for i, (rr, _, sl, _, _) in enumerate(units):
        o = (o_loc[i] + o_ctx[i]) / den[i]
        o_ref[0, rr * GRID_W:(rr + 1) * GRID_W, sl] = jnp.where(lo, o[:GRID_W], o[GRID_W:])


def na_attention(q, k, v, kc, vc, bias_tab):
    B, S, C = q.shape
    rows = S // GRID_W
    n_ctx = kc.shape[1]
    seq = pl.BlockSpec((1, S, C), lambda b, r: (b, 0, 0))
    cx = pl.BlockSpec((1, n_ctx, C), lambda b, r: (b, 0, 0))
    row = pl.BlockSpec((1, NA_ROWS_PER_STEP * GRID_W, C), lambda b, r: (b, r, 0))
    return pl.pallas_call(
        functools.partial(_na_kernel, rows=rows),
        out_shape=jax.ShapeDtypeStruct((B, S, C), jnp.float32),
        grid=(B, rows // NA_ROWS_PER_STEP),
        in_specs=[row, seq, seq, cx, cx, pl.BlockSpec(bias_tab.shape, lambda b, r: (0, 0, 0, 0))],
        out_specs=row,
        compiler_params=_cparams("parallel", "arbitrary"),
        name="na_attention",
    )(q, k, v, kc, vc, bias_tab)


def _ctx_attn_kernel(q_ref, k_ref, v_ref, o_ref):
    n = q_ref.shape[1]
    lo = _lo_lanes(n)
    sls = [slice(p * HP, (p + 1) * HP) for p in range(BRANCH_WIDTH // HP)]
    sc = [_dot_nt(_stack_pair(q_ref[0, :, sl], lo), k_ref[0, :, sl]) for sl in sls]
    e = [jnp.exp(x - jnp.max(x, axis=1, keepdims=True)) for x in sc]
    o = [_bdot(e[p], v_ref[0, :, sl]) / jnp.sum(e[p], axis=1, keepdims=True) for p, sl in enumerate(sls)]
    for p, sl in enumerate(sls):
        o_ref[0, :, sl] = jnp.where(lo, o[p][:n], o[p][n:])


def ctx_attention(q, k, v):
    B, n, C = q.shape
    blk = pl.BlockSpec((1, n, C), lambda b: (b, 0, 0))
    return pl.pallas_call(
        _ctx_attn_kernel, out_shape=jax.ShapeDtypeStruct((B, n, C), jnp.float32), grid=(B,),
        in_specs=[blk, blk, blk], out_specs=blk,
        compiler_params=_cparams("parallel"),
        name="ctx_attention",
    )(q, k, v)


def na_mix(proj_x, proj_c, qn_g, kn_g, rpb, need_ctx, tm):
    q, k, v = na_qknorm(proj_x, qn_g, kn_g, tm)
    qc, kc, vc = na_qknorm(proj_c, qn_g, kn_g, tm)
    out_x = na_attention(q, k, v, kc, vc, na_bias_table(rpb))
    out_c = ctx_attention(qc, kc, vc) if need_ctx else None
    return out_x, out_c


def rope_tables(n_tokens):
    t = jnp.arange(n_tokens)
    pos = jnp.stack([t // GRID_W, t % GRID_W], axis=-1).astype(jnp.float32)
    inv = ROPE_BASE ** (-jnp.arange(ROPE_ROT, dtype=jnp.float32) / ROPE_ROT)
    ang = pos[:, :, None] * inv
    cos_h = jnp.concatenate([jnp.cos(ang), jnp.cos(ang)], axis=-1).reshape(n_tokens, MLSTM_DQK)
    sin_h = jnp.concatenate([-jnp.sin(ang), jnp.sin(ang)], axis=-1).reshape(n_tokens, MLSTM_DQK)
    col = np.arange(MLSTM_QK)
    partner = np.where((col % (2 * ROPE_ROT)) < ROPE_ROT, col + ROPE_ROT, col - ROPE_ROT)
    perm = np.zeros((MLSTM_QK, MLSTM_QK), np.float32)
    perm[partner, col] = 1.0
    return jnp.tile(cos_h, (1, MLSTM_HEADS)), jnp.tile(sin_h, (1, MLSTM_HEADS)), jnp.asarray(perm, BF)


def _mlstm_prep_kernel(z_ref, gp_ref, ib_ref, fb_ref, *rest, rope):
    if rope:
        cos_ref, sin_ref, perm_ref, q_ref, k_ref, v_ref, g_ref = rest
    else:
        q_ref, k_ref, v_ref, g_ref = rest
    z = z_ref[0].astype(jnp.float32)
    q = z[:, 0:MLSTM_QK]
    k = z[:, MLSTM_QK:2 * MLSTM_QK]
    if rope:
        perm = perm_ref[...]
        parts = 1 if ACT_DTYPE == BF else 3
        q = q * cos_ref[...] + _split_dot(q, perm, parts) * sin_ref[...]
        k = k * cos_ref[...] + _split_dot(k, perm, parts) * sin_ref[...]
    q_ref[0] = (q * MLSTM_DQK ** -0.5).astype(BF)
    k_ref[0] = k.astype(BF)
    v_ref[0] = z[:, 2 * MLSTM_QK:2 * MLSTM_QK + BRANCH_WIDTH].astype(BF)
    gp = gp_ref[0].astype(jnp.float32)
    lane = lax.broadcasted_iota(jnp.int32, gp.shape, 1)
    ig = GATE_CAP * jnp.tanh((gp + ib_ref[...]) / GATE_CAP)
    fg = GATE_CAP * jnp.tanh((gp + fb_ref[...]) / GATE_CAP)
    g_ref[0] = jnp.where(lane < N_DIR * MLSTM_HEADS, ig, jax.nn.log_sigmoid(fg))


def mlstm_prep(proj, i_bias, f_bias, rope, tm):
    B, T, _ = proj.shape
    tm = min(tm, T)
    ng = N_DIR * MLSTM_HEADS
    ib = jnp.zeros((1, V7X_LANES), jnp.float32).at[0, 0:ng].set(i_bias.reshape(-1))
    fb = jnp.zeros((1, V7X_LANES), jnp.float32).at[0, ng:2 * ng].set(f_bias.reshape(-1))
    tok = lambda w: pl.BlockSpec((1, tm, w), lambda b, i: (b, i, 0))
    par = pl.BlockSpec((1, V7X_LANES), lambda b, i: (0, 0))
    args = [proj, proj, ib, fb]
    specs = [pl.BlockSpec((1, tm, MLSTM_MAIN_COLS), lambda b, i: (b, i, ML_BLK)),
             pl.BlockSpec((1, tm, V7X_LANES), lambda b, i: (b, i, ML_GATE_BLK)), par, par]
    if rope is not None:
        args += list(rope)
        specs += [pl.BlockSpec((tm, MLSTM_QK), lambda b, i: (i, 0)), pl.BlockSpec((tm, MLSTM_QK), lambda b, i: (i, 0)),
                  pl.BlockSpec((MLSTM_QK, MLSTM_QK), lambda b, i: (0, 0))]
    return pl.pallas_call(
        functools.partial(_mlstm_prep_kernel, rope=rope is not None),
        out_shape=(jax.ShapeDtypeStruct((B, T, MLSTM_QK), BF), jax.ShapeDtypeStruct((B, T, MLSTM_QK), BF),
                   jax.ShapeDtypeStruct((B, T, BRANCH_WIDTH), BF), jax.ShapeDtypeStruct((B, T, V7X_LANES), jnp.float32)),
        grid=(B, T // tm), in_specs=specs,
        out_specs=(tok(MLSTM_QK), tok(MLSTM_QK), tok(BRANCH_WIDTH), tok(V7X_LANES)),
        compiler_params=_cparams("parallel", "parallel"),
        name="mlstm_prep",
    )(*args)


def _mlstm_kernel(q_ref, k_ref, v_ref, gr_ref, c0_ref, n0_ref, m0_ref, *rest, reverse, bb, has_prev):
    if has_prev:
        hprev_ref, h_ref, cout_ref, nout_ref, mout_ref, c_scr, n_scr, m_scr = rest
    else:
        h_ref, cout_ref, nout_ref, mout_ref, c_scr, n_scr, m_scr = rest
    L = MLSTM_CHUNK
    DV = MLSTM_DV
    H = MLSTM_HEADS
    SL = MLSTM_GATE_SLOTS
    cidx = pl.program_id(1)

    @pl.when(cidx == 0)
    def _():
        c_scr[...] = c0_ref[...]
        n_scr[...] = n0_ref[...]
        m_scr[...] = m0_ref[...]

    ti = lax.broadcasted_iota(jnp.int32, (L, L), 0)
    si = lax.broadcasted_iota(jnp.int32, (L, L), 1)
    before = (si >= ti) if reverse else (si <= ti)
    tri = before.astype(jnp.float32)
    last = 0 if reverse else L - 1
    lo = _lo_lanes(L)
    lo_row = lax.broadcasted_iota(jnp.int32, (1, HP), 1) < (HP // 2)
    lo_col = lax.broadcasted_iota(jnp.int32, (HP, 1), 0) < (HP // 2)
    units = [(bi, p) for bi in range(bb) for p in range(H // 2)]
    blocks = [(bi, h) for bi in range(bb) for h in range(H)]
    nb = len(blocks)
    cat = lambda xs: jnp.concatenate(xs, axis=0)
    row_of = lambda bi, h: bi * SL + h

    i_rows = cat([gr_ref[bi, 0, 0, 0:SL, :] for bi in range(bb)])
    lf_rows = cat([gr_ref[bi, 0, 0, SL:2 * SL, :] for bi in range(bb)])
    m_prev = cat([m_scr[bi] for bi in range(bb)])
    b_rows = lax.dot_general(lf_rows, tri, (((1,), (1,)), ((), ())), precision=HIGHEST, preferred_element_type=jnp.float32)
    g_rows = i_rows - b_rows
    run = g_rows
    neg = jnp.full_like(g_rows, -jnp.inf)
    step = 1
    while step < L:
        shifted = (jnp.concatenate([run[:, step:], neg[:, :step]], axis=1) if reverse
                   else jnp.concatenate([neg[:, :step], run[:, :L - step]], axis=1))
        run = jnp.maximum(run, shifted)
        step *= 2
    m_rows = jnp.maximum(m_prev, run)
    m_end = jnp.broadcast_to(m_rows[:, last:last + 1], m_rows.shape)
    b_end = jnp.broadcast_to(b_rows[:, last:last + 1], b_rows.shape)
    sc_rows = jnp.exp(m_prev - m_rows)
    floor_rows = jnp.exp(-(b_rows + m_rows))
    wexp_rows = jnp.exp(g_rows - m_end)
    dec_rows = jnp.exp(m_prev - m_end)
    cols = cat([m_rows, sc_rows, floor_rows, wexp_rows]).T
    nr = bb * SL
    col_of = lambda kind, bi, h: cols[:, kind * nr + row_of(bi, h):kind * nr + row_of(bi, h) + 1]
    m_col = cat([col_of(0, bi, h) for bi, h in blocks])
    sc_col = cat([col_of(1, bi, h) for bi, h in blocks])
    floor_col = cat([col_of(2, bi, h) for bi, h in blocks])
    wexp_col = cat([col_of(3, bi, h) for bi, h in blocks])
    g_bcast = cat([jnp.broadcast_to(g_rows[row_of(bi, h):row_of(bi, h) + 1, :], (L, L)) for bi, h in blocks])
    decay = jnp.where(cat([before] * nb), jnp.exp(g_bcast - m_col), 0.0)

    qs = [_stack_pair(q_ref[bi, :, p * HP:(p + 1) * HP], lo) for bi, p in units]
    kp = [k_ref[bi, :, p * HP:(p + 1) * HP] for bi, p in units]
    vb = [v_ref[bi, :, h * DV:(h + 1) * DV] for bi, h in blocks]
    c_pair = [c_scr[bi, p] for bi, p in units]
    n_pair = [n_scr[bi, p] for bi, p in units]
    qk = cat([_dot_nt(qs[u], kp[u]) for u in range(len(units))])
    qc = cat([_bdot(qs[u], c_pair[u]) for u in range(len(units))])
    qn = cat([jnp.sum(qs[u].astype(jnp.float32) * n_pair[u], axis=1, keepdims=True) for u in range(len(units))])
    smat = qk * decay
    den = sc_col * qn + jnp.sum(smat, axis=1, keepdims=True)
    sb = smat.astype(BF)
    sv = cat([_bdot(sb[j * L:(j + 1) * L], vb[j]) for j in range(nb)])
    hout = (sc_col * qc + sv) / jnp.maximum(jnp.abs(den), floor_col)
    for j, (bi, h) in enumerate(blocks):
        hs = slice(h * DV, (h + 1) * DV)
        part = hout[j * L:(j + 1) * L]
        h_ref[bi, :, hs] = (part + hprev_ref[bi, :, hs]) if has_prev else part

    dec = lambda bi, h: dec_rows[row_of(bi, h):row_of(bi, h) + 1, 0:1]
    for u, (bi, p) in enumerate(units):
        j0 = bi * H + 2 * p
        kws = _stack_pair(kp[u], lo).astype(jnp.float32) * wexp_col[2 * u * L:2 * (u + 1) * L]
        upd = _dot_tn(kws, cat([vb[j0], vb[j0 + 1]]))
        c_scr[bi, p] = jnp.where(lo_col, dec(bi, 2 * p), dec(bi, 2 * p + 1)) * c_pair[u] + upd
        n_scr[bi, p] = jnp.where(lo_row, dec(bi, 2 * p), dec(bi, 2 * p + 1)) * n_pair[u] + jnp.sum(kws, axis=0, keepdims=True)
    m_next = b_end + m_end
    for bi in range(bb):
        m_scr[bi] = m_next[bi * SL:(bi + 1) * SL]

    @pl.when(cidx == pl.num_programs(1) - 1)
    def _():
        cout_ref[...] = c_scr[...]
        nout_ref[...] = n_scr[...]
        mout_ref[...] = m_scr[...]


def mlstm_gate_rows(gates):
    B, T, _ = gates.shape
    L = MLSTM_CHUNK
    H = MLSTM_HEADS
    SL = MLSTM_GATE_SLOTS
    ig = gates[:, :, :N_DIR * H].reshape(B, T, N_DIR, H)
    fg = gates[:, :, N_DIR * H:2 * N_DIR * H].reshape(B, T, N_DIR, H)
    zpad = jnp.zeros((B, T, N_DIR, SL - H), jnp.float32)
    rows = jnp.concatenate([ig, zpad, fg, zpad], axis=-1)
    return rows.reshape(B, T // L, L, N_DIR, 2 * SL).transpose(0, 1, 3, 4, 2)


def mlstm_chunked(q, k, v, g_rows, state, h_prev, direction):
    B, T, _ = q.shape
    L = MLSTM_CHUNK
    SL = MLSTM_GATE_SLOTS
    bb = MLSTM_SEQS_PER_STEP
    nc = T // L
    reverse = direction == 1
    cm = (lambda c: nc - 1 - c) if reverse else (lambda c: c)
    tok = lambda w: pl.BlockSpec((bb, L, w), lambda b, c: (b, cm(c), 0))
    st = lambda a: pl.BlockSpec((bb,) + a.shape[1:], lambda b, c: (b,) + (0,) * (a.ndim - 1))
    c0, n0, m0 = state
    has_prev = h_prev is not None
    args = [q, k, v, g_rows, c0, n0, m0] + ([h_prev] if has_prev else [])
    outs = pl.pallas_call(
        functools.partial(_mlstm_kernel, reverse=reverse, bb=bb, has_prev=has_prev),
        out_shape=(jax.ShapeDtypeStruct((B, T, BRANCH_WIDTH), jnp.float32),) + tuple(
            jax.ShapeDtypeStruct(a.shape, jnp.float32) for a in state),
        grid=(B // bb, nc),
        in_specs=[tok(MLSTM_QK), tok(MLSTM_QK), tok(BRANCH_WIDTH),
                  pl.BlockSpec((bb, 1, 1, 2 * SL, L), lambda b, c: (b, cm(c), direction, 0, 0)), st(c0), st(n0), st(m0)]
                 + ([tok(BRANCH_WIDTH)] if has_prev else []),
        out_specs=(tok(BRANCH_WIDTH), st(c0), st(n0), st(m0)),
        scratch_shapes=[pltpu.VMEM((bb,) + a.shape[1:], jnp.float32) for a in state],
        compiler_params=_cparams("parallel", "arbitrary"),
        name="mlstm_chunked",
    )(*args)
    return outs[0], outs[1:]


def _mlstm_readout_tile(h, o, norm_g):
    parts = []
    for hd in range(MLSTM_HEADS):
        x = h[:, hd * MLSTM_DV:(hd + 1) * MLSTM_DV]
        parts.append(x * lax.rsqrt(jnp.mean(x * x, axis=1, keepdims=True) + RMS_EPS))
    return jnp.concatenate(parts, axis=1) * norm_g * jax.nn.sigmoid(o)


def mlstm_mix(proj_x, proj_c, i_bias, f_bias, rope, tm):
    B = proj_x.shape[0]
    qx, kx, vx, gx = mlstm_prep(proj_x, i_bias, f_bias, rope, tm)
    qc, kc, vc, gc = mlstm_prep(proj_c, i_bias, f_bias, None, tm)
    gx, gc = mlstm_gate_rows(gx), mlstm_gate_rows(gc)
    h_x = h_c = None
    for d in range(N_DIR):
        st0 = (jnp.zeros((B, MLSTM_HEADS // 2, HP, HP), jnp.float32), jnp.zeros((B, MLSTM_HEADS // 2, 1, HP), jnp.float32),
               jnp.zeros((B, MLSTM_GATE_SLOTS, MLSTM_CHUNK), jnp.float32))
        h_c, st_ctx = mlstm_chunked(qc, kc, vc, gc, st0, h_c, d)
        h_x, _ = mlstm_chunked(qx, kx, vx, gx, st_ctx, h_x, d)
    return h_x, h_c


def _merge_kernel(y_ref, r_ref, k_ref, v_ref, g_ref, ag0_ref, ag1_ref, yb_ref, h_ref, og_ref, ga_ref, gb_ref, gc_ref,
                  x_ref, mod_ref, ka_ref, rk_ref, lg_ref, lb_ref, ones_ref, ng_ref, wb_ref, wo_ref, o_ref):
    f32 = lambda ref: ref[0].astype(jnp.float32)
    ya = _rwkv_readout_tile(y_ref[0], f32(r_ref), f32(k_ref), f32(v_ref), f32(g_ref), f32(ag0_ref), f32(ag1_ref),
                            ka_ref[...], rk_ref[...], lg_ref[...], lb_ref[...], ones_ref[...])
    yc = _mlstm_readout_tile(h_ref[0], f32(og_ref), ng_ref[...])
    merged = None
    for i, (y, gate_ref) in enumerate(((ya, ga_ref), (yb_ref[0], gb_ref), (yc, gc_ref))):
        t = jax.nn.sigmoid(f32(gate_ref)) * _bdot(y, wb_ref[i])
        merged = t if merged is None else merged + t
    o_ref[0] = x_ref[0] + mod_ref[0, 2:3, :] * _bdot(merged, wo_ref[...])


def merge_apply(rw, yb, h_ml, proj, x, mod, rw_params, ml_norm_g, w_branch, w_out, tm):
    B, T, D = x.shape
    tm = min(tm, T)
    C = BRANCH_WIDTH
    tok = lambda w: pl.BlockSpec((1, tm, w), lambda b, i: (b, i, 0))
    gate = lambda k: pl.BlockSpec((1, tm, D), lambda b, i: (b, i, GATE_BLK0 + k))
    par = pl.BlockSpec((1, C), lambda b, i: (0, 0))
    params = [p.reshape(1, C) for p in rw_params]
    return pl.pallas_call(
        _merge_kernel, out_shape=jax.ShapeDtypeStruct((B, T, D), jnp.float32), grid=(B, T // tm),
        in_specs=[tok(C)] * 9 + [pl.BlockSpec((1, tm, C), lambda b, i: (b, i, ML_OGATE_BLK)), gate(0), gate(1), gate(2), tok(D),
                  pl.BlockSpec((1, 6, D), lambda b, i: (b, 0, 0))] + [par] * 4
                 + [pl.BlockSpec((C, C), lambda b, i: (0, 0)), par,
                    pl.BlockSpec(w_branch.shape, lambda b, i: (0, 0, 0)), pl.BlockSpec(w_out.shape, lambda b, i: (0, 0))],
        out_specs=tok(D),
        compiler_params=_cparams("parallel", "parallel"),
        name="merge_branches",
    )(*rw, yb, h_ml, proj, proj, proj, proj, x, mod, *params, _head_ones(C, RWKV_HEAD), ml_norm_g.reshape(1, C),
      w_branch, w_out)


def _route_kernel(x_ref, g_ref, mod_ref, wr_ref, rb_ref, h_ref, gate_ref):
    h = _norm_mod(x_ref[0], g_ref[...], mod_ref[0, 4:5, :], mod_ref[0, 3:4, :])
    h_ref[0] = h.astype(BF)
    tm = h.shape[0]
    logits = lax.dot_general(wr_ref[...], h, (((1,), (1,)), ((), ())), precision=HIGHEST, preferred_element_type=jnp.float32)
    scores = jax.nn.sigmoid(logits)
    sel = scores + rb_ref[...]
    gsz = N_EXPERTS // N_GROUPS
    grp = sel.reshape(N_GROUPS, gsz, tm)
    iota_in = lax.broadcasted_iota(jnp.int32, grp.shape, 1)
    m1 = jnp.max(grp, axis=1, keepdims=True)
    first = jnp.min(jnp.where(grp == m1, iota_in, gsz), axis=1, keepdims=True)
    m2 = jnp.max(jnp.where(iota_in == first, -jnp.inf, grp), axis=1, keepdims=True)
    gscore = (m1 + m2).reshape(N_GROUPS, tm)
    gi = lax.broadcasted_iota(jnp.int32, (N_GROUPS, tm), 0)
    rank = jnp.zeros((N_GROUPS, tm), jnp.int32)
    for g2 in range(N_GROUPS):
        other = gscore[g2:g2 + 1, :]
        rank = rank + ((other > gscore) | ((other == gscore) & (g2 < gi))).astype(jnp.int32)
    gmask = rank < TOPK_GROUPS
    emask = jnp.broadcast_to(gmask.reshape(N_GROUPS, 1, tm), (N_GROUPS, gsz, tm)).reshape(N_EXPERTS, tm)
    cand = jnp.where(emask, sel, NEG_INF)
    ei = lax.broadcasted_iota(jnp.int32, (N_EXPERTS, tm), 0)
    chosen = jnp.zeros((N_EXPERTS, tm), jnp.bool_)
    for _ in range(TOP_K):
        mx = jnp.max(cand, axis=0, keepdims=True)
        idx = jnp.min(jnp.where(cand == mx, ei, N_EXPERTS), axis=0, keepdims=True)
        hit = ei == idx
        chosen = chosen | hit
        cand = jnp.where(hit, -jnp.inf, cand)
    w = jnp.where(chosen, scores, 0.0)
    w = w / jnp.sum(w, axis=0, keepdims=True) * ROUTED_SCALE
    gate_ref[0] = jnp.concatenate([w, jnp.zeros((V7X_LANES - N_EXPERTS, tm), jnp.float32)], axis=0).T


def moe_route(x, g, mod, w_router, router_bias, tm):
    B, T, D = x.shape
    tm = min(tm, T)
    tok = pl.BlockSpec((1, tm, D), lambda b, i: (b, i, 0))
    return pl.pallas_call(
        _route_kernel,
        out_shape=(jax.ShapeDtypeStruct((B, T, D), BF), jax.ShapeDtypeStruct((B, T, V7X_LANES), jnp.float32)),
        grid=(B, T // tm),
        in_specs=[tok, pl.BlockSpec((1, D), lambda b, i: (0, 0)), pl.BlockSpec((1, 6, D), lambda b, i: (b, 0, 0)),
                  pl.BlockSpec((N_EXPERTS, D), lambda b, i: (0, 0)), pl.BlockSpec((N_EXPERTS, 1), lambda b, i: (0, 0))],
        out_specs=(tok, pl.BlockSpec((1, tm, V7X_LANES), lambda b, i: (b, i, 0))),
        compiler_params=_cparams("parallel", "parallel"),
        name="moe_route",
    )(x, g.reshape(1, D), mod, w_router.T, router_bias.reshape(N_EXPERTS, 1))


def _moe_kernel(h_ref, gate_ref, x_ref, mod_ref, sel_ref, wg_ref, wu_ref, wd_ref, sg_ref, su_ref, sd_ref, o_ref, *, tm):
    j = pl.program_id(1)
    rows = pl.ds(pl.multiple_of(pl.program_id(2) * tm, tm), tm)
    h = h_ref[0]

    @pl.when(j == 0)
    def _():
        sh = jax.nn.silu(_bdot(h, sg_ref[...])) * _bdot(h, su_ref[...])
        o_ref[0, rows, :] = _bdot(sh, sd_ref[...])

    g8 = _split_dot(gate_ref[0], sel_ref[0], 2)
    act = jax.nn.silu(_bdot(h, wg_ref[...])) * _bdot(h, wu_ref[...])
    act = jnp.concatenate([act[:, e * D_EXPERT:(e + 1) * D_EXPERT] * g8[:, e:e + 1] for e in range(MOE_STEP_EXPERTS)],
                          axis=1)
    o_ref[0, rows, :] += _bdot(act, wd_ref[...])

    @pl.when(j == pl.num_programs(1) - 1)
    def _():
        o_ref[0, rows, :] = x_ref[0] + mod_ref[0, 5:6, :] * o_ref[0, rows, :]


def _moe_select_table():
    se = MOE_STEP_EXPERTS
    t = np.zeros((N_EXPERTS // se, V7X_LANES, V7X_LANES), np.float32)
    for j in range(N_EXPERTS // se):
        for e in range(se):
            t[j, j * se + e, e] = 1.0
    return jnp.asarray(t, BF)


def moe_apply(h2, gates, x, mod, wg, wu, wd, sg, su, sd, tm):
    B, T, D = x.shape
    sw = MOE_STEP_EXPERTS * D_EXPERT
    n_groups = N_EXPERTS // MOE_STEP_EXPERTS
    tok = pl.BlockSpec((1, tm, D), lambda b, j, i: (b, i, 0))
    x_last = pl.BlockSpec((1, tm, D), lambda b, j, i: (b, jnp.where(j == n_groups - 1, i, 0), 0))
    full = lambda a: pl.BlockSpec(a.shape, lambda b, j, i: (0,) * a.ndim)
    return pl.pallas_call(
        functools.partial(_moe_kernel, tm=tm), out_shape=jax.ShapeDtypeStruct((B, T, D), jnp.float32),
        grid=(B, n_groups, T // tm),
        in_specs=[tok, pl.BlockSpec((1, tm, V7X_LANES), lambda b, j, i: (b, i, 0)), x_last,
                  pl.BlockSpec((1, 6, D), lambda b, j, i: (b, 0, 0)),
                  pl.BlockSpec((1, V7X_LANES, V7X_LANES), lambda b, j, i: (j, 0, 0)),
                  pl.BlockSpec((D, sw), lambda b, j, i: (0, j)), pl.BlockSpec((D, sw), lambda b, j, i: (0, j)),
                  pl.BlockSpec((sw, D), lambda b, j, i: (j, 0)), full(sg), full(su), full(sd)],
        out_specs=pl.BlockSpec((1, T, D), lambda b, j, i: (b, 0, 0)),
        compiler_params=pltpu.CompilerParams(dimension_semantics=("parallel", "arbitrary", "arbitrary"),
                                             vmem_limit_bytes=MOE_VMEM_LIMIT_BYTES),
        name="moe_experts",
    )(h2, gates, x, mod, _moe_select_table(), wg, wu, wd, sg, su, sd)


def _reorder_w_in(w):
    o_na = RWKV_COLS
    o_ml = o_na + NA_COLS
    o_mg = o_ml + MLSTM_MAIN_COLS
    o_gate = o_mg + MLSTM_GATES
    pad = jnp.zeros((w.shape[0], V7X_LANES - MLSTM_GATES), w.dtype)
    return jnp.concatenate([w[:, o_na:o_ml], w[:, o_ml:o_mg], w[:, o_gate:], w[:, :RWKV_COLS], w[:, o_mg:o_gate], pad],
                           axis=1).astype(BF)


def kernel(x, c, ctx, c_ctx, w_ada, b_ada, norm1_g, norm2_g, w_in, rw_mu, rw_w0, rw_w2, rw_a0, rw_a2, rw_k_k, rw_k_a, rw_r_k, rw_g2, rw_lnx_g, rw_lnx_b, rw_v0, rw_v1, rw_v2, na_qn_g, na_kn_g, na_rpb, ml_i_bias, ml_f_bias, ml_norm_g, w_branch, w_out, moe_router, moe_bias, moe_w_gate, moe_w_up, moe_w_down, sh_w_gate, sh_w_up, sh_w_down):
    B, S, D = x.shape
    n_ctx = ctx.shape[1]
    tm = TOKEN_TILE
    assert S % tm == 0 and n_ctx % min(tm, n_ctx) == 0 and PROJ_COLS == IN_COLS + V7X_LANES - MLSTM_GATES
    rope = rope_tables(S)
    n_cond = B + 1
    cond_pad = (-n_cond) % V7X_SUBLANES
    s_cond = jnp.pad(jnp.concatenate([jax.nn.silu(c), jax.nn.silu(c_ctx)[None]], axis=0), ((0, cond_pad), (0, 0)))
    vf_x = vf_c = None
    for l in range(DEPTH):
        need_ctx = l < DEPTH - 1
        mod = pmm(s_cond, w_ada[l]) + b_ada[l]
        mod_x = mod[:B].reshape(B, 6, D)
        mod_c = jnp.broadcast_to(mod[B].reshape(1, 6, D), (B, 6, D))
        w_proj = _reorder_w_in(w_in[l])
        proj_x = norm_proj(x, norm1_g[l], mod_x, w_proj)
        proj_c = norm_proj(ctx.reshape(1, B * n_ctx, D), norm1_g[l], mod_c[:1], w_proj).reshape(B, n_ctx, PROJ_COLS)
        vres = None if l == 0 else (rw_v0[l - 1], rw_v1[l - 1], rw_v2[l - 1])
        rw_x, rw_c, vf_x, vf_c = rwkv_mix(proj_x, proj_c, vf_x, vf_c, rw_mu[l], rw_w0[l], rw_w2[l], rw_a0[l], rw_a2[l],
                                          rw_k_k[l], rw_k_a[l], rw_g2[l], vres, tm)
        rw_params = (rw_k_a[l], rw_r_k[l], rw_lnx_g[l], rw_lnx_b[l])
        yb_x, yb_c = na_mix(proj_x, proj_c, na_qn_g[l], na_kn_g[l], na_rpb[l], need_ctx, tm)
        hm_x, hm_c = mlstm_mix(proj_x, proj_c, ml_i_bias[l], ml_f_bias[l], rope, tm)
        wb = w_branch[l].astype(BF)
        wo = w_out[l].astype(BF)
        wg = moe_w_gate[l].transpose(1, 0, 2).reshape(D, N_EXPERTS * D_EXPERT).astype(BF)
        wu = moe_w_up[l].transpose(1, 0, 2).reshape(D, N_EXPERTS * D_EXPERT).astype(BF)
        wd = moe_w_down[l].reshape(N_EXPERTS * D_EXPERT, D).astype(BF)
        shared = (sh_w_gate[l].astype(BF), sh_w_up[l].astype(BF), sh_w_down[l].astype(BF))
        x = merge_apply(rw_x, yb_x, hm_x, proj_x, x, mod_x, rw_params, ml_norm_g[l], wb, wo, tm)
        h2, gates = moe_route(x, norm2_g[l], mod_x, moe_router[l], moe_bias[l], tm)
        x = moe_apply(h2, gates, x, mod_x, wg, wu, wd, *shared, MOE_TOKEN_TILE)
        if need_ctx:
            ctx = merge_apply(rw_c, yb_c, hm_c, proj_c, ctx, mod_c, rw_params, ml_norm_g[l], wb, wo, tm)
            h2, gates = moe_route(ctx, norm2_g[l], mod_c, moe_router[l], moe_bias[l], tm)
            ctx = moe_apply(h2, gates, ctx, mod_c, wg, wu, wd, *shared, min(tm, n_ctx))
    return x
```

```python
import functools

import numpy as np
import jax
import jax.numpy as jnp
from jax import lax
from jax.experimental import pallas as pl
from jax.experimental.pallas import tpu as pltpu

D_MODEL = 1024
DEPTH = 2
GRID_W = 64
N_DIR = 2
N_BRANCH = 3
BRANCH_WIDTH = 512
RMS_EPS = 1e-6
NEG_INF = -1e30

RWKV_HEAD = 64
RWKV_DECAY_LORA = 64
RWKV_LORA_COLS = 384
RWKV_COLS = 3 * BRANCH_WIDTH + RWKV_LORA_COLS
RWKV_LNX_EPS = 64e-5
RWKV_DECAY_SCALE = float(np.exp(-0.5))
RWKV_CHUNK = 64
WKV_SEQS_PER_STEP = 8

NA_HEAD = 64
NA_HEADS = BRANCH_WIDTH // NA_HEAD
NA_WIN_R = 8
NA_WIN_C = 16
NA_COLS = 3 * BRANCH_WIDTH
NA_ROWS_PER_STEP = 4

MLSTM_HEADS = 4
MLSTM_DQK = 64
MLSTM_DV = BRANCH_WIDTH // MLSTM_HEADS
MLSTM_QK = MLSTM_HEADS * MLSTM_DQK
MLSTM_CHUNK = 64
MLSTM_SEQS_PER_STEP = 8
MLSTM_GATE_SLOTS = 8
MLSTM_MAIN_COLS = 2 * MLSTM_QK + 2 * BRANCH_WIDTH
MLSTM_GATES = 2 * N_DIR * MLSTM_HEADS
GATE_CAP = 15.0
ROPE_ROT = MLSTM_DQK // 4
ROPE_BASE = 10000.0

N_EXPERTS = 64
TOP_K = 8
N_GROUPS = 8
TOPK_GROUPS = 4
D_EXPERT = 128
ROUTED_SCALE = 2.5
MOE_STEP_EXPERTS = 8
MOE_TOKEN_TILE = 1024

GATE_COLS = N_BRANCH * D_MODEL
IN_COLS = RWKV_COLS + NA_COLS + MLSTM_MAIN_COLS + MLSTM_GATES + GATE_COLS

TOKEN_TILE = 512
V7X_LANES = 128
V7X_SUBLANES = 8
HP = 2 * RWKV_HEAD
VMEM_LIMIT_BYTES = 48 * 1024 * 1024
MOE_VMEM_LIMIT_BYTES = 56 * 1024 * 1024

PROJ_COLS = 8192
NA_BLK = 0
ML_BLK = 1
ML_OGATE_BLK = (NA_COLS + 2 * MLSTM_QK + BRANCH_WIDTH) // BRANCH_WIDTH
GATE_BLK0 = (NA_COLS + MLSTM_MAIN_COLS) // D_MODEL
RWKV_COL0 = NA_COLS + MLSTM_MAIN_COLS + GATE_COLS
RWKV_BLK0 = RWKV_COL0 // BRANCH_WIDTH
RWKV_LORA_BLK = (RWKV_COL0 + 3 * BRANCH_WIDTH) // RWKV_LORA_COLS
ML_GATE_BLK = (RWKV_COL0 + RWKV_COLS) // V7X_LANES

BF = jnp.bfloat16
ACT_DTYPE = BF
HALO_ROWS = 16
HIGHEST = lax.Precision.HIGHEST


def _cparams(*sem):
    return pltpu.CompilerParams(dimension_semantics=sem, vmem_limit_bytes=VMEM_LIMIT_BYTES)


def _bdot(a, b):
    return jnp.dot(a.astype(BF), b.astype(BF), preferred_element_type=jnp.float32)


def _dot_nt(a, b):
    return lax.dot_general(a.astype(BF), b.astype(BF), (((1,), (1,)), ((), ())), preferred_element_type=jnp.float32)


def _dot_tn(a, b):
    return lax.dot_general(a.astype(BF), b.astype(BF), (((0,), (0,)), ((), ())), preferred_element_type=jnp.float32)


def _split_dot(x, w, parts):
    out = None
    rem = x
    for _ in range(parts):
        piece = rem.astype(BF)
        rem = rem - piece.astype(jnp.float32)
        t = jnp.dot(piece, w, preferred_element_type=jnp.float32)
        out = t if out is None else out + t
    return out


def _pair_sums(x):
    lo = _lo_lanes(x.shape[0])
    parts = []
    for c in range(x.shape[1] // HP):
        blk = x[:, c * HP:(c + 1) * HP]
        s_lo = jnp.sum(jnp.where(lo, blk, 0.0), axis=1, keepdims=True)
        s_hi = jnp.sum(jnp.where(lo, 0.0, blk), axis=1, keepdims=True)
        parts.append(jnp.where(lo, s_lo, s_hi))
    return jnp.concatenate(parts, axis=1)


def _head_ones(width, head):
    i = np.arange(width) // head
    return jnp.asarray(i[:, None] == i[None, :], BF)


def _stack_pair(x, lo):
    zero = jnp.zeros_like(x)
    return jnp.concatenate([jnp.where(lo, x, zero), jnp.where(lo, zero, x)], axis=0)


def _lo_lanes(n):
    return lax.broadcasted_iota(jnp.int32, (n, HP), 1) < (HP // 2)


def _mm_kernel(x_ref, w_ref, o_ref):
    part = _bdot(x_ref[...], w_ref[...])

    @pl.when(pl.program_id(2) == 0)
    def _():
        o_ref[...] = part

    @pl.when(pl.program_id(2) > 0)
    def _():
        o_ref[...] += part


def _pick_tile(n, cands):
    for c in cands:
        if n % c == 0:
            return c
    return n


def pmm(x, w):
    M, K = x.shape
    N = w.shape[1]
    tm = _pick_tile(M, (1024, 512, 256, 128, 64, 32, 16, 8))
    tn = _pick_tile(N, (1024, 512, 384, 256, 128))
    tk = _pick_tile(K, (1024,)) if K > 1024 else K
    return pl.pallas_call(
        _mm_kernel,
        out_shape=jax.ShapeDtypeStruct((M, N), jnp.float32),
        grid=(M // tm, N // tn, K // tk),
        in_specs=[pl.BlockSpec((tm, tk), lambda i, j, k: (i, k)),
                  pl.BlockSpec((tk, tn), lambda i, j, k: (k, j))],
        out_specs=pl.BlockSpec((tm, tn), lambda i, j, k: (i, j)),
        compiler_params=_cparams("parallel", "parallel", "arbitrary"),
        name="tiled_matmul",
    )(x, w)


def _norm_mod(x, g, scale, shift):
    xn = x * lax.rsqrt(jnp.mean(x * x, axis=-1, keepdims=True) + RMS_EPS)
    return xn * g * (1.0 + scale) + shift


def _norm_proj_kernel(x_ref, g_ref, mod_ref, w_ref, o_ref, h_scr):
    @pl.when(pl.program_id(2) == 0)
    def _():
        h_scr[...] = _norm_mod(x_ref[0], g_ref[...], mod_ref[0, 1:2, :], mod_ref[0, 0:1, :]).astype(BF)

    o_ref[0] = jnp.dot(h_scr[...], w_ref[...], preferred_element_type=jnp.float32).astype(o_ref.dtype)


def norm_proj(x, g, mod, w):
    B, T, D = x.shape
    N = w.shape[1]
    tm = _pick_tile(T, (1024, 512, 256))
    tn = _pick_tile(N, (1024, 512, 256, 128))
    return pl.pallas_call(
        _norm_proj_kernel,
        out_shape=jax.ShapeDtypeStruct((B, T, N), ACT_DTYPE), grid=(B, T // tm, N // tn),
        in_specs=[pl.BlockSpec((1, tm, D), lambda b, i, j: (b, i, 0)), pl.BlockSpec((1, D), lambda b, i, j: (0, 0)),
                  pl.BlockSpec((1, 6, D), lambda b, i, j: (b, 0, 0)), pl.BlockSpec((D, tn), lambda b, i, j: (0, j))],
        out_specs=pl.BlockSpec((1, tm, tn), lambda b, i, j: (b, i, j)),
        scratch_shapes=[pltpu.VMEM((tm, D), BF)],
        compiler_params=_cparams("parallel", "parallel", "arbitrary"),
        name="norm_proj",
    )(x, g.reshape(1, D), mod, w)


def _shifted(z, prev_row, next_row):
    n = z.shape[0]
    s = V7X_SUBLANES
    row = lax.broadcasted_iota(jnp.int32, (s, z.shape[1]), 0)
    zp = pltpu.roll(z, 1, axis=0)
    zn = pltpu.roll(z, n - 1, axis=0)
    zp = jnp.concatenate([jnp.where(row == 0, prev_row, zp[:s]), zp[s:]], axis=0)
    zn = jnp.concatenate([zn[:n - s], jnp.where(row == s - 1, next_row, zn[n - s:])], axis=0)
    return zp, zn


def _rwkv_feat_kernel(zr_ref, zk_ref, zv_ref, zl_ref, pr_ref, pk_ref, pv_ref, pl_ref, nr_ref, nk_ref, nv_ref, nl_ref,
                      mu_ref, w0_ref, w2_ref, a0_ref, a2_ref, kk_ref, g2_ref, ones_ref, *rest, has_vres):
    if has_vres:
        v0_ref, v1_ref, v2_ref, vf_ref = rest[:4]
        outs = rest[4:]
    else:
        outs = rest
    r_ref, k_ref, v_ref, kn_ref, g_ref, lw0_ref, lw1_ref, ag0_ref, ag1_ref = outs
    C = BRANCH_WIDTH
    first = pl.program_id(1) == 0
    last = pl.program_id(1) == pl.num_programs(1) - 1

    def shift(z_ref, p_ref, n_ref, c0, c1):
        z = z_ref[0].astype(jnp.float32)
        prev_row = jnp.where(first, 0.0, p_ref[0].astype(jnp.float32)[HALO_ROWS - 1:HALO_ROWS, :])
        next_row = jnp.where(last, 0.0, n_ref[0].astype(jnp.float32)[0:1, :])
        zp, zn = _shifted(z, prev_row, next_row)
        return z + mu_ref[:, c0:c1] * (0.5 * (zp + zn) - z)

    r = shift(zr_ref, pr_ref, nr_ref, 0, C)
    k = shift(zk_ref, pk_ref, nk_ref, C, 2 * C)
    v = shift(zv_ref, pv_ref, nv_ref, 2 * C, 3 * C)
    zl = shift(zl_ref, pl_ref, nl_ref, 3 * C, RWKV_COLS)
    wd = jnp.tanh(zl[:, 0:2 * RWKV_DECAY_LORA])
    ad = zl[:, 2 * RWKV_DECAY_LORA:4 * RWKV_DECAY_LORA]
    gd = zl[:, 4 * RWKV_DECAY_LORA:]
    for d, (lw_ref, ag_ref) in enumerate(((lw0_ref, ag0_ref), (lw1_ref, ag1_ref))):
        u = w0_ref[d:d + 1, :] + _bdot(wd, w2_ref[d])
        lw_ref[0] = -RWKV_DECAY_SCALE * jax.nn.sigmoid(u)
        ag_ref[0] = jax.nn.sigmoid(a0_ref[d:d + 1, :] + _bdot(ad, a2_ref[d])).astype(ag_ref.dtype)
    kq = k * kk_ref[...]
    ss = _split_dot(kq * kq, ones_ref[...], 2)
    kn_ref[0] = (kq * lax.rsqrt(jnp.maximum(ss, 1e-24))).astype(kn_ref.dtype)
    if has_vres:
        lora = _bdot(_bdot(v, v1_ref[...]), v2_ref[...])
        v = v + (vf_ref[0].astype(jnp.float32) - v) * jax.nn.sigmoid(v0_ref[...] + lora)
    g_ref[0] = _bdot(jax.nn.sigmoid(gd), g2_ref[...]).astype(g_ref.dtype)
    r_ref[0] = r.astype(r_ref.dtype)
    k_ref[0] = k.astype(k_ref.dtype)
    v_ref[0] = v.astype(v_ref.dtype)


def rwkv_features(proj, mu, w0, w2, a0, a2, k_k, g2, vres, v_first, tm):
    B, T, _ = proj.shape
    tm = min(tm, T)
    C = BRANCH_WIDTH
    nt = T // tm
    zpad = jnp.zeros((RWKV_DECAY_LORA, C), jnp.float32)
    pad_dirs = lambda w: jnp.stack([jnp.concatenate([w[0], zpad], 0), jnp.concatenate([zpad, w[1]], 0)]).astype(BF)
    sub = HALO_ROWS
    blk = lambda w, j: pl.BlockSpec((1, tm, w), lambda b, i: (b, i, j))
    before = lambda w, j: pl.BlockSpec((1, sub, w), lambda b, i: (b, jnp.maximum(i * (tm // sub) - 1, 0), j))
    after = lambda w, j: pl.BlockSpec((1, sub, w), lambda b, i: (b, jnp.minimum((i + 1) * (tm // sub), T // sub - 1), j))
    tok = pl.BlockSpec((1, tm, C), lambda b, i: (b, i, 0))
    full = lambda a: pl.BlockSpec(a.shape, lambda b, i: (0,) * a.ndim)
    params = [mu.reshape(1, -1), w0, pad_dirs(w2), a0, pad_dirs(a2), k_k.reshape(1, -1), g2.astype(BF),
              _head_ones(C, RWKV_HEAD)]
    pieces = [(C, RWKV_BLK0), (C, RWKV_BLK0 + 1), (C, RWKV_BLK0 + 2), (RWKV_LORA_COLS, RWKV_LORA_BLK)]
    args = [proj] * (3 * len(pieces)) + params
    specs = ([blk(w, j) for w, j in pieces] + [before(w, j) for w, j in pieces] + [after(w, j) for w, j in pieces]
             + [full(a) for a in params])
    if vres is not None:
        v0, v1, v2 = vres
        extra = [v0.reshape(1, -1), jnp.pad(v1, ((0, 0), (0, V7X_LANES - v1.shape[1]))).astype(BF),
                 jnp.pad(v2, ((0, V7X_LANES - v2.shape[0]), (0, 0))).astype(BF)]
        args += extra + [v_first]
        specs += [full(a) for a in extra] + [tok]
    return pl.pallas_call(
        functools.partial(_rwkv_feat_kernel, has_vres=vres is not None),
        out_shape=(jax.ShapeDtypeStruct((B, T, C), ACT_DTYPE),) * 5 + (jax.ShapeDtypeStruct((B, T, C), jnp.float32),) * 2
                  + (jax.ShapeDtypeStruct((B, T, C), ACT_DTYPE),) * 2,
        grid=(B, nt), in_specs=specs, out_specs=(tok,) * 9,
        compiler_params=_cparams("parallel", "parallel"),
        name="rwkv_features",
    )(*args)


def _wkv_kernel(r_ref, lw_ref, kk_ref, a_ref, k_ref, v_ref, ka_ref, s0_ref, *rest, reverse, bb, n_pairs, has_prev):
    if has_prev:
        yprev_ref, y_ref, sout_ref, s_scr = rest
    else:
        y_ref, sout_ref, s_scr = rest
    C = RWKV_CHUNK
    c_idx = pl.program_id(1)

    @pl.when(c_idx == 0)
    def _():
        s_scr[...] = s0_ref[...]

    ti = lax.broadcasted_iota(jnp.int32, (C, C), 0)
    si = lax.broadcasted_iota(jnp.int32, (C, C), 1)
    tri = ((si >= ti) if reverse else (si <= ti)).astype(jnp.float32)
    tp = lax.broadcasted_iota(jnp.int32, (C, 2 * C), 0)
    sp = lax.broadcasted_iota(jnp.int32, (C, 2 * C), 1) % C
    m_strict = (sp > tp) if reverse else (sp < tp)
    m_incl = (sp >= tp) if reverse else (sp <= tp)
    eye = (tp == sp).astype(jnp.float32)
    t2 = lax.broadcasted_iota(jnp.int32, (2 * C, 2 * C), 0)
    s2 = lax.broadcasted_iota(jnp.int32, (2 * C, 2 * C), 1)
    same_head = (t2 // C) == (s2 // C)
    lo = _lo_lanes(C)
    units = [(bi, slice(p * HP, (p + 1) * HP), p) for bi in range(bb) for p in range(n_pairs)]
    n = len(units)
    cat = lambda xs: jnp.concatenate(xs, axis=0)
    stack = lambda x: _stack_pair(x, lo)
    bdiag = lambda x: jnp.where(same_head, cat([x, x]), jnp.zeros((), x.dtype))

    ar, bk, bkh, v, e_tot = [], [], [], [], []
    for bi, sl, _ in units:
        lw = lw_ref[bi, :, sl]
        kk = kk_ref[bi, :, sl].astype(jnp.float32)
        ag = a_ref[bi, :, sl].astype(jnp.float32)
        kd = k_ref[bi, :, sl].astype(jnp.float32) * (1.0 + (ag - 1.0) * ka_ref[:, sl])
        cum = jnp.dot(tri, lw, precision=HIGHEST, preferred_element_type=jnp.float32)
        tot = jnp.sum(lw, axis=0, keepdims=True)
        e_neg = jnp.exp(-cum)
        e_end = jnp.exp(tot - cum)
        b = kk * ag
        ar.append(cat([-kk * jnp.exp(cum - lw), r_ref[bi, :, sl].astype(jnp.float32) * jnp.exp(cum)]).astype(BF))
        bk.append(cat([stack(b * e_neg), stack(kd * e_neg)]).astype(BF))
        bkh.append(cat([b * e_end, kd * e_end]).astype(BF))
        v.append(v_ref[bi, :, sl].astype(BF))
        e_tot.append(jnp.exp(tot))
    gram = [_dot_nt(ar[i], bk[i]) for i in range(n)]
    l_ab = [jnp.where(m_strict, g[:C, :2 * C], 0.0) for g in gram]
    l_ak = [jnp.where(m_strict, g[:C, 2 * C:], 0.0).astype(BF) for g in gram]
    l_rbk = [jnp.concatenate([jnp.where(m_incl, g[C:, :2 * C], 0.0), jnp.where(m_incl, g[C:, 2 * C:], 0.0)],
                             axis=1).astype(BF) for g in gram]
    vs = [stack(x) for x in v]
    s0 = [s_scr[bi, p] for bi, _, p in units]
    proj = [_dot_nt(ar[i], s0[i]) for i in range(n)]
    lv = [_bdot(l_ak[i], vs[i]) for i in range(n)]
    tinv = [eye + m for m in l_ab]
    pw_bd = [bdiag(m.astype(BF)) for m in l_ab]
    pw = [_bdot(l_ab[i], pw_bd[i]).astype(BF) for i in range(n)]
    levels = 5
    for lvl in range(1, levels + 1):
        pw_bd = [bdiag(m) for m in pw]
        if lvl < levels:
            both = [_bdot(cat([pw[i], tinv[i].astype(BF)]), pw_bd[i]) for i in range(n)]
            pw = [x[:C].astype(BF) for x in both]
            tinv = [tinv[i] + both[i][C:] for i in range(n)]
        else:
            tinv = [tinv[i] + _bdot(tinv[i], pw_bd[i]) for i in range(n)]
    u = [_bdot(tinv[i], stack((proj[i][:C] + lv[i]).astype(BF))) for i in range(n)]
    ub = [x.astype(BF) for x in u]
    ys = [proj[i][C:] + _bdot(l_rbk[i], cat([stack(ub[i]), vs[i]])) for i in range(n)]
    upd = [_dot_tn(cat([ub[i], v[i]]), bkh[i]) for i in range(n)]
    for i, (bi, sl, p) in enumerate(units):
        y = ys[i]
        if has_prev:
            y = y + yprev_ref[bi, :, sl]
        y_ref[bi, :, sl] = y
        s_scr[bi, p] = s0[i] * e_tot[i] + jnp.where(same_head, upd[i], 0.0)

    @pl.when(c_idx == pl.num_programs(1) - 1)
    def _():
        sout_ref[...] = s_scr[...]


def wkv_chunked(r, lw, kk, ag, k, v, k_a, s0, y_prev, reverse):
    B, T, W = r.shape
    C = RWKV_CHUNK
    bb = WKV_SEQS_PER_STEP
    nc = T // C
    n_pairs = W // HP
    cmap = (lambda b, c: (b, nc - 1 - c, 0)) if reverse else (lambda b, c: (b, c, 0))
    tok = pl.BlockSpec((bb, C, W), cmap)
    st = pl.BlockSpec((bb, n_pairs, HP, HP), lambda b, c: (b, 0, 0, 0))
    has_prev = y_prev is not None
    args = [r, lw, kk, ag, k, v, k_a, s0] + ([y_prev] if has_prev else [])
    return pl.pallas_call(
        functools.partial(_wkv_kernel, reverse=reverse, bb=bb, n_pairs=n_pairs, has_prev=has_prev),
        out_shape=(jax.ShapeDtypeStruct((B, T, W), jnp.float32), jax.ShapeDtypeStruct(s0.shape, jnp.float32)),
        grid=(B // bb, nc),
        in_specs=[tok] * 6 + [pl.BlockSpec((1, W), lambda b, c: (0, 0)), st] + ([tok] if has_prev else []),
        out_specs=(tok, st),
        scratch_shapes=[pltpu.VMEM((bb, n_pairs, HP, HP), jnp.float32)],
        compiler_params=_cparams("parallel", "arbitrary"),
        name="wkv_chunked",
    )(*args)


def _rwkv_readout_tile(y, r, k, v, g, ag0, ag1, ka, rk, lnx_g, lnx_b):
    mean = _pair_sums(y) * (1.0 / RWKV_HEAD)
    yc = y - mean
    var = _pair_sums(yc * yc) * (1.0 / RWKV_HEAD)
    yn = yc * lax.rsqrt(var + RWKV_LNX_EPS) * lnx_g + lnx_b
    ksum = k * (2.0 + (ag0 + ag1 - 2.0) * ka)
    bonus = _pair_sums(r * ksum * rk) * v
    return (yn + bonus) * g


def rwkv_mix(proj_x, proj_c, vf_x, vf_c, mu, w0, w2, a0, a2, k_k, k_a, g2, vres, tm):
    B = proj_x.shape[0]
    fx = rwkv_features(proj_x, mu, w0, w2, a0, a2, k_k, g2, vres, vf_x, tm)
    fc = rwkv_features(proj_c, mu, w0, w2, a0, a2, k_k, g2, vres, vf_c, tm)
    ka = k_a.reshape(1, -1)
    y_x = y_c = None
    for d in range(N_DIR):
        s0 = jnp.zeros((B, BRANCH_WIDTH // HP, HP, HP), jnp.float32)
        y_c, s_ctx = wkv_chunked(fc[0], fc[5 + d], fc[3], fc[7 + d], fc[1], fc[2], ka, s0, y_c, d == 1)
        y_x, _ = wkv_chunked(fx[0], fx[5 + d], fx[3], fx[7 + d], fx[1], fx[2], ka, s_ctx, y_x, d == 1)
    pick = lambda y, f: (y, f[0], f[1], f[2], f[4], f[7], f[8])
    vf_x = fx[2] if vres is None else vf_x
    vf_c = fc[2] if vres is None else vf_c
    return pick(y_x, fx), pick(y_c, fc), vf_x, vf_c


def _qknorm_kernel(z_ref, qg_ref, kg_ref, q_ref, k_ref, v_ref):
    C = BRANCH_WIDTH
    z = z_ref[0].astype(jnp.float32)
    q = z[:, 0:C]
    k = z[:, C:2 * C]
    qn = q * lax.rsqrt(_pair_sums(q * q) * (1.0 / NA_HEAD) + RMS_EPS) * qg_ref[...]
    kn = k * lax.rsqrt(_pair_sums(k * k) * (1.0 / NA_HEAD) + RMS_EPS) * kg_ref[...]
    q_ref[0] = (qn * NA_HEAD ** -0.5).astype(BF)
    k_ref[0] = kn.astype(BF)
    v_ref[0] = z[:, 2 * C:3 * C].astype(BF)


def na_qknorm(proj, qn_g, kn_g, tm):
    B, T, _ = proj.shape
    tm = min(tm, T)
    C = BRANCH_WIDTH
    tok = pl.BlockSpec((1, tm, C), lambda b, i: (b, i, 0))
    par = pl.BlockSpec((1, C), lambda b, i: (0, 0))
    sd = jax.ShapeDtypeStruct((B, T, C), BF)
    return pl.pallas_call(
        _qknorm_kernel, out_shape=(sd, sd, sd), grid=(B, T // tm),
        in_specs=[pl.BlockSpec((1, tm, NA_COLS), lambda b, i: (b, i, NA_BLK)), par, par],
        out_specs=(tok, tok, tok),
        compiler_params=_cparams("parallel", "parallel"),
        name="na_qknorm",
    )(proj, jnp.tile(qn_g, NA_HEADS).reshape(1, C), jnp.tile(kn_g, NA_HEADS).reshape(1, C))


def na_bias_table(rpb):
    qc = np.arange(GRID_W)[:, None]
    kc = np.arange(GRID_W)[None, :]
    cs = np.clip(qc - NA_WIN_C // 2, 0, GRID_W - NA_WIN_C)
    valid = (kc >= cs) & (kc < cs + NA_WIN_C)
    cidx = np.clip(kc - qc + NA_WIN_C - 1, 0, 2 * NA_WIN_C - 2)
    t = jnp.where(valid[None, None], rpb[:, :, cidx], NEG_INF)
    t2 = jnp.concatenate([t[:, :-1], t[:, 1:]], axis=-1)
    H = rpb.shape[0]
    t2 = t2.reshape(H // 2, 2, 2 * NA_WIN_R - 2, GRID_W, 2 * GRID_W).transpose(0, 2, 1, 3, 4)
    return t2.reshape(H // 2, 2 * NA_WIN_R - 2, 2 * GRID_W, 2 * GRID_W)


def _na_kernel(q_ref, k_ref, v_ref, kc_ref, vc_ref, bias_ref, o_ref, *, rows):
    nwin = NA_WIN_R * GRID_W
    lo = _lo_lanes(GRID_W)
    units = []
    for rr in range(NA_ROWS_PER_STEP):
        r = pl.program_id(1) * NA_ROWS_PER_STEP + rr
        rs = jnp.clip(r - NA_WIN_R // 2, 0, rows - NA_WIN_R)
        k0 = pl.multiple_of(rs * GRID_W, GRID_W)
        for p in range(BRANCH_WIDTH // HP):
            units.append((rr, p, slice(p * HP, (p + 1) * HP), rs - r + NA_WIN_R - 1, k0))
    qs = [_stack_pair(q_ref[0, rr * GRID_W:(rr + 1) * GRID_W, sl], lo) for rr, _, sl, _, _ in units]
    s_loc = [_dot_nt(qs[i], k_ref[0, pl.ds(u[4], nwin), u[2]]) for i, u in enumerate(units)]
    s_ctx = [_dot_nt(qs[i], kc_ref[0, :, u[2]]) for i, u in enumerate(units)]
    p_loc, p_ctx, den = [], [], []
    for i, (_, p, _, base, _) in enumerate(units):
        sl_b = s_loc[i] + jnp.concatenate([bias_ref[p, base + 2 * j] for j in range(NA_WIN_R // 2)], axis=1)
        m = jnp.maximum(jnp.max(sl_b, axis=1, keepdims=True), jnp.max(s_ctx[i], axis=1, keepdims=True))
        el = jnp.exp(sl_b - m)
        ec = jnp.exp(s_ctx[i] - m)
        den.append(jnp.sum(el, axis=1, keepdims=True) + jnp.sum(ec, axis=1, keepdims=True))
        p_loc.append(el.astype(BF))
        p_ctx.append(ec.astype(BF))
    o_loc = [_bdot(p_loc[i], v_ref[0, pl.ds(u[4], nwin), u[2]]) for i, u in enumerate(units)]
    o_ctx = [_bdot(p_ctx[i], vc_ref[0, :, u[2]]) for i, u in enumerate(units)]
    for i, (rr, _, sl, _, _) in enumerate(units):
        o = (o_loc[i] + o_ctx[i]) / den[i]
        o_ref[0, rr * GRID_W:(rr + 1) * GRID_W, sl] = jnp.where(lo, o[:GRID_W], o[GRID_W:])


def na_attention(q, k, v, kc, vc, bias_tab):
    B, S, C = q.shape
    rows = S // GRID_W
    n_ctx = kc.shape[1]
    seq = pl.BlockSpec((1, S, C), lambda b, r: (b, 0, 0))
    cx = pl.BlockSpec((1, n_ctx, C), lambda b, r: (b, 0, 0))
    row = pl.BlockSpec((1, NA_ROWS_PER_STEP * GRID_W, C), lambda b, r: (b, r, 0))
    return pl.pallas_call(
        functools.partial(_na_kernel, rows=rows),
        out_shape=jax.ShapeDtypeStruct((B, S, C), jnp.float32),
        grid=(B, rows // NA_ROWS_PER_STEP),
        in_specs=[row, seq, seq, cx, cx, pl.BlockSpec(bias_tab.shape, lambda b, r: (0, 0, 0, 0))],
        out_specs=row,
        compiler_params=_cparams("parallel", "arbitrary"),
        name="na_attention",
    )(q, k, v, kc, vc, bias_tab)


def _ctx_attn_kernel(q_ref, k_ref, v_ref, o_ref):
    n = q_ref.shape[1]
    lo = _lo_lanes(n)
    sls = [slice(p * HP, (p + 1) * HP) for p in range(BRANCH_WIDTH // HP)]
    sc = [_dot_nt(_stack_pair(q_ref[0, :, sl], lo), k_ref[0, :, sl]) for sl in sls]
    e = [jnp.exp(x - jnp.max(x, axis=1, keepdims=True)) for x in sc]
    o = [_bdot(e[p], v_ref[0, :, sl]) / jnp.sum(e[p], axis=1, keepdims=True) for p, sl in enumerate(sls)]
    for p, sl in enumerate(sls):
        o_ref[0, :, sl] = jnp.where(lo, o[p][:n], o[p][n:])


def ctx_attention(q, k, v):
    B, n, C = q.shape
    blk = pl.BlockSpec((1, n, C), lambda b: (b, 0, 0))
    return pl.pallas_call(
        _ctx_attn_kernel, out_shape=jax.ShapeDtypeStruct((B, n, C), jnp.float32), grid=(B,),
        in_specs=[blk, blk, blk], out_specs=blk,
        compiler_params=_cparams("parallel"),
        name="ctx_attention",
    )(q, k, v)


def na_mix(proj_x, proj_c, qn_g, kn_g, rpb, need_ctx, tm):
    q, k, v = na_qknorm(proj_x, qn_g, kn_g, tm)
    qc, kc, vc = na_qknorm(proj_c, qn_g, kn_g, tm)
    out_x = na_attention(q, k, v, kc, vc, na_bias_table(rpb))
    out_c = ctx_attention(qc, kc, vc) if need_ctx else None
    return out_x, out_c


def rope_tables(n_tokens):
    t = jnp.arange(n_tokens)
    pos = jnp.stack([t // GRID_W, t % GRID_W], axis=-1).astype(jnp.float32)
    inv = ROPE_BASE ** (-jnp.arange(ROPE_ROT, dtype=jnp.float32) / ROPE_ROT)
    ang = pos[:, :, None] * inv
    cos_h = jnp.concatenate([jnp.cos(ang), jnp.cos(ang)], axis=-1).reshape(n_tokens, MLSTM_DQK)
    sin_h = jnp.concatenate([-jnp.sin(ang), jnp.sin(ang)], axis=-1).reshape(n_tokens, MLSTM_DQK)
    col = np.arange(MLSTM_QK)
    partner = np.where((col % (2 * ROPE_ROT)) < ROPE_ROT, col + ROPE_ROT, col - ROPE_ROT)
    perm = np.zeros((MLSTM_QK, MLSTM_QK), np.float32)
    perm[partner, col] = 1.0
    return jnp.tile(cos_h, (1, MLSTM_HEADS)), jnp.tile(sin_h, (1, MLSTM_HEADS)), jnp.asarray(perm, BF)


def _mlstm_prep_kernel(z_ref, gp_ref, ib_ref, fb_ref, *rest, rope):
    if rope:
        cos_ref, sin_ref, perm_ref, q_ref, k_ref, v_ref, g_ref = rest
    else:
        q_ref, k_ref, v_ref, g_ref = rest
    z = z_ref[0].astype(jnp.float32)
    q = z[:, 0:MLSTM_QK]
    k = z[:, MLSTM_QK:2 * MLSTM_QK]
    if rope:
        perm = perm_ref[...]
        parts = 1 if ACT_DTYPE == BF else 3
        q = q * cos_ref[...] + _split_dot(q, perm, parts) * sin_ref[...]
        k = k * cos_ref[...] + _split_dot(k, perm, parts) * sin_ref[...]
    q_ref[0] = (q * MLSTM_DQK ** -0.5).astype(BF)
    k_ref[0] = k.astype(BF)
    v_ref[0] = z[:, 2 * MLSTM_QK:2 * MLSTM_QK + BRANCH_WIDTH].astype(BF)
    gp = gp_ref[0].astype(jnp.float32)
    lane = lax.broadcasted_iota(jnp.int32, gp.shape, 1)
    ig = GATE_CAP * jnp.tanh((gp + ib_ref[...]) / GATE_CAP)
    fg = GATE_CAP * jnp.tanh((gp + fb_ref[...]) / GATE_CAP)
    g_ref[0] = jnp.where(lane < N_DIR * MLSTM_HEADS, ig, jax.nn.log_sigmoid(fg))


def mlstm_prep(proj, i_bias, f_bias, rope, tm):
    B, T, _ = proj.shape
    tm = min(tm, T)
    ng = N_DIR * MLSTM_HEADS
    ib = jnp.zeros((1, V7X_LANES), jnp.float32).at[0, 0:ng].set(i_bias.reshape(-1))
    fb = jnp.zeros((1, V7X_LANES), jnp.float32).at[0, ng:2 * ng].set(f_bias.reshape(-1))
    tok = lambda w: pl.BlockSpec((1, tm, w), lambda b, i: (b, i, 0))
    par = pl.BlockSpec((1, V7X_LANES), lambda b, i: (0, 0))
    args = [proj, proj, ib, fb]
    specs = [pl.BlockSpec((1, tm, MLSTM_MAIN_COLS), lambda b, i: (b, i, ML_BLK)),
             pl.BlockSpec((1, tm, V7X_LANES), lambda b, i: (b, i, ML_GATE_BLK)), par, par]
    if rope is not None:
        args += list(rope)
        specs += [pl.BlockSpec((tm, MLSTM_QK), lambda b, i: (i, 0)), pl.BlockSpec((tm, MLSTM_QK), lambda b, i: (i, 0)),
                  pl.BlockSpec((MLSTM_QK, MLSTM_QK), lambda b, i: (0, 0))]
    return pl.pallas_call(
        functools.partial(_mlstm_prep_kernel, rope=rope is not None),
        out_shape=(jax.ShapeDtypeStruct((B, T, MLSTM_QK), BF), jax.ShapeDtypeStruct((B, T, MLSTM_QK), BF),
                   jax.ShapeDtypeStruct((B, T, BRANCH_WIDTH), BF), jax.ShapeDtypeStruct((B, T, V7X_LANES), jnp.float32)),
        grid=(B, T // tm), in_specs=specs,
        out_specs=(tok(MLSTM_QK), tok(MLSTM_QK), tok(BRANCH_WIDTH), tok(V7X_LANES)),
        compiler_params=_cparams("parallel", "parallel"),
        name="mlstm_prep",
    )(*args)


def _mlstm_kernel(q_ref, k_ref, v_ref, gr_ref, c0_ref, n0_ref, m0_ref, *rest, reverse, bb, has_prev):
    if has_prev:
        hprev_ref, h_ref, cout_ref, nout_ref, mout_ref, c_scr, n_scr, m_scr = rest
    else:
        h_ref, cout_ref, nout_ref, mout_ref, c_scr, n_scr, m_scr = rest
    L = MLSTM_CHUNK
    DV = MLSTM_DV
    H = MLSTM_HEADS
    SL = MLSTM_GATE_SLOTS
    cidx = pl.program_id(1)

    @pl.when(cidx == 0)
    def _():
        c_scr[...] = c0_ref[...]
        n_scr[...] = n0_ref[...]
        m_scr[...] = m0_ref[...]

    ti = lax.broadcasted_iota(jnp.int32, (L, L), 0)
    si = lax.broadcasted_iota(jnp.int32, (L, L), 1)
    before = (si >= ti) if reverse else (si <= ti)
    tri = before.astype(jnp.float32)
    last = 0 if reverse else L - 1
    lo = _lo_lanes(L)
    lo_row = lax.broadcasted_iota(jnp.int32, (1, HP), 1) < (HP // 2)
    lo_col = lax.broadcasted_iota(jnp.int32, (HP, 1), 0) < (HP // 2)
    units = [(bi, p) for bi in range(bb) for p in range(H // 2)]
    blocks = [(bi, h) for bi in range(bb) for h in range(H)]
    nb = len(blocks)
    cat = lambda xs: jnp.concatenate(xs, axis=0)
    row_of = lambda bi, h: bi * SL + h

    i_rows = cat([gr_ref[bi, 0, 0, 0:SL, :] for bi in range(bb)])
    lf_rows = cat([gr_ref[bi, 0, 0, SL:2 * SL, :] for bi in range(bb)])
    m_prev = cat([m_scr[bi] for bi in range(bb)])
    b_rows = lax.dot_general(lf_rows, tri, (((1,), (1,)), ((), ())), precision=HIGHEST, preferred_element_type=jnp.float32)
    g_rows = i_rows - b_rows
    run = g_rows
    neg = jnp.full_like(g_rows, -jnp.inf)
    step = 1
    while step < L:
        shifted = (jnp.concatenate([run[:, step:], neg[:, :step]], axis=1) if reverse
                   else jnp.concatenate([neg[:, :step], run[:, :L - step]], axis=1))
        run = jnp.maximum(run, shifted)
        step *= 2
    m_rows = jnp.maximum(m_prev, run)
    m_end = jnp.broadcast_to(m_rows[:, last:last + 1], m_rows.shape)
    b_end = jnp.broadcast_to(b_rows[:, last:last + 1], b_rows.shape)
    sc_rows = jnp.exp(m_prev - m_rows)
    floor_rows = jnp.exp(-(b_rows + m_rows))
    wexp_rows = jnp.exp(g_rows - m_end)
    dec_rows = jnp.exp(m_prev - m_end)
    cols = cat([m_rows, sc_rows, floor_rows, wexp_rows]).T
    nr = bb * SL
    col_of = lambda kind, bi, h: cols[:, kind * nr + row_of(bi, h):kind * nr + row_of(bi, h) + 1]
    m_col = cat([col_of(0, bi, h) for bi, h in blocks])
    sc_col = cat([col_of(1, bi, h) for bi, h in blocks])
    floor_col = cat([col_of(2, bi, h) for bi, h in blocks])
    wexp_col = cat([col_of(3, bi, h) for bi, h in blocks])
    g_bcast = cat([jnp.broadcast_to(g_rows[row_of(bi, h):row_of(bi, h) + 1, :], (L, L)) for bi, h in blocks])
    decay = jnp.where(cat([before] * nb), jnp.exp(g_bcast - m_col), 0.0)

    qs = [_stack_pair(q_ref[bi, :, p * HP:(p + 1) * HP], lo) for bi, p in units]
    kp = [k_ref[bi, :, p * HP:(p + 1) * HP] for bi, p in units]
    vb = [v_ref[bi, :, h * DV:(h + 1) * DV] for bi, h in blocks]
    c_pair = [c_scr[bi, p] for bi, p in units]
    n_pair = [n_scr[bi, p] for bi, p in units]
    qk = cat([_dot_nt(qs[u], kp[u]) for u in range(len(units))])
    qc = cat([_bdot(qs[u], c_pair[u]) for u in range(len(units))])
    qn = cat([jnp.sum(qs[u].astype(jnp.float32) * n_pair[u], axis=1, keepdims=True) for u in range(len(units))])
    smat = qk * decay
    den = sc_col * qn + jnp.sum(smat, axis=1, keepdims=True)
    sb = smat.astype(BF)
    sv = cat([_bdot(sb[j * L:(j + 1) * L], vb[j]) for j in range(nb)])
    hout = (sc_col * qc + sv) / jnp.maximum(jnp.abs(den), floor_col)
    for j, (bi, h) in enumerate(blocks):
        hs = slice(h * DV, (h + 1) * DV)
        part = hout[j * L:(j + 1) * L]
        h_ref[bi, :, hs] = (part + hprev_ref[bi, :, hs]) if has_prev else part

    dec = lambda bi, h: dec_rows[row_of(bi, h):row_of(bi, h) + 1, 0:1]
    for u, (bi, p) in enumerate(units):
        j0 = bi * H + 2 * p
        kws = _stack_pair(kp[u], lo).astype(jnp.float32) * wexp_col[2 * u * L:2 * (u + 1) * L]
        upd = _dot_tn(kws, cat([vb[j0], vb[j0 + 1]]))
        c_scr[bi, p] = jnp.where(lo_col, dec(bi, 2 * p), dec(bi, 2 * p + 1)) * c_pair[u] + upd
        n_scr[bi, p] = jnp.where(lo_row, dec(bi, 2 * p), dec(bi, 2 * p + 1)) * n_pair[u] + jnp.sum(kws, axis=0, keepdims=True)
    m_next = b_end + m_end
    for bi in range(bb):
        m_scr[bi] = m_next[bi * SL:(bi + 1) * SL]

    @pl.when(cidx == pl.num_programs(1) - 1)
    def _():
        cout_ref[...] = c_scr[...]
        nout_ref[...] = n_scr[...]
        mout_ref[...] = m_scr[...]


def mlstm_gate_rows(gates):
    B, T, _ = gates.shape
    L = MLSTM_CHUNK
    H = MLSTM_HEADS
    SL = MLSTM_GATE_SLOTS
    ig = gates[:, :, :N_DIR * H].reshape(B, T, N_DIR, H)
    fg = gates[:, :, N_DIR * H:2 * N_DIR * H].reshape(B, T, N_DIR, H)
    zpad = jnp.zeros((B, T, N_DIR, SL - H), jnp.float32)
    rows = jnp.concatenate([ig, zpad, fg, zpad], axis=-1)
    return rows.reshape(B, T // L, L, N_DIR, 2 * SL).transpose(0, 1, 3, 4, 2)


def mlstm_chunked(q, k, v, g_rows, state, h_prev, direction):
    B, T, _ = q.shape
    L = MLSTM_CHUNK
    SL = MLSTM_GATE_SLOTS
    bb = MLSTM_SEQS_PER_STEP
    nc = T // L
    reverse = direction == 1
    cm = (lambda c: nc - 1 - c) if reverse else (lambda c: c)
    tok = lambda w: pl.BlockSpec((bb, L, w), lambda b, c: (b, cm(c), 0))
    st = lambda a: pl.BlockSpec((bb,) + a.shape[1:], lambda b, c: (b,) + (0,) * (a.ndim - 1))
    c0, n0, m0 = state
    has_prev = h_prev is not None
    args = [q, k, v, g_rows, c0, n0, m0] + ([h_prev] if has_prev else [])
    outs = pl.pallas_call(
        functools.partial(_mlstm_kernel, reverse=reverse, bb=bb, has_prev=has_prev),
        out_shape=(jax.ShapeDtypeStruct((B, T, BRANCH_WIDTH), jnp.float32),) + tuple(
            jax.ShapeDtypeStruct(a.shape, jnp.float32) for a in state),
        grid=(B // bb, nc),
        in_specs=[tok(MLSTM_QK), tok(MLSTM_QK), tok(BRANCH_WIDTH),
                  pl.BlockSpec((bb, 1, 1, 2 * SL, L), lambda b, c: (b, cm(c), direction, 0, 0)), st(c0), st(n0), st(m0)]
                 + ([tok(BRANCH_WIDTH)] if has_prev else []),
        out_specs=(tok(BRANCH_WIDTH), st(c0), st(n0), st(m0)),
        scratch_shapes=[pltpu.VMEM((bb,) + a.shape[1:], jnp.float32) for a in state],
        compiler_params=_cparams("parallel", "arbitrary"),
        name="mlstm_chunked",
    )(*args)
    return outs[0], outs[1:]


def _mlstm_readout_tile(h, o, norm_g):
    parts = []
    for hd in range(MLSTM_HEADS):
        x = h[:, hd * MLSTM_DV:(hd + 1) * MLSTM_DV]
        parts.append(x * lax.rsqrt(jnp.mean(x * x, axis=1, keepdims=True) + RMS_EPS))
    return jnp.concatenate(parts, axis=1) * norm_g * jax.nn.sigmoid(o)


def mlstm_mix(proj_x, proj_c, i_bias, f_bias, rope, tm):
    B = proj_x.shape[0]
    qx, kx, vx, gx = mlstm_prep(proj_x, i_bias, f_bias, rope, tm)
    qc, kc, vc, gc = mlstm_prep(proj_c, i_bias, f_bias, None, tm)
    gx, gc = mlstm_gate_rows(gx), mlstm_gate_rows(gc)
    h_x = h_c = None
    for d in range(N_DIR):
        st0 = (jnp.zeros((B, MLSTM_HEADS // 2, HP, HP), jnp.float32), jnp.zeros((B, MLSTM_HEADS // 2, 1, HP), jnp.float32),
               jnp.zeros((B, MLSTM_GATE_SLOTS, MLSTM_CHUNK), jnp.float32))
        h_c, st_ctx = mlstm_chunked(qc, kc, vc, gc, st0, h_c, d)
        h_x, _ = mlstm_chunked(qx, kx, vx, gx, st_ctx, h_x, d)
    return h_x, h_c


def _merge_kernel(y_ref, r_ref, k_ref, v_ref, g_ref, ag0_ref, ag1_ref, yb_ref, h_ref, og_ref, ga_ref, gb_ref, gc_ref,
                  x_ref, mod_ref, ka_ref, rk_ref, lg_ref, lb_ref, ng_ref, wb_ref, wo_ref, o_ref):
    f32 = lambda ref: ref[0].astype(jnp.float32)
    ya = _rwkv_readout_tile(y_ref[0], f32(r_ref), f32(k_ref), f32(v_ref), f32(g_ref), f32(ag0_ref), f32(ag1_ref),
                            ka_ref[...], rk_ref[...], lg_ref[...], lb_ref[...])
    yc = _mlstm_readout_tile(h_ref[0], f32(og_ref), ng_ref[...])
    merged = None
    for i, (y, gate_ref) in enumerate(((ya, ga_ref), (yb_ref[0], gb_ref), (yc, gc_ref))):
        t = jax.nn.sigmoid(f32(gate_ref)) * _bdot(y, wb_ref[i])
        merged = t if merged is None else merged + t
    o_ref[0] = x_ref[0] + mod_ref[0, 2:3, :] * _bdot(merged, wo_ref[...])


def merge_apply(rw, yb, h_ml, proj, x, mod, rw_params, ml_norm_g, w_branch, w_out, tm):
    B, T, D = x.shape
    tm = min(tm, T)
    C = BRANCH_WIDTH
    tok = lambda w: pl.BlockSpec((1, tm, w), lambda b, i: (b, i, 0))
    gate = lambda k: pl.BlockSpec((1, tm, D), lambda b, i: (b, i, GATE_BLK0 + k))
    par = pl.BlockSpec((1, C), lambda b, i: (0, 0))
    params = [p.reshape(1, C) for p in rw_params]
    return pl.pallas_call(
        _merge_kernel, out_shape=jax.ShapeDtypeStruct((B, T, D), jnp.float32), grid=(B, T // tm),
        in_specs=[tok(C)] * 9 + [pl.BlockSpec((1, tm, C), lambda b, i: (b, i, ML_OGATE_BLK)), gate(0), gate(1), gate(2), tok(D),
                  pl.BlockSpec((1, 6, D), lambda b, i: (b, 0, 0))] + [par] * 4
                 + [par, pl.BlockSpec(w_branch.shape, lambda b, i: (0, 0, 0)), pl.BlockSpec(w_out.shape, lambda b, i: (0, 0))],
        out_specs=tok(D),
        compiler_params=_cparams("parallel", "parallel"),
        name="merge_branches",
    )(*rw, yb, h_ml, proj, proj, proj, proj, x, mod, *params, ml_norm_g.reshape(1, C), w_branch, w_out)


def _route_kernel(x_ref, g_ref, mod_ref, wr_ref, rb_ref, h_ref, gate_ref):
    h = _norm_mod(x_ref[0], g_ref[...], mod_ref[0, 4:5, :], mod_ref[0, 3:4, :])
    h_ref[0] = h.astype(BF)
    tm = h.shape[0]
    logits = lax.dot_general(wr_ref[...], h, (((1,), (1,)), ((), ())), precision=HIGHEST, preferred_element_type=jnp.float32)
    scores = jax.nn.sigmoid(logits)
    sel = scores + rb_ref[...]
    gsz = N_EXPERTS // N_GROUPS
    grp = sel.reshape(N_GROUPS, gsz, tm)
    iota_in = lax.broadcasted_iota(jnp.int32, grp.shape, 1)
    m1 = jnp.max(grp, axis=1, keepdims=True)
    first = jnp.min(jnp.where(grp == m1, iota_in, gsz), axis=1, keepdims=True)
    m2 = jnp.max(jnp.where(iota_in == first, -jnp.inf, grp), axis=1, keepdims=True)
    gscore = (m1 + m2).reshape(N_GROUPS, tm)
    gi = lax.broadcasted_iota(jnp.int32, (N_GROUPS, tm), 0)
    rank = jnp.zeros((N_GROUPS, tm), jnp.int32)
    for g2 in range(N_GROUPS):
        other = gscore[g2:g2 + 1, :]
        rank = rank + ((other > gscore) | ((other == gscore) & (g2 < gi))).astype(jnp.int32)
    gmask = rank < TOPK_GROUPS
    emask = jnp.broadcast_to(gmask.reshape(N_GROUPS, 1, tm), (N_GROUPS, gsz, tm)).reshape(N_EXPERTS, tm)
    cand = jnp.where(emask, sel, NEG_INF)
    ei = lax.broadcasted_iota(jnp.int32, (N_EXPERTS, tm), 0)
    chosen = jnp.zeros((N_EXPERTS, tm), jnp.bool_)
    for _ in range(TOP_K):
        mx = jnp.max(cand, axis=0, keepdims=True)
        idx = jnp.min(jnp.where(cand == mx, ei, N_EXPERTS), axis=0, keepdims=True)
        hit = ei == idx
        chosen = chosen | hit
        cand = jnp.where(hit, -jnp.inf, cand)
    w = jnp.where(chosen, scores, 0.0)
    w = w / jnp.sum(w, axis=0, keepdims=True) * ROUTED_SCALE
    gate_ref[0] = jnp.concatenate([w, jnp.zeros((V7X_LANES - N_EXPERTS, tm), jnp.float32)], axis=0).T


def moe_route(x, g, mod, w_router, router_bias, tm):
    B, T, D = x.shape
    tm = min(tm, T)
    tok = pl.BlockSpec((1, tm, D), lambda b, i: (b, i, 0))
    return pl.pallas_call(
        _route_kernel,
        out_shape=(jax.ShapeDtypeStruct((B, T, D), BF), jax.ShapeDtypeStruct((B, T, V7X_LANES), jnp.float32)),
        grid=(B, T // tm),
        in_specs=[tok, pl.BlockSpec((1, D), lambda b, i: (0, 0)), pl.BlockSpec((1, 6, D), lambda b, i: (b, 0, 0)),
                  pl.BlockSpec((N_EXPERTS, D), lambda b, i: (0, 0)), pl.BlockSpec((N_EXPERTS, 1), lambda b, i: (0, 0))],
        out_specs=(tok, pl.BlockSpec((1, tm, V7X_LANES), lambda b, i: (b, i, 0))),
        compiler_params=_cparams("parallel", "parallel"),
        name="moe_route",
    )(x, g.reshape(1, D), mod, w_router.T, router_bias.reshape(N_EXPERTS, 1))


def _moe_kernel(h_ref, gate_ref, x_ref, mod_ref, sel_ref, wg_ref, wu_ref, wd_ref, sg_ref, su_ref, sd_ref, o_ref, *, tm):
    j = pl.program_id(1)
    rows = pl.ds(pl.multiple_of(pl.program_id(2) * tm, tm), tm)
    h = h_ref[0]

    @pl.when(j == 0)
    def _():
        sh = jax.nn.silu(_bdot(h, sg_ref[...])) * _bdot(h, su_ref[...])
        o_ref[0, rows, :] = _bdot(sh, sd_ref[...])

    g8 = _split_dot(gate_ref[0], sel_ref[0], 2)
    act = jax.nn.silu(_bdot(h, wg_ref[...])) * _bdot(h, wu_ref[...])
    act = jnp.concatenate([act[:, e * D_EXPERT:(e + 1) * D_EXPERT] * g8[:, e:e + 1] for e in range(MOE_STEP_EXPERTS)],
                          axis=1)
    o_ref[0, rows, :] += _bdot(act, wd_ref[...])

    @pl.when(j == pl.num_programs(1) - 1)
    def _():
        o_ref[0, rows, :] = x_ref[0] + mod_ref[0, 5:6, :] * o_ref[0, rows, :]


def _moe_select_table():
    se = MOE_STEP_EXPERTS
    t = np.zeros((N_EXPERTS // se, V7X_LANES, V7X_LANES), np.float32)
    for j in range(N_EXPERTS // se):
        for e in range(se):
            t[j, j * se + e, e] = 1.0
    return jnp.asarray(t, BF)


def moe_apply(h2, gates, x, mod, wg, wu, wd, sg, su, sd, tm):
    B, T, D = x.shape
    sw = MOE_STEP_EXPERTS * D_EXPERT
    n_groups = N_EXPERTS // MOE_STEP_EXPERTS
    tok = pl.BlockSpec((1, tm, D), lambda b, j, i: (b, i, 0))
    x_last = pl.BlockSpec((1, tm, D), lambda b, j, i: (b, jnp.where(j == n_groups - 1, i, 0), 0))
    full = lambda a: pl.BlockSpec(a.shape, lambda b, j, i: (0,) * a.ndim)
    return pl.pallas_call(
        functools.partial(_moe_kernel, tm=tm), out_shape=jax.ShapeDtypeStruct((B, T, D), jnp.float32),
        grid=(B, n_groups, T // tm),
        in_specs=[tok, pl.BlockSpec((1, tm, V7X_LANES), lambda b, j, i: (b, i, 0)), x_last,
                  pl.BlockSpec((1, 6, D), lambda b, j, i: (b, 0, 0)),
                  pl.BlockSpec((1, V7X_LANES, V7X_LANES), lambda b, j, i: (j, 0, 0)),
                  pl.BlockSpec((D, sw), lambda b, j, i: (0, j)), pl.BlockSpec((D, sw), lambda b, j, i: (0, j)),
                  pl.BlockSpec((sw, D), lambda b, j, i: (j, 0)), full(sg), full(su), full(sd)],
        out_specs=pl.BlockSpec((1, T, D), lambda b, j, i: (b, 0, 0)),
        compiler_params=pltpu.CompilerParams(dimension_semantics=("parallel", "arbitrary", "arbitrary"),
                                             vmem_limit_bytes=MOE_VMEM_LIMIT_BYTES),
        name="moe_experts",
    )(h2, gates, x, mod, _moe_select_table(), wg, wu, wd, sg, su, sd)


def _reorder_w_in(w):
    o_na = RWKV_COLS
    o_ml = o_na + NA_COLS
    o_mg = o_ml + MLSTM_MAIN_COLS
    o_gate = o_mg + MLSTM_GATES
    pad = jnp.zeros((w.shape[0], V7X_LANES - MLSTM_GATES), w.dtype)
    return jnp.concatenate([w[:, o_na:o_ml], w[:, o_ml:o_mg], w[:, o_gate:], w[:, :RWKV_COLS], w[:, o_mg:o_gate], pad],
                           axis=1).astype(BF)


def kernel(x, c, ctx, c_ctx, w_ada, b_ada, norm1_g, norm2_g, w_in, rw_mu, rw_w0, rw_w2, rw_a0, rw_a2, rw_k_k, rw_k_a, rw_r_k, rw_g2, rw_lnx_g, rw_lnx_b, rw_v0, rw_v1, rw_v2, na_qn_g, na_kn_g, na_rpb, ml_i_bias, ml_f_bias, ml_norm_g, w_branch, w_out, moe_router, moe_bias, moe_w_gate, moe_w_up, moe_w_down, sh_w_gate, sh_w_up, sh_w_down):
    B, S, D = x.shape
    n_ctx = ctx.shape[1]
    tm = TOKEN_TILE
    assert S % tm == 0 and n_ctx % min(tm, n_ctx) == 0 and PROJ_COLS == IN_COLS + V7X_LANES - MLSTM_GATES
    rope = rope_tables(S)
    n_cond = B + 1
    cond_pad = (-n_cond) % V7X_SUBLANES
    s_cond = jnp.pad(jnp.concatenate([jax.nn.silu(c), jax.nn.silu(c_ctx)[None]], axis=0), ((0, cond_pad), (0, 0)))
    vf_x = vf_c = None
    for l in range(DEPTH):
        need_ctx = l < DEPTH - 1
        mod = pmm(s_cond, w_ada[l]) + b_ada[l]
        mod_x = mod[:B].reshape(B, 6, D)
        mod_c = jnp.broadcast_to(mod[B].reshape(1, 6, D), (B, 6, D))
        w_proj = _reorder_w_in(w_in[l])
        proj_x = norm_proj(x, norm1_g[l], mod_x, w_proj)
        proj_c = norm_proj(ctx.reshape(1, B * n_ctx, D), norm1_g[l], mod_c[:1], w_proj).reshape(B, n_ctx, PROJ_COLS)
        vres = None if l == 0 else (rw_v0[l - 1], rw_v1[l - 1], rw_v2[l - 1])
        rw_x, rw_c, vf_x, vf_c = rwkv_mix(proj_x, proj_c, vf_x, vf_c, rw_mu[l], rw_w0[l], rw_w2[l], rw_a0[l], rw_a2[l],
                                          rw_k_k[l], rw_k_a[l], rw_g2[l], vres, tm)
        rw_params = (rw_k_a[l], rw_r_k[l], rw_lnx_g[l], rw_lnx_b[l])
        yb_x, yb_c = na_mix(proj_x, proj_c, na_qn_g[l], na_kn_g[l], na_rpb[l], need_ctx, tm)
        hm_x, hm_c = mlstm_mix(proj_x, proj_c, ml_i_bias[l], ml_f_bias[l], rope, tm)
        wb = w_branch[l].astype(BF)
        wo = w_out[l].astype(BF)
        wg = moe_w_gate[l].transpose(1, 0, 2).reshape(D, N_EXPERTS * D_EXPERT).astype(BF)
        wu = moe_w_up[l].transpose(1, 0, 2).reshape(D, N_EXPERTS * D_EXPERT).astype(BF)
        wd = moe_w_down[l].reshape(N_EXPERTS * D_EXPERT, D).astype(BF)
        shared = (sh_w_gate[l].astype(BF), sh_w_up[l].astype(BF), sh_w_down[l].astype(BF))
        x = merge_apply(rw_x, yb_x, hm_x, proj_x, x, mod_x, rw_params, ml_norm_g[l], wb, wo, tm)
        h2, gates = moe_route(x, norm2_g[l], mod_x, moe_router[l], moe_bias[l], tm)
        x = moe_apply(h2, gates, x, mod_x, wg, wu, wd, *shared, MOE_TOKEN_TILE)
        if need_ctx:
            ctx = merge_apply(rw_c, yb_c, hm_c, proj_c, ctx, mod_c, rw_params, ml_norm_g[l], wb, wo, tm)
            h2, gates = moe_route(ctx, norm2_g[l], mod_c, moe_router[l], moe_bias[l], tm)
            ctx = moe_apply(h2, gates, ctx, mod_c, wg, wu, wd, *shared, min(tm, n_ctx))
    return x
```

```python
import functools

import numpy as np
import jax
import jax.numpy as jnp
from jax import lax
from jax.experimental import pallas as pl
from jax.experimental.pallas import tpu as pltpu

D_MODEL = 1024
DEPTH = 2
GRID_W = 64
N_DIR = 2
N_BRANCH = 3
BRANCH_WIDTH = 512
RMS_EPS = 1e-6
NEG_INF = -1e30

RWKV_HEAD = 64
RWKV_DECAY_LORA = 64
RWKV_LORA_COLS = 384
RWKV_COLS = 3 * BRANCH_WIDTH + RWKV_LORA_COLS
RWKV_LNX_EPS = 64e-5
RWKV_DECAY_SCALE = float(np.exp(-0.5))
RWKV_CHUNK = 64
WKV_SEQS_PER_STEP = 8

NA_HEAD = 64
NA_HEADS = BRANCH_WIDTH // NA_HEAD
NA_WIN_R = 8
NA_WIN_C = 16
NA_COLS = 3 * BRANCH_WIDTH
NA_ROWS_PER_STEP = 4

MLSTM_HEADS = 4
MLSTM_DQK = 64
MLSTM_DV = BRANCH_WIDTH // MLSTM_HEADS
MLSTM_QK = MLSTM_HEADS * MLSTM_DQK
MLSTM_CHUNK = 64
MLSTM_SEQS_PER_STEP = 16
MLSTM_GATE_SLOTS = 8
MLSTM_MAIN_COLS = 2 * MLSTM_QK + 2 * BRANCH_WIDTH
MLSTM_GATES = 2 * N_DIR * MLSTM_HEADS
GATE_CAP = 15.0
ROPE_ROT = MLSTM_DQK // 4
ROPE_BASE = 10000.0

N_EXPERTS = 64
TOP_K = 8
N_GROUPS = 8
TOPK_GROUPS = 4
D_EXPERT = 128
ROUTED_SCALE = 2.5
MOE_STEP_EXPERTS = 8
MOE_TOKEN_TILE = 1024

GATE_COLS = N_BRANCH * D_MODEL
IN_COLS = RWKV_COLS + NA_COLS + MLSTM_MAIN_COLS + MLSTM_GATES + GATE_COLS

TOKEN_TILE = 512
V7X_LANES = 128
V7X_SUBLANES = 8
HP = 2 * RWKV_HEAD
VMEM_LIMIT_BYTES = 48 * 1024 * 1024
MOE_VMEM_LIMIT_BYTES = 56 * 1024 * 1024

PROJ_COLS = 8192
NA_BLK = 0
ML_BLK = 1
ML_OGATE_BLK = (NA_COLS + 2 * MLSTM_QK + BRANCH_WIDTH) // BRANCH_WIDTH
GATE_BLK0 = (NA_COLS + MLSTM_MAIN_COLS) // D_MODEL
RWKV_COL0 = NA_COLS + MLSTM_MAIN_COLS + GATE_COLS
RWKV_BLK0 = RWKV_COL0 // BRANCH_WIDTH
RWKV_LORA_BLK = (RWKV_COL0 + 3 * BRANCH_WIDTH) // RWKV_LORA_COLS
ML_GATE_BLK = (RWKV_COL0 + RWKV_COLS) // V7X_LANES

BF = jnp.bfloat16
ACT_DTYPE = BF
HALO_ROWS = 16
HIGHEST = lax.Precision.HIGHEST


def _cparams(*sem):
    return pltpu.CompilerParams(dimension_semantics=sem, vmem_limit_bytes=VMEM_LIMIT_BYTES)


def _bdot(a, b):
    return jnp.dot(a.astype(BF), b.astype(BF), preferred_element_type=jnp.float32)


def _dot_nt(a, b):
    return lax.dot_general(a.astype(BF), b.astype(BF), (((1,), (1,)), ((), ())), preferred_element_type=jnp.float32)


def _dot_tn(a, b):
    return lax.dot_general(a.astype(BF), b.astype(BF), (((0,), (0,)), ((), ())), preferred_element_type=jnp.float32)


def _split_dot(x, w, parts):
    out = None
    rem = x
    for _ in range(parts):
        piece = rem.astype(BF)
        rem = rem - piece.astype(jnp.float32)
        t = jnp.dot(piece, w, preferred_element_type=jnp.float32)
        out = t if out is None else out + t
    return out


def _pair_sums(x):
    lo = _lo_lanes(x.shape[0])
    parts = []
    for c in range(x.shape[1] // HP):
        blk = x[:, c * HP:(c + 1) * HP]
        s_lo = jnp.sum(jnp.where(lo, blk, 0.0), axis=1, keepdims=True)
        s_hi = jnp.sum(jnp.where(lo, 0.0, blk), axis=1, keepdims=True)
        parts.append(jnp.where(lo, s_lo, s_hi))
    return jnp.concatenate(parts, axis=1)


def _head_ones(width, head):
    i = np.arange(width) // head
    return jnp.asarray(i[:, None] == i[None, :], BF)


def _stack_pair(x, lo):
    zero = jnp.zeros_like(x)
    return jnp.concatenate([jnp.where(lo, x, zero), jnp.where(lo, zero, x)], axis=0)


def _lo_lanes(n):
    return lax.broadcasted_iota(jnp.int32, (n, HP), 1) < (HP // 2)


def _mm_kernel(x_ref, w_ref, o_ref):
    part = _bdot(x_ref[...], w_ref[...])

    @pl.when(pl.program_id(2) == 0)
    def _():
        o_ref[...] = part

    @pl.when(pl.program_id(2) > 0)
    def _():
        o_ref[...] += part


def _pick_tile(n, cands):
    for c in cands:
        if n % c == 0:
            return c
    return n


def pmm(x, w):
    M, K = x.shape
    N = w.shape[1]
    tm = _pick_tile(M, (1024, 512, 256, 128, 64, 32, 16, 8))
    tn = _pick_tile(N, (1024, 512, 384, 256, 128))
    tk = _pick_tile(K, (1024,)) if K > 1024 else K
    return pl.pallas_call(
        _mm_kernel,
        out_shape=jax.ShapeDtypeStruct((M, N), jnp.float32),
        grid=(M // tm, N // tn, K // tk),
        in_specs=[pl.BlockSpec((tm, tk), lambda i, j, k: (i, k)),
                  pl.BlockSpec((tk, tn), lambda i, j, k: (k, j))],
        out_specs=pl.BlockSpec((tm, tn), lambda i, j, k: (i, j)),
        compiler_params=_cparams("parallel", "parallel", "arbitrary"),
        name="tiled_matmul",
    )(x, w)


def _norm_mod(x, g, scale, shift):
    xn = x * lax.rsqrt(jnp.mean(x * x, axis=-1, keepdims=True) + RMS_EPS)
    return xn * g * (1.0 + scale) + shift


def _norm_proj_kernel(x_ref, g_ref, mod_ref, w_ref, o_ref, h_scr):
    @pl.when(pl.program_id(2) == 0)
    def _():
        h_scr[...] = _norm_mod(x_ref[0], g_ref[...], mod_ref[0, 1:2, :], mod_ref[0, 0:1, :]).astype(BF)

    o_ref[0] = jnp.dot(h_scr[...], w_ref[...], preferred_element_type=jnp.float32).astype(o_ref.dtype)


def norm_proj(x, g, mod, w):
    B, T, D = x.shape
    N = w.shape[1]
    tm = _pick_tile(T, (1024, 512, 256))
    tn = _pick_tile(N, (1024, 512, 256, 128))
    return pl.pallas_call(
        _norm_proj_kernel,
        out_shape=jax.ShapeDtypeStruct((B, T, N), ACT_DTYPE), grid=(B, T // tm, N // tn),
        in_specs=[pl.BlockSpec((1, tm, D), lambda b, i, j: (b, i, 0)), pl.BlockSpec((1, D), lambda b, i, j: (0, 0)),
                  pl.BlockSpec((1, 6, D), lambda b, i, j: (b, 0, 0)), pl.BlockSpec((D, tn), lambda b, i, j: (0, j))],
        out_specs=pl.BlockSpec((1, tm, tn), lambda b, i, j: (b, i, j)),
        scratch_shapes=[pltpu.VMEM((tm, D), BF)],
        compiler_params=_cparams("parallel", "parallel", "arbitrary"),
        name="norm_proj",
    )(x, g.reshape(1, D), mod, w)


def _shifted(z, prev_row, next_row):
    n = z.shape[0]
    s = V7X_SUBLANES
    row = lax.broadcasted_iota(jnp.int32, (s, z.shape[1]), 0)
    zp = pltpu.roll(z, 1, axis=0)
    zn = pltpu.roll(z, n - 1, axis=0)
    zp = jnp.concatenate([jnp.where(row == 0, prev_row, zp[:s]), zp[s:]], axis=0)
    zn = jnp.concatenate([zn[:n - s], jnp.where(row == s - 1, next_row, zn[n - s:])], axis=0)
    return zp, zn


def _rwkv_feat_kernel(zr_ref, zk_ref, zv_ref, zl_ref, pr_ref, pk_ref, pv_ref, pl_ref, nr_ref, nk_ref, nv_ref, nl_ref,
                      mu_ref, w0_ref, w2_ref, a0_ref, a2_ref, kk_ref, g2_ref, ones_ref, *rest, has_vres):
    if has_vres:
        v0_ref, v1_ref, v2_ref, vf_ref = rest[:4]
        outs = rest[4:]
    else:
        outs = rest
    r_ref, k_ref, v_ref, kn_ref, g_ref, lw0_ref, lw1_ref, ag0_ref, ag1_ref = outs
    C = BRANCH_WIDTH
    first = pl.program_id(1) == 0
    last = pl.program_id(1) == pl.num_programs(1) - 1

    def shift(z_ref, p_ref, n_ref, c0, c1):
        z = z_ref[0].astype(jnp.float32)
        prev_row = jnp.where(first, 0.0, p_ref[0].astype(jnp.float32)[HALO_ROWS - 1:HALO_ROWS, :])
        next_row = jnp.where(last, 0.0, n_ref[0].astype(jnp.float32)[0:1, :])
        zp, zn = _shifted(z, prev_row, next_row)
        return z + mu_ref[:, c0:c1] * (0.5 * (zp + zn) - z)

    r = shift(zr_ref, pr_ref, nr_ref, 0, C)
    k = shift(zk_ref, pk_ref, nk_ref, C, 2 * C)
    v = shift(zv_ref, pv_ref, nv_ref, 2 * C, 3 * C)
    zl = shift(zl_ref, pl_ref, nl_ref, 3 * C, RWKV_COLS)
    wd = jnp.tanh(zl[:, 0:2 * RWKV_DECAY_LORA])
    ad = zl[:, 2 * RWKV_DECAY_LORA:4 * RWKV_DECAY_LORA]
    gd = zl[:, 4 * RWKV_DECAY_LORA:]
    for d, (lw_ref, ag_ref) in enumerate(((lw0_ref, ag0_ref), (lw1_ref, ag1_ref))):
        u = w0_ref[d:d + 1, :] + _bdot(wd, w2_ref[d])
        lw_ref[0] = -RWKV_DECAY_SCALE * jax.nn.sigmoid(u)
        ag_ref[0] = jax.nn.sigmoid(a0_ref[d:d + 1, :] + _bdot(ad, a2_ref[d])).astype(ag_ref.dtype)
    kq = k * kk_ref[...]
    ss = _split_dot(kq * kq, ones_ref[...], 2)
    kn_ref[0] = (kq * lax.rsqrt(jnp.maximum(ss, 1e-24))).astype(kn_ref.dtype)
    if has_vres:
        lora = _bdot(_bdot(v, v1_ref[...]), v2_ref[...])
        v = v + (vf_ref[0].astype(jnp.float32) - v) * jax.nn.sigmoid(v0_ref[...] + lora)
    g_ref[0] = _bdot(jax.nn.sigmoid(gd), g2_ref[...]).astype(g_ref.dtype)
    r_ref[0] = r.astype(r_ref.dtype)
    k_ref[0] = k.astype(k_ref.dtype)
    v_ref[0] = v.astype(v_ref.dtype)


def rwkv_features(proj, mu, w0, w2, a0, a2, k_k, g2, vres, v_first, tm):
    B, T, _ = proj.shape
    tm = min(tm, T)
    C = BRANCH_WIDTH
    nt = T // tm
    zpad = jnp.zeros((RWKV_DECAY_LORA, C), jnp.float32)
    pad_dirs = lambda w: jnp.stack([jnp.concatenate([w[0], zpad], 0), jnp.concatenate([zpad, w[1]], 0)]).astype(BF)
    sub = HALO_ROWS
    blk = lambda w, j: pl.BlockSpec((1, tm, w), lambda b, i: (b, i, j))
    before = lambda w, j: pl.BlockSpec((1, sub, w), lambda b, i: (b, jnp.maximum(i * (tm // sub) - 1, 0), j))
    after = lambda w, j: pl.BlockSpec((1, sub, w), lambda b, i: (b, jnp.minimum((i + 1) * (tm // sub), T // sub - 1), j))
    tok = pl.BlockSpec((1, tm, C), lambda b, i: (b, i, 0))
    full = lambda a: pl.BlockSpec(a.shape, lambda b, i: (0,) * a.ndim)
    params = [mu.reshape(1, -1), w0, pad_dirs(w2), a0, pad_dirs(a2), k_k.reshape(1, -1), g2.astype(BF),
              _head_ones(C, RWKV_HEAD)]
    pieces = [(C, RWKV_BLK0), (C, RWKV_BLK0 + 1), (C, RWKV_BLK0 + 2), (RWKV_LORA_COLS, RWKV_LORA_BLK)]
    args = [proj] * (3 * len(pieces)) + params
    specs = ([blk(w, j) for w, j in pieces] + [before(w, j) for w, j in pieces] + [after(w, j) for w, j in pieces]
             + [full(a) for a in params])
    if vres is not None:
        v0, v1, v2 = vres
        extra = [v0.reshape(1, -1), jnp.pad(v1, ((0, 0), (0, V7X_LANES - v1.shape[1]))).astype(BF),
                 jnp.pad(v2, ((0, V7X_LANES - v2.shape[0]), (0, 0))).astype(BF)]
        args += extra + [v_first]
        specs += [full(a) for a in extra] + [tok]
    return pl.pallas_call(
        functools.partial(_rwkv_feat_kernel, has_vres=vres is not None),
        out_shape=(jax.ShapeDtypeStruct((B, T, C), ACT_DTYPE),) * 5 + (jax.ShapeDtypeStruct((B, T, C), jnp.float32),) * 2
                  + (jax.ShapeDtypeStruct((B, T, C), ACT_DTYPE),) * 2,
        grid=(B, nt), in_specs=specs, out_specs=(tok,) * 9,
        compiler_params=_cparams("parallel", "parallel"),
        name="rwkv_features",
    )(*args)


def _wkv_kernel(r_ref, lw_ref, kk_ref, a_ref, k_ref, v_ref, ka_ref, s0_ref, *rest, reverse, bb, n_pairs, has_prev):
    if has_prev:
        yprev_ref, y_ref, sout_ref, s_scr = rest
    else:
        y_ref, sout_ref, s_scr = rest
    C = RWKV_CHUNK
    c_idx = pl.program_id(1)

    @pl.when(c_idx == 0)
    def _():
        s_scr[...] = s0_ref[...]

    ti = lax.broadcasted_iota(jnp.int32, (C, C), 0)
    si = lax.broadcasted_iota(jnp.int32, (C, C), 1)
    tri = ((si >= ti) if reverse else (si <= ti)).astype(jnp.float32)
    tp = lax.broadcasted_iota(jnp.int32, (C, 2 * C), 0)
    sp = lax.broadcasted_iota(jnp.int32, (C, 2 * C), 1) % C
    m_strict = (sp > tp) if reverse else (sp < tp)
    m_incl = (sp >= tp) if reverse else (sp <= tp)
    eye = (tp == sp).astype(jnp.float32)
    t2 = lax.broadcasted_iota(jnp.int32, (2 * C, 2 * C), 0)
    s2 = lax.broadcasted_iota(jnp.int32, (2 * C, 2 * C), 1)
    same_head = (t2 // C) == (s2 // C)
    lo = _lo_lanes(C)
    units = [(bi, slice(p * HP, (p + 1) * HP), p) for bi in range(bb) for p in range(n_pairs)]
    n = len(units)
    cat = lambda xs: jnp.concatenate(xs, axis=0)
    stack = lambda x: _stack_pair(x, lo)
    bdiag = lambda x: jnp.where(same_head, cat([x, x]), jnp.zeros((), x.dtype))

    ar, bk, bkh, v, e_tot = [], [], [], [], []
    for bi, sl, _ in units:
        lw = lw_ref[bi, :, sl]
        kk = kk_ref[bi, :, sl].astype(jnp.float32)
        ag = a_ref[bi, :, sl].astype(jnp.float32)
        kd = k_ref[bi, :, sl].astype(jnp.float32) * (1.0 + (ag - 1.0) * ka_ref[:, sl])
        cum = jnp.dot(tri, lw, precision=HIGHEST, preferred_element_type=jnp.float32)
        tot = jnp.sum(lw, axis=0, keepdims=True)
        e_neg = jnp.exp(-cum)
        e_end = jnp.exp(tot - cum)
        b = kk * ag
        ar.append(cat([-kk * jnp.exp(cum - lw), r_ref[bi, :, sl].astype(jnp.float32) * jnp.exp(cum)]).astype(BF))
        bk.append(cat([stack(b * e_neg), stack(kd * e_neg)]).astype(BF))
        bkh.append(cat([b * e_end, kd * e_end]).astype(BF))
        v.append(v_ref[bi, :, sl].astype(BF))
        e_tot.append(jnp.exp(tot))
    gram = [_dot_nt(ar[i], bk[i]) for i in range(n)]
    l_ab = [jnp.where(m_strict, g[:C, :2 * C], 0.0) for g in gram]
    l_ak = [jnp.where(m_strict, g[:C, 2 * C:], 0.0).astype(BF) for g in gram]
    l_rbk = [jnp.concatenate([jnp.where(m_incl, g[C:, :2 * C], 0.0), jnp.where(m_incl, g[C:, 2 * C:], 0.0)],
                             axis=1).astype(BF) for g in gram]
    vs = [stack(x) for x in v]
    s0 = [s_scr[bi, p] for bi, _, p in units]
    proj = [_dot_nt(ar[i], s0[i]) for i in range(n)]
    lv = [_bdot(l_ak[i], vs[i]) for i in range(n)]
    tinv = [eye + m for m in l_ab]
    pw_bd = [bdiag(m.astype(BF)) for m in l_ab]
    pw = [_bdot(l_ab[i], pw_bd[i]).astype(BF) for i in range(n)]
    levels = 5
    for lvl in range(1, levels + 1):
        pw_bd = [bdiag(m) for m in pw]
        if lvl < levels:
            both = [_bdot(cat([pw[i], tinv[i].astype(BF)]), pw_bd[i]) for i in range(n)]
            pw = [x[:C].astype(BF) for x in both]
            tinv = [tinv[i] + both[i][C:] for i in range(n)]
        else:
            tinv = [tinv[i] + _bdot(tinv[i], pw_bd[i]) for i in range(n)]
    u = [_bdot(tinv[i], stack((proj[i][:C] + lv[i]).astype(BF))) for i in range(n)]
    ub = [x.astype(BF) for x in u]
    ys = [proj[i][C:] + _bdot(l_rbk[i], cat([stack(ub[i]), vs[i]])) for i in range(n)]
    upd = [_dot_tn(cat([ub[i], v[i]]), bkh[i]) for i in range(n)]
    for i, (bi, sl, p) in enumerate(units):
        y = ys[i]
        if has_prev:
            y = y + yprev_ref[bi, :, sl]
        y_ref[bi, :, sl] = y
        s_scr[bi, p] = s0[i] * e_tot[i] + jnp.where(same_head, upd[i], 0.0)

    @pl.when(c_idx == pl.num_programs(1) - 1)
    def _():
        sout_ref[...] = s_scr[...]


def wkv_chunked(r, lw, kk, ag, k, v, k_a, s0, y_prev, reverse):
    B, T, W = r.shape
    C = RWKV_CHUNK
    bb = WKV_SEQS_PER_STEP
    nc = T // C
    n_pairs = W // HP
    cmap = (lambda b, c: (b, nc - 1 - c, 0)) if reverse else (lambda b, c: (b, c, 0))
    tok = pl.BlockSpec((bb, C, W), cmap)
    st = pl.BlockSpec((bb, n_pairs, HP, HP), lambda b, c: (b, 0, 0, 0))
    has_prev = y_prev is not None
    args = [r, lw, kk, ag, k, v, k_a, s0] + ([y_prev] if has_prev else [])
    return pl.pallas_call(
        functools.partial(_wkv_kernel, reverse=reverse, bb=bb, n_pairs=n_pairs, has_prev=has_prev),
        out_shape=(jax.ShapeDtypeStruct((B, T, W), jnp.float32), jax.ShapeDtypeStruct(s0.shape, jnp.float32)),
        grid=(B // bb, nc),
        in_specs=[tok] * 6 + [pl.BlockSpec((1, W), lambda b, c: (0, 0)), st] + ([tok] if has_prev else []),
        out_specs=(tok, st),
        scratch_shapes=[pltpu.VMEM((bb, n_pairs, HP, HP), jnp.float32)],
        compiler_params=_cparams("parallel", "arbitrary"),
        name="wkv_chunked",
    )(*args)


def _rwkv_readout_tile(y, r, k, v, g, ag0, ag1, ka, rk, lnx_g, lnx_b):
    mean = _pair_sums(y) * (1.0 / RWKV_HEAD)
    yc = y - mean
    var = _pair_sums(yc * yc) * (1.0 / RWKV_HEAD)
    yn = yc * lax.rsqrt(var + RWKV_LNX_EPS) * lnx_g + lnx_b
    ksum = k * (2.0 + (ag0 + ag1 - 2.0) * ka)
    bonus = _pair_sums(r * ksum * rk) * v
    return (yn + bonus) * g


def rwkv_mix(proj_x, proj_c, vf_x, vf_c, mu, w0, w2, a0, a2, k_k, k_a, g2, vres, tm):
    B = proj_x.shape[0]
    fx = rwkv_features(proj_x, mu, w0, w2, a0, a2, k_k, g2, vres, vf_x, tm)
    fc = rwkv_features(proj_c, mu, w0, w2, a0, a2, k_k, g2, vres, vf_c, tm)
    ka = k_a.reshape(1, -1)
    y_x = y_c = None
    for d in range(N_DIR):
        s0 = jnp.zeros((B, BRANCH_WIDTH // HP, HP, HP), jnp.float32)
        y_c, s_ctx = wkv_chunked(fc[0], fc[5 + d], fc[3], fc[7 + d], fc[1], fc[2], ka, s0, y_c, d == 1)
        y_x, _ = wkv_chunked(fx[0], fx[5 + d], fx[3], fx[7 + d], fx[1], fx[2], ka, s_ctx, y_x, d == 1)
    pick = lambda y, f: (y, f[0], f[1], f[2], f[4], f[7], f[8])
    vf_x = fx[2] if vres is None else vf_x
    vf_c = fc[2] if vres is None else vf_c
    return pick(y_x, fx), pick(y_c, fc), vf_x, vf_c


def _qknorm_kernel(z_ref, qg_ref, kg_ref, q_ref, k_ref, v_ref):
    C = BRANCH_WIDTH
    z = z_ref[0].astype(jnp.float32)
    q = z[:, 0:C]
    k = z[:, C:2 * C]
    qn = q * lax.rsqrt(_pair_sums(q * q) * (1.0 / NA_HEAD) + RMS_EPS) * qg_ref[...]
    kn = k * lax.rsqrt(_pair_sums(k * k) * (1.0 / NA_HEAD) + RMS_EPS) * kg_ref[...]
    q_ref[0] = (qn * NA_HEAD ** -0.5).astype(BF)
    k_ref[0] = kn.astype(BF)
    v_ref[0] = z[:, 2 * C:3 * C].astype(BF)


def na_qknorm(proj, qn_g, kn_g, tm):
    B, T, _ = proj.shape
    tm = min(tm, T)
    C = BRANCH_WIDTH
    tok = pl.BlockSpec((1, tm, C), lambda b, i: (b, i, 0))
    par = pl.BlockSpec((1, C), lambda b, i: (0, 0))
    sd = jax.ShapeDtypeStruct((B, T, C), BF)
    return pl.pallas_call(
        _qknorm_kernel, out_shape=(sd, sd, sd), grid=(B, T // tm),
        in_specs=[pl.BlockSpec((1, tm, NA_COLS), lambda b, i: (b, i, NA_BLK)), par, par],
        out_specs=(tok, tok, tok),
        compiler_params=_cparams("parallel", "parallel"),
        name="na_qknorm",
    )(proj, jnp.tile(qn_g, NA_HEADS).reshape(1, C), jnp.tile(kn_g, NA_HEADS).reshape(1, C))


def na_bias_table(rpb):
    qc = np.arange(GRID_W)[:, None]
    kc = np.arange(GRID_W)[None, :]
    cs = np.clip(qc - NA_WIN_C // 2, 0, GRID_W - NA_WIN_C)
    valid = (kc >= cs) & (kc < cs + NA_WIN_C)
    cidx = np.clip(kc - qc + NA_WIN_C - 1, 0, 2 * NA_WIN_C - 2)
    t = jnp.where(valid[None, None], rpb[:, :, cidx], NEG_INF)
    t2 = jnp.concatenate([t[:, :-1], t[:, 1:]], axis=-1)
    H = rpb.shape[0]
    t2 = t2.reshape(H // 2, 2, 2 * NA_WIN_R - 2, GRID_W, 2 * GRID_W).transpose(0, 2, 1, 3, 4)
    return t2.reshape(H // 2, 2 * NA_WIN_R - 2, 2 * GRID_W, 2 * GRID_W)


def _na_kernel(q_ref, k_ref, v_ref, kc_ref, vc_ref, bias_ref, o_ref, *, rows):
    nwin = NA_WIN_R * GRID_W
    lo = _lo_lanes(GRID_W)
    units = []
    for rr in range(NA_ROWS_PER_STEP):
        r = pl.program_id(1) * NA_ROWS_PER_STEP + rr
        rs = jnp.clip(r - NA_WIN_R // 2, 0, rows - NA_WIN_R)
        k0 = pl.multiple_of(rs * GRID_W, GRID_W)
        for p in range(BRANCH_WIDTH // HP):
            units.append((rr, p, slice(p * HP, (p + 1) * HP), rs - r + NA_WIN_R - 1, k0))
    qs = [_stack_pair(q_ref[0, rr * GRID_W:(rr + 1) * GRID_W, sl], lo) for rr, _, sl, _, _ in units]
    s_loc = [_dot_nt(qs[i], k_ref[0, pl.ds(u[4], nwin), u[2]]) for i, u in enumerate(units)]
    s_ctx = [_dot_nt(qs[i], kc_ref[0, :, u[2]]) for i, u in enumerate(units)]
    p_loc, p_ctx, den = [], [], []
    for i, (_, p, _, base, _) in enumerate(units):
        sl_b = s_loc[i] + jnp.concatenate([bias_ref[p, base + 2 * j] for j in range(NA_WIN_R // 2)], axis=1)
        m = jnp.maximum(jnp.max(sl_b, axis=1, keepdims=True), jnp.max(s_ctx[i], axis=1, keepdims=True))
        el = jnp.exp(sl_b - m)
        ec = jnp.exp(s_ctx[i] - m)
        den.append(jnp.sum(el, axis=1, keepdims=True) + jnp.sum(ec, axis=1, keepdims=True))
        p_loc.append(el.astype(BF))
        p_ctx.append(ec.astype(BF))
    o_loc = [_bdot(p_loc[i], v_ref[0, pl.ds(u[4], nwin), u[2]]) for i, u in enumerate(units)]
    o_ctx = [_bdot(p_ctx[i], vc_ref[0, :, u[2]]) for i, u in enumerate(units)]
    for i, (rr, _, sl, _, _) in enumerate(units):
        o = (o_loc[i] + o_ctx[i]) / den[i]
        o_ref[0, rr * GRID_W:(rr + 1) * GRID_W, sl] = jnp.where(lo, o[:GRID_W], o[GRID_W:])


def na_attention(q, k, v, kc, vc, bias_tab):
    B, S, C = q.shape
    rows = S // GRID_W
    n_ctx = kc.shape[1]
    seq = pl.BlockSpec((1, S, C), lambda b, r: (b, 0, 0))
    cx = pl.BlockSpec((1, n_ctx, C), lambda b, r: (b, 0, 0))
    row = pl.BlockSpec((1, NA_ROWS_PER_STEP * GRID_W, C), lambda b, r: (b, r, 0))
    return pl.pallas_call(
        functools.partial(_na_kernel, rows=rows),
        out_shape=jax.ShapeDtypeStruct((B, S, C), jnp.float32),
        grid=(B, rows // NA_ROWS_PER_STEP),
        in_specs=[row, seq, seq, cx, cx, pl.BlockSpec(bias_tab.shape, lambda b, r: (0, 0, 0, 0))],
        out_specs=row,
        compiler_params=_cparams("parallel", "arbitrary"),
        name="na_attention",
    )(q, k, v, kc, vc, bias_tab)


def _ctx_attn_kernel(q_ref, k_ref, v_ref, o_ref):
    n = q_ref.shape[1]
    lo = _lo_lanes(n)
    sls = [slice(p * HP, (p + 1) * HP) for p in range(BRANCH_WIDTH // HP)]
    sc = [_dot_nt(_stack_pair(q_ref[0, :, sl], lo), k_ref[0, :, sl]) for sl in sls]
    e = [jnp.exp(x - jnp.max(x, axis=1, keepdims=True)) for x in sc]
    o = [_bdot(e[p], v_ref[0, :, sl]) / jnp.sum(e[p], axis=1, keepdims=True) for p, sl in enumerate(sls)]
    for p, sl in enumerate(sls):
        o_ref[0, :, sl] = jnp.where(lo, o[p][:n], o[p][n:])


def ctx_attention(q, k, v):
    B, n, C = q.shape
    blk = pl.BlockSpec((1, n, C), lambda b: (b, 0, 0))
    return pl.pallas_call(
        _ctx_attn_kernel, out_shape=jax.ShapeDtypeStruct((B, n, C), jnp.float32), grid=(B,),
        in_specs=[blk, blk, blk], out_specs=blk,
        compiler_params=_cparams("parallel"),
        name="ctx_attention",
    )(q, k, v)


def na_mix(proj_x, proj_c, qn_g, kn_g, rpb, need_ctx, tm):
    q, k, v = na_qknorm(proj_x, qn_g, kn_g, tm)
    qc, kc, vc = na_qknorm(proj_c, qn_g, kn_g, tm)
    out_x = na_attention(q, k, v, kc, vc, na_bias_table(rpb))
    out_c = ctx_attention(qc, kc, vc) if need_ctx else None
    return out_x, out_c


def rope_tables(n_tokens):
    t = jnp.arange(n_tokens)
    pos = jnp.stack([t // GRID_W, t % GRID_W], axis=-1).astype(jnp.float32)
    inv = ROPE_BASE ** (-jnp.arange(ROPE_ROT, dtype=jnp.float32) / ROPE_ROT)
    ang = pos[:, :, None] * inv
    cos_h = jnp.concatenate([jnp.cos(ang), jnp.cos(ang)], axis=-1).reshape(n_tokens, MLSTM_DQK)
    sin_h = jnp.concatenate([-jnp.sin(ang), jnp.sin(ang)], axis=-1).reshape(n_tokens, MLSTM_DQK)
    col = np.arange(MLSTM_QK)
    partner = np.where((col % (2 * ROPE_ROT)) < ROPE_ROT, col + ROPE_ROT, col - ROPE_ROT)
    perm = np.zeros((MLSTM_QK, MLSTM_QK), np.float32)
    perm[partner, col] = 1.0
    return jnp.tile(cos_h, (1, MLSTM_HEADS)), jnp.tile(sin_h, (1, MLSTM_HEADS)), jnp.asarray(perm, BF)


def _mlstm_prep_kernel(z_ref, gp_ref, ib_ref, fb_ref, *rest, rope):
    if rope:
        cos_ref, sin_ref, perm_ref, q_ref, k_ref, v_ref, g_ref = rest
    else:
        q_ref, k_ref, v_ref, g_ref = rest
    z = z_ref[0].astype(jnp.float32)
    q = z[:, 0:MLSTM_QK]
    k = z[:, MLSTM_QK:2 * MLSTM_QK]
    if rope:
        perm = perm_ref[...]
        parts = 1 if ACT_DTYPE == BF else 3
        q = q * cos_ref[...] + _split_dot(q, perm, parts) * sin_ref[...]
        k = k * cos_ref[...] + _split_dot(k, perm, parts) * sin_ref[...]
    q_ref[0] = (q * MLSTM_DQK ** -0.5).astype(BF)
    k_ref[0] = k.astype(BF)
    v_ref[0] = z[:, 2 * MLSTM_QK:2 * MLSTM_QK + BRANCH_WIDTH].astype(BF)
    gp = gp_ref[0].astype(jnp.float32)
    lane = lax.broadcasted_iota(jnp.int32, gp.shape, 1)
    ig = GATE_CAP * jnp.tanh((gp + ib_ref[...]) / GATE_CAP)
    fg = GATE_CAP * jnp.tanh((gp + fb_ref[...]) / GATE_CAP)
    g_ref[0] = jnp.where(lane < N_DIR * MLSTM_HEADS, ig, jax.nn.log_sigmoid(fg))


def mlstm_prep(proj, i_bias, f_bias, rope, tm):
    B, T, _ = proj.shape
    tm = min(tm, T)
    ng = N_DIR * MLSTM_HEADS
    ib = jnp.zeros((1, V7X_LANES), jnp.float32).at[0, 0:ng].set(i_bias.reshape(-1))
    fb = jnp.zeros((1, V7X_LANES), jnp.float32).at[0, ng:2 * ng].set(f_bias.reshape(-1))
    tok = lambda w: pl.BlockSpec((1, tm, w), lambda b, i: (b, i, 0))
    par = pl.BlockSpec((1, V7X_LANES), lambda b, i: (0, 0))
    args = [proj, proj, ib, fb]
    specs = [pl.BlockSpec((1, tm, MLSTM_MAIN_COLS), lambda b, i: (b, i, ML_BLK)),
             pl.BlockSpec((1, tm, V7X_LANES), lambda b, i: (b, i, ML_GATE_BLK)), par, par]
    if rope is not None:
        args += list(rope)
        specs += [pl.BlockSpec((tm, MLSTM_QK), lambda b, i: (i, 0)), pl.BlockSpec((tm, MLSTM_QK), lambda b, i: (i, 0)),
                  pl.BlockSpec((MLSTM_QK, MLSTM_QK), lambda b, i: (0, 0))]
    return pl.pallas_call(
        functools.partial(_mlstm_prep_kernel, rope=rope is not None),
        out_shape=(jax.ShapeDtypeStruct((B, T, MLSTM_QK), BF), jax.ShapeDtypeStruct((B, T, MLSTM_QK), BF),
                   jax.ShapeDtypeStruct((B, T, BRANCH_WIDTH), BF), jax.ShapeDtypeStruct((B, T, V7X_LANES), jnp.float32)),
        grid=(B, T // tm), in_specs=specs,
        out_specs=(tok(MLSTM_QK), tok(MLSTM_QK), tok(BRANCH_WIDTH), tok(V7X_LANES)),
        compiler_params=_cparams("parallel", "parallel"),
        name="mlstm_prep",
    )(*args)


def _mlstm_kernel(q_ref, k_ref, v_ref, gr_ref, c0_ref, n0_ref, m0_ref, *rest, reverse, bb, has_prev):
    if has_prev:
        hprev_ref, h_ref, cout_ref, nout_ref, mout_ref, c_scr, n_scr, m_scr = rest
    else:
        h_ref, cout_ref, nout_ref, mout_ref, c_scr, n_scr, m_scr = rest
    L = MLSTM_CHUNK
    DV = MLSTM_DV
    H = MLSTM_HEADS
    SL = MLSTM_GATE_SLOTS
    cidx = pl.program_id(1)

    @pl.when(cidx == 0)
    def _():
        c_scr[...] = c0_ref[...]
        n_scr[...] = n0_ref[...]
        m_scr[...] = m0_ref[...]

    ti = lax.broadcasted_iota(jnp.int32, (L, L), 0)
    si = lax.broadcasted_iota(jnp.int32, (L, L), 1)
    before = (si >= ti) if reverse else (si <= ti)
    tri = before.astype(jnp.float32)
    last = 0 if reverse else L - 1
    lo = _lo_lanes(L)
    lo_row = lax.broadcasted_iota(jnp.int32, (1, HP), 1) < (HP // 2)
    lo_col = lax.broadcasted_iota(jnp.int32, (HP, 1), 0) < (HP // 2)
    units = [(bi, p) for bi in range(bb) for p in range(H // 2)]
    blocks = [(bi, h) for bi in range(bb) for h in range(H)]
    nb = len(blocks)
    cat = lambda xs: jnp.concatenate(xs, axis=0)
    row_of = lambda bi, h: bi * SL + h

    i_rows = cat([gr_ref[bi, 0, 0, 0:SL, :] for bi in range(bb)])
    lf_rows = cat([gr_ref[bi, 0, 0, SL:2 * SL, :] for bi in range(bb)])
    m_prev = cat([m_scr[bi] for bi in range(bb)])
    b_rows = lax.dot_general(lf_rows, tri, (((1,), (1,)), ((), ())), precision=HIGHEST, preferred_element_type=jnp.float32)
    g_rows = i_rows - b_rows
    run = g_rows
    neg = jnp.full_like(g_rows, -jnp.inf)
    step = 1
    while step < L:
        shifted = (jnp.concatenate([run[:, step:], neg[:, :step]], axis=1) if reverse
                   else jnp.concatenate([neg[:, :step], run[:, :L - step]], axis=1))
        run = jnp.maximum(run, shifted)
        step *= 2
    m_rows = jnp.maximum(m_prev, run)
    m_end = jnp.broadcast_to(m_rows[:, last:last + 1], m_rows.shape)
    b_end = jnp.broadcast_to(b_rows[:, last:last + 1], b_rows.shape)
    sc_rows = jnp.exp(m_prev - m_rows)
    floor_rows = jnp.exp(-(b_rows + m_rows))
    wexp_rows = jnp.exp(g_rows - m_end)
    dec_rows = jnp.exp(m_prev - m_end)
    cols = cat([m_rows, sc_rows, floor_rows, wexp_rows]).T
    nr = bb * SL
    col_of = lambda kind, bi, h: cols[:, kind * nr + row_of(bi, h):kind * nr + row_of(bi, h) + 1]
    m_col = cat([col_of(0, bi, h) for bi, h in blocks])
    sc_col = cat([col_of(1, bi, h) for bi, h in blocks])
    floor_col = cat([col_of(2, bi, h) for bi, h in blocks])
    wexp_col = cat([col_of(3, bi, h) for bi, h in blocks])
    g_bcast = cat([jnp.broadcast_to(g_rows[row_of(bi, h):row_of(bi, h) + 1, :], (L, L)) for bi, h in blocks])
    decay = jnp.where(cat([before] * nb), jnp.exp(g_bcast - m_col), 0.0)

    qs = [_stack_pair(q_ref[bi, :, p * HP:(p + 1) * HP], lo) for bi, p in units]
    kp = [k_ref[bi, :, p * HP:(p + 1) * HP] for bi, p in units]
    vb = [v_ref[bi, :, h * DV:(h + 1) * DV] for bi, h in blocks]
    c_pair = [c_scr[bi, p] for bi, p in units]
    n_pair = [n_scr[bi, p] for bi, p in units]
    qk = cat([_dot_nt(qs[u], kp[u]) for u in range(len(units))])
    qc = cat([_bdot(qs[u], c_pair[u]) for u in range(len(units))])
    qn = cat([jnp.sum(qs[u].astype(jnp.float32) * n_pair[u], axis=1, keepdims=True) for u in range(len(units))])
    smat = qk * decay
    den = sc_col * qn + jnp.sum(smat, axis=1, keepdims=True)
    sb = smat.astype(BF)
    sv = cat([_bdot(sb[j * L:(j + 1) * L], vb[j]) for j in range(nb)])
    hout = (sc_col * qc + sv) / jnp.maximum(jnp.abs(den), floor_col)
    for j, (bi, h) in enumerate(blocks):
        hs = slice(h * DV, (h + 1) * DV)
        part = hout[j * L:(j + 1) * L]
        h_ref[bi, :, hs] = (part + hprev_ref[bi, :, hs]) if has_prev else part

    dec = lambda bi, h: dec_rows[row_of(bi, h):row_of(bi, h) + 1, 0:1]
    for u, (bi, p) in enumerate(units):
        j0 = bi * H + 2 * p
        kws = _stack_pair(kp[u], lo).astype(jnp.float32) * wexp_col[2 * u * L:2 * (u + 1) * L]
        upd = _dot_tn(kws, cat([vb[j0], vb[j0 + 1]]))
        c_scr[bi, p] = jnp.where(lo_col, dec(bi, 2 * p), dec(bi, 2 * p + 1)) * c_pair[u] + upd
        n_scr[bi, p] = jnp.where(lo_row, dec(bi, 2 * p), dec(bi, 2 * p + 1)) * n_pair[u] + jnp.sum(kws, axis=0, keepdims=True)
    m_next = b_end + m_end
    for bi in range(bb):
        m_scr[bi] = m_next[bi * SL:(bi + 1) * SL]

    @pl.when(cidx == pl.num_programs(1) - 1)
    def _():
        cout_ref[...] = c_scr[...]
        nout_ref[...] = n_scr[...]
        mout_ref[...] = m_scr[...]


def mlstm_gate_rows(gates):
    B, T, _ = gates.shape
    L = MLSTM_CHUNK
    H = MLSTM_HEADS
    SL = MLSTM_GATE_SLOTS
    ig = gates[:, :, :N_DIR * H].reshape(B, T, N_DIR, H)
    fg = gates[:, :, N_DIR * H:2 * N_DIR * H].reshape(B, T, N_DIR, H)
    zpad = jnp.zeros((B, T, N_DIR, SL - H), jnp.float32)
    rows = jnp.concatenate([ig, zpad, fg, zpad], axis=-1)
    return rows.reshape(B, T // L, L, N_DIR, 2 * SL).transpose(0, 1, 3, 4, 2)


def mlstm_chunked(q, k, v, g_rows, state, h_prev, direction):
    B, T, _ = q.shape
    L = MLSTM_CHUNK
    SL = MLSTM_GATE_SLOTS
    bb = MLSTM_SEQS_PER_STEP
    nc = T // L
    reverse = direction == 1
    cm = (lambda c: nc - 1 - c) if reverse else (lambda c: c)
    tok = lambda w: pl.BlockSpec((bb, L, w), lambda b, c: (b, cm(c), 0))
    st = lambda a: pl.BlockSpec((bb,) + a.shape[1:], lambda b, c: (b,) + (0,) * (a.ndim - 1))
    c0, n0, m0 = state
    has_prev = h_prev is not None
    args = [q, k, v, g_rows, c0, n0, m0] + ([h_prev] if has_prev else [])
    outs = pl.pallas_call(
        functools.partial(_mlstm_kernel, reverse=reverse, bb=bb, has_prev=has_prev),
        out_shape=(jax.ShapeDtypeStruct((B, T, BRANCH_WIDTH), jnp.float32),) + tuple(
            jax.ShapeDtypeStruct(a.shape, jnp.float32) for a in state),
        grid=(B // bb, nc),
        in_specs=[tok(MLSTM_QK), tok(MLSTM_QK), tok(BRANCH_WIDTH),
                  pl.BlockSpec((bb, 1, 1, 2 * SL, L), lambda b, c: (b, cm(c), direction, 0, 0)), st(c0), st(n0), st(m0)]
                 + ([tok(BRANCH_WIDTH)] if has_prev else []),
        out_specs=(tok(BRANCH_WIDTH), st(c0), st(n0), st(m0)),
        scratch_shapes=[pltpu.VMEM((bb,) + a.shape[1:], jnp.float32) for a in state],
        compiler_params=_cparams("parallel", "arbitrary"),
        name="mlstm_chunked",
    )(*args)
    return outs[0], outs[1:]


def _mlstm_readout_tile(h, o, norm_g):
    parts = []
    for hd in range(MLSTM_HEADS):
        x = h[:, hd * MLSTM_DV:(hd + 1) * MLSTM_DV]
        parts.append(x * lax.rsqrt(jnp.mean(x * x, axis=1, keepdims=True) + RMS_EPS))
    return jnp.concatenate(parts, axis=1) * norm_g * jax.nn.sigmoid(o)


def mlstm_mix(proj_x, proj_c, i_bias, f_bias, rope, tm):
    B = proj_x.shape[0]
    qx, kx, vx, gx = mlstm_prep(proj_x, i_bias, f_bias, rope, tm)
    qc, kc, vc, gc = mlstm_prep(proj_c, i_bias, f_bias, None, tm)
    gx, gc = mlstm_gate_rows(gx), mlstm_gate_rows(gc)
    h_x = h_c = None
    for d in range(N_DIR):
        st0 = (jnp.zeros((B, MLSTM_HEADS // 2, HP, HP), jnp.float32), jnp.zeros((B, MLSTM_HEADS // 2, 1, HP), jnp.float32),
               jnp.zeros((B, MLSTM_GATE_SLOTS, MLSTM_CHUNK), jnp.float32))
        h_c, st_ctx = mlstm_chunked(qc, kc, vc, gc, st0, h_c, d)
        h_x, _ = mlstm_chunked(qx, kx, vx, gx, st_ctx, h_x, d)
    return h_x, h_c


def _merge_kernel(y_ref, r_ref, k_ref, v_ref, g_ref, ag0_ref, ag1_ref, yb_ref, h_ref, og_ref, ga_ref, gb_ref, gc_ref,
                  x_ref, mod_ref, ka_ref, rk_ref, lg_ref, lb_ref, ng_ref, wb_ref, wo_ref, o_ref):
    f32 = lambda ref: ref[0].astype(jnp.float32)
    ya = _rwkv_readout_tile(y_ref[0], f32(r_ref), f32(k_ref), f32(v_ref), f32(g_ref), f32(ag0_ref), f32(ag1_ref),
                            ka_ref[...], rk_ref[...], lg_ref[...], lb_ref[...])
    yc = _mlstm_readout_tile(h_ref[0], f32(og_ref), ng_ref[...])
    merged = None
    for i, (y, gate_ref) in enumerate(((ya, ga_ref), (yb_ref[0], gb_ref), (yc, gc_ref))):
        t = jax.nn.sigmoid(f32(gate_ref)) * _bdot(y, wb_ref[i])
        merged = t if merged is None else merged + t
    o_ref[0] = x_ref[0] + mod_ref[0, 2:3, :] * _bdot(merged, wo_ref[...])


def merge_apply(rw, yb, h_ml, proj, x, mod, rw_params, ml_norm_g, w_branch, w_out, tm):
    B, T, D = x.shape
    tm = min(tm, T)
    C = BRANCH_WIDTH
    tok = lambda w: pl.BlockSpec((1, tm, w), lambda b, i: (b, i, 0))
    gate = lambda k: pl.BlockSpec((1, tm, D), lambda b, i: (b, i, GATE_BLK0 + k))
    par = pl.BlockSpec((1, C), lambda b, i: (0, 0))
    params = [p.reshape(1, C) for p in rw_params]
    return pl.pallas_call(
        _merge_kernel, out_shape=jax.ShapeDtypeStruct((B, T, D), jnp.float32), grid=(B, T // tm),
        in_specs=[tok(C)] * 9 + [pl.BlockSpec((1, tm, C), lambda b, i: (b, i, ML_OGATE_BLK)), gate(0), gate(1), gate(2), tok(D),
                  pl.BlockSpec((1, 6, D), lambda b, i: (b, 0, 0))] + [par] * 4
                 + [par, pl.BlockSpec(w_branch.shape, lambda b, i: (0, 0, 0)), pl.BlockSpec(w_out.shape, lambda b, i: (0, 0))],
        out_specs=tok(D),
        compiler_params=_cparams("parallel", "parallel"),
        name="merge_branches",
    )(*rw, yb, h_ml, proj, proj, proj, proj, x, mod, *params, ml_norm_g.reshape(1, C), w_branch, w_out)


def _route_kernel(x_ref, g_ref, mod_ref, wr_ref, rb_ref, h_ref, gate_ref):
    h = _norm_mod(x_ref[0], g_ref[...], mod_ref[0, 4:5, :], mod_ref[0, 3:4, :])
    h_ref[0] = h.astype(BF)
    tm = h.shape[0]
    logits = lax.dot_general(wr_ref[...], h, (((1,), (1,)), ((), ())), precision=HIGHEST, preferred_element_type=jnp.float32)
    scores = jax.nn.sigmoid(logits)
    sel = scores + rb_ref[...]
    gsz = N_EXPERTS // N_GROUPS
    grp = sel.reshape(N_GROUPS, gsz, tm)
    iota_in = lax.broadcasted_iota(jnp.int32, grp.shape, 1)
    m1 = jnp.max(grp, axis=1, keepdims=True)
    first = jnp.min(jnp.where(grp == m1, iota_in, gsz), axis=1, keepdims=True)
    m2 = jnp.max(jnp.where(iota_in == first, -jnp.inf, grp), axis=1, keepdims=True)
    gscore = (m1 + m2).reshape(N_GROUPS, tm)
    gi = lax.broadcasted_iota(jnp.int32, (N_GROUPS, tm), 0)
    rank = jnp.zeros((N_GROUPS, tm), jnp.int32)
    for g2 in range(N_GROUPS):
        other = gscore[g2:g2 + 1, :]
        rank = rank + ((other > gscore) | ((other == gscore) & (g2 < gi))).astype(jnp.int32)
    gmask = rank < TOPK_GROUPS
    emask = jnp.broadcast_to(gmask.reshape(N_GROUPS, 1, tm), (N_GROUPS, gsz, tm)).reshape(N_EXPERTS, tm)
    cand = jnp.where(emask, sel, NEG_INF)
    ei = lax.broadcasted_iota(jnp.int32, (N_EXPERTS, tm), 0)
    chosen = jnp.zeros((N_EXPERTS, tm), jnp.bool_)
    for _ in range(TOP_K):
        mx = jnp.max(cand, axis=0, keepdims=True)
        idx = jnp.min(jnp.where(cand == mx, ei, N_EXPERTS), axis=0, keepdims=True)
        hit = ei == idx
        chosen = chosen | hit
        cand = jnp.where(hit, -jnp.inf, cand)
    w = jnp.where(chosen, scores, 0.0)
    w = w / jnp.sum(w, axis=0, keepdims=True) * ROUTED_SCALE
    gate_ref[0] = jnp.concatenate([w, jnp.zeros((V7X_LANES - N_EXPERTS, tm), jnp.float32)], axis=0).T


def moe_route(x, g, mod, w_router, router_bias, tm):
    B, T, D = x.shape
    tm = min(tm, T)
    tok = pl.BlockSpec((1, tm, D), lambda b, i: (b, i, 0))
    return pl.pallas_call(
        _route_kernel,
        out_shape=(jax.ShapeDtypeStruct((B, T, D), BF), jax.ShapeDtypeStruct((B, T, V7X_LANES), jnp.float32)),
        grid=(B, T // tm),
        in_specs=[tok, pl.BlockSpec((1, D), lambda b, i: (0, 0)), pl.BlockSpec((1, 6, D), lambda b, i: (b, 0, 0)),
                  pl.BlockSpec((N_EXPERTS, D), lambda b, i: (0, 0)), pl.BlockSpec((N_EXPERTS, 1), lambda b, i: (0, 0))],
        out_specs=(tok, pl.BlockSpec((1, tm, V7X_LANES), lambda b, i: (b, i, 0))),
        compiler_params=_cparams("parallel", "parallel"),
        name="moe_route",
    )(x, g.reshape(1, D), mod, w_router.T, router_bias.reshape(N_EXPERTS, 1))


def _moe_kernel(h_ref, gate_ref, x_ref, mod_ref, sel_ref, wg_ref, wu_ref, wd_ref, sg_ref, su_ref, sd_ref, o_ref, *, tm):
    j = pl.program_id(1)
    rows = pl.ds(pl.multiple_of(pl.program_id(2) * tm, tm), tm)
    h = h_ref[0]

    @pl.when(j == 0)
    def _():
        sh = jax.nn.silu(_bdot(h, sg_ref[...])) * _bdot(h, su_ref[...])
        o_ref[0, rows, :] = _bdot(sh, sd_ref[...])

    g8 = _split_dot(gate_ref[0], sel_ref[0], 2)
    act = jax.nn.silu(_bdot(h, wg_ref[...])) * _bdot(h, wu_ref[...])
    act = jnp.concatenate([act[:, e * D_EXPERT:(e + 1) * D_EXPERT] * g8[:, e:e + 1] for e in range(MOE_STEP_EXPERTS)],
                          axis=1)
    o_ref[0, rows, :] += _bdot(act, wd_ref[...])

    @pl.when(j == pl.num_programs(1) - 1)
    def _():
        o_ref[0, rows, :] = x_ref[0] + mod_ref[0, 5:6, :] * o_ref[0, rows, :]


def _moe_select_table():
    se = MOE_STEP_EXPERTS
    t = np.zeros((N_EXPERTS // se, V7X_LANES, V7X_LANES), np.float32)
    for j in range(N_EXPERTS // se):
        for e in range(se):
            t[j, j * se + e, e] = 1.0
    return jnp.asarray(t, BF)


def moe_apply(h2, gates, x, mod, wg, wu, wd, sg, su, sd, tm):
    B, T, D = x.shape
    sw = MOE_STEP_EXPERTS * D_EXPERT
    n_groups = N_EXPERTS // MOE_STEP_EXPERTS
    tok = pl.BlockSpec((1, tm, D), lambda b, j, i: (b, i, 0))
    x_last = pl.BlockSpec((1, tm, D), lambda b, j, i: (b, jnp.where(j == n_groups - 1, i, 0), 0))
    full = lambda a: pl.BlockSpec(a.shape, lambda b, j, i: (0,) * a.ndim)
    return pl.pallas_call(
        functools.partial(_moe_kernel, tm=tm), out_shape=jax.ShapeDtypeStruct((B, T, D), jnp.float32),
        grid=(B, n_groups, T // tm),
        in_specs=[tok, pl.BlockSpec((1, tm, V7X_LANES), lambda b, j, i: (b, i, 0)), x_last,
                  pl.BlockSpec((1, 6, D), lambda b, j, i: (b, 0, 0)),
                  pl.BlockSpec((1, V7X_LANES, V7X_LANES), lambda b, j, i: (j, 0, 0)),
                  pl.BlockSpec((D, sw), lambda b, j, i: (0, j)), pl.BlockSpec((D, sw), lambda b, j, i: (0, j)),
                  pl.BlockSpec((sw, D), lambda b, j, i: (j, 0)), full(sg), full(su), full(sd)],
        out_specs=pl.BlockSpec((1, T, D), lambda b, j, i: (b, 0, 0)),
        compiler_params=pltpu.CompilerParams(dimension_semantics=("parallel", "arbitrary", "arbitrary"),
                                             vmem_limit_bytes=MOE_VMEM_LIMIT_BYTES),
        name="moe_experts",
    )(h2, gates, x, mod, _moe_select_table(), wg, wu, wd, sg, su, sd)


def _reorder_w_in(w):
    o_na = RWKV_COLS
    o_ml = o_na + NA_COLS
    o_mg = o_ml + MLSTM_MAIN_COLS
    o_gate = o_mg + MLSTM_GATES
    pad = jnp.zeros((w.shape[0], V7X_LANES - MLSTM_GATES), w.dtype)
    return jnp.concatenate([w[:, o_na:o_ml], w[:, o_ml:o_mg], w[:, o_gate:], w[:, :RWKV_COLS], w[:, o_mg:o_gate], pad],
                           axis=1).astype(BF)


def kernel(x, c, ctx, c_ctx, w_ada, b_ada, norm1_g, norm2_g, w_in, rw_mu, rw_w0, rw_w2, rw_a0, rw_a2, rw_k_k, rw_k_a, rw_r_k, rw_g2, rw_lnx_g, rw_lnx_b, rw_v0, rw_v1, rw_v2, na_qn_g, na_kn_g, na_rpb, ml_i_bias, ml_f_bias, ml_norm_g, w_branch, w_out, moe_router, moe_bias, moe_w_gate, moe_w_up, moe_w_down, sh_w_gate, sh_w_up, sh_w_down):
    B, S, D = x.shape
    n_ctx = ctx.shape[1]
    tm = TOKEN_TILE
    assert S % tm == 0 and n_ctx % min(tm, n_ctx) == 0 and PROJ_COLS == IN_COLS + V7X_LANES - MLSTM_GATES
    rope = rope_tables(S)
    n_cond = B + 1
    cond_pad = (-n_cond) % V7X_SUBLANES
    s_cond = jnp.pad(jnp.concatenate([jax.nn.silu(c), jax.nn.silu(c_ctx)[None]], axis=0), ((0, cond_pad), (0, 0)))
    vf_x = vf_c = None
    for l in range(DEPTH):
        need_ctx = l < DEPTH - 1
        mod = pmm(s_cond, w_ada[l]) + b_ada[l]
        mod_x = mod[:B].reshape(B, 6, D)
        mod_c = jnp.broadcast_to(mod[B].reshape(1, 6, D), (B, 6, D))
        w_proj = _reorder_w_in(w_in[l])
        proj_x = norm_proj(x, norm1_g[l], mod_x, w_proj)
        proj_c = norm_proj(ctx.reshape(1, B * n_ctx, D), norm1_g[l], mod_c[:1], w_proj).reshape(B, n_ctx, PROJ_COLS)
        vres = None if l == 0 else (rw_v0[l - 1], rw_v1[l - 1], rw_v2[l - 1])
        rw_x, rw_c, vf_x, vf_c = rwkv_mix(proj_x, proj_c, vf_x, vf_c, rw_mu[l], rw_w0[l], rw_w2[l], rw_a0[l], rw_a2[l],
                                          rw_k_k[l], rw_k_a[l], rw_g2[l], vres, tm)
        rw_params = (rw_k_a[l], rw_r_k[l], rw_lnx_g[l], rw_lnx_b[l])
        yb_x, yb_c = na_mix(proj_x, proj_c, na_qn_g[l], na_kn_g[l], na_rpb[l], need_ctx, tm)
        hm_x, hm_c = mlstm_mix(proj_x, proj_c, ml_i_bias[l], ml_f_bias[l], rope, tm)
        wb = w_branch[l].astype(BF)
        wo = w_out[l].astype(BF)
        wg = moe_w_gate[l].transpose(1, 0, 2).reshape(D, N_EXPERTS * D_EXPERT).astype(BF)
        wu = moe_w_up[l].transpose(1, 0, 2).reshape(D, N_EXPERTS * D_EXPERT).astype(BF)
        wd = moe_w_down[l].reshape(N_EXPERTS * D_EXPERT, D).astype(BF)
        shared = (sh_w_gate[l].astype(BF), sh_w_up[l].astype(BF), sh_w_down[l].astype(BF))
        x = merge_apply(rw_x, yb_x, hm_x, proj_x, x, mod_x, rw_params, ml_norm_g[l], wb, wo, tm)
        h2, gates = moe_route(x, norm2_g[l], mod_x, moe_router[l], moe_bias[l], tm)
        x = moe_apply(h2, gates, x, mod_x, wg, wu, wd, *shared, MOE_TOKEN_TILE)
        if need_ctx:
            ctx = merge_apply(rw_c, yb_c, hm_c, proj_c, ctx, mod_c, rw_params, ml_norm_g[l], wb, wo, tm)
            h2, gates = moe_route(ctx, norm2_g[l], mod_c, moe_router[l], moe_bias[l], tm)
            ctx = moe_apply(h2, gates, ctx, mod_c, wg, wu, wd, *shared, min(tm, n_ctx))
    return x
```

```python
import functools

import numpy as np
import jax
import jax.numpy as jnp
from jax import lax
from jax.experimental import pallas as pl
from jax.experimental.pallas import tpu as pltpu

D_MODEL = 1024
DEPTH = 2
GRID_W = 64
N_DIR = 2
N_BRANCH = 3
BRANCH_WIDTH = 512
RMS_EPS = 1e-6
NEG_INF = -1e30

RWKV_HEAD = 64
RWKV_DECAY_LORA = 64
RWKV_LORA_COLS = 384
RWKV_COLS = 3 * BRANCH_WIDTH + RWKV_LORA_COLS
RWKV_LNX_EPS = 64e-5
RWKV_DECAY_SCALE = float(np.exp(-0.5))
RWKV_CHUNK = 64
WKV_SEQS_PER_STEP = 8

NA_HEAD = 64
NA_HEADS = BRANCH_WIDTH // NA_HEAD
NA_WIN_R = 8
NA_WIN_C = 16
NA_COLS = 3 * BRANCH_WIDTH
NA_ROWS_PER_STEP = 8

MLSTM_HEADS = 4
MLSTM_DQK = 64
MLSTM_DV = BRANCH_WIDTH // MLSTM_HEADS
MLSTM_QK = MLSTM_HEADS * MLSTM_DQK
MLSTM_CHUNK = 64
MLSTM_SEQS_PER_STEP = 16
MLSTM_GATE_SLOTS = 8
MLSTM_MAIN_COLS = 2 * MLSTM_QK + 2 * BRANCH_WIDTH
MLSTM_GATES = 2 * N_DIR * MLSTM_HEADS
GATE_CAP = 15.0
ROPE_ROT = MLSTM_DQK // 4
ROPE_BASE = 10000.0

N_EXPERTS = 64
TOP_K = 8
N_GROUPS = 8
TOPK_GROUPS = 4
D_EXPERT = 128
ROUTED_SCALE = 2.5
MOE_STEP_EXPERTS = 8
MOE_TOKEN_TILE = 1024

GATE_COLS = N_BRANCH * D_MODEL
IN_COLS = RWKV_COLS + NA_COLS + MLSTM_MAIN_COLS + MLSTM_GATES + GATE_COLS

TOKEN_TILE = 512
V7X_LANES = 128
V7X_SUBLANES = 8
HP = 2 * RWKV_HEAD
VMEM_LIMIT_BYTES = 48 * 1024 * 1024
MOE_VMEM_LIMIT_BYTES = 56 * 1024 * 1024

PROJ_COLS = 8192
NA_BLK = 0
ML_BLK = 1
ML_OGATE_BLK = (NA_COLS + 2 * MLSTM_QK + BRANCH_WIDTH) // BRANCH_WIDTH
GATE_BLK0 = (NA_COLS + MLSTM_MAIN_COLS) // D_MODEL
RWKV_COL0 = NA_COLS + MLSTM_MAIN_COLS + GATE_COLS
RWKV_BLK0 = RWKV_COL0 // BRANCH_WIDTH
RWKV_LORA_BLK = (RWKV_COL0 + 3 * BRANCH_WIDTH) // RWKV_LORA_COLS
ML_GATE_BLK = (RWKV_COL0 + RWKV_COLS) // V7X_LANES

BF = jnp.bfloat16
ACT_DTYPE = BF
HALO_ROWS = 16
HIGHEST = lax.Precision.HIGHEST


def _cparams(*sem):
    return pltpu.CompilerParams(dimension_semantics=sem, vmem_limit_bytes=VMEM_LIMIT_BYTES)


def _bdot(a, b):
    return jnp.dot(a.astype(BF), b.astype(BF), preferred_element_type=jnp.float32)


def _dot_nt(a, b):
    return lax.dot_general(a.astype(BF), b.astype(BF), (((1,), (1,)), ((), ())), preferred_element_type=jnp.float32)


def _dot_tn(a, b):
    return lax.dot_general(a.astype(BF), b.astype(BF), (((0,), (0,)), ((), ())), preferred_element_type=jnp.float32)


def _split_dot(x, w, parts):
    out = None
    rem = x
    for _ in range(parts):
        piece = rem.astype(BF)
        rem = rem - piece.astype(jnp.float32)
        t = jnp.dot(piece, w, preferred_element_type=jnp.float32)
        out = t if out is None else out + t
    return out


def _pair_sums(x):
    lo = _lo_lanes(x.shape[0])
    parts = []
    for c in range(x.shape[1] // HP):
        blk = x[:, c * HP:(c + 1) * HP]
        s_lo = jnp.sum(jnp.where(lo, blk, 0.0), axis=1, keepdims=True)
        s_hi = jnp.sum(jnp.where(lo, 0.0, blk), axis=1, keepdims=True)
        parts.append(jnp.where(lo, s_lo, s_hi))
    return jnp.concatenate(parts, axis=1)


def _head_ones(width, head):
    i = np.arange(width) // head
    return jnp.asarray(i[:, None] == i[None, :], BF)


def _stack_pair(x, lo):
    zero = jnp.zeros_like(x)
    return jnp.concatenate([jnp.where(lo, x, zero), jnp.where(lo, zero, x)], axis=0)


def _lo_lanes(n):
    return lax.broadcasted_iota(jnp.int32, (n, HP), 1) < (HP // 2)


def _mm_kernel(x_ref, w_ref, o_ref):
    part = _bdot(x_ref[...], w_ref[...])

    @pl.when(pl.program_id(2) == 0)
    def _():
        o_ref[...] = part

    @pl.when(pl.program_id(2) > 0)
    def _():
        o_ref[...] += part


def _pick_tile(n, cands):
    for c in cands:
        if n % c == 0:
            return c
    return n


def pmm(x, w):
    M, K = x.shape
    N = w.shape[1]
    tm = _pick_tile(M, (1024, 512, 256, 128, 64, 32, 16, 8))
    tn = _pick_tile(N, (1024, 512, 384, 256, 128))
    tk = _pick_tile(K, (1024,)) if K > 1024 else K
    return pl.pallas_call(
        _mm_kernel,
        out_shape=jax.ShapeDtypeStruct((M, N), jnp.float32),
        grid=(M // tm, N // tn, K // tk),
        in_specs=[pl.BlockSpec((tm, tk), lambda i, j, k: (i, k)),
                  pl.BlockSpec((tk, tn), lambda i, j, k: (k, j))],
        out_specs=pl.BlockSpec((tm, tn), lambda i, j, k: (i, j)),
        compiler_params=_cparams("parallel", "parallel", "arbitrary"),
        name="tiled_matmul",
    )(x, w)


def _norm_mod(x, g, scale, shift):
    xn = x * lax.rsqrt(jnp.mean(x * x, axis=-1, keepdims=True) + RMS_EPS)
    return xn * g * (1.0 + scale) + shift


def _norm_proj_kernel(x_ref, g_ref, mod_ref, w_ref, o_ref, h_scr):
    @pl.when(pl.program_id(2) == 0)
    def _():
        h_scr[...] = _norm_mod(x_ref[0], g_ref[...], mod_ref[0, 1:2, :], mod_ref[0, 0:1, :]).astype(BF)

    o_ref[0] = jnp.dot(h_scr[...], w_ref[...], preferred_element_type=jnp.float32).astype(o_ref.dtype)


def norm_proj(x, g, mod, w):
    B, T, D = x.shape
    N = w.shape[1]
    tm = _pick_tile(T, (1024, 512, 256))
    tn = _pick_tile(N, (1024, 512, 256, 128))
    return pl.pallas_call(
        _norm_proj_kernel,
        out_shape=jax.ShapeDtypeStruct((B, T, N), ACT_DTYPE), grid=(B, T // tm, N // tn),
        in_specs=[pl.BlockSpec((1, tm, D), lambda b, i, j: (b, i, 0)), pl.BlockSpec((1, D), lambda b, i, j: (0, 0)),
                  pl.BlockSpec((1, 6, D), lambda b, i, j: (b, 0, 0)), pl.BlockSpec((D, tn), lambda b, i, j: (0, j))],
        out_specs=pl.BlockSpec((1, tm, tn), lambda b, i, j: (b, i, j)),
        scratch_shapes=[pltpu.VMEM((tm, D), BF)],
        compiler_params=_cparams("parallel", "parallel", "arbitrary"),
        name="norm_proj",
    )(x, g.reshape(1, D), mod, w)


def _shifted(z, prev_row, next_row):
    n = z.shape[0]
    s = V7X_SUBLANES
    row = lax.broadcasted_iota(jnp.int32, (s, z.shape[1]), 0)
    zp = pltpu.roll(z, 1, axis=0)
    zn = pltpu.roll(z, n - 1, axis=0)
    zp = jnp.concatenate([jnp.where(row == 0, prev_row, zp[:s]), zp[s:]], axis=0)
    zn = jnp.concatenate([zn[:n - s], jnp.where(row == s - 1, next_row, zn[n - s:])], axis=0)
    return zp, zn


def _rwkv_feat_kernel(zr_ref, zk_ref, zv_ref, zl_ref, pr_ref, pk_ref, pv_ref, pl_ref, nr_ref, nk_ref, nv_ref, nl_ref,
                      mu_ref, w0_ref, w2_ref, a0_ref, a2_ref, kk_ref, g2_ref, ones_ref, *rest, has_vres):
    if has_vres:
        v0_ref, v1_ref, v2_ref, vf_ref = rest[:4]
        outs = rest[4:]
    else:
        outs = rest
    r_ref, k_ref, v_ref, kn_ref, g_ref, lw0_ref, lw1_ref, ag0_ref, ag1_ref = outs
    C = BRANCH_WIDTH
    first = pl.program_id(1) == 0
    last = pl.program_id(1) == pl.num_programs(1) - 1

    def shift(z_ref, p_ref, n_ref, c0, c1):
        z = z_ref[0].astype(jnp.float32)
        prev_row = jnp.where(first, 0.0, p_ref[0].astype(jnp.float32)[HALO_ROWS - 1:HALO_ROWS, :])
        next_row = jnp.where(last, 0.0, n_ref[0].astype(jnp.float32)[0:1, :])
        zp, zn = _shifted(z, prev_row, next_row)
        return z + mu_ref[:, c0:c1] * (0.5 * (zp + zn) - z)

    r = shift(zr_ref, pr_ref, nr_ref, 0, C)
    k = shift(zk_ref, pk_ref, nk_ref, C, 2 * C)
    v = shift(zv_ref, pv_ref, nv_ref, 2 * C, 3 * C)
    zl = shift(zl_ref, pl_ref, nl_ref, 3 * C, RWKV_COLS)
    wd = jnp.tanh(zl[:, 0:2 * RWKV_DECAY_LORA])
    ad = zl[:, 2 * RWKV_DECAY_LORA:4 * RWKV_DECAY_LORA]
    gd = zl[:, 4 * RWKV_DECAY_LORA:]
    for d, (lw_ref, ag_ref) in enumerate(((lw0_ref, ag0_ref), (lw1_ref, ag1_ref))):
        u = w0_ref[d:d + 1, :] + _bdot(wd, w2_ref[d])
        lw_ref[0] = -RWKV_DECAY_SCALE * jax.nn.sigmoid(u)
        ag_ref[0] = jax.nn.sigmoid(a0_ref[d:d + 1, :] + _bdot(ad, a2_ref[d])).astype(ag_ref.dtype)
    kq = k * kk_ref[...]
    ss = _split_dot(kq * kq, ones_ref[...], 2)
    kn_ref[0] = (kq * lax.rsqrt(jnp.maximum(ss, 1e-24))).astype(kn_ref.dtype)
    if has_vres:
        lora = _bdot(_bdot(v, v1_ref[...]), v2_ref[...])
        v = v + (vf_ref[0].astype(jnp.float32) - v) * jax.nn.sigmoid(v0_ref[...] + lora)
    g_ref[0] = _bdot(jax.nn.sigmoid(gd), g2_ref[...]).astype(g_ref.dtype)
    r_ref[0] = r.astype(r_ref.dtype)
    k_ref[0] = k.astype(k_ref.dtype)
    v_ref[0] = v.astype(v_ref.dtype)


def rwkv_features(proj, mu, w0, w2, a0, a2, k_k, g2, vres, v_first, tm):
    B, T, _ = proj.shape
    tm = min(tm, T)
    C = BRANCH_WIDTH
    nt = T // tm
    zpad = jnp.zeros((RWKV_DECAY_LORA, C), jnp.float32)
    pad_dirs = lambda w: jnp.stack([jnp.concatenate([w[0], zpad], 0), jnp.concatenate([zpad, w[1]], 0)]).astype(BF)
    sub = HALO_ROWS
    blk = lambda w, j: pl.BlockSpec((1, tm, w), lambda b, i: (b, i, j))
    before = lambda w, j: pl.BlockSpec((1, sub, w), lambda b, i: (b, jnp.maximum(i * (tm // sub) - 1, 0), j))
    after = lambda w, j: pl.BlockSpec((1, sub, w), lambda b, i: (b, jnp.minimum((i + 1) * (tm // sub), T // sub - 1), j))
    tok = pl.BlockSpec((1, tm, C), lambda b, i: (b, i, 0))
    full = lambda a: pl.BlockSpec(a.shape, lambda b, i: (0,) * a.ndim)
    params = [mu.reshape(1, -1), w0, pad_dirs(w2), a0, pad_dirs(a2), k_k.reshape(1, -1), g2.astype(BF),
              _head_ones(C, RWKV_HEAD)]
    pieces = [(C, RWKV_BLK0), (C, RWKV_BLK0 + 1), (C, RWKV_BLK0 + 2), (RWKV_LORA_COLS, RWKV_LORA_BLK)]
    args = [proj] * (3 * len(pieces)) + params
    specs = ([blk(w, j) for w, j in pieces] + [before(w, j) for w, j in pieces] + [after(w, j) for w, j in pieces]
             + [full(a) for a in params])
    if vres is not None:
        v0, v1, v2 = vres
        extra = [v0.reshape(1, -1), jnp.pad(v1, ((0, 0), (0, V7X_LANES - v1.shape[1]))).astype(BF),
                 jnp.pad(v2, ((0, V7X_LANES - v2.shape[0]), (0, 0))).astype(BF)]
        args += extra + [v_first]
        specs += [full(a) for a in extra] + [tok]
    return pl.pallas_call(
        functools.partial(_rwkv_feat_kernel, has_vres=vres is not None),
        out_shape=(jax.ShapeDtypeStruct((B, T, C), ACT_DTYPE),) * 5 + (jax.ShapeDtypeStruct((B, T, C), jnp.float32),) * 2
                  + (jax.ShapeDtypeStruct((B, T, C), ACT_DTYPE),) * 2,
        grid=(B, nt), in_specs=specs, out_specs=(tok,) * 9,
        compiler_params=_cparams("parallel", "parallel"),
        name="rwkv_features",
    )(*args)


def _wkv_kernel(r_ref, lw_ref, kk_ref, a_ref, k_ref, v_ref, ka_ref, s0_ref, *rest, reverse, bb, n_pairs, has_prev):
    if has_prev:
        yprev_ref, y_ref, sout_ref, s_scr = rest
    else:
        y_ref, sout_ref, s_scr = rest
    C = RWKV_CHUNK
    c_idx = pl.program_id(1)

    @pl.when(c_idx == 0)
    def _():
        s_scr[...] = s0_ref[...]

    ti = lax.broadcasted_iota(jnp.int32, (C, C), 0)
    si = lax.broadcasted_iota(jnp.int32, (C, C), 1)
    tri = ((si >= ti) if reverse else (si <= ti)).astype(jnp.float32)
    tp = lax.broadcasted_iota(jnp.int32, (C, 2 * C), 0)
    sp = lax.broadcasted_iota(jnp.int32, (C, 2 * C), 1) % C
    m_strict = (sp > tp) if reverse else (sp < tp)
    m_incl = (sp >= tp) if reverse else (sp <= tp)
    eye = (tp == sp).astype(jnp.float32)
    t2 = lax.broadcasted_iota(jnp.int32, (2 * C, 2 * C), 0)
    s2 = lax.broadcasted_iota(jnp.int32, (2 * C, 2 * C), 1)
    same_head = (t2 // C) == (s2 // C)
    lo = _lo_lanes(C)
    units = [(bi, slice(p * HP, (p + 1) * HP), p) for bi in range(bb) for p in range(n_pairs)]
    n = len(units)
    cat = lambda xs: jnp.concatenate(xs, axis=0)
    stack = lambda x: _stack_pair(x, lo)
    bdiag = lambda x: jnp.where(same_head, cat([x, x]), jnp.zeros((), x.dtype))

    ar, bk, bkh, v, e_tot = [], [], [], [], []
    for bi, sl, _ in units:
        lw = lw_ref[bi, :, sl]
        kk = kk_ref[bi, :, sl].astype(jnp.float32)
        ag = a_ref[bi, :, sl].astype(jnp.float32)
        kd = k_ref[bi, :, sl].astype(jnp.float32) * (1.0 + (ag - 1.0) * ka_ref[:, sl])
        cum = jnp.dot(tri, lw, precision=HIGHEST, preferred_element_type=jnp.float32)
        tot = jnp.sum(lw, axis=0, keepdims=True)
        e_neg = jnp.exp(-cum)
        e_end = jnp.exp(tot - cum)
        b = kk * ag
        ar.append(cat([-kk * jnp.exp(cum - lw), r_ref[bi, :, sl].astype(jnp.float32) * jnp.exp(cum)]).astype(BF))
        bk.append(cat([stack(b * e_neg), stack(kd * e_neg)]).astype(BF))
        bkh.append(cat([b * e_end, kd * e_end]).astype(BF))
        v.append(v_ref[bi, :, sl].astype(BF))
        e_tot.append(jnp.exp(tot))
    gram = [_dot_nt(ar[i], bk[i]) for i in range(n)]
    l_ab = [jnp.where(m_strict, g[:C, :2 * C], 0.0) for g in gram]
    l_ak = [jnp.where(m_strict, g[:C, 2 * C:], 0.0).astype(BF) for g in gram]
    l_rbk = [jnp.concatenate([jnp.where(m_incl, g[C:, :2 * C], 0.0), jnp.where(m_incl, g[C:, 2 * C:], 0.0)],
                             axis=1).astype(BF) for g in gram]
    vs = [stack(x) for x in v]
    s0 = [s_scr[bi, p] for bi, _, p in units]
    proj = [_dot_nt(ar[i], s0[i]) for i in range(n)]
    lv = [_bdot(l_ak[i], vs[i]) for i in range(n)]
    tinv = [eye + m for m in l_ab]
    pw_bd = [bdiag(m.astype(BF)) for m in l_ab]
    pw = [_bdot(l_ab[i], pw_bd[i]).astype(BF) for i in range(n)]
    levels = 5
    for lvl in range(1, levels + 1):
        pw_bd = [bdiag(m) for m in pw]
        if lvl < levels:
            both = [_bdot(cat([pw[i], tinv[i].astype(BF)]), pw_bd[i]) for i in range(n)]
            pw = [x[:C].astype(BF) for x in both]
            tinv = [tinv[i] + both[i][C:] for i in range(n)]
        else:
            tinv = [tinv[i] + _bdot(tinv[i], pw_bd[i]) for i in range(n)]
    u = [_bdot(tinv[i], stack((proj[i][:C] + lv[i]).astype(BF))) for i in range(n)]
    ub = [x.astype(BF) for x in u]
    ys = [proj[i][C:] + _bdot(l_rbk[i], cat([stack(ub[i]), vs[i]])) for i in range(n)]
    upd = [_dot_tn(cat([ub[i], v[i]]), bkh[i]) for i in range(n)]
    for i, (bi, sl, p) in enumerate(units):
        y = ys[i]
        if has_prev:
            y = y + yprev_ref[bi, :, sl]
        y_ref[bi, :, sl] = y
        s_scr[bi, p] = s0[i] * e_tot[i] + jnp.where(same_head, upd[i], 0.0)

    @pl.when(c_idx == pl.num_programs(1) - 1)
    def _():
        sout_ref[...] = s_scr[...]


def wkv_chunked(r, lw, kk, ag, k, v, k_a, s0, y_prev, reverse):
    B, T, W = r.shape
    C = RWKV_CHUNK
    bb = WKV_SEQS_PER_STEP
    nc = T // C
    n_pairs = W // HP
    cmap = (lambda b, c: (b, nc - 1 - c, 0)) if reverse else (lambda b, c: (b, c, 0))
    tok = pl.BlockSpec((bb, C, W), cmap)
    st = pl.BlockSpec((bb, n_pairs, HP, HP), lambda b, c: (b, 0, 0, 0))
    has_prev = y_prev is not None
    args = [r, lw, kk, ag, k, v, k_a, s0] + ([y_prev] if has_prev else [])
    return pl.pallas_call(
        functools.partial(_wkv_kernel, reverse=reverse, bb=bb, n_pairs=n_pairs, has_prev=has_prev),
        out_shape=(jax.ShapeDtypeStruct((B, T, W), jnp.float32), jax.ShapeDtypeStruct(s0.shape, jnp.float32)),
        grid=(B // bb, nc),
        in_specs=[tok] * 6 + [pl.BlockSpec((1, W), lambda b, c: (0, 0)), st] + ([tok] if has_prev else []),
        out_specs=(tok, st),
        scratch_shapes=[pltpu.VMEM((bb, n_pairs, HP, HP), jnp.float32)],
        compiler_params=_cparams("parallel", "arbitrary"),
        name="wkv_chunked",
    )(*args)


def _rwkv_readout_tile(y, r, k, v, g, ag0, ag1, ka, rk, lnx_g, lnx_b):
    mean = _pair_sums(y) * (1.0 / RWKV_HEAD)
    yc = y - mean
    var = _pair_sums(yc * yc) * (1.0 / RWKV_HEAD)
    yn = yc * lax.rsqrt(var + RWKV_LNX_EPS) * lnx_g + lnx_b
    ksum = k * (2.0 + (ag0 + ag1 - 2.0) * ka)
    bonus = _pair_sums(r * ksum * rk) * v
    return (yn + bonus) * g


def rwkv_mix(proj_x, proj_c, vf_x, vf_c, mu, w0, w2, a0, a2, k_k, k_a, g2, vres, tm):
    B = proj_x.shape[0]
    fx = rwkv_features(proj_x, mu, w0, w2, a0, a2, k_k, g2, vres, vf_x, tm)
    fc = rwkv_features(proj_c, mu, w0, w2, a0, a2, k_k, g2, vres, vf_c, tm)
    ka = k_a.reshape(1, -1)
    y_x = y_c = None
    for d in range(N_DIR):
        s0 = jnp.zeros((B, BRANCH_WIDTH // HP, HP, HP), jnp.float32)
        y_c, s_ctx = wkv_chunked(fc[0], fc[5 + d], fc[3], fc[7 + d], fc[1], fc[2], ka, s0, y_c, d == 1)
        y_x, _ = wkv_chunked(fx[0], fx[5 + d], fx[3], fx[7 + d], fx[1], fx[2], ka, s_ctx, y_x, d == 1)
    pick = lambda y, f: (y, f[0], f[1], f[2], f[4], f[7], f[8])
    vf_x = fx[2] if vres is None else vf_x
    vf_c = fc[2] if vres is None else vf_c
    return pick(y_x, fx), pick(y_c, fc), vf_x, vf_c


def _qknorm_kernel(z_ref, qg_ref, kg_ref, q_ref, k_ref, v_ref):
    C = BRANCH_WIDTH
    z = z_ref[0].astype(jnp.float32)
    q = z[:, 0:C]
    k = z[:, C:2 * C]
    qn = q * lax.rsqrt(_pair_sums(q * q) * (1.0 / NA_HEAD) + RMS_EPS) * qg_ref[...]
    kn = k * lax.rsqrt(_pair_sums(k * k) * (1.0 / NA_HEAD) + RMS_EPS) * kg_ref[...]
    q_ref[0] = (qn * NA_HEAD ** -0.5).astype(BF)
    k_ref[0] = kn.astype(BF)
    v_ref[0] = z[:, 2 * C:3 * C].astype(BF)


def na_qknorm(proj, qn_g, kn_g, tm):
    B, T, _ = proj.shape
    tm = min(tm, T)
    C = BRANCH_WIDTH
    tok = pl.BlockSpec((1, tm, C), lambda b, i: (b, i, 0))
    par = pl.BlockSpec((1, C), lambda b, i: (0, 0))
    sd = jax.ShapeDtypeStruct((B, T, C), BF)
    return pl.pallas_call(
        _qknorm_kernel, out_shape=(sd, sd, sd), grid=(B, T // tm),
        in_specs=[pl.BlockSpec((1, tm, NA_COLS), lambda b, i: (b, i, NA_BLK)), par, par],
        out_specs=(tok, tok, tok),
        compiler_params=_cparams("parallel", "parallel"),
        name="na_qknorm",
    )(proj, jnp.tile(qn_g, NA_HEADS).reshape(1, C), jnp.tile(kn_g, NA_HEADS).reshape(1, C))


def na_bias_table(rpb):
    qc = np.arange(GRID_W)[:, None]
    kc = np.arange(GRID_W)[None, :]
    cs = np.clip(qc - NA_WIN_C // 2, 0, GRID_W - NA_WIN_C)
    valid = (kc >= cs) & (kc < cs + NA_WIN_C)
    cidx = np.clip(kc - qc + NA_WIN_C - 1, 0, 2 * NA_WIN_C - 2)
    t = jnp.where(valid[None, None], rpb[:, :, cidx], NEG_INF)
    t2 = jnp.concatenate([t[:, :-1], t[:, 1:]], axis=-1)
    H = rpb.shape[0]
    t2 = t2.reshape(H // 2, 2, 2 * NA_WIN_R - 2, GRID_W, 2 * GRID_W).transpose(0, 2, 1, 3, 4)
    return t2.reshape(H // 2, 2 * NA_WIN_R - 2, 2 * GRID_W, 2 * GRID_W)


def _na_kernel(q_ref, k_ref, v_ref, kc_ref, vc_ref, bias_ref, o_ref, *, rows):
    nwin = NA_WIN_R * GRID_W
    lo = _lo_lanes(GRID_W)
    units = []
    for rr in range(NA_ROWS_PER_STEP):
        r = pl.program_id(1) * NA_ROWS_PER_STEP + rr
        rs = jnp.clip(r - NA_WIN_R // 2, 0, rows - NA_WIN_R)
        k0 = pl.multiple_of(rs * GRID_W, GRID_W)
        for p in range(BRANCH_WIDTH // HP):
            units.append((rr, p, slice(p * HP, (p + 1) * HP), rs - r + NA_WIN_R - 1, k0))
    qs = [_stack_pair(q_ref[0, rr * GRID_W:(rr + 1) * GRID_W, sl], lo) for rr, _, sl, _, _ in units]
    s_loc = [_dot_nt(qs[i], k_ref[0, pl.ds(u[4], nwin), u[2]]) for i, u in enumerate(units)]
    s_ctx = [_dot_nt(qs[i], kc_ref[0, :, u[2]]) for i, u in enumerate(units)]
    p_loc, p_ctx, den = [], [], []
    for i, (_, p, _, base, _) in enumerate(units):
        sl_b = s_loc[i] + jnp.concatenate([bias_ref[p, base + 2 * j] for j in range(NA_WIN_R // 2)], axis=1)
        m = jnp.maximum(jnp.max(sl_b, axis=1, keepdims=True), jnp.max(s_ctx[i], axis=1, keepdims=True))
        el = jnp.exp(sl_b - m)
        ec = jnp.exp(s_ctx[i] - m)
        den.append(jnp.sum(el, axis=1, keepdims=True) + jnp.sum(ec, axis=1, keepdims=True))
        p_loc.append(el.astype(BF))
        p_ctx.append(ec.astype(BF))
    o_loc = [_bdot(p_loc[i], v_ref[0, pl.ds(u[4], nwin), u[2]]) for i, u in enumerate(units)]
    o_ctx = [_bdot(p_ctx[i], vc_ref[0, :, u[2]]) for i, u in enumerate(units)]
    for i, (rr, _, sl, _, _) in enumerate(units):
        o = (o_loc[i] + o_ctx[i]) / den[i]
        o_ref[0, rr * GRID_W:(rr + 1) * GRID_W, sl] = jnp.where(lo, o[:GRID_W], o[GRID_W:])


def na_attention(q, k, v, kc, vc, bias_tab):
    B, S, C = q.shape
    rows = S // GRID_W
    n_ctx = kc.shape[1]
    seq = pl.BlockSpec((1, S, C), lambda b, r: (b, 0, 0))
    cx = pl.BlockSpec((1, n_ctx, C), lambda b, r: (b, 0, 0))
    row = pl.BlockSpec((1, NA_ROWS_PER_STEP * GRID_W, C), lambda b, r: (b, r, 0))
    return pl.pallas_call(
        functools.partial(_na_kernel, rows=rows),
        out_shape=jax.ShapeDtypeStruct((B, S, C), jnp.float32),
        grid=(B, rows // NA_ROWS_PER_STEP),
        in_specs=[row, seq, seq, cx, cx, pl.BlockSpec(bias_tab.shape, lambda b, r: (0, 0, 0, 0))],
        out_specs=row,
        compiler_params=_cparams("parallel", "arbitrary"),
        name="na_attention",
    )(q, k, v, kc, vc, bias_tab)


def _ctx_attn_kernel(q_ref, k_ref, v_ref, o_ref):
    n = q_ref.shape[1]
    lo = _lo_lanes(n)
    sls = [slice(p * HP, (p + 1) * HP) for p in range(BRANCH_WIDTH // HP)]
    sc = [_dot_nt(_stack_pair(q_ref[0, :, sl], lo), k_ref[0, :, sl]) for sl in sls]
    e = [jnp.exp(x - jnp.max(x, axis=1, keepdims=True)) for x in sc]
    o = [_bdot(e[p], v_ref[0, :, sl]) / jnp.sum(e[p], axis=1, keepdims=True) for p, sl in enumerate(sls)]
    for p, sl in enumerate(sls):
        o_ref[0, :, sl] = jnp.where(lo, o[p][:n], o[p][n:])


def ctx_attention(q, k, v):
    B, n, C = q.shape
    blk = pl.BlockSpec((1, n, C), lambda b: (b, 0, 0))
    return pl.pallas_call(
        _ctx_attn_kernel, out_shape=jax.ShapeDtypeStruct((B, n, C), jnp.float32), grid=(B,),
        in_specs=[blk, blk, blk], out_specs=blk,
        compiler_params=_cparams("parallel"),
        name="ctx_attention",
    )(q, k, v)


def na_mix(proj_x, proj_c, qn_g, kn_g, rpb, need_ctx, tm):
    q, k, v = na_qknorm(proj_x, qn_g, kn_g, tm)
    qc, kc, vc = na_qknorm(proj_c, qn_g, kn_g, tm)
    out_x = na_attention(q, k, v, kc, vc, na_bias_table(rpb))
    out_c = ctx_attention(qc, kc, vc) if need_ctx else None
    return out_x, out_c


def rope_tables(n_tokens):
    t = jnp.arange(n_tokens)
    pos = jnp.stack([t // GRID_W, t % GRID_W], axis=-1).astype(jnp.float32)
    inv = ROPE_BASE ** (-jnp.arange(ROPE_ROT, dtype=jnp.float32) / ROPE_ROT)
    ang = pos[:, :, None] * inv
    cos_h = jnp.concatenate([jnp.cos(ang), jnp.cos(ang)], axis=-1).reshape(n_tokens, MLSTM_DQK)
    sin_h = jnp.concatenate([-jnp.sin(ang), jnp.sin(ang)], axis=-1).reshape(n_tokens, MLSTM_DQK)
    col = np.arange(MLSTM_QK)
    partner = np.where((col % (2 * ROPE_ROT)) < ROPE_ROT, col + ROPE_ROT, col - ROPE_ROT)
    perm = np.zeros((MLSTM_QK, MLSTM_QK), np.float32)
    perm[partner, col] = 1.0
    return jnp.tile(cos_h, (1, MLSTM_HEADS)), jnp.tile(sin_h, (1, MLSTM_HEADS)), jnp.asarray(perm, BF)


def _mlstm_prep_kernel(z_ref, gp_ref, ib_ref, fb_ref, *rest, rope):
    if rope:
        cos_ref, sin_ref, perm_ref, q_ref, k_ref, v_ref, g_ref = rest
    else:
        q_ref, k_ref, v_ref, g_ref = rest
    z = z_ref[0].astype(jnp.float32)
    q = z[:, 0:MLSTM_QK]
    k = z[:, MLSTM_QK:2 * MLSTM_QK]
    if rope:
        perm = perm_ref[...]
        parts = 1 if ACT_DTYPE == BF else 3
        q = q * cos_ref[...] + _split_dot(q, perm, parts) * sin_ref[...]
        k = k * cos_ref[...] + _split_dot(k, perm, parts) * sin_ref[...]
    q_ref[0] = (q * MLSTM_DQK ** -0.5).astype(BF)
    k_ref[0] = k.astype(BF)
    v_ref[0] = z[:, 2 * MLSTM_QK:2 * MLSTM_QK + BRANCH_WIDTH].astype(BF)
    gp = gp_ref[0].astype(jnp.float32)
    lane = lax.broadcasted_iota(jnp.int32, gp.shape, 1)
    ig = GATE_CAP * jnp.tanh((gp + ib_ref[...]) / GATE_CAP)
    fg = GATE_CAP * jnp.tanh((gp + fb_ref[...]) / GATE_CAP)
    g_ref[0] = jnp.where(lane < N_DIR * MLSTM_HEADS, ig, jax.nn.log_sigmoid(fg))


def mlstm_prep(proj, i_bias, f_bias, rope, tm):
    B, T, _ = proj.shape
    tm = min(tm, T)
    ng = N_DIR * MLSTM_HEADS
    ib = jnp.zeros((1, V7X_LANES), jnp.float32).at[0, 0:ng].set(i_bias.reshape(-1))
    fb = jnp.zeros((1, V7X_LANES), jnp.float32).at[0, ng:2 * ng].set(f_bias.reshape(-1))
    tok = lambda w: pl.BlockSpec((1, tm, w), lambda b, i: (b, i, 0))
    par = pl.BlockSpec((1, V7X_LANES), lambda b, i: (0, 0))
    args = [proj, proj, ib, fb]
    specs = [pl.BlockSpec((1, tm, MLSTM_MAIN_COLS), lambda b, i: (b, i, ML_BLK)),
             pl.BlockSpec((1, tm, V7X_LANES), lambda b, i: (b, i, ML_GATE_BLK)), par, par]
    if rope is not None:
        args += list(rope)
        specs += [pl.BlockSpec((tm, MLSTM_QK), lambda b, i: (i, 0)), pl.BlockSpec((tm, MLSTM_QK), lambda b, i: (i, 0)),
                  pl.BlockSpec((MLSTM_QK, MLSTM_QK), lambda b, i: (0, 0))]
    return pl.pallas_call(
        functools.partial(_mlstm_prep_kernel, rope=rope is not None),
        out_shape=(jax.ShapeDtypeStruct((B, T, MLSTM_QK), BF), jax.ShapeDtypeStruct((B, T, MLSTM_QK), BF),
                   jax.ShapeDtypeStruct((B, T, BRANCH_WIDTH), BF), jax.ShapeDtypeStruct((B, T, V7X_LANES), jnp.float32)),
        grid=(B, T // tm), in_specs=specs,
        out_specs=(tok(MLSTM_QK), tok(MLSTM_QK), tok(BRANCH_WIDTH), tok(V7X_LANES)),
        compiler_params=_cparams("parallel", "parallel"),
        name="mlstm_prep",
    )(*args)


def _mlstm_kernel(q_ref, k_ref, v_ref, gr_ref, c0_ref, n0_ref, m0_ref, *rest, reverse, bb, has_prev):
    if has_prev:
        hprev_ref, h_ref, cout_ref, nout_ref, mout_ref, c_scr, n_scr, m_scr = rest
    else:
        h_ref, cout_ref, nout_ref, mout_ref, c_scr, n_scr, m_scr = rest
    L = MLSTM_CHUNK
    DV = MLSTM_DV
    H = MLSTM_HEADS
    SL = MLSTM_GATE_SLOTS
    cidx = pl.program_id(1)

    @pl.when(cidx == 0)
    def _():
        c_scr[...] = c0_ref[...]
        n_scr[...] = n0_ref[...]
        m_scr[...] = m0_ref[...]

    ti = lax.broadcasted_iota(jnp.int32, (L, L), 0)
    si = lax.broadcasted_iota(jnp.int32, (L, L), 1)
    before = (si >= ti) if reverse else (si <= ti)
    tri = before.astype(jnp.float32)
    last = 0 if reverse else L - 1
    lo = _lo_lanes(L)
    lo_row = lax.broadcasted_iota(jnp.int32, (1, HP), 1) < (HP // 2)
    lo_col = lax.broadcasted_iota(jnp.int32, (HP, 1), 0) < (HP // 2)
    units = [(bi, p) for bi in range(bb) for p in range(H // 2)]
    blocks = [(bi, h) for bi in range(bb) for h in range(H)]
    nb = len(blocks)
    cat = lambda xs: jnp.concatenate(xs, axis=0)
    row_of = lambda bi, h: bi * SL + h

    i_rows = cat([gr_ref[bi, 0, 0, 0:SL, :] for bi in range(bb)])
    lf_rows = cat([gr_ref[bi, 0, 0, SL:2 * SL, :] for bi in range(bb)])
    m_prev = cat([m_scr[bi] for bi in range(bb)])
    b_rows = lax.dot_general(lf_rows, tri, (((1,), (1,)), ((), ())), precision=HIGHEST, preferred_element_type=jnp.float32)
    g_rows = i_rows - b_rows
    run = g_rows
    neg = jnp.full_like(g_rows, -jnp.inf)
    step = 1
    while step < L:
        shifted = (jnp.concatenate([run[:, step:], neg[:, :step]], axis=1) if reverse
                   else jnp.concatenate([neg[:, :step], run[:, :L - step]], axis=1))
        run = jnp.maximum(run, shifted)
        step *= 2
    m_rows = jnp.maximum(m_prev, run)
    m_end = jnp.broadcast_to(m_rows[:, last:last + 1], m_rows.shape)
    b_end = jnp.broadcast_to(b_rows[:, last:last + 1], b_rows.shape)
    sc_rows = jnp.exp(m_prev - m_rows)
    floor_rows = jnp.exp(-(b_rows + m_rows))
    wexp_rows = jnp.exp(g_rows - m_end)
    dec_rows = jnp.exp(m_prev - m_end)
    cols = cat([m_rows, sc_rows, floor_rows, wexp_rows]).T
    nr = bb * SL
    col_of = lambda kind, bi, h: cols[:, kind * nr + row_of(bi, h):kind * nr + row_of(bi, h) + 1]
    m_col = cat([col_of(0, bi, h) for bi, h in blocks])
    sc_col = cat([col_of(1, bi, h) for bi, h in blocks])
    floor_col = cat([col_of(2, bi, h) for bi, h in blocks])
    wexp_col = cat([col_of(3, bi, h) for bi, h in blocks])
    g_bcast = cat([jnp.broadcast_to(g_rows[row_of(bi, h):row_of(bi, h) + 1, :], (L, L)) for bi, h in blocks])
    decay = jnp.where(cat([before] * nb), jnp.exp(g_bcast - m_col), 0.0)

    qs = [_stack_pair(q_ref[bi, :, p * HP:(p + 1) * HP], lo) for bi, p in units]
    kp = [k_ref[bi, :, p * HP:(p + 1) * HP] for bi, p in units]
    vb = [v_ref[bi, :, h * DV:(h + 1) * DV] for bi, h in blocks]
    c_pair = [c_scr[bi, p] for bi, p in units]
    n_pair = [n_scr[bi, p] for bi, p in units]
    qk = cat([_dot_nt(qs[u], kp[u]) for u in range(len(units))])
    qc = cat([_bdot(qs[u], c_pair[u]) for u in range(len(units))])
    qn = cat([jnp.sum(qs[u].astype(jnp.float32) * n_pair[u], axis=1, keepdims=True) for u in range(len(units))])
    smat = qk * decay
    den = sc_col * qn + jnp.sum(smat, axis=1, keepdims=True)
    sb = smat.astype(BF)
    sv = cat([_bdot(sb[j * L:(j + 1) * L], vb[j]) for j in range(nb)])
    hout = (sc_col * qc + sv) / jnp.maximum(jnp.abs(den), floor_col)
    for j, (bi, h) in enumerate(blocks):
        hs = slice(h * DV, (h + 1) * DV)
        part = hout[j * L:(j + 1) * L]
        h_ref[bi, :, hs] = (part + hprev_ref[bi, :, hs]) if has_prev else part

    dec = lambda bi, h: dec_rows[row_of(bi, h):row_of(bi, h) + 1, 0:1]
    for u, (bi, p) in enumerate(units):
        j0 = bi * H + 2 * p
        kws = _stack_pair(kp[u], lo).astype(jnp.float32) * wexp_col[2 * u * L:2 * (u + 1) * L]
        upd = _dot_tn(kws, cat([vb[j0], vb[j0 + 1]]))
        c_scr[bi, p] = jnp.where(lo_col, dec(bi, 2 * p), dec(bi, 2 * p + 1)) * c_pair[u] + upd
        n_scr[bi, p] = jnp.where(lo_row, dec(bi, 2 * p), dec(bi, 2 * p + 1)) * n_pair[u] + jnp.sum(kws, axis=0, keepdims=True)
    m_next = b_end + m_end
    for bi in range(bb):
        m_scr[bi] = m_next[bi * SL:(bi + 1) * SL]

    @pl.when(cidx == pl.num_programs(1) - 1)
    def _():
        cout_ref[...] = c_scr[...]
        nout_ref[...] = n_scr[...]
        mout_ref[...] = m_scr[...]


def mlstm_gate_rows(gates):
    B, T, _ = gates.shape
    L = MLSTM_CHUNK
    H = MLSTM_HEADS
    SL = MLSTM_GATE_SLOTS
    ig = gates[:, :, :N_DIR * H].reshape(B, T, N_DIR, H)
    fg = gates[:, :, N_DIR * H:2 * N_DIR * H].reshape(B, T, N_DIR, H)
    zpad = jnp.zeros((B, T, N_DIR, SL - H), jnp.float32)
    rows = jnp.concatenate([ig, zpad, fg, zpad], axis=-1)
    return rows.reshape(B, T // L, L, N_DIR, 2 * SL).transpose(0, 1, 3, 4, 2)


def mlstm_chunked(q, k, v, g_rows, state, h_prev, direction):
    B, T, _ = q.shape
    L = MLSTM_CHUNK
    SL = MLSTM_GATE_SLOTS
    bb = MLSTM_SEQS_PER_STEP
    nc = T // L
    reverse = direction == 1
    cm = (lambda c: nc - 1 - c) if reverse else (lambda c: c)
    tok = lambda w: pl.BlockSpec((bb, L, w), lambda b, c: (b, cm(c), 0))
    st = lambda a: pl.BlockSpec((bb,) + a.shape[1:], lambda b, c: (b,) + (0,) * (a.ndim - 1))
    c0, n0, m0 = state
    has_prev = h_prev is not None
    args = [q, k, v, g_rows, c0, n0, m0] + ([h_prev] if has_prev else [])
    outs = pl.pallas_call(
        functools.partial(_mlstm_kernel, reverse=reverse, bb=bb, has_prev=has_prev),
        out_shape=(jax.ShapeDtypeStruct((B, T, BRANCH_WIDTH), jnp.float32),) + tuple(
            jax.ShapeDtypeStruct(a.shape, jnp.float32) for a in state),
        grid=(B // bb, nc),
        in_specs=[tok(MLSTM_QK), tok(MLSTM_QK), tok(BRANCH_WIDTH),
                  pl.BlockSpec((bb, 1, 1, 2 * SL, L), lambda b, c: (b, cm(c), direction, 0, 0)), st(c0), st(n0), st(m0)]
                 + ([tok(BRANCH_WIDTH)] if has_prev else []),
        out_specs=(tok(BRANCH_WIDTH), st(c0), st(n0), st(m0)),
        scratch_shapes=[pltpu.VMEM((bb,) + a.shape[1:], jnp.float32) for a in state],
        compiler_params=_cparams("parallel", "arbitrary"),
        name="mlstm_chunked",
    )(*args)
    return outs[0], outs[1:]


def _mlstm_readout_tile(h, o, norm_g):
    parts = []
    for hd in range(MLSTM_HEADS):
        x = h[:, hd * MLSTM_DV:(hd + 1) * MLSTM_DV]
        parts.append(x * lax.rsqrt(jnp.mean(x * x, axis=1, keepdims=True) + RMS_EPS))
    return jnp.concatenate(parts, axis=1) * norm_g * jax.nn.sigmoid(o)


def mlstm_mix(proj_x, proj_c, i_bias, f_bias, rope, tm):
    B = proj_x.shape[0]
    qx, kx, vx, gx = mlstm_prep(proj_x, i_bias, f_bias, rope, tm)
    qc, kc, vc, gc = mlstm_prep(proj_c, i_bias, f_bias, None, tm)
    gx, gc = mlstm_gate_rows(gx), mlstm_gate_rows(gc)
    h_x = h_c = None
    for d in range(N_DIR):
        st0 = (jnp.zeros((B, MLSTM_HEADS // 2, HP, HP), jnp.float32), jnp.zeros((B, MLSTM_HEADS // 2, 1, HP), jnp.float32),
               jnp.zeros((B, MLSTM_GATE_SLOTS, MLSTM_CHUNK), jnp.float32))
        h_c, st_ctx = mlstm_chunked(qc, kc, vc, gc, st0, h_c, d)
        h_x, _ = mlstm_chunked(qx, kx, vx, gx, st_ctx, h_x, d)
    return h_x, h_c


def _merge_kernel(y_ref, r_ref, k_ref, v_ref, g_ref, ag0_ref, ag1_ref, yb_ref, h_ref, og_ref, ga_ref, gb_ref, gc_ref,
                  x_ref, mod_ref, ka_ref, rk_ref, lg_ref, lb_ref, ng_ref, wb_ref, wo_ref, o_ref):
    f32 = lambda ref: ref[0].astype(jnp.float32)
    ya = _rwkv_readout_tile(y_ref[0], f32(r_ref), f32(k_ref), f32(v_ref), f32(g_ref), f32(ag0_ref), f32(ag1_ref),
                            ka_ref[...], rk_ref[...], lg_ref[...], lb_ref[...])
    yc = _mlstm_readout_tile(h_ref[0], f32(og_ref), ng_ref[...])
    merged = None
    for i, (y, gate_ref) in enumerate(((ya, ga_ref), (yb_ref[0], gb_ref), (yc, gc_ref))):
        t = jax.nn.sigmoid(f32(gate_ref)) * _bdot(y, wb_ref[i])
        merged = t if merged is None else merged + t
    o_ref[0] = x_ref[0] + mod_ref[0, 2:3, :] * _bdot(merged, wo_ref[...])


def merge_apply(rw, yb, h_ml, proj, x, mod, rw_params, ml_norm_g, w_branch, w_out, tm):
    B, T, D = x.shape
    tm = min(tm, T)
    C = BRANCH_WIDTH
    tok = lambda w: pl.BlockSpec((1, tm, w), lambda b, i: (b, i, 0))
    gate = lambda k: pl.BlockSpec((1, tm, D), lambda b, i: (b, i, GATE_BLK0 + k))
    par = pl.BlockSpec((1, C), lambda b, i: (0, 0))
    params = [p.reshape(1, C) for p in rw_params]
    return pl.pallas_call(
        _merge_kernel, out_shape=jax.ShapeDtypeStruct((B, T, D), jnp.float32), grid=(B, T // tm),
        in_specs=[tok(C)] * 9 + [pl.BlockSpec((1, tm, C), lambda b, i: (b, i, ML_OGATE_BLK)), gate(0), gate(1), gate(2), tok(D),
                  pl.BlockSpec((1, 6, D), lambda b, i: (b, 0, 0))] + [par] * 4
                 + [par, pl.BlockSpec(w_branch.shape, lambda b, i: (0, 0, 0)), pl.BlockSpec(w_out.shape, lambda b, i: (0, 0))],
        out_specs=tok(D),
        compiler_params=_cparams("parallel", "parallel"),
        name="merge_branches",
    )(*rw, yb, h_ml, proj, proj, proj, proj, x, mod, *params, ml_norm_g.reshape(1, C), w_branch, w_out)


def _route_kernel(x_ref, g_ref, mod_ref, wr_ref, rb_ref, h_ref, gate_ref):
    h = _norm_mod(x_ref[0], g_ref[...], mod_ref[0, 4:5, :], mod_ref[0, 3:4, :])
    h_ref[0] = h.astype(BF)
    tm = h.shape[0]
    logits = lax.dot_general(wr_ref[...], h, (((1,), (1,)), ((), ())), precision=HIGHEST, preferred_element_type=jnp.float32)
    scores = jax.nn.sigmoid(logits)
    sel = scores + rb_ref[...]
    gsz = N_EXPERTS // N_GROUPS
    grp = sel.reshape(N_GROUPS, gsz, tm)
    iota_in = lax.broadcasted_iota(jnp.int32, grp.shape, 1)
    m1 = jnp.max(grp, axis=1, keepdims=True)
    first = jnp.min(jnp.where(grp == m1, iota_in, gsz), axis=1, keepdims=True)
    m2 = jnp.max(jnp.where(iota_in == first, -jnp.inf, grp), axis=1, keepdims=True)
    gscore = (m1 + m2).reshape(N_GROUPS, tm)
    gi = lax.broadcasted_iota(jnp.int32, (N_GROUPS, tm), 0)
    rank = jnp.zeros((N_GROUPS, tm), jnp.int32)
    for g2 in range(N_GROUPS):
        other = gscore[g2:g2 + 1, :]
        rank = rank + ((other > gscore) | ((other == gscore) & (g2 < gi))).astype(jnp.int32)
    gmask = rank < TOPK_GROUPS
    emask = jnp.broadcast_to(gmask.reshape(N_GROUPS, 1, tm), (N_GROUPS, gsz, tm)).reshape(N_EXPERTS, tm)
    cand = jnp.where(emask, sel, NEG_INF)
    ei = lax.broadcasted_iota(jnp.int32, (N_EXPERTS, tm), 0)
    chosen = jnp.zeros((N_EXPERTS, tm), jnp.bool_)
    for _ in range(TOP_K):
        mx = jnp.max(cand, axis=0, keepdims=True)
        idx = jnp.min(jnp.where(cand == mx, ei, N_EXPERTS), axis=0, keepdims=True)
        hit = ei == idx
        chosen = chosen | hit
        cand = jnp.where(hit, -jnp.inf, cand)
    w = jnp.where(chosen, scores, 0.0)
    w = w / jnp.sum(w, axis=0, keepdims=True) * ROUTED_SCALE
    gate_ref[0] = jnp.concatenate([w, jnp.zeros((V7X_LANES - N_EXPERTS, tm), jnp.float32)], axis=0).T


def moe_route(x, g, mod, w_router, router_bias, tm):
    B, T, D = x.shape
    tm = min(tm, T)
    tok = pl.BlockSpec((1, tm, D), lambda b, i: (b, i, 0))
    return pl.pallas_call(
        _route_kernel,
        out_shape=(jax.ShapeDtypeStruct((B, T, D), BF), jax.ShapeDtypeStruct((B, T, V7X_LANES), jnp.float32)),
        grid=(B, T // tm),
        in_specs=[tok, pl.BlockSpec((1, D), lambda b, i: (0, 0)), pl.BlockSpec((1, 6, D), lambda b, i: (b, 0, 0)),
                  pl.BlockSpec((N_EXPERTS, D), lambda b, i: (0, 0)), pl.BlockSpec((N_EXPERTS, 1), lambda b, i: (0, 0))],
        out_specs=(tok, pl.BlockSpec((1, tm, V7X_LANES), lambda b, i: (b, i, 0))),
        compiler_params=_cparams("parallel", "parallel"),
        name="moe_route",
    )(x, g.reshape(1, D), mod, w_router.T, router_bias.reshape(N_EXPERTS, 1))


def _moe_kernel(h_ref, gate_ref, x_ref, mod_ref, sel_ref, wg_ref, wu_ref, wd_ref, sg_ref, su_ref, sd_ref, o_ref, *, tm):
    j = pl.program_id(1)
    rows = pl.ds(pl.multiple_of(pl.program_id(2) * tm, tm), tm)
    h = h_ref[0]

    @pl.when(j == 0)
    def _():
        sh = jax.nn.silu(_bdot(h, sg_ref[...])) * _bdot(h, su_ref[...])
        o_ref[0, rows, :] = _bdot(sh, sd_ref[...])

    g8 = _split_dot(gate_ref[0], sel_ref[0], 2)
    act = jax.nn.silu(_bdot(h, wg_ref[...])) * _bdot(h, wu_ref[...])
    act = jnp.concatenate([act[:, e * D_EXPERT:(e + 1) * D_EXPERT] * g8[:, e:e + 1] for e in range(MOE_STEP_EXPERTS)],
                          axis=1)
    o_ref[0, rows, :] += _bdot(act, wd_ref[...])

    @pl.when(j == pl.num_programs(1) - 1)
    def _():
        o_ref[0, rows, :] = x_ref[0] + mod_ref[0, 5:6, :] * o_ref[0, rows, :]


def _moe_select_table():
    se = MOE_STEP_EXPERTS
    t = np.zeros((N_EXPERTS // se, V7X_LANES, V7X_LANES), np.float32)
    for j in range(N_EXPERTS // se):
        for e in range(se):
            t[j, j * se + e, e] = 1.0
    return jnp.asarray(t, BF)


def moe_apply(h2, gates, x, mod, wg, wu, wd, sg, su, sd, tm):
    B, T, D = x.shape
    sw = MOE_STEP_EXPERTS * D_EXPERT
    n_groups = N_EXPERTS // MOE_STEP_EXPERTS
    tok = pl.BlockSpec((1, tm, D), lambda b, j, i: (b, i, 0))
    x_last = pl.BlockSpec((1, tm, D), lambda b, j, i: (b, jnp.where(j == n_groups - 1, i, 0), 0))
    full = lambda a: pl.BlockSpec(a.shape, lambda b, j, i: (0,) * a.ndim)
    return pl.pallas_call(
        functools.partial(_moe_kernel, tm=tm), out_shape=jax.ShapeDtypeStruct((B, T, D), jnp.float32),
        grid=(B, n_groups, T // tm),
        in_specs=[tok, pl.BlockSpec((1, tm, V7X_LANES), lambda b, j, i: (b, i, 0)), x_last,
                  pl.BlockSpec((1, 6, D), lambda b, j, i: (b, 0, 0)),
                  pl.BlockSpec((1, V7X_LANES, V7X_LANES), lambda b, j, i: (j, 0, 0)),
                  pl.BlockSpec((D, sw), lambda b, j, i: (0, j)), pl.BlockSpec((D, sw), lambda b, j, i: (0, j)),
                  pl.BlockSpec((sw, D), lambda b, j, i: (j, 0)), full(sg), full(su), full(sd)],
        out_specs=pl.BlockSpec((1, T, D), lambda b, j, i: (b, 0, 0)),
        compiler_params=pltpu.CompilerParams(dimension_semantics=("parallel", "arbitrary", "arbitrary"),
                                             vmem_limit_bytes=MOE_VMEM_LIMIT_BYTES),
        name="moe_experts",
    )(h2, gates, x, mod, _moe_select_table(), wg, wu, wd, sg, su, sd)


def _reorder_w_in(w):
    o_na = RWKV_COLS
    o_ml = o_na + NA_COLS
    o_mg = o_ml + MLSTM_MAIN_COLS
    o_gate = o_mg + MLSTM_GATES
    pad = jnp.zeros((w.shape[0], V7X_LANES - MLSTM_GATES), w.dtype)
    return jnp.concatenate([w[:, o_na:o_ml], w[:, o_ml:o_mg], w[:, o_gate:], w[:, :RWKV_COLS], w[:, o_mg:o_gate], pad],
                           axis=1).astype(BF)


def kernel(x, c, ctx, c_ctx, w_ada, b_ada, norm1_g, norm2_g, w_in, rw_mu, rw_w0, rw_w2, rw_a0, rw_a2, rw_k_k, rw_k_a, rw_r_k, rw_g2, rw_lnx_g, rw_lnx_b, rw_v0, rw_v1, rw_v2, na_qn_g, na_kn_g, na_rpb, ml_i_bias, ml_f_bias, ml_norm_g, w_branch, w_out, moe_router, moe_bias, moe_w_gate, moe_w_up, moe_w_down, sh_w_gate, sh_w_up, sh_w_down):
    B, S, D = x.shape
    n_ctx = ctx.shape[1]
    tm = TOKEN_TILE
    assert S % tm == 0 and n_ctx % min(tm, n_ctx) == 0 and PROJ_COLS == IN_COLS + V7X_LANES - MLSTM_GATES
    rope = rope_tables(S)
    n_cond = B + 1
    cond_pad = (-n_cond) % V7X_SUBLANES
    s_cond = jnp.pad(jnp.concatenate([jax.nn.silu(c), jax.nn.silu(c_ctx)[None]], axis=0), ((0, cond_pad), (0, 0)))
    vf_x = vf_c = None
    for l in range(DEPTH):
        need_ctx = l < DEPTH - 1
        mod = pmm(s_cond, w_ada[l]) + b_ada[l]
        mod_x = mod[:B].reshape(B, 6, D)
        mod_c = jnp.broadcast_to(mod[B].reshape(1, 6, D), (B, 6, D))
        w_proj = _reorder_w_in(w_in[l])
        proj_x = norm_proj(x, norm1_g[l], mod_x, w_proj)
        proj_c = norm_proj(ctx.reshape(1, B * n_ctx, D), norm1_g[l], mod_c[:1], w_proj).reshape(B, n_ctx, PROJ_COLS)
        vres = None if l == 0 else (rw_v0[l - 1], rw_v1[l - 1], rw_v2[l - 1])
        rw_x, rw_c, vf_x, vf_c = rwkv_mix(proj_x, proj_c, vf_x, vf_c, rw_mu[l], rw_w0[l], rw_w2[l], rw_a0[l], rw_a2[l],
                                          rw_k_k[l], rw_k_a[l], rw_g2[l], vres, tm)
        rw_params = (rw_k_a[l], rw_r_k[l], rw_lnx_g[l], rw_lnx_b[l])
        yb_x, yb_c = na_mix(proj_x, proj_c, na_qn_g[l], na_kn_g[l], na_rpb[l], need_ctx, tm)
        hm_x, hm_c = mlstm_mix(proj_x, proj_c, ml_i_bias[l], ml_f_bias[l], rope, tm)
        wb = w_branch[l].astype(BF)
        wo = w_out[l].astype(BF)
        wg = moe_w_gate[l].transpose(1, 0, 2).reshape(D, N_EXPERTS * D_EXPERT).astype(BF)
        wu = moe_w_up[l].transpose(1, 0, 2).reshape(D, N_EXPERTS * D_EXPERT).astype(BF)
        wd = moe_w_down[l].reshape(N_EXPERTS * D_EXPERT, D).astype(BF)
        shared = (sh_w_gate[l].astype(BF), sh_w_up[l].astype(BF), sh_w_down[l].astype(BF))
        x = merge_apply(rw_x, yb_x, hm_x, proj_x, x, mod_x, rw_params, ml_norm_g[l], wb, wo, tm)
        h2, gates = moe_route(x, norm2_g[l], mod_x, moe_router[l], moe_bias[l], tm)
        x = moe_apply(h2, gates, x, mod_x, wg, wu, wd, *shared, MOE_TOKEN_TILE)
        if need_ctx:
            ctx = merge_apply(rw_c, yb_c, hm_c, proj_c, ctx, mod_c, rw_params, ml_norm_g[l], wb, wo, tm)
            h2, gates = moe_route(ctx, norm2_g[l], mod_c, moe_router[l], moe_bias[l], tm)
            ctx = moe_apply(h2, gates, ctx, mod_c, wg, wu, wd, *shared, min(tm, n_ctx))
    return x
```
